```python
import jax
import jax.numpy as jnp
from jax import lax
import numpy as np

D_MODEL = 2048
BATCH = 4
SEQ = 2048
DEPTH = 4

CTX_LEN = 256
GRID_W = 64
EPS = 1e-6
F32 = jnp.float32

MLA_HEADS = 8
MLA_NOPE = 64
MLA_ROPE = 32
MLA_V = 64
MLA_Q_RANK = 512
MLA_KV_RANK = 256
ROPE_THETA = 10000.0
Q_BLOCK = 128
CF_WIDTH = 512
CF_KERNEL = 31
GLA_HEADS = 4
GLA_DK = 64
GLA_DV = 128
GLA_GATE_RANK = 16
GLA_GATE_TAU = 16.0
GLA_CHUNK = 64
SC_WIDTH = 512
SC_KERNEL = 3
N_BRANCHES = 4
N_EXPERTS = 16
N_GROUPS = 4
TOP_K = 2
EXPERT_FF = 512

PROJ_PARTS = (
    ('mla_cq', MLA_Q_RANK),
    ('mla_ckv', MLA_KV_RANK),
    ('mla_kr', MLA_ROPE),
    ('cf_in', 2 * CF_WIDTH),
    ('gla_q', GLA_HEADS * GLA_DK),
    ('gla_k', GLA_HEADS * GLA_DK),
    ('gla_v', GLA_HEADS * GLA_DV),
    ('gla_gf', GLA_GATE_RANK),
    ('gla_gb', GLA_GATE_RANK),
    ('gla_r', GLA_HEADS * GLA_DV),
    ('sc_in', 3 * SC_WIDTH),
    ('merge_gate', N_BRANCHES * D_MODEL),
)
PROJ_WIDTH = sum(width for _, width in PROJ_PARTS)
CTX_STATE_PARTS = ('mla_ckv', 'mla_kr', 'gla_k', 'gla_v', 'gla_gf', 'gla_gb')

kernel_name = 'hybrid_prefix_diffusion_trunk'


def proj_offsets():
    offsets, start = {}, 0
    for name, width in PROJ_PARTS:
        offsets[name] = (start, start + width)
        start += width
    return offsets


def split_parts(u, offsets):
    return {name: u[..., a:b] for name, (a, b) in offsets.items()}


def rmsnorm(x, g):
    x32 = x.astype(F32)
    y = x32 * lax.rsqrt(jnp.mean(x32 * x32, axis=-1, keepdims=True) + EPS)
    return y.astype(x.dtype) * g


def layernorm(x, g, b):
    x32 = x.astype(F32)
    xc = x32 - jnp.mean(x32, axis=-1, keepdims=True)
    y = xc * lax.rsqrt(jnp.mean(xc * xc, axis=-1, keepdims=True) + EPS)
    return y.astype(x.dtype) * g + b


def modulate(x, shift, scale):
    return x * (1.0 + scale) + shift


def dwconv(x, w):
    return lax.conv_general_dilated(x, w[:, None, :], window_strides=(1,), padding='SAME',
                                    dimension_numbers=('NWC', 'WIO', 'NWC'),
                                    feature_group_count=x.shape[-1])


def flip(t):
    return jnp.flip(t, axis=1)


def axial_rope_tables(rows, dtype):
    pos = jnp.arange(rows * GRID_W)
    row = (pos // GRID_W).astype(F32)
    col = (pos % GRID_W).astype(F32)
    n_freq = MLA_ROPE // 4
    freqs = ROPE_THETA ** (-jnp.arange(n_freq, dtype=F32) / n_freq)
    ang = jnp.concatenate([row[:, None] * freqs, col[:, None] * freqs], axis=-1)
    return jnp.cos(ang).astype(dtype), jnp.sin(ang).astype(dtype)


def apply_axial_rope(x, cos, sin):
    n_freq = MLA_ROPE // 4
    half = MLA_ROPE // 2
    cos, sin = cos[:, None, :], sin[:, None, :]

    def rot(t, cs, sn):
        t1, t2 = t[..., :n_freq], t[..., n_freq:]
        return jnp.concatenate([t1 * cs - t2 * sn, t1 * sn + t2 * cs], axis=-1)

    return jnp.concatenate([rot(x[..., :half], cos[..., :n_freq], sin[..., :n_freq]),
                            rot(x[..., half:], cos[..., n_freq:], sin[..., n_freq:])], axis=-1)


def mla_branch(p, pc, lp, cos, sin, need_ctx):
    heads, nope = MLA_HEADS, MLA_NOPE
    scale = (MLA_NOPE + MLA_ROPE) ** -0.5

    def q_proj(cq):
        q = (rmsnorm(cq, lp['g_q']) @ lp['w_uq']).reshape(*cq.shape[:2], heads, MLA_NOPE + MLA_ROPE)
        return q[..., :nope], q[..., nope:]

    def kv_proj(ckv):
        kv = (rmsnorm(ckv, lp['g_kv']) @ lp['w_ukv']).reshape(*ckv.shape[:2], heads, MLA_NOPE + MLA_V)
        return kv[..., :nope], kv[..., nope:]

    def probs(s):
        return jax.nn.softmax(s.astype(F32) * scale, axis=-1)

    kn_c, v_c = kv_proj(pc['mla_ckv'])
    kr_c = pc['mla_kr']
    qn, qr = q_proj(p['mla_cq'])
    kn, v = kv_proj(p['mla_ckv'])
    qr_rot = apply_axial_rope(qr, cos, sin)
    kr_rot = apply_axial_rope(p['mla_kr'][:, :, None, :], cos, sin)[:, :, 0]
    bsz, seq = qn.shape[:2]
    n_blk = seq // Q_BLOCK

    def to_blocks(t):
        return t.reshape(bsz, n_blk, Q_BLOCK, *t.shape[2:]).swapaxes(0, 1)

    def attend_block(blk):
        qn_b, qr_b, qp_b = blk
        s_lat = jnp.einsum('bqhd,bkhd->bhqk', qn_b, kn) + jnp.einsum('bqhr,bkr->bhqk', qr_b, kr_rot)
        s_ctx = jnp.einsum('bqhd,bkhd->bhqk', qn_b, kn_c) + jnp.einsum('bqhr,bkr->bhqk', qp_b, kr_c)
        pr = probs(jnp.concatenate([s_lat, s_ctx], axis=-1)).astype(v.dtype)
        return (jnp.einsum('bhqk,bkhd->bqhd', pr[..., :seq], v)
                + jnp.einsum('bhqk,bkhd->bqhd', pr[..., seq:], v_c))

    o = lax.map(attend_block, (to_blocks(qn), to_blocks(qr_rot), to_blocks(qr)))
    y_lat = o.swapaxes(0, 1).reshape(bsz, seq, heads * MLA_V) @ lp['w_mla_o']
    if not need_ctx:
        return y_lat, None
    qn_c, qr_c = q_proj(pc['mla_cq'])
    s = jnp.einsum('bqhd,bkhd->bhqk', qn_c, kn_c) + jnp.einsum('bqhr,bkr->bhqk', qr_c, kr_c)
    o_c = jnp.einsum('bhqk,bkhd->bqhd', probs(s).astype(v_c.dtype), v_c)
    y_ctx = o_c.reshape(*o_c.shape[:2], heads * MLA_V) @ lp['w_mla_o']
    return y_lat, y_ctx


def conformer_branch(u, lp):
    a, b = jnp.split(u, 2, axis=-1)
    h = a * jax.nn.sigmoid(b)
    h = dwconv(h, lp['w_cf_dw']) + lp['b_cf_dw']
    h = jax.nn.silu(layernorm(h, lp['g_cf_ln'], lp['b_cf_ln']))
    return h @ lp['w_cf_o']


def shortconv_branch(u, lp):
    gate_b, gate_c, h = jnp.split(u, 3, axis=-1)
    return (gate_b * dwconv(gate_c * h, lp['w_sc_conv'])) @ lp['w_sc_o']


def gla_chunk_scan(q, k, v, log_a, s0):
    bsz, length, n_heads, _ = q.shape
    n_chunk = length // GLA_CHUNK
    causal = jnp.tril(jnp.ones((GLA_CHUNK, GLA_CHUNK), dtype=jnp.bool_))

    def chunks(t):
        return t.astype(F32).reshape(bsz, n_chunk, GLA_CHUNK, n_heads, t.shape[-1]).transpose(1, 0, 3, 2, 4)

    def step(state, inp):
        qc, kc, vc, ac = inp
        b = jnp.cumsum(ac, axis=2)
        diff = b[:, :, :, None, :] - b[:, :, None, :, :]
        decay = jnp.where(causal[:, :, None], jnp.exp(jnp.minimum(diff, 0.0)), 0.0)
        attn = jnp.einsum('bhtd,bhtsd,bhsd->bhts', qc, decay, kc)
        o = jnp.einsum('bhts,bhsv->bhtv', attn, vc) + jnp.einsum('bhtd,bhdv->bhtv', qc * jnp.exp(b), state)
        b_last = b[:, :, -1:, :]
        state = (jnp.exp(b_last[:, :, 0, :])[..., None] * state
                 + jnp.einsum('bhsd,bhsv->bhdv', kc * jnp.exp(b_last - b), vc))
        return state, o

    state, o = lax.scan(step, s0, (chunks(q), chunks(k), chunks(v), chunks(log_a)))
    o = o.transpose(1, 0, 3, 2, 4).reshape(bsz, length, n_heads, v.shape[-1]).astype(v.dtype)
    return o, state


def gla_final_state(k, v, log_a):
    b = jnp.cumsum(log_a, axis=1)
    w = jnp.exp(b[:, -1:] - b)
    return jnp.einsum('blhd,blhv->bhdv', k.astype(F32) * w, v.astype(F32))


def gla_branch(p, pc, lp, need_ctx):
    scale = GLA_DK ** -0.5

    def heads(t, dim):
        return t.reshape(*t.shape[:2], GLA_HEADS, dim)

    def log_decay(low, w2, b2):
        return heads(jax.nn.log_sigmoid((low @ w2 + b2).astype(F32)) / GLA_GATE_TAU, GLA_DK)

    def readout(o, r):
        o = rmsnorm(o, lp['g_gla_norm']) * jax.nn.silu(heads(r, GLA_DV))
        return o.reshape(*o.shape[:2], GLA_HEADS * GLA_DV) @ lp['w_gla_o']

    k_c, v_c = heads(pc['gla_k'], GLA_DK), heads(pc['gla_v'], GLA_DV)
    af_c = log_decay(pc['gla_gf'], lp['w_gla_gf'], lp['b_gla_gf'])
    ab_c = log_decay(pc['gla_gb'], lp['w_gla_gb'], lp['b_gla_gb'])
    y_ctx = None
    if need_ctx:
        q_c = heads(pc['gla_q'], GLA_DK) * scale
        zero = jnp.zeros((k_c.shape[0], GLA_HEADS, GLA_DK, GLA_DV), F32)
        o_cf, s_f = gla_chunk_scan(q_c, k_c, v_c, af_c, zero)
        o_cb, s_b = gla_chunk_scan(flip(q_c), flip(k_c), flip(v_c), flip(ab_c), zero)
        y_ctx = readout(o_cf + flip(o_cb), pc['gla_r'])
    else:
        s_f = gla_final_state(k_c, v_c, af_c)
        s_b = gla_final_state(flip(k_c), flip(v_c), flip(ab_c))
    q = heads(p['gla_q'], GLA_DK) * scale
    k, v = heads(p['gla_k'], GLA_DK), heads(p['gla_v'], GLA_DV)
    af = log_decay(p['gla_gf'], lp['w_gla_gf'], lp['b_gla_gf'])
    ab = log_decay(p['gla_gb'], lp['w_gla_gb'], lp['b_gla_gb'])
    o_f, _ = gla_chunk_scan(q, k, v, af, s_f)
    o_b, _ = gla_chunk_scan(flip(q), flip(k), flip(v), flip(ab), s_b)
    return readout(o_f + flip(o_b), p['gla_r']), y_ctx


def moe(h, w_router, b_router, w_gate, w_up, w_down):
    n_tok = h.shape[0]
    per_group = N_EXPERTS // N_GROUPS
    affinity = jax.nn.sigmoid((h @ w_router).astype(F32))
    select = (affinity + b_router.astype(F32)).reshape(n_tok, N_GROUPS, per_group)
    group_score = lax.top_k(select, TOP_K)[0].sum(axis=-1)
    in_group = jax.nn.one_hot(jnp.argmax(group_score, axis=-1), N_GROUPS, dtype=jnp.bool_)
    masked = jnp.where(in_group[:, :, None], select, -jnp.inf).reshape(n_tok, N_EXPERTS)
    _, idx = lax.top_k(masked, TOP_K)
    w = jnp.take_along_axis(affinity, idx, axis=-1)
    w = w / jnp.sum(w, axis=-1, keepdims=True)
    combine = jnp.sum(jax.nn.one_hot(idx, N_EXPERTS, dtype=F32) * w[..., None], axis=1)
    g = jnp.einsum('td,edf->etf', h, w_gate)
    u = jnp.einsum('td,edf->etf', h, w_up)
    act = jax.nn.silu(g) * u * combine.T[:, :, None].astype(h.dtype)
    return jnp.einsum('etf,efd->td', act, w_down)


def mixer(h, hc, lp, cos, sin, need_ctx):
    offsets = proj_offsets()
    p = split_parts(h @ lp['w_in'], offsets)
    if need_ctx:
        pc = split_parts(hc @ lp['w_in'], offsets)
    else:
        pc = {name: hc @ lp['w_in'][:, offsets[name][0]:offsets[name][1]] for name in CTX_STATE_PARTS}
    a_lat, a_ctx = mla_branch(p, pc, lp, cos, sin, need_ctx)
    g_lat, g_ctx = gla_branch(p, pc, lp, need_ctx)

    def merge(parts, branches):
        logits = parts['merge_gate']
        gates = jax.nn.sigmoid(logits).reshape(*logits.shape[:-1], N_BRANCHES, -1)
        y = gates[..., 0, :] * branches[0]
        for i in range(1, N_BRANCHES):
            y = y + gates[..., i, :] * branches[i]
        return y @ lp['w_mix_o']

    y_lat = merge(p, (a_lat, conformer_branch(p['cf_in'], lp), g_lat, shortconv_branch(p['sc_in'], lp)))
    if not need_ctx:
        return y_lat, None
    y_ctx = merge(pc, (a_ctx, conformer_branch(pc['cf_in'], lp), g_ctx, shortconv_branch(pc['sc_in'], lp)))
    return y_lat, y_ctx


def setup_inputs(seed: int = 0) -> dict:
    key = jax.random.key(seed)
    keys = iter(jax.random.split(key, 40))

    def normal(shape, scale):
        return jax.random.normal(next(keys), shape, F32) * scale

    def gain(shape):
        return 1.0 + normal(shape, 0.05)

    L, D = DEPTH, D_MODEL
    return {
        'x': normal((BATCH, SEQ, D), 1.0),
        'c': normal((BATCH, D), 1.0),
        'ctx': normal((BATCH, CTX_LEN, D), 1.0),
        'c_ctx': normal((D,), 1.0),
        'w_mod': normal((L, D, 6 * D), 0.5 * D ** -0.5),
        'b_mod': normal((L, 6 * D), 0.01),
        'g_mix': gain((L, D)),
        'g_ffn': gain((L, D)),
        'w_in': normal((L, D, PROJ_WIDTH), D ** -0.5),
        'g_q': gain((L, MLA_Q_RANK)),
        'w_uq': normal((L, MLA_Q_RANK, MLA_HEADS * (MLA_NOPE + MLA_ROPE)), MLA_Q_RANK ** -0.5),
        'g_kv': gain((L, MLA_KV_RANK)),
        'w_ukv': normal((L, MLA_KV_RANK, MLA_HEADS * (MLA_NOPE + MLA_V)), MLA_KV_RANK ** -0.5),
        'w_mla_o': normal((L, MLA_HEADS * MLA_V, D), (MLA_HEADS * MLA_V) ** -0.5),
        'w_cf_dw': normal((L, CF_KERNEL, CF_WIDTH), CF_KERNEL ** -0.5),
        'b_cf_dw': normal((L, CF_WIDTH), 0.01),
        'g_cf_ln': gain((L, CF_WIDTH)),
        'b_cf_ln': normal((L, CF_WIDTH), 0.01),
        'w_cf_o': normal((L, CF_WIDTH, D), CF_WIDTH ** -0.5),
        'w_gla_gf': normal((L, GLA_GATE_RANK, GLA_HEADS * GLA_DK), GLA_GATE_RANK ** -0.5),
        'b_gla_gf': normal((L, GLA_HEADS * GLA_DK), 0.1),
        'w_gla_gb': normal((L, GLA_GATE_RANK, GLA_HEADS * GLA_DK), GLA_GATE_RANK ** -0.5),
        'b_gla_gb': normal((L, GLA_HEADS * GLA_DK), 0.1),
        'g_gla_norm': gain((L, GLA_DV)),
        'w_gla_o': normal((L, GLA_HEADS * GLA_DV, D), (GLA_HEADS * GLA_DV) ** -0.5),
        'w_sc_conv': normal((L, SC_KERNEL, SC_WIDTH), SC_KERNEL ** -0.5),
        'w_sc_o': normal((L, SC_WIDTH, D), SC_WIDTH ** -0.5),
        'w_mix_o': normal((L, D, D), D ** -0.5),
        'w_router': normal((D, N_EXPERTS), D ** -0.5),
        'b_router': normal((N_EXPERTS,), 0.01),
        'w_e_gate': normal((L, N_EXPERTS, D, EXPERT_FF), D ** -0.5),
        'w_e_up': normal((L, N_EXPERTS, D, EXPERT_FF), D ** -0.5),
        'w_e_down': normal((L, N_EXPERTS, EXPERT_FF, D), EXPERT_FF ** -0.5),
        'g_final': gain((D,)),
    }


def reference(x, c, ctx, c_ctx, w_mod, b_mod, g_mix, g_ffn, w_in, g_q, w_uq, g_kv, w_ukv, w_mla_o,
              w_cf_dw, b_cf_dw, g_cf_ln, b_cf_ln, w_cf_o, w_gla_gf, b_gla_gf, w_gla_gb, b_gla_gb,
              g_gla_norm, w_gla_o, w_sc_conv, w_sc_o, w_mix_o, w_router, b_router,
              w_e_gate, w_e_up, w_e_down, g_final):
    bsz, seq, d = x.shape
    n_ctx = ctx.shape[1]
    rows = seq // GRID_W
    cos, sin = axial_rope_tables(rows, x.dtype)
    silu_c = jax.nn.silu(c)
    silu_cc = jax.nn.silu(c_ctx)
    xc = ctx
    for l in range(DEPTH):
        need_ctx = l < DEPTH - 1
        lp = {
            'w_in': w_in[l], 'g_q': g_q[l], 'w_uq': w_uq[l], 'g_kv': g_kv[l], 'w_ukv': w_ukv[l],
            'w_mla_o': w_mla_o[l], 'w_cf_dw': w_cf_dw[l], 'b_cf_dw': b_cf_dw[l], 'g_cf_ln': g_cf_ln[l],
            'b_cf_ln': b_cf_ln[l], 'w_cf_o': w_cf_o[l], 'w_gla_gf': w_gla_gf[l], 'b_gla_gf': b_gla_gf[l],
            'w_gla_gb': w_gla_gb[l], 'b_gla_gb': b_gla_gb[l], 'g_gla_norm': g_gla_norm[l],
            'w_gla_o': w_gla_o[l], 'w_sc_conv': w_sc_conv[l], 'w_sc_o': w_sc_o[l], 'w_mix_o': w_mix_o[l],
        }
        sh1, sc1, gt1, sh2, sc2, gt2 = [m[:, None, :] for m in jnp.split(silu_c @ w_mod[l] + b_mod[l], 6, axis=-1)]
        sh1c, sc1c, gt1c, sh2c, sc2c, gt2c = jnp.split(silu_cc @ w_mod[l] + b_mod[l], 6, axis=-1)
        h = modulate(rmsnorm(x, g_mix[l]), sh1, sc1)
        hc = modulate(rmsnorm(xc, g_mix[l]), sh1c, sc1c)
        y, yc = mixer(h, hc, lp, cos, sin, need_ctx)
        x = x + gt1 * y
        hf = modulate(rmsnorm(x, g_ffn[l]), sh2, sc2).reshape(bsz * seq, d)
        if need_ctx:
            xc = xc + gt1c * yc
            hfc = modulate(rmsnorm(xc, g_ffn[l]), sh2c, sc2c).reshape(bsz * n_ctx, d)
            f = moe(jnp.concatenate([hf, hfc], axis=0), w_router, b_router, w_e_gate[l], w_e_up[l], w_e_down[l])
            x = x + gt2 * f[:bsz * seq].reshape(bsz, seq, d)
            xc = xc + gt2c * f[bsz * seq:].reshape(bsz, n_ctx, d)
        else:
            f = moe(hf, w_router, b_router, w_e_gate[l], w_e_up[l], w_e_down[l])
            x = x + gt2 * f.reshape(bsz, seq, d)
    return rmsnorm(x, g_final)
```

```python
import functools

import numpy as np
import jax
import jax.numpy as jnp
from jax import lax
from jax.experimental import pallas as pl
from jax.experimental.pallas import tpu as pltpu

F32 = jnp.float32
BF16 = jnp.bfloat16
EPS = 1e-6

GRID_W = 64
ROPE_THETA = 10000.0
MLA_HEADS = 8
MLA_NOPE = 64
MLA_ROPE = 32
MLA_V = 64
MLA_Q_RANK = 512
MLA_KV_RANK = 256
MLA_SLOT = 128
CF_WIDTH = 512
CF_KERNEL = 31
GLA_HEADS = 4
GLA_DK = 64
GLA_DV = 128
GLA_GATE_RANK = 16
GLA_GATE_TAU = 16.0
GLA_CHUNK = 64
GLA_LEVELS = 6
SC_WIDTH = 512
SC_KERNEL = 3
N_BRANCHES = 4
N_EXPERTS = 16
N_GROUPS = 4
EXPERT_FF = 512
CONV_HALO = 16
VMEM_LIMIT = 56 * 1024 * 1024


def _cparams(*sem):
    return pltpu.CompilerParams(dimension_semantics=sem, vmem_limit_bytes=VMEM_LIMIT)


def _pick(n, cands):
    for c in cands:
        if n % c == 0:
            return c
    raise ValueError(f"no tile for {n}")


def _dot(a, b):
    return jnp.dot(a, b, preferred_element_type=F32)


def _dot_nt(a, b):
    return lax.dot_general(a, b, (((1,), (1,)), ((), ())), preferred_element_type=F32)


def _sigmoid(x):
    return 0.5 * jnp.tanh(0.5 * x) + 0.5


def _proj_layout(d):
    parts = [("merge", N_BRANCHES * d), ("sc_b", SC_WIDTH), ("sc_c", SC_WIDTH), ("sc_h", SC_WIDTH),
             ("cf_a", CF_WIDTH), ("cf_b", CF_WIDTH), ("cq", MLA_Q_RANK), ("gla_v", GLA_HEADS * GLA_DV),
             ("gla_r", GLA_HEADS * GLA_DV), ("ckv", MLA_KV_RANK), ("gla_q", GLA_HEADS * GLA_DK),
             ("gla_k", GLA_HEADS * GLA_DK), ("kr", 128), ("gate", 128)]
    off, start = {}, 0
    for name, width in parts:
        assert start % width == 0 or name == "merge"
        off[name] = start
        start += width
    return off, start


def _src_layout(d):
    parts = (("cq", MLA_Q_RANK), ("ckv", MLA_KV_RANK), ("kr", MLA_ROPE), ("cf", 2 * CF_WIDTH),
             ("gla_q", GLA_HEADS * GLA_DK), ("gla_k", GLA_HEADS * GLA_DK), ("gla_v", GLA_HEADS * GLA_DV),
             ("gf", GLA_GATE_RANK), ("gb", GLA_GATE_RANK), ("gla_r", GLA_HEADS * GLA_DV),
             ("sc", 3 * SC_WIDTH), ("merge", N_BRANCHES * d))
    off, start = {}, 0
    for name, width in parts:
        off[name] = (start, start + width)
        start += width
    return off


def _mod_kernel(c_ref, w_ref, b_ref, o_ref):
    c = c_ref[...]
    a = (c * jax.nn.sigmoid(c)).astype(BF16)
    o_ref[...] = _dot(a, w_ref[...].astype(BF16)) + b_ref[...]


def _modulation(cc, w_mod, b_mod):
    n_l, d, n6 = w_mod.shape
    tn = _pick(n6, (1024, 512, 256, 128))
    return pl.pallas_call(
        _mod_kernel,
        out_shape=jax.ShapeDtypeStruct((n_l, 8, n6), F32),
        grid=(n_l, n6 // tn),
        in_specs=[pl.BlockSpec((8, d), lambda l, j: (0, 0)),
                  pl.BlockSpec((None, d, tn), lambda l, j: (l, 0, j)),
                  pl.BlockSpec((None, 1, tn), lambda l, j: (l, 0, j))],
        out_specs=pl.BlockSpec((None, 8, tn), lambda l, j: (l, 0, j)),
        compiler_params=_cparams("parallel", "parallel"),
    )(cc, w_mod, b_mod.reshape(n_l, 1, n6))


class _Geom:
    def __init__(self, bsz, n_ctx, seq, d):
        self.b, self.c, self.s, self.d = bsz, n_ctx, seq, d
        self.n = n_ctx + seq
        self.m = bsz * self.n
        self.tr = _pick(int(np.gcd(n_ctx, seq)), (256, 128, 64))
        self.nt = self.n // self.tr
        self.nct = n_ctx // self.tr

    def mod_row(self, i):
        b, j = i // self.nt, i % self.nt
        return jnp.where(j < self.nct, self.b, b)

    def mod_spec(self, layer, part):
        return pl.BlockSpec((None, None, 1, self.d), lambda i, *_: (layer, self.mod_row(i), 0, part))


def _norm_mod_kernel(x_ref, g_ref, sh_ref, sc_ref, o_ref):
    x = x_ref[...]
    y = x * lax.rsqrt(jnp.mean(x * x, axis=-1, keepdims=True) + EPS) * g_ref[...]
    o_ref[...] = (y * (1.0 + sc_ref[...]) + sh_ref[...]).astype(o_ref.dtype)


def _norm_mod(geo, x, g, mod, layer, sh_part, sc_part):
    tr, d = geo.tr, geo.d
    return pl.pallas_call(
        _norm_mod_kernel,
        out_shape=jax.ShapeDtypeStruct((geo.m, d), BF16),
        grid=(geo.m // tr,),
        in_specs=[pl.BlockSpec((tr, d), lambda i: (i, 0)),
                  pl.BlockSpec((None, 1, d), lambda i: (layer, 0, 0)),
                  geo.mod_spec(layer, sh_part), geo.mod_spec(layer, sc_part)],
        out_specs=pl.BlockSpec((tr, d), lambda i: (i, 0)),
        compiler_params=_cparams("parallel"),
    )(x, g, mod, mod)


def _mm_kernel(a_ref, w_ref, o_ref):
    o_ref[...] = _dot(a_ref[...], w_ref[...]).astype(o_ref.dtype)


def _matmul(a, w, layer, out_dtype):
    m, k = a.shape
    n = w.shape[-1]
    tm = _pick(m, (512, 256, 128))
    tn = _pick(n, (1024, 512, 256, 128))
    return pl.pallas_call(
        _mm_kernel,
        out_shape=jax.ShapeDtypeStruct((m, n), out_dtype),
        grid=(n // tn, m // tm),
        in_specs=[pl.BlockSpec((tm, k), lambda j, i: (i, 0)),
                  pl.BlockSpec((None, k, tn), lambda j, i: (layer, 0, j))],
        out_specs=pl.BlockSpec((tm, tn), lambda j, i: (i, j)),
        compiler_params=_cparams("parallel", "parallel"),
    )(a, w)


def _mm_res_kernel(a_ref, w_ref, x_ref, gt_ref, o_ref):
    o_ref[...] = x_ref[...] + gt_ref[...] * _dot(a_ref[...], w_ref[...])


def _matmul_residual(geo, a, w, layer, x, mod, gate_part):
    m, k = a.shape
    d = geo.d
    tm = geo.tr
    tn = _pick(d, (1024, 512, 256, 128))
    nj = d // tn
    gate_spec = pl.BlockSpec((None, None, 1, tn), lambda j, i: (layer, geo.mod_row(i), 0, gate_part * nj + j))
    return pl.pallas_call(
        _mm_res_kernel,
        out_shape=jax.ShapeDtypeStruct((m, d), F32),
        grid=(nj, m // tm),
        in_specs=[pl.BlockSpec((tm, k), lambda j, i: (i, 0)),
                  pl.BlockSpec((None, k, tn), lambda j, i: (layer, 0, j)),
                  pl.BlockSpec((tm, tn), lambda j, i: (i, j)),
                  gate_spec],
        out_specs=pl.BlockSpec((tm, tn), lambda j, i: (i, j)),
        compiler_params=_cparams("parallel", "parallel"),
    )(a, w, x, mod)


def _rms(x, g):
    return x * lax.rsqrt(jnp.mean(x * x, axis=-1, keepdims=True) + EPS) * g


def _qkv_kernel(cq_ref, ckv_ref, kr_ref, gq_ref, gkv_ref, wqa_ref, wqb_ref, wka_ref, sela_ref, selb_ref,
                wv_ref, qca_ref, qsb_ref, kca_ref, ksb_ref, q_ref, k_ref, v_ref):
    cqn = _rms(cq_ref[...].astype(F32), gq_ref[...]).astype(BF16)
    ckvn = _rms(ckv_ref[...].astype(F32), gkv_ref[...]).astype(BF16)
    kr = kr_ref[...]
    qa = _dot(cqn, wqa_ref[...])
    qb = _dot(cqn, wqb_ref[...])
    ka = _dot(ckvn, wka_ref[...]) + _dot(kr, sela_ref[...])
    kb = _dot(kr, selb_ref[...])
    qca, qsb, kca, ksb = qca_ref[...], qsb_ref[...], kca_ref[...], ksb_ref[...]
    for h in range(MLA_HEADS):
        sl = slice(h * MLA_SLOT, (h + 1) * MLA_SLOT)
        q_ref[:, sl] = (qa[:, sl] * qca + qb[:, sl] * qsb).astype(BF16)
        k_ref[:, sl] = (ka[:, sl] * kca + kb[:, sl] * ksb).astype(BF16)
    v_ref[...] = _dot(ckvn, wv_ref[...]).astype(BF16)


def _mla_qkv(geo, u, off, layer, gq, gkv, wqa, wqb, wka, sela, selb, wv, tabs):
    tr, m = geo.tr, geo.m
    hq = MLA_HEADS * MLA_SLOT
    hv = MLA_HEADS * MLA_V
    row = lambda w, o: pl.BlockSpec((tr, w), lambda i: (i, o // w))
    lw = lambda a: pl.BlockSpec((None,) + a.shape[1:], lambda i: (layer,) + (0,) * (a.ndim - 1))
    cw = lambda a: pl.BlockSpec(a.shape, lambda i: (0,) * a.ndim)
    tab = pl.BlockSpec((tr, MLA_SLOT), lambda i: (i % geo.nt, 0))
    return pl.pallas_call(
        _qkv_kernel,
        out_shape=(jax.ShapeDtypeStruct((m, hq), BF16), jax.ShapeDtypeStruct((m, hq), BF16),
                   jax.ShapeDtypeStruct((m, hv), BF16)),
        grid=(m // tr,),
        in_specs=[row(MLA_Q_RANK, off["cq"]), row(MLA_KV_RANK, off["ckv"]), row(128, off["kr"]),
                  lw(gq), lw(gkv), lw(wqa), lw(wqb), lw(wka), cw(sela), cw(selb), lw(wv), tab, tab, tab, tab],
        out_specs=(pl.BlockSpec((tr, hq), lambda i: (i, 0)), pl.BlockSpec((tr, hq), lambda i: (i, 0)),
                   pl.BlockSpec((tr, hv), lambda i: (i, 0))),
        compiler_params=_cparams("parallel"),
    )(u, u, u, gq, gkv, wqa, wqb, wka, sela, selb, wv, *tabs)


def _attn_kernel(q_ref, k_ref, v_ref, o_ref, *, n_ctx, n_all, n_ctx_tiles):
    qi = pl.program_id(2)
    tq = q_ref.shape[0]
    lane = lax.broadcasted_iota(jnp.int32, (tq, 2 * MLA_V), 1)

    def run(nk):
        outs = []
        for hh in range(2):
            sl = slice(hh * MLA_SLOT, (hh + 1) * MLA_SLOT)
            s = _dot_nt(q_ref[:, sl], k_ref[0:nk, sl])
            p = jnp.exp(s - jnp.max(s, axis=-1, keepdims=True))
            inv = 1.0 / jnp.sum(p, axis=-1, keepdims=True)
            outs.append(_dot(p.astype(BF16), v_ref[0:nk, :]) * inv)
        o_ref[...] = jnp.where(lane < MLA_V, outs[0], outs[1]).astype(o_ref.dtype)

    @pl.when(qi < n_ctx_tiles)
    def _():
        run(n_ctx)

    @pl.when(qi >= n_ctx_tiles)
    def _():
        run(n_all)


def _mla_attention(geo, q, k, v):
    tq, n, bsz = geo.tr, geo.n, geo.b
    q3, k3, v3 = (t.reshape(bsz, n, t.shape[-1]) for t in (q, k, v))
    kern = functools.partial(_attn_kernel, n_ctx=geo.c, n_all=n, n_ctx_tiles=geo.nct)
    out = pl.pallas_call(
        kern,
        out_shape=jax.ShapeDtypeStruct((bsz, n, MLA_HEADS * MLA_V), BF16),
        grid=(bsz, MLA_HEADS // 2, geo.nt),
        in_specs=[pl.BlockSpec((None, tq, 2 * MLA_SLOT), lambda b, hp, i: (b, i, hp)),
                  pl.BlockSpec((None, n, 2 * MLA_SLOT), lambda b, hp, i: (b, 0, hp)),
                  pl.BlockSpec((None, n, 2 * MLA_V), lambda b, hp, i: (b, 0, hp))],
        out_specs=pl.BlockSpec((None, tq, 2 * MLA_V), lambda b, hp, i: (b, i, hp)),
        compiler_params=_cparams("parallel", "parallel", "parallel"),
    )(q3, k3, v3)
    return out.reshape(geo.m, MLA_HEADS * MLA_V)


def _halo_flags(geo, i):
    j = i % geo.nt
    left = jnp.logical_and(j != 0, j != geo.nct)
    right = jnp.logical_and(j != geo.nct - 1, j != geo.nt - 1)
    return left.astype(F32), right.astype(F32)


def _conv_taps(pad_ref, w_ref, n_taps, rows):
    base = CONV_HALO - n_taps // 2
    acc = w_ref[0:1, :] * pad_ref[base:base + rows, :]
    for kk in range(1, n_taps):
        acc = acc + w_ref[kk:kk + 1, :] * pad_ref[base + kk:base + kk + rows, :]
    return acc


def _conf_kernel(a_ref, b_ref, ap_ref, bp_ref, an_ref, bn_ref, w_ref, bias_ref, g_ref, beta_ref, o_ref,
                 pad_ref, *, geo):
    left, right = _halo_flags(geo, pl.program_id(0))
    rows = a_ref.shape[0]
    glu = lambda a, b: a[...].astype(F32) * jax.nn.sigmoid(b[...].astype(F32))
    pad_ref[0:CONV_HALO, :] = glu(ap_ref, bp_ref) * left
    pad_ref[CONV_HALO:CONV_HALO + rows, :] = glu(a_ref, b_ref)
    pad_ref[CONV_HALO + rows:2 * CONV_HALO + rows, :] = glu(an_ref, bn_ref) * right
    h = _conv_taps(pad_ref, w_ref, CF_KERNEL, rows) + bias_ref[...]
    hc = h - jnp.mean(h, axis=-1, keepdims=True)
    y = hc * lax.rsqrt(jnp.mean(hc * hc, axis=-1, keepdims=True) + EPS) * g_ref[...] + beta_ref[...]
    o_ref[...] = (y * jax.nn.sigmoid(y)).astype(o_ref.dtype)


def _sconv_kernel(gb_ref, gc_ref, h_ref, gcp_ref, hp_ref, gcn_ref, hn_ref, w_ref, o_ref, pad_ref, *, geo):
    left, right = _halo_flags(geo, pl.program_id(0))
    rows = h_ref.shape[0]
    prod = lambda a, b: a[...].astype(F32) * b[...].astype(F32)
    pad_ref[0:CONV_HALO, :] = prod(gcp_ref, hp_ref) * left
    pad_ref[CONV_HALO:CONV_HALO + rows, :] = prod(gc_ref, h_ref)
    pad_ref[CONV_HALO + rows:2 * CONV_HALO + rows, :] = prod(gcn_ref, hn_ref) * right
    y = _conv_taps(pad_ref, w_ref, SC_KERNEL, rows)
    o_ref[...] = (gb_ref[...].astype(F32) * y).astype(o_ref.dtype)


def _halo_specs(geo, width, offset):
    tr = geo.tr
    per = tr // CONV_HALO
    last = geo.m // CONV_HALO - 1
    cur = pl.BlockSpec((tr, width), lambda i: (i, offset // width))
    prev = pl.BlockSpec((CONV_HALO, width), lambda i: (jnp.maximum(i * per - 1, 0), offset // width))
    nxt = pl.BlockSpec((CONV_HALO, width), lambda i: (jnp.minimum((i + 1) * per, last), offset // width))
    return cur, prev, nxt


def _conformer(geo, u, off, layer, w_dw, b_dw, g_ln, b_ln):
    tr = geo.tr
    a_c, a_p, a_n = _halo_specs(geo, CF_WIDTH, off["cf_a"])
    b_c, b_p, b_n = _halo_specs(geo, CF_WIDTH, off["cf_b"])
    lw = lambda a: pl.BlockSpec((None,) + a.shape[1:], lambda i: (layer,) + (0,) * (a.ndim - 1))
    return pl.pallas_call(
        functools.partial(_conf_kernel, geo=geo),
        out_shape=jax.ShapeDtypeStruct((geo.m, CF_WIDTH), BF16),
        grid=(geo.m // tr,),
        in_specs=[a_c, b_c, a_p, b_p, a_n, b_n, lw(w_dw), lw(b_dw), lw(g_ln), lw(b_ln)],
        out_specs=pl.BlockSpec((tr, CF_WIDTH), lambda i: (i, 0)),
        scratch_shapes=[pltpu.VMEM((tr + 2 * CONV_HALO, CF_WIDTH), F32)],
        compiler_params=_cparams("parallel"),
    )(u, u, u, u, u, u, w_dw, b_dw, g_ln, b_ln)


def _shortconv(geo, u, off, layer, w_conv):
    tr = geo.tr
    gb_c, _, _ = _halo_specs(geo, SC_WIDTH, off["sc_b"])
    gc_c, gc_p, gc_n = _halo_specs(geo, SC_WIDTH, off["sc_c"])
    h_c, h_p, h_n = _halo_specs(geo, SC_WIDTH, off["sc_h"])
    lw = lambda a: pl.BlockSpec((None,) + a.shape[1:], lambda i: (layer,) + (0,) * (a.ndim - 1))
    return pl.pallas_call(
        functools.partial(_sconv_kernel, geo=geo),
        out_shape=jax.ShapeDtypeStruct((geo.m, SC_WIDTH), BF16),
        grid=(geo.m // tr,),
        in_specs=[gb_c, gc_c, h_c, gc_p, h_p, gc_n, h_n, lw(w_conv)],
        out_specs=pl.BlockSpec((tr, SC_WIDTH), lambda i: (i, 0)),
        scratch_shapes=[pltpu.VMEM((tr + 2 * CONV_HALO, SC_WIDTH), F32)],
        compiler_params=_cparams("parallel"),
    )(u, u, u, u, u, u, u, w_conv)


def _gla_tables():
    c = GLA_CHUNK
    n_rows = (GLA_LEVELS + 2) * c + 8
    mat = np.zeros((2, n_rows, c), np.float32)
    msk = np.zeros((2, GLA_LEVELS + 1, c, c), np.float32)
    for lvl in range(GLA_LEVELS):
        hs = (c // 2) >> lvl
        for t in range(c):
            mid = (t // (2 * hs)) * 2 * hs + hs
            if t >= mid:
                mat[0, lvl * c + t, mid + 1:t + 1] = 1.0
                msk[0, lvl, t, mid - hs:mid] = 1.0
            else:
                mat[0, lvl * c + t, t + 1:mid + 1] = 1.0
    msk[0, GLA_LEVELS] = np.eye(c, dtype=np.float32)
    for t in range(c):
        mat[0, GLA_LEVELS * c + t, :t + 1] = 1.0
        mat[0, (GLA_LEVELS + 1) * c + t, t + 1:] = 1.0
    mat[0, (GLA_LEVELS + 2) * c:, :] = 1.0
    n_blk = GLA_LEVELS + 2
    mat[1, :n_blk * c] = mat[0, :n_blk * c].reshape(n_blk, c, c)[:, ::-1, ::-1].reshape(n_blk * c, c)
    mat[1, n_blk * c:] = 1.0
    msk[1] = msk[0][:, ::-1, ::-1]
    mat3 = np.concatenate([mat, mat, mat], axis=2)
    msk4 = np.tile(msk, (1, 1, GLA_HEADS, 1))
    return jnp.asarray(mat3, BF16), jnp.asarray(msk4, F32)


def _gla_kernel(q_ref, k_ref, v_ref, g_ref, w2_ref, b2_ref, mat_ref, msk_ref, o_ref, st_ref):
    c = GLA_CHUNK

    @pl.when(pl.program_id(2) == 0)
    def _():
        st_ref[...] = jnp.zeros_like(st_ref)

    z = _dot(g_ref[...], w2_ref[...]) + b2_ref[...]
    la = (jnp.minimum(z, 0.0) - jnp.log1p(jnp.exp(-jnp.abs(z)))) * (1.0 / GLA_GATE_TAU)
    hi = la.astype(BF16)
    r1 = la - hi.astype(F32)
    lo = r1.astype(BF16)
    lo2 = (r1 - lo.astype(F32)).astype(BF16)
    ea = jnp.exp(_dot(mat_ref[...], jnp.concatenate([hi, lo, lo2], axis=0)))

    head = lax.broadcasted_iota(jnp.int32, (1, GLA_HEADS * GLA_DK), 1) // GLA_DK
    zero = jnp.zeros((), BF16)

    def stack(x):
        xb = x.astype(BF16)
        return jnp.concatenate([jnp.where(head == h, xb, zero) for h in range(GLA_HEADS)], axis=0)

    q = q_ref[...].astype(F32) * (GLA_DK ** -0.5)
    k = k_ref[...].astype(F32)
    vb = v_ref[...]

    attn = _dot_nt(stack(q), k.astype(BF16)) * msk_ref[GLA_LEVELS]
    for lvl in range(GLA_LEVELS):
        e = ea[lvl * c:(lvl + 1) * c]
        attn = attn + _dot_nt(stack(q * e), (k * e).astype(BF16)) * msk_ref[lvl]
    o_intra = _dot(attn.astype(BF16), vb)

    st = st_ref[...]
    o_inter = _dot_nt(stack(q * ea[GLA_LEVELS * c:(GLA_LEVELS + 1) * c]), st.astype(BF16))
    for h in range(GLA_HEADS):
        rows = slice(h * c, (h + 1) * c)
        cols = slice(h * GLA_DV, (h + 1) * GLA_DV)
        o_ref[:, cols] = o_intra[rows, cols] + o_inter[rows, :]

    k_dec = stack(k * ea[(GLA_LEVELS + 1) * c:(GLA_LEVELS + 2) * c])
    v_rows = jnp.concatenate([vb[:, h * GLA_DV:(h + 1) * GLA_DV] for h in range(GLA_HEADS)], axis=0)
    eye = (lax.broadcasted_iota(jnp.int32, (GLA_DV, GLA_DV), 0)
           == lax.broadcasted_iota(jnp.int32, (GLA_DV, GLA_DV), 1)).astype(BF16)
    v_t = _dot_nt(eye, v_rows).astype(BF16)
    tot = ea[(GLA_LEVELS + 2) * c:(GLA_LEVELS + 2) * c + 1]
    st_ref[...] = st * tot + _dot(v_t, k_dec)


def _gla_scan(geo, u, off, layer, w2, b2, mat3, msk4):
    c = GLA_CHUNK
    ncc = geo.c // c
    nc = geo.n // c
    dk, dv = GLA_HEADS * GLA_DK, GLA_HEADS * GLA_DV

    def chunk(b, d, i):
        fwd = i
        bwd = jnp.where(i < ncc, ncc - 1 - i, nc - 1 - (i - ncc))
        return b * nc + jnp.where(d == 0, fwd, bwd)

    row = lambda w, o: pl.BlockSpec((c, w), lambda b, d, i: (chunk(b, d, i), o // w))
    return pl.pallas_call(
        _gla_kernel,
        out_shape=jax.ShapeDtypeStruct((2, geo.m, dv), F32),
        grid=(geo.b, 2, nc),
        in_specs=[row(dk, off["gla_q"]), row(dk, off["gla_k"]), row(dv, off["gla_v"]), row(128, off["gate"]),
                  pl.BlockSpec((None, None, 128, dk), lambda b, d, i: (layer, d, 0, 0)),
                  pl.BlockSpec((None, None, 1, dk), lambda b, d, i: (layer, d, 0, 0)),
                  pl.BlockSpec((None,) + mat3.shape[1:], lambda b, d, i: (d, 0, 0)),
                  pl.BlockSpec((None,) + msk4.shape[1:], lambda b, d, i: (d, 0, 0, 0))],
        out_specs=pl.BlockSpec((None, c, dv), lambda b, d, i: (d, chunk(b, d, i), 0)),
        scratch_shapes=[pltpu.VMEM((GLA_DV, dk), F32)],
        compiler_params=_cparams("parallel", "parallel", "arbitrary"),
    )(u, u, u, u, w2, b2, mat3, msk4)


def _merge_kernel(attn_ref, conf_ref, gla_ref, r_ref, gn_ref, sc_ref, g0_ref, g1_ref, g2_ref, g3_ref, w_ref,
                  y_ref):
    o = gla_ref[0] + gla_ref[1]
    r = r_ref[...].astype(F32)
    gn = gn_ref[...]
    normed = [_rms(o[:, h * GLA_DV:(h + 1) * GLA_DV], gn) for h in range(GLA_HEADS)]
    gla = (jnp.concatenate(normed, axis=1) * (r * jax.nn.sigmoid(r))).astype(BF16)
    acts = (attn_ref[...], conf_ref[...], gla, sc_ref[...])
    gates = (g0_ref, g1_ref, g2_ref, g3_ref)
    y = None
    for i in range(N_BRANCHES):
        term = _sigmoid(gates[i][...].astype(F32)) * _dot(acts[i], w_ref[i])
        y = term if y is None else y + term
    y_ref[...] = y.astype(y_ref.dtype)


def _merge(geo, attn, conf, gla2, u, off, g_norm, sconv, w_br, layer):
    tm, d, m = geo.tr, geo.d, geo.m
    tn = _pick(d, (1024, 512, 256, 128))
    nj = d // tn
    w512 = GLA_HEADS * GLA_DV
    act = pl.BlockSpec((tm, w512), lambda j, i: (i, 0))
    gate = lambda b: pl.BlockSpec((tm, tn), lambda j, i: (i, (off["merge"] + b * d) // tn + j))
    return pl.pallas_call(
        _merge_kernel,
        out_shape=jax.ShapeDtypeStruct((m, d), BF16),
        grid=(nj, m // tm),
        in_specs=[act, act, pl.BlockSpec((2, tm, w512), lambda j, i: (0, i, 0)),
                  pl.BlockSpec((tm, w512), lambda j, i: (i, off["gla_r"] // w512)),
                  pl.BlockSpec((None, 1, GLA_DV), lambda j, i: (layer, 0, 0)),
                  act, gate(0), gate(1), gate(2), gate(3),
                  pl.BlockSpec((None, N_BRANCHES, w512, tn), lambda j, i: (layer, 0, 0, j))],
        out_specs=pl.BlockSpec((tm, tn), lambda j, i: (i, j)),
        compiler_params=_cparams("parallel", "parallel"),
    )(attn, conf, gla2, u, g_norm, sconv, u, u, u, u, w_br)


def _second_largest(a, b, c, d):
    return jnp.maximum(jnp.maximum(jnp.minimum(a, b), jnp.minimum(c, d)),
                       jnp.minimum(jnp.maximum(a, b), jnp.maximum(c, d)))


def _ffn_pre_kernel(x_ref, g_ref, sh_ref, sc_ref, wr_ref, br_ref, h_ref, comb_ref):
    x = x_ref[...]
    y = _rms(x, g_ref[...]) * (1.0 + sc_ref[...]) + sh_ref[...]
    hi = y.astype(BF16)
    lo = (y - hi.astype(F32)).astype(BF16)
    h_ref[...] = hi
    ne = N_EXPERTS
    two = _dot_nt(wr_ref[...], hi)
    logits = two[0:ne] + two[ne:2 * ne] + _dot_nt(wr_ref[0:ne, :], lo)
    aff = 1.0 / (1.0 + jnp.exp(-logits))
    sel = aff + br_ref[...]
    per = ne // N_GROUPS
    gs = []
    for g in range(N_GROUPS):
        a, b, c, d = (sel[g * per + r:g * per + r + 1] for r in range(per))
        top1 = jnp.maximum(jnp.maximum(a, b), jnp.maximum(c, d))
        gs.append(top1 + _second_largest(a, b, c, d))
    best = gs[0]
    for g in range(1, N_GROUPS):
        best = jnp.maximum(best, gs[g])
    grp = jnp.full(best.shape, N_GROUPS, jnp.int32)
    for g in reversed(range(N_GROUPS)):
        grp = jnp.where(gs[g] == best, g, grp)
    erow = lax.broadcasted_iota(jnp.int32, sel.shape, 0)
    masked = jnp.where(erow // per == grp, sel, -jnp.inf)
    m1 = jnp.max(masked, axis=0, keepdims=True)
    i1 = jnp.min(jnp.where(masked == m1, erow, ne), axis=0, keepdims=True)
    one1 = erow == i1
    masked2 = jnp.where(one1, -jnp.inf, masked)
    m2 = jnp.max(masked2, axis=0, keepdims=True)
    i2 = jnp.min(jnp.where(masked2 == m2, erow, ne), axis=0, keepdims=True)
    one2 = erow == i2
    picked = jnp.where(jnp.logical_or(one1, one2), aff, 0.0)
    comb_ref[...] = picked / jnp.sum(picked, axis=0, keepdims=True)


def _ffn_pre(geo, x, g, mod, layer, wr_stack, b_router):
    tr, d, m = geo.tr, geo.d, geo.m
    return pl.pallas_call(
        _ffn_pre_kernel,
        out_shape=(jax.ShapeDtypeStruct((m, d), BF16), jax.ShapeDtypeStruct((N_EXPERTS, m), F32)),
        grid=(m // tr,),
        in_specs=[pl.BlockSpec((tr, d), lambda i: (i, 0)),
                  pl.BlockSpec((None, 1, d), lambda i: (layer, 0, 0)),
                  geo.mod_spec(layer, 3), geo.mod_spec(layer, 4),
                  pl.BlockSpec(wr_stack.shape, lambda i: (0, 0)),
                  pl.BlockSpec(b_router.shape, lambda i: (0, 0))],
        out_specs=(pl.BlockSpec((tr, d), lambda i: (i, 0)), pl.BlockSpec((N_EXPERTS, tr), lambda i: (0, i))),
        compiler_params=_cparams("parallel"),
    )(x, g, mod, mod, wr_stack, b_router)


def _moe_dense_kernel(h_ref, comb_ref, wg_ref, wu_ref, wd_ref, o_ref, acc_ref):
    e = pl.program_id(1)

    @pl.when(e == 0)
    def _():
        acc_ref[...] = jnp.zeros_like(acc_ref)

    h = h_ref[...]
    g = _dot(h, wg_ref[...])
    u = _dot(h, wu_ref[...])
    act = (g * jax.nn.sigmoid(g) * u * comb_ref[...]).astype(BF16)
    acc_ref[...] += _dot(act, wd_ref[...])

    @pl.when(e == pl.num_programs(1) - 1)
    def _():
        o_ref[...] = acc_ref[...].astype(o_ref.dtype)


def _moe_dense(hf, comb_cols, wg, wu, wd, layer):
    m, d = hf.shape
    tm = _pick(m, (512, 256, 128))
    ff = wg.shape[-1]
    return pl.pallas_call(
        _moe_dense_kernel,
        out_shape=jax.ShapeDtypeStruct((m, d), BF16),
        grid=(m // tm, N_EXPERTS),
        in_specs=[pl.BlockSpec((tm, d), lambda i, e: (i, 0)),
                  pl.BlockSpec((None, tm, 1), lambda i, e: (e, i, 0)),
                  pl.BlockSpec((None, None, d, ff), lambda i, e: (layer, e, 0, 0)),
                  pl.BlockSpec((None, None, d, ff), lambda i, e: (layer, e, 0, 0)),
                  pl.BlockSpec((None, None, ff, d), lambda i, e: (layer, e, 0, 0))],
        out_specs=pl.BlockSpec((tm, d), lambda i, e: (i, 0)),
        scratch_shapes=[pltpu.VMEM((tm, d), F32)],
        compiler_params=_cparams("parallel", "arbitrary"),
    )(hf, comb_cols, wg, wu, wd)


def _residual_kernel(x_ref, f_ref, gt_ref, o_ref):
    o_ref[...] = x_ref[...] + gt_ref[...] * f_ref[...].astype(F32)


def _residual(geo, x, f, mod, layer, gate_part):
    tr, d = geo.tr, geo.d
    return pl.pallas_call(
        _residual_kernel,
        out_shape=jax.ShapeDtypeStruct((geo.m, d), F32),
        grid=(geo.m // tr,),
        in_specs=[pl.BlockSpec((tr, d), lambda i: (i, 0)), pl.BlockSpec((tr, d), lambda i: (i, 0)),
                  geo.mod_spec(layer, gate_part)],
        out_specs=pl.BlockSpec((tr, d), lambda i: (i, 0)),
        compiler_params=_cparams("parallel"),
    )(x, f, mod)


def _final_kernel(x_ref, g_ref, o_ref):
    o_ref[...] = _rms(x_ref[...], g_ref[...])


def _final_norm(geo, x, g):
    tr, d = geo.tr, geo.d
    x3 = x.reshape(geo.b, geo.n, d)
    return pl.pallas_call(
        _final_kernel,
        out_shape=jax.ShapeDtypeStruct((geo.b, geo.s, d), F32),
        grid=(geo.b, geo.s // tr),
        in_specs=[pl.BlockSpec((None, tr, d), lambda b, i: (b, i + geo.nct, 0)),
                  pl.BlockSpec((1, d), lambda b, i: (0, 0))],
        out_specs=pl.BlockSpec((None, tr, d), lambda b, i: (b, i, 0)),
        compiler_params=_cparams("parallel", "parallel"),
    )(x3, g.reshape(1, d))


def _pad_cols(w, width):
    return jnp.pad(w, [(0, 0)] * (w.ndim - 1) + [(0, width - w.shape[-1])])


def _prep_w_in(w_in, d):
    src = _src_layout(d)
    off, total = _proj_layout(d)
    cut = lambda name: w_in[..., src[name][0]:src[name][1]]
    cf, sc = cut("cf"), cut("sc")
    gate = jnp.concatenate([cut("gf"), cut("gb")], axis=-1)
    pieces = {"merge": cut("merge"), "sc_b": sc[..., :SC_WIDTH], "sc_c": sc[..., SC_WIDTH:2 * SC_WIDTH],
              "sc_h": sc[..., 2 * SC_WIDTH:], "cf_a": cf[..., :CF_WIDTH], "cf_b": cf[..., CF_WIDTH:],
              "cq": cut("cq"), "gla_v": cut("gla_v"), "gla_r": cut("gla_r"), "ckv": cut("ckv"),
              "gla_q": cut("gla_q"), "gla_k": cut("gla_k"), "kr": _pad_cols(cut("kr"), 128),
              "gate": _pad_cols(gate, 128)}
    order = sorted(off, key=off.get)
    out = jnp.concatenate([pieces[name] for name in order], axis=-1).astype(BF16)
    assert out.shape[-1] == total
    return out, off


def _rope_swap_perm():
    q = MLA_ROPE // 4
    return np.concatenate([np.arange(q, 2 * q), np.arange(0, q), np.arange(3 * q, 4 * q), np.arange(2 * q, 3 * q)])


def _prep_mla(w_uq, w_ukv):
    n_l = w_uq.shape[0]
    perm = _rope_swap_perm()
    wq = w_uq.reshape(n_l, MLA_Q_RANK, MLA_HEADS, MLA_NOPE + MLA_ROPE)
    qn, qr = wq[..., :MLA_NOPE], wq[..., MLA_NOPE:]
    zeros = lambda *s: jnp.zeros((n_l,) + s, w_uq.dtype)
    wqa = jnp.concatenate([qn, qr, qr], axis=-1)
    wqb = jnp.concatenate([zeros(MLA_Q_RANK, MLA_HEADS, MLA_NOPE), qr[..., perm],
                           zeros(MLA_Q_RANK, MLA_HEADS, MLA_ROPE)], axis=-1)
    wkv = w_ukv.reshape(n_l, MLA_KV_RANK, MLA_HEADS, MLA_NOPE + MLA_V)
    wka = jnp.concatenate([wkv[..., :MLA_NOPE], zeros(MLA_KV_RANK, MLA_HEADS, MLA_SLOT - MLA_NOPE)], axis=-1)
    wv = wkv[..., MLA_NOPE:]
    flat = lambda w: w.reshape(n_l, w.shape[1], -1).astype(BF16)
    sela = np.zeros((128, MLA_HEADS, MLA_SLOT), np.float32)
    selb = np.zeros((128, MLA_HEADS, MLA_SLOT), np.float32)
    for r in range(MLA_ROPE):
        sela[r, :, MLA_NOPE + r] = 1.0
        sela[r, :, MLA_NOPE + MLA_ROPE + r] = 1.0
        selb[perm[r], :, MLA_NOPE + r] = 1.0
    sel = lambda s: jnp.asarray(s.reshape(128, -1), BF16)
    return flat(wqa), flat(wqb), flat(wka), sel(sela), sel(selb), flat(wv)


def _rope_tables(geo):
    rows = geo.s // GRID_W
    pos = jnp.arange(rows * GRID_W)
    row = (pos // GRID_W).astype(F32)
    col = (pos % GRID_W).astype(F32)
    n_freq = MLA_ROPE // 4
    freqs = ROPE_THETA ** (-jnp.arange(n_freq, dtype=F32) / n_freq)
    cr, sr = jnp.cos(row[:, None] * freqs), jnp.sin(row[:, None] * freqs)
    cc, sn = jnp.cos(col[:, None] * freqs), jnp.sin(col[:, None] * freqs)
    cos32 = jnp.concatenate([cr, cr, cc, cc], axis=-1)
    sin32 = jnp.concatenate([-sr, sr, -sn, sn], axis=-1)
    s, c = geo.s, geo.c
    one = lambda n, w: jnp.ones((n, w), F32)
    zero = lambda n, w: jnp.zeros((n, w), F32)
    scale = (MLA_NOPE + MLA_ROPE) ** -0.5
    qca = jnp.concatenate([jnp.concatenate([one(c, 64), zero(c, 32), one(c, 32)], 1),
                           jnp.concatenate([one(s, 64), cos32, one(s, 32)], 1)], 0) * scale
    qsb = jnp.concatenate([zero(c, 128), jnp.concatenate([zero(s, 64), sin32, zero(s, 32)], 1)], 0) * scale
    kca = jnp.concatenate([jnp.concatenate([one(c, 64), zero(c, 32), one(c, 32)], 1),
                           jnp.concatenate([one(s, 64), cos32, zero(s, 32)], 1)], 0)
    ksb = jnp.concatenate([zero(c, 128), jnp.concatenate([zero(s, 64), sin32, zero(s, 32)], 1)], 0)
    return qca, qsb, kca, ksb


def kernel(x, c, ctx, c_ctx, w_mod, b_mod, g_mix, g_ffn, w_in, g_q, w_uq, g_kv, w_ukv, w_mla_o, w_cf_dw, b_cf_dw, g_cf_ln, b_cf_ln, w_cf_o, w_gla_gf, b_gla_gf, w_gla_gb, b_gla_gb, g_gla_norm, w_gla_o, w_sc_conv, w_sc_o, w_mix_o, w_router, b_router, w_e_gate, w_e_up, w_e_down, g_final):
    bsz, seq, d = x.shape
    n_ctx = ctx.shape[1]
    depth = w_in.shape[0]
    geo = _Geom(bsz, n_ctx, seq, d)
    assert bsz + 1 <= 8 and seq % GRID_W == 0 and n_ctx % GLA_CHUNK == 0 and seq % GLA_CHUNK == 0

    w_in_p, off = _prep_w_in(w_in, d)
    wqa, wqb, wka, sela, selb, wv = _prep_mla(w_uq, w_ukv)
    tabs = _rope_tables(geo)
    w_br = jnp.stack([w_mla_o, w_cf_o, w_gla_o, w_sc_o], axis=1).astype(BF16)
    w_mix = w_mix_o.astype(BF16)
    wg, wu, wd = w_e_gate.astype(BF16), w_e_up.astype(BF16), w_e_down.astype(BF16)
    gate_w = jnp.zeros((depth, 2, 128, GLA_HEADS * GLA_DK), F32)
    gate_w = gate_w.at[:, 0, :GLA_GATE_RANK].set(w_gla_gf).at[:, 1, GLA_GATE_RANK:2 * GLA_GATE_RANK].set(w_gla_gb)
    gate_w = gate_w.astype(BF16)
    gate_b = jnp.stack([b_gla_gf, b_gla_gb], axis=1)[:, :, None, :]
    mat3, msk4 = _gla_tables()
    wr_t = w_router.T
    wr_hi = wr_t.astype(BF16)
    wr_stack = jnp.concatenate([wr_hi, (wr_t - wr_hi.astype(F32)).astype(BF16)], axis=0)
    vec = lambda a: a.reshape(a.shape[0], 1, a.shape[-1])

    cc = jnp.zeros((8, d), F32).at[:bsz].set(c).at[bsz].set(c_ctx)
    mod = _modulation(cc, w_mod, b_mod).reshape(depth, 8, 1, 6 * d)

    xa = jnp.concatenate([ctx, x], axis=1).reshape(geo.m, d)
    for l in range(depth):
        h = _norm_mod(geo, xa, vec(g_mix), mod, l, 0, 1)
        u = _matmul(h, w_in_p, l, BF16)
        q, k, v = _mla_qkv(geo, u, off, l, vec(g_q), vec(g_kv), wqa, wqb, wka, sela, selb, wv, tabs)
        attn = _mla_attention(geo, q, k, v)
        conf = _conformer(geo, u, off, l, w_cf_dw, vec(b_cf_dw), vec(g_cf_ln), vec(b_cf_ln))
        sconv = _shortconv(geo, u, off, l, w_sc_conv)
        gla2 = _gla_scan(geo, u, off, l, gate_w, gate_b, mat3, msk4)
        y = _merge(geo, attn, conf, gla2, u, off, vec(g_gla_norm), sconv, w_br, l)
        xa = _matmul_residual(geo, y, w_mix, l, xa, mod, 2)
        hf, comb_t = _ffn_pre(geo, xa, vec(g_ffn), mod, l, wr_stack, b_router.reshape(N_EXPERTS, 1))
        f = _moe_dense(hf, comb_t[:, :, None], wg, wu, wd, l)
        xa = _residual(geo, xa, f, mod, l, 5)
    return _final_norm(geo, xa, g_final)
```

```python
import functools

import numpy as np
import jax
import jax.numpy as jnp
from jax import lax
from jax.experimental import pallas as pl
from jax.experimental.pallas import tpu as pltpu

F32 = jnp.float32
BF16 = jnp.bfloat16
EPS = 1e-6

GRID_W = 64
ROPE_THETA = 10000.0
MLA_HEADS = 8
MLA_NOPE = 64
MLA_ROPE = 32
MLA_V = 64
MLA_Q_RANK = 512
MLA_KV_RANK = 256
MLA_SLOT = 128
CF_WIDTH = 512
CF_KERNEL = 31
GLA_HEADS = 4
GLA_DK = 64
GLA_DV = 128
GLA_GATE_RANK = 16
GLA_GATE_TAU = 16.0
GLA_CHUNK = 64
GLA_LEVELS = 6
SC_WIDTH = 512
SC_KERNEL = 3
N_BRANCHES = 4
N_EXPERTS = 16
N_GROUPS = 4
EXPERT_FF = 512
CONV_HALO = 16
VMEM_LIMIT = 56 * 1024 * 1024


def _cparams(*sem):
    return pltpu.CompilerParams(dimension_semantics=sem, vmem_limit_bytes=VMEM_LIMIT)


def _pick(n, cands):
    for c in cands:
        if n % c == 0:
            return c
    raise ValueError(f"no tile for {n}")


def _dot(a, b):
    return jnp.dot(a, b, preferred_element_type=F32)


def _dot_nt(a, b):
    return lax.dot_general(a, b, (((1,), (1,)), ((), ())), preferred_element_type=F32)


def _sigmoid(x):
    return 0.5 * jnp.tanh(0.5 * x) + 0.5


def _proj_layout(d):
    parts = [("merge", N_BRANCHES * d), ("sc_b", SC_WIDTH), ("sc_c", SC_WIDTH), ("sc_h", SC_WIDTH),
             ("cf_a", CF_WIDTH), ("cf_b", CF_WIDTH), ("cq", MLA_Q_RANK), ("gla_v", GLA_HEADS * GLA_DV),
             ("gla_r", GLA_HEADS * GLA_DV), ("ckv", MLA_KV_RANK), ("gla_q", GLA_HEADS * GLA_DK),
             ("gla_k", GLA_HEADS * GLA_DK), ("kr", 128), ("gate", 128)]
    off, start = {}, 0
    for name, width in parts:
        assert start % width == 0 or name == "merge"
        off[name] = start
        start += width
    return off, start


def _src_layout(d):
    parts = (("cq", MLA_Q_RANK), ("ckv", MLA_KV_RANK), ("kr", MLA_ROPE), ("cf", 2 * CF_WIDTH),
             ("gla_q", GLA_HEADS * GLA_DK), ("gla_k", GLA_HEADS * GLA_DK), ("gla_v", GLA_HEADS * GLA_DV),
             ("gf", GLA_GATE_RANK), ("gb", GLA_GATE_RANK), ("gla_r", GLA_HEADS * GLA_DV),
             ("sc", 3 * SC_WIDTH), ("merge", N_BRANCHES * d))
    off, start = {}, 0
    for name, width in parts:
        off[name] = (start, start + width)
        start += width
    return off


def _mod_kernel(c_ref, w_ref, b_ref, o_ref):
    c = c_ref[...]
    a = (c * jax.nn.sigmoid(c)).astype(BF16)
    o_ref[...] = _dot(a, w_ref[...].astype(BF16)) + b_ref[...]


def _modulation(cc, w_mod, b_mod):
    n_l, d, n6 = w_mod.shape
    tn = _pick(n6, (1024, 512, 256, 128))
    return pl.pallas_call(
        _mod_kernel,
        out_shape=jax.ShapeDtypeStruct((n_l, 8, n6), F32),
        grid=(n_l, n6 // tn),
        in_specs=[pl.BlockSpec((8, d), lambda l, j: (0, 0)),
                  pl.BlockSpec((None, d, tn), lambda l, j: (l, 0, j)),
                  pl.BlockSpec((None, 1, tn), lambda l, j: (l, 0, j))],
        out_specs=pl.BlockSpec((None, 8, tn), lambda l, j: (l, 0, j)),
        name="adaln_mod",
        compiler_params=_cparams("parallel", "parallel"),
    )(cc, w_mod, b_mod.reshape(n_l, 1, n6))


class _Geom:
    def __init__(self, bsz, n_ctx, seq, d):
        self.b, self.c, self.s, self.d = bsz, n_ctx, seq, d
        self.n = n_ctx + seq
        self.m = bsz * self.n
        self.tr = _pick(int(np.gcd(n_ctx, seq)), (256, 128, 64))
        self.nt = self.n // self.tr
        self.nct = n_ctx // self.tr

    def mod_row(self, i):
        b, j = i // self.nt, i % self.nt
        return jnp.where(j < self.nct, self.b, b)

    def mod_spec(self, layer, part):
        return pl.BlockSpec((None, None, 1, self.d), lambda i, *_: (layer, self.mod_row(i), 0, part))


def _norm_mod_kernel(x_ref, g_ref, sh_ref, sc_ref, o_ref):
    x = x_ref[...]
    y = x * lax.rsqrt(jnp.mean(x * x, axis=-1, keepdims=True) + EPS) * g_ref[...]
    o_ref[...] = (y * (1.0 + sc_ref[...]) + sh_ref[...]).astype(o_ref.dtype)


def _norm_mod(geo, x, g, mod, layer, sh_part, sc_part):
    tr, d = geo.tr, geo.d
    return pl.pallas_call(
        _norm_mod_kernel,
        out_shape=jax.ShapeDtypeStruct((geo.m, d), BF16),
        grid=(geo.m // tr,),
        in_specs=[pl.BlockSpec((tr, d), lambda i: (i, 0)),
                  pl.BlockSpec((None, 1, d), lambda i: (layer, 0, 0)),
                  geo.mod_spec(layer, sh_part), geo.mod_spec(layer, sc_part)],
        out_specs=pl.BlockSpec((tr, d), lambda i: (i, 0)),
        name="norm_mod",
        compiler_params=_cparams("parallel"),
    )(x, g, mod, mod)


def _mm_kernel(a_ref, w_ref, o_ref):
    o_ref[...] = _dot(a_ref[...], w_ref[...]).astype(o_ref.dtype)


def _matmul(a, w, layer, out_dtype):
    m, k = a.shape
    n = w.shape[-1]
    tm = _pick(m, (512, 256, 128))
    tn = _pick(n, (1024, 512, 256, 128))
    return pl.pallas_call(
        _mm_kernel,
        out_shape=jax.ShapeDtypeStruct((m, n), out_dtype),
        grid=(n // tn, m // tm),
        in_specs=[pl.BlockSpec((tm, k), lambda j, i: (i, 0)),
                  pl.BlockSpec((None, k, tn), lambda j, i: (layer, 0, j))],
        out_specs=pl.BlockSpec((tm, tn), lambda j, i: (i, j)),
        name="proj_in",
        compiler_params=_cparams("parallel", "parallel"),
    )(a, w)


def _mm_res_kernel(a_ref, w_ref, x_ref, gt_ref, o_ref):
    o_ref[...] = x_ref[...] + gt_ref[...] * _dot(a_ref[...], w_ref[...])


def _matmul_residual(geo, a, w, layer, x, mod, gate_part):
    m, k = a.shape
    d = geo.d
    tm = geo.tr
    tn = _pick(d, (1024, 512, 256, 128))
    nj = d // tn
    gate_spec = pl.BlockSpec((None, None, 1, tn), lambda j, i: (layer, geo.mod_row(i), 0, gate_part * nj + j))
    return pl.pallas_call(
        _mm_res_kernel,
        out_shape=jax.ShapeDtypeStruct((m, d), F32),
        grid=(nj, m // tm),
        in_specs=[pl.BlockSpec((tm, k), lambda j, i: (i, 0)),
                  pl.BlockSpec((None, k, tn), lambda j, i: (layer, 0, j)),
                  pl.BlockSpec((tm, tn), lambda j, i: (i, j)),
                  gate_spec],
        out_specs=pl.BlockSpec((tm, tn), lambda j, i: (i, j)),
        name="mix_out",
        compiler_params=_cparams("parallel", "parallel"),
    )(a, w, x, mod)


def _rms(x, g):
    return x * lax.rsqrt(jnp.mean(x * x, axis=-1, keepdims=True) + EPS) * g


def _qkv_kernel(cq_ref, ckv_ref, kr_ref, gq_ref, gkv_ref, wqa_ref, wqb_ref, wka_ref, sela_ref, selb_ref,
                wv_ref, qca_ref, qsb_ref, kca_ref, ksb_ref, q_ref, k_ref, v_ref):
    cqn = _rms(cq_ref[...].astype(F32), gq_ref[...]).astype(BF16)
    ckvn = _rms(ckv_ref[...].astype(F32), gkv_ref[...]).astype(BF16)
    kr = kr_ref[...]
    qa = _dot(cqn, wqa_ref[...])
    qb = _dot(cqn, wqb_ref[...])
    ka = _dot(ckvn, wka_ref[...]) + _dot(kr, sela_ref[...])
    kb = _dot(kr, selb_ref[...])
    qca, qsb, kca, ksb = qca_ref[...], qsb_ref[...], kca_ref[...], ksb_ref[...]
    for h in range(MLA_HEADS):
        sl = slice(h * MLA_SLOT, (h + 1) * MLA_SLOT)
        q_ref[:, sl] = (qa[:, sl] * qca + qb[:, sl] * qsb).astype(BF16)
        k_ref[:, sl] = (ka[:, sl] * kca + kb[:, sl] * ksb).astype(BF16)
    v_ref[...] = _dot(ckvn, wv_ref[...]).astype(BF16)


def _mla_qkv(geo, u, off, layer, gq, gkv, wqa, wqb, wka, sela, selb, wv, tabs):
    tr, m = geo.tr, geo.m
    hq = MLA_HEADS * MLA_SLOT
    hv = MLA_HEADS * MLA_V
    row = lambda w, o: pl.BlockSpec((tr, w), lambda i: (i, o // w))
    lw = lambda a: pl.BlockSpec((None,) + a.shape[1:], lambda i: (layer,) + (0,) * (a.ndim - 1))
    cw = lambda a: pl.BlockSpec(a.shape, lambda i: (0,) * a.ndim)
    tab = pl.BlockSpec((tr, MLA_SLOT), lambda i: (i % geo.nt, 0))
    return pl.pallas_call(
        _qkv_kernel,
        out_shape=(jax.ShapeDtypeStruct((m, hq), BF16), jax.ShapeDtypeStruct((m, hq), BF16),
                   jax.ShapeDtypeStruct((m, hv), BF16)),
        grid=(m // tr,),
        in_specs=[row(MLA_Q_RANK, off["cq"]), row(MLA_KV_RANK, off["ckv"]), row(128, off["kr"]),
                  lw(gq), lw(gkv), lw(wqa), lw(wqb), lw(wka), cw(sela), cw(selb), lw(wv), tab, tab, tab, tab],
        out_specs=(pl.BlockSpec((tr, hq), lambda i: (i, 0)), pl.BlockSpec((tr, hq), lambda i: (i, 0)),
                   pl.BlockSpec((tr, hv), lambda i: (i, 0))),
        name="mla_qkv",
        compiler_params=_cparams("parallel"),
    )(u, u, u, gq, gkv, wqa, wqb, wka, sela, selb, wv, *tabs)


def _attn_kernel(q_ref, k_ref, v_ref, o_ref, *, n_ctx, n_all, n_ctx_tiles):
    qi = pl.program_id(2)
    tq = q_ref.shape[0]
    lane = lax.broadcasted_iota(jnp.int32, (tq, 2 * MLA_V), 1)

    def run(nk):
        outs = []
        for hh in range(2):
            sl = slice(hh * MLA_SLOT, (hh + 1) * MLA_SLOT)
            s = _dot_nt(q_ref[:, sl], k_ref[0:nk, sl])
            p = jnp.exp(s - jnp.max(s, axis=-1, keepdims=True))
            inv = 1.0 / jnp.sum(p, axis=-1, keepdims=True)
            outs.append(_dot(p.astype(BF16), v_ref[0:nk, :]) * inv)
        o_ref[...] = jnp.where(lane < MLA_V, outs[0], outs[1]).astype(o_ref.dtype)

    @pl.when(qi < n_ctx_tiles)
    def _():
        run(n_ctx)

    @pl.when(qi >= n_ctx_tiles)
    def _():
        run(n_all)


def _mla_attention(geo, q, k, v):
    tq, n, bsz = geo.tr, geo.n, geo.b
    q3, k3, v3 = (t.reshape(bsz, n, t.shape[-1]) for t in (q, k, v))
    kern = functools.partial(_attn_kernel, n_ctx=geo.c, n_all=n, n_ctx_tiles=geo.nct)
    out = pl.pallas_call(
        kern,
        out_shape=jax.ShapeDtypeStruct((bsz, n, MLA_HEADS * MLA_V), BF16),
        grid=(bsz, MLA_HEADS // 2, geo.nt),
        in_specs=[pl.BlockSpec((None, tq, 2 * MLA_SLOT), lambda b, hp, i: (b, i, hp)),
                  pl.BlockSpec((None, n, 2 * MLA_SLOT), lambda b, hp, i: (b, 0, hp)),
                  pl.BlockSpec((None, n, 2 * MLA_V), lambda b, hp, i: (b, 0, hp))],
        out_specs=pl.BlockSpec((None, tq, 2 * MLA_V), lambda b, hp, i: (b, i, hp)),
        name="mla_attn",
        compiler_params=_cparams("parallel", "parallel", "parallel"),
    )(q3, k3, v3)
    return out.reshape(geo.m, MLA_HEADS * MLA_V)


def _halo_flags(geo, i):
    j = i % geo.nt
    left = jnp.logical_and(j != 0, j != geo.nct)
    right = jnp.logical_and(j != geo.nct - 1, j != geo.nt - 1)
    return left.astype(F32), right.astype(F32)


def _conv_taps(pad_ref, w_ref, n_taps, rows):
    base = CONV_HALO - n_taps // 2
    acc = w_ref[0:1, :] * pad_ref[base:base + rows, :]
    for kk in range(1, n_taps):
        acc = acc + w_ref[kk:kk + 1, :] * pad_ref[base + kk:base + kk + rows, :]
    return acc


def _conf_kernel(a_ref, b_ref, ap_ref, bp_ref, an_ref, bn_ref, w_ref, bias_ref, g_ref, beta_ref, o_ref,
                 pad_ref, *, geo):
    left, right = _halo_flags(geo, pl.program_id(0))
    rows = a_ref.shape[0]
    glu = lambda a, b: a[...].astype(F32) * jax.nn.sigmoid(b[...].astype(F32))
    pad_ref[0:CONV_HALO, :] = glu(ap_ref, bp_ref) * left
    pad_ref[CONV_HALO:CONV_HALO + rows, :] = glu(a_ref, b_ref)
    pad_ref[CONV_HALO + rows:2 * CONV_HALO + rows, :] = glu(an_ref, bn_ref) * right
    h = _conv_taps(pad_ref, w_ref, CF_KERNEL, rows) + bias_ref[...]
    hc = h - jnp.mean(h, axis=-1, keepdims=True)
    y = hc * lax.rsqrt(jnp.mean(hc * hc, axis=-1, keepdims=True) + EPS) * g_ref[...] + beta_ref[...]
    o_ref[...] = (y * jax.nn.sigmoid(y)).astype(o_ref.dtype)


def _sconv_kernel(gb_ref, gc_ref, h_ref, gcp_ref, hp_ref, gcn_ref, hn_ref, w_ref, o_ref, pad_ref, *, geo):
    left, right = _halo_flags(geo, pl.program_id(0))
    rows = h_ref.shape[0]
    prod = lambda a, b: a[...].astype(F32) * b[...].astype(F32)
    pad_ref[0:CONV_HALO, :] = prod(gcp_ref, hp_ref) * left
    pad_ref[CONV_HALO:CONV_HALO + rows, :] = prod(gc_ref, h_ref)
    pad_ref[CONV_HALO + rows:2 * CONV_HALO + rows, :] = prod(gcn_ref, hn_ref) * right
    y = _conv_taps(pad_ref, w_ref, SC_KERNEL, rows)
    o_ref[...] = (gb_ref[...].astype(F32) * y).astype(o_ref.dtype)


def _halo_specs(geo, width, offset):
    tr = geo.tr
    per = tr // CONV_HALO
    last = geo.m // CONV_HALO - 1
    cur = pl.BlockSpec((tr, width), lambda i: (i, offset // width))
    prev = pl.BlockSpec((CONV_HALO, width), lambda i: (jnp.maximum(i * per - 1, 0), offset // width))
    nxt = pl.BlockSpec((CONV_HALO, width), lambda i: (jnp.minimum((i + 1) * per, last), offset // width))
    return cur, prev, nxt


def _conformer(geo, u, off, layer, w_dw, b_dw, g_ln, b_ln):
    tr = geo.tr
    a_c, a_p, a_n = _halo_specs(geo, CF_WIDTH, off["cf_a"])
    b_c, b_p, b_n = _halo_specs(geo, CF_WIDTH, off["cf_b"])
    lw = lambda a: pl.BlockSpec((None,) + a.shape[1:], lambda i: (layer,) + (0,) * (a.ndim - 1))
    return pl.pallas_call(
        functools.partial(_conf_kernel, geo=geo),
        out_shape=jax.ShapeDtypeStruct((geo.m, CF_WIDTH), BF16),
        grid=(geo.m // tr,),
        in_specs=[a_c, b_c, a_p, b_p, a_n, b_n, lw(w_dw), lw(b_dw), lw(g_ln), lw(b_ln)],
        out_specs=pl.BlockSpec((tr, CF_WIDTH), lambda i: (i, 0)),
        scratch_shapes=[pltpu.VMEM((tr + 2 * CONV_HALO, CF_WIDTH), F32)],
        name="conformer",
        compiler_params=_cparams("parallel"),
    )(u, u, u, u, u, u, w_dw, b_dw, g_ln, b_ln)


def _shortconv(geo, u, off, layer, w_conv):
    tr = geo.tr
    gb_c, _, _ = _halo_specs(geo, SC_WIDTH, off["sc_b"])
    gc_c, gc_p, gc_n = _halo_specs(geo, SC_WIDTH, off["sc_c"])
    h_c, h_p, h_n = _halo_specs(geo, SC_WIDTH, off["sc_h"])
    lw = lambda a: pl.BlockSpec((None,) + a.shape[1:], lambda i: (layer,) + (0,) * (a.ndim - 1))
    return pl.pallas_call(
        functools.partial(_sconv_kernel, geo=geo),
        out_shape=jax.ShapeDtypeStruct((geo.m, SC_WIDTH), BF16),
        grid=(geo.m // tr,),
        in_specs=[gb_c, gc_c, h_c, gc_p, h_p, gc_n, h_n, lw(w_conv)],
        out_specs=pl.BlockSpec((tr, SC_WIDTH), lambda i: (i, 0)),
        scratch_shapes=[pltpu.VMEM((tr + 2 * CONV_HALO, SC_WIDTH), F32)],
        name="shortconv",
        compiler_params=_cparams("parallel"),
    )(u, u, u, u, u, u, u, w_conv)


def _gla_tables():
    c = GLA_CHUNK
    n_rows = (GLA_LEVELS + 2) * c + 8
    mat = np.zeros((2, n_rows, c), np.float32)
    msk = np.zeros((2, GLA_LEVELS + 1, c, c), np.float32)
    for lvl in range(GLA_LEVELS):
        hs = (c // 2) >> lvl
        for t in range(c):
            mid = (t // (2 * hs)) * 2 * hs + hs
            if t >= mid:
                mat[0, lvl * c + t, mid + 1:t + 1] = 1.0
                msk[0, lvl, t, mid - hs:mid] = 1.0
            else:
                mat[0, lvl * c + t, t + 1:mid + 1] = 1.0
    msk[0, GLA_LEVELS] = np.eye(c, dtype=np.float32)
    for t in range(c):
        mat[0, GLA_LEVELS * c + t, :t + 1] = 1.0
        mat[0, (GLA_LEVELS + 1) * c + t, t + 1:] = 1.0
    mat[0, (GLA_LEVELS + 2) * c:, :] = 1.0
    n_blk = GLA_LEVELS + 2
    mat[1, :n_blk * c] = mat[0, :n_blk * c].reshape(n_blk, c, c)[:, ::-1, ::-1].reshape(n_blk * c, c)
    mat[1, n_blk * c:] = 1.0
    msk[1] = msk[0][:, ::-1, ::-1]
    mat3 = np.concatenate([mat, mat, mat], axis=2)
    msk4 = np.tile(msk, (1, 1, GLA_HEADS, 1))
    return jnp.asarray(mat3, BF16), jnp.asarray(msk4, F32)


def _gla_kernel(q_ref, k_ref, v_ref, g_ref, w2_ref, b2_ref, mat_ref, msk_ref, o_ref, st_ref):
    c = GLA_CHUNK

    @pl.when(pl.program_id(2) == 0)
    def _():
        st_ref[...] = jnp.zeros_like(st_ref)

    z = _dot(g_ref[...], w2_ref[...]) + b2_ref[...]
    la = (jnp.minimum(z, 0.0) - jnp.log1p(jnp.exp(-jnp.abs(z)))) * (1.0 / GLA_GATE_TAU)
    hi = la.astype(BF16)
    r1 = la - hi.astype(F32)
    lo = r1.astype(BF16)
    lo2 = (r1 - lo.astype(F32)).astype(BF16)
    ea = jnp.exp(_dot(mat_ref[...], jnp.concatenate([hi, lo, lo2], axis=0)))

    head = lax.broadcasted_iota(jnp.int32, (1, GLA_HEADS * GLA_DK), 1) // GLA_DK
    zero = jnp.zeros((), BF16)

    def stack(x):
        xb = x.astype(BF16)
        return jnp.concatenate([jnp.where(head == h, xb, zero) for h in range(GLA_HEADS)], axis=0)

    q = q_ref[...].astype(F32) * (GLA_DK ** -0.5)
    k = k_ref[...].astype(F32)
    vb = v_ref[...]

    attn = _dot_nt(stack(q), k.astype(BF16)) * msk_ref[GLA_LEVELS]
    for lvl in range(GLA_LEVELS):
        e = ea[lvl * c:(lvl + 1) * c]
        attn = attn + _dot_nt(stack(q * e), (k * e).astype(BF16)) * msk_ref[lvl]
    o_intra = _dot(attn.astype(BF16), vb)

    st = st_ref[...]
    o_inter = _dot_nt(stack(q * ea[GLA_LEVELS * c:(GLA_LEVELS + 1) * c]), st.astype(BF16))
    for h in range(GLA_HEADS):
        rows = slice(h * c, (h + 1) * c)
        cols = slice(h * GLA_DV, (h + 1) * GLA_DV)
        o_ref[:, cols] = o_intra[rows, cols] + o_inter[rows, :]

    k_dec = stack(k * ea[(GLA_LEVELS + 1) * c:(GLA_LEVELS + 2) * c])
    v_rows = jnp.concatenate([vb[:, h * GLA_DV:(h + 1) * GLA_DV] for h in range(GLA_HEADS)], axis=0)
    eye = (lax.broadcasted_iota(jnp.int32, (GLA_DV, GLA_DV), 0)
           == lax.broadcasted_iota(jnp.int32, (GLA_DV, GLA_DV), 1)).astype(BF16)
    v_t = _dot_nt(eye, v_rows).astype(BF16)
    tot = ea[(GLA_LEVELS + 2) * c:(GLA_LEVELS + 2) * c + 1]
    st_ref[...] = st * tot + _dot(v_t, k_dec)


def _gla_scan(geo, u, off, layer, w2, b2, mat3, msk4):
    c = GLA_CHUNK
    ncc = geo.c // c
    nc = geo.n // c
    dk, dv = GLA_HEADS * GLA_DK, GLA_HEADS * GLA_DV

    def chunk(b, d, i):
        fwd = i
        bwd = jnp.where(i < ncc, ncc - 1 - i, nc - 1 - (i - ncc))
        return b * nc + jnp.where(d == 0, fwd, bwd)

    row = lambda w, o: pl.BlockSpec((c, w), lambda b, d, i: (chunk(b, d, i), o // w))
    return pl.pallas_call(
        _gla_kernel,
        out_shape=jax.ShapeDtypeStruct((2, geo.m, dv), F32),
        grid=(geo.b, 2, nc),
        in_specs=[row(dk, off["gla_q"]), row(dk, off["gla_k"]), row(dv, off["gla_v"]), row(128, off["gate"]),
                  pl.BlockSpec((None, None, 128, dk), lambda b, d, i: (layer, d, 0, 0)),
                  pl.BlockSpec((None, None, 1, dk), lambda b, d, i: (layer, d, 0, 0)),
                  pl.BlockSpec((None,) + mat3.shape[1:], lambda b, d, i: (d, 0, 0)),
                  pl.BlockSpec((None,) + msk4.shape[1:], lambda b, d, i: (d, 0, 0, 0))],
        out_specs=pl.BlockSpec((None, c, dv), lambda b, d, i: (d, chunk(b, d, i), 0)),
        scratch_shapes=[pltpu.VMEM((GLA_DV, dk), F32)],
        name="gla_scan",
        compiler_params=_cparams("parallel", "parallel", "arbitrary"),
    )(u, u, u, u, w2, b2, mat3, msk4)


def _merge_kernel(attn_ref, conf_ref, gla_ref, r_ref, gn_ref, sc_ref, g0_ref, g1_ref, g2_ref, g3_ref, w_ref,
                  y_ref):
    o = gla_ref[0] + gla_ref[1]
    r = r_ref[...].astype(F32)
    gn = gn_ref[...]
    normed = [_rms(o[:, h * GLA_DV:(h + 1) * GLA_DV], gn) for h in range(GLA_HEADS)]
    gla = (jnp.concatenate(normed, axis=1) * (r * jax.nn.sigmoid(r))).astype(BF16)
    acts = (attn_ref[...], conf_ref[...], gla, sc_ref[...])
    gates = (g0_ref, g1_ref, g2_ref, g3_ref)
    y = None
    for i in range(N_BRANCHES):
        term = _sigmoid(gates[i][...].astype(F32)) * _dot(acts[i], w_ref[i])
        y = term if y is None else y + term
    y_ref[...] = y.astype(y_ref.dtype)


def _merge(geo, attn, conf, gla2, u, off, g_norm, sconv, w_br, layer):
    tm, d, m = geo.tr, geo.d, geo.m
    tn = _pick(d, (1024, 512, 256, 128))
    nj = d // tn
    w512 = GLA_HEADS * GLA_DV
    act = pl.BlockSpec((tm, w512), lambda j, i: (i, 0))
    gate = lambda b: pl.BlockSpec((tm, tn), lambda j, i: (i, (off["merge"] + b * d) // tn + j))
    return pl.pallas_call(
        _merge_kernel,
        out_shape=jax.ShapeDtypeStruct((m, d), BF16),
        grid=(nj, m // tm),
        in_specs=[act, act, pl.BlockSpec((2, tm, w512), lambda j, i: (0, i, 0)),
                  pl.BlockSpec((tm, w512), lambda j, i: (i, off["gla_r"] // w512)),
                  pl.BlockSpec((None, 1, GLA_DV), lambda j, i: (layer, 0, 0)),
                  act, gate(0), gate(1), gate(2), gate(3),
                  pl.BlockSpec((None, N_BRANCHES, w512, tn), lambda j, i: (layer, 0, 0, j))],
        out_specs=pl.BlockSpec((tm, tn), lambda j, i: (i, j)),
        name="branch_merge",
        compiler_params=_cparams("parallel", "parallel"),
    )(attn, conf, gla2, u, g_norm, sconv, u, u, u, u, w_br)


def _second_largest(a, b, c, d):
    return jnp.maximum(jnp.maximum(jnp.minimum(a, b), jnp.minimum(c, d)),
                       jnp.minimum(jnp.maximum(a, b), jnp.maximum(c, d)))


def _pack_rows(ref, y):
    n = ref.shape[1]
    half = n * 128
    bits = lambda v: pltpu.bitcast(v.astype(BF16).astype(F32), jnp.uint32)
    packed = (bits(y[:, :half]) >> 16) | bits(y[:, half:])
    for cc in range(n):
        ref[:, cc, :] = packed[:, cc * 128:(cc + 1) * 128]


def _unpack_cols(words):
    lo = pltpu.bitcast(words << 16, F32)
    hi = pltpu.bitcast(words & jnp.uint32(0xFFFF0000), F32)
    return lo, hi


def _ffn_pre_kernel(x_ref, g_ref, sh_ref, sc_ref, wr_ref, br_ref, tri_ref, h_ref, idx_ref, wgt_ref, rank_ref,
                    cnt_ref, carry_ref):
    @pl.when(pl.program_id(0) == 0)
    def _():
        carry_ref[...] = jnp.zeros_like(carry_ref)

    x = x_ref[...]
    y = _rms(x, g_ref[...]) * (1.0 + sc_ref[...]) + sh_ref[...]
    hi = y.astype(BF16)
    lo = (y - hi.astype(F32)).astype(BF16)
    _pack_rows(h_ref, y)
    ne = N_EXPERTS
    two = _dot_nt(wr_ref[...], hi)
    logits = two[0:ne] + two[ne:2 * ne] + _dot_nt(wr_ref[0:ne, :], lo)
    aff = 1.0 / (1.0 + jnp.exp(-logits))
    sel = aff + br_ref[...]
    per = ne // N_GROUPS
    gs = []
    for g in range(N_GROUPS):
        a, b, c, d = (sel[g * per + r:g * per + r + 1] for r in range(per))
        top1 = jnp.maximum(jnp.maximum(a, b), jnp.maximum(c, d))
        gs.append(top1 + _second_largest(a, b, c, d))
    best = gs[0]
    for g in range(1, N_GROUPS):
        best = jnp.maximum(best, gs[g])
    grp = jnp.full(best.shape, N_GROUPS, jnp.int32)
    for g in reversed(range(N_GROUPS)):
        grp = jnp.where(gs[g] == best, g, grp)
    erow = lax.broadcasted_iota(jnp.int32, sel.shape, 0)
    masked = jnp.where(erow // per == grp, sel, -jnp.inf)
    m1 = jnp.max(masked, axis=0, keepdims=True)
    i1 = jnp.min(jnp.where(masked == m1, erow, ne), axis=0, keepdims=True)
    one1 = erow == i1
    masked2 = jnp.where(one1, -jnp.inf, masked)
    m2 = jnp.max(masked2, axis=0, keepdims=True)
    i2 = jnp.min(jnp.where(masked2 == m2, erow, ne), axis=0, keepdims=True)
    one2 = erow == i2
    w1 = jnp.sum(jnp.where(one1, aff, 0.0), axis=0, keepdims=True)
    w2 = jnp.sum(jnp.where(one2, aff, 0.0), axis=0, keepdims=True)
    inv = 1.0 / (w1 + w2)
    idx_ref[0:1, :] = i1
    idx_ref[1:2, :] = i2
    wgt_ref[0:1, :] = w1 * inv
    wgt_ref[1:2, :] = w2 * inv
    o1 = jnp.where(one1, 1.0, 0.0)
    o2 = jnp.where(one2, 1.0, 0.0)
    tri = tri_ref[...]
    c0 = carry_ref[...]
    c1 = c0 + jnp.sum(o1, axis=1, keepdims=True)
    rank_ref[0:1, :] = jnp.sum(o1 * (c0 + _dot(o1.astype(BF16), tri)), axis=0, keepdims=True).astype(jnp.int32)
    rank_ref[1:2, :] = jnp.sum(o2 * (c1 + _dot(o2.astype(BF16), tri)), axis=0, keepdims=True).astype(jnp.int32)
    c2 = c1 + jnp.sum(o2, axis=1, keepdims=True)
    carry_ref[...] = c2
    cnt_ref[...] = c2


def _ffn_pre(geo, x, g, mod, layer, wr_stack, b_router):
    tr, d, m = geo.tr, geo.d, geo.m
    nch = d // 256
    tri = jnp.asarray(np.triu(np.ones((tr, tr), np.float32), 1), BF16)
    pair = lambda dt: jax.ShapeDtypeStruct((2, m), dt)
    pair_spec = pl.BlockSpec((2, tr), lambda i: (0, i))
    return pl.pallas_call(
        _ffn_pre_kernel,
        out_shape=(jax.ShapeDtypeStruct((m, nch, 128), jnp.uint32), pair(jnp.int32), pair(F32), pair(jnp.int32),
                   jax.ShapeDtypeStruct((N_EXPERTS, 1), F32)),
        grid=(m // tr,),
        in_specs=[pl.BlockSpec((tr, d), lambda i: (i, 0)),
                  pl.BlockSpec((None, 1, d), lambda i: (layer, 0, 0)),
                  geo.mod_spec(layer, 3), geo.mod_spec(layer, 4),
                  pl.BlockSpec(wr_stack.shape, lambda i: (0, 0)),
                  pl.BlockSpec(b_router.shape, lambda i: (0, 0)),
                  pl.BlockSpec((tr, tr), lambda i: (0, 0))],
        out_specs=(pl.BlockSpec((tr, nch, 128), lambda i: (i, 0, 0)), pair_spec, pair_spec, pair_spec,
                   pl.BlockSpec((N_EXPERTS, 1), lambda i: (0, 0))),
        scratch_shapes=[pltpu.VMEM((N_EXPERTS, 1), F32)],
        name="ffn_pre",
        compiler_params=_cparams("arbitrary"),
    )(x, g, mod, mod, wr_stack, b_router, tri)


def _dispatch_kernel(dest_ref, h_hbm, z_hbm, xs_hbm, sem, *, m, window):
    del z_hbm
    n_blk = (2 * m) // window

    def copy(a):
        t = jnp.where(a >= m, a - m, a)
        return pltpu.make_async_copy(h_hbm.at[t], xs_hbm.at[dest_ref[a]], sem)

    def issue(blk):
        def body(r, carry):
            copy(blk * window + r).start()
            return carry
        lax.fori_loop(0, window, body, 0)

    def drain(blk):
        def body(r, carry):
            copy(blk * window + r).wait()
            return carry
        lax.fori_loop(0, window, body, 0)

    issue(0)

    def step(blk, carry):
        issue(blk)
        drain(blk - 1)
        return carry

    lax.fori_loop(1, n_blk, step, 0)
    drain(n_blk - 1)


def _dispatch(hfp, dest, n_rows):
    m, nch, _ = hfp.shape
    window = _pick(2 * m, (256, 128, 64))
    zeros = jnp.zeros((n_rows, nch, 128), jnp.uint32)
    any_spec = pl.BlockSpec(memory_space=pl.ANY)
    return pl.pallas_call(
        functools.partial(_dispatch_kernel, m=m, window=window),
        out_shape=jax.ShapeDtypeStruct(zeros.shape, jnp.uint32),
        grid_spec=pltpu.PrefetchScalarGridSpec(
            num_scalar_prefetch=1, grid=(1,), in_specs=[any_spec, any_spec], out_specs=any_spec,
            scratch_shapes=[pltpu.SemaphoreType.DMA(())]),
        input_output_aliases={2: 0},
        name="moe_dispatch",
        compiler_params=_cparams("arbitrary"),
    )(dest, hfp, zeros)


def _gmm_kernel(te_ref, nv_ref, xs_ref, wg_ref, wu_ref, wd_ref, ys_ref, wgb_ref, wub_ref, wdb_ref, prev_ref):
    j = pl.program_id(0)

    @pl.when(j == 0)
    def _():
        prev_ref[0] = -1

    @pl.when(j < nv_ref[0])
    def _():
        e = te_ref[j]

        @pl.when(e != prev_ref[0])
        def _():
            wgb_ref[...] = wg_ref[...].astype(BF16)
            wub_ref[...] = wu_ref[...].astype(BF16)
            wdb_ref[...] = wd_ref[...].astype(BF16)
            prev_ref[0] = e

        n = xs_ref.shape[1]
        half = n * 128
        parts = [_unpack_cols(xs_ref[:, cc, :]) for cc in range(n)]
        xa = jnp.concatenate([p[0].astype(BF16) for p in parts], axis=1)
        xb = jnp.concatenate([p[1].astype(BF16) for p in parts], axis=1)
        g = _dot(xa, wgb_ref[0:half, :]) + _dot(xb, wgb_ref[half:2 * half, :])
        u = _dot(xa, wub_ref[0:half, :]) + _dot(xb, wub_ref[half:2 * half, :])
        act = (g * jax.nn.sigmoid(g) * u).astype(BF16)
        _pack_rows(ys_ref, _dot(act, wdb_ref[...]))

    @pl.when(j >= nv_ref[0])
    def _():
        ys_ref[...] = jnp.zeros_like(ys_ref)


def _gmm(xs, tile_expert, n_valid, wg, wu, wd, layer, tm):
    n_rows, nch, _ = xs.shape
    d, ff = wg.shape[-2], wg.shape[-1]
    row = lambda j, te, nv: (jnp.minimum(j, nv[0] - 1), 0, 0)
    wsel = lambda j, te, nv: (layer, te[jnp.minimum(j, nv[0] - 1)], 0, 0)
    return pl.pallas_call(
        _gmm_kernel,
        out_shape=jax.ShapeDtypeStruct(xs.shape, jnp.uint32),
        grid_spec=pltpu.PrefetchScalarGridSpec(
            num_scalar_prefetch=2, grid=(n_rows // tm,),
            in_specs=[pl.BlockSpec((tm, nch, 128), row),
                      pl.BlockSpec((None, None, d, ff), wsel), pl.BlockSpec((None, None, d, ff), wsel),
                      pl.BlockSpec((None, None, ff, d), wsel)],
            out_specs=pl.BlockSpec((tm, nch, 128), lambda j, te, nv: (j, 0, 0)),
            scratch_shapes=[pltpu.VMEM((d, ff), BF16), pltpu.VMEM((d, ff), BF16), pltpu.VMEM((ff, d), BF16),
                            pltpu.SMEM((1,), jnp.int32)]),
        name="moe_experts",
        compiler_params=_cparams("arbitrary"),
    )(tile_expert, n_valid, xs, wg, wu, wd)


def _combine_kernel(d1_ref, d2_ref, x_ref, w_ref, gt_ref, ys_hbm, o_ref, buf_ref, sem_ref):
    i = pl.program_id(0)
    n_tiles = pl.num_programs(0)
    tr = x_ref.shape[0]

    def copies(tile, slot, r):
        t = tile * tr + r
        return (pltpu.make_async_copy(ys_hbm.at[d1_ref[t]], buf_ref.at[slot, 0, r], sem_ref.at[slot]),
                pltpu.make_async_copy(ys_hbm.at[d2_ref[t]], buf_ref.at[slot, 1, r], sem_ref.at[slot]))

    def issue(tile, slot):
        def body(r, carry):
            for cp in copies(tile, slot, r):
                cp.start()
            return carry
        lax.fori_loop(0, tr, body, 0)

    def drain(tile, slot):
        def body(r, carry):
            for cp in copies(tile, slot, r):
                cp.wait()
            return carry
        lax.fori_loop(0, tr, body, 0)

    @pl.when(i == 0)
    def _():
        issue(0, 0)

    @pl.when(i + 1 < n_tiles)
    def _():
        issue(i + 1, (i + 1) % 2)

    slot = i % 2
    drain(i, slot)
    w1, w2 = w_ref[:, 0:1], w_ref[:, 1:2]
    n = buf_ref.shape[3]
    half = n * 128
    for cc in range(n):
        lo1, hi1 = _unpack_cols(buf_ref[slot, 0, :, cc, :])
        lo2, hi2 = _unpack_cols(buf_ref[slot, 1, :, cc, :])
        ca = slice(cc * 128, (cc + 1) * 128)
        cb = slice(half + cc * 128, half + (cc + 1) * 128)
        o_ref[:, ca] = x_ref[:, ca] + gt_ref[:, ca] * (w1 * lo1 + w2 * lo2)
        o_ref[:, cb] = x_ref[:, cb] + gt_ref[:, cb] * (w1 * hi1 + w2 * hi2)


def _combine(geo, x, ys, d1, d2, w12, mod, layer, gate_part):
    tr, d = geo.tr, geo.d
    nch = ys.shape[1]
    gate = pl.BlockSpec((None, None, 1, d), lambda i, a, b: (layer, geo.mod_row(i), 0, gate_part))
    return pl.pallas_call(
        _combine_kernel,
        out_shape=jax.ShapeDtypeStruct((geo.m, d), F32),
        grid_spec=pltpu.PrefetchScalarGridSpec(
            num_scalar_prefetch=2, grid=(geo.m // tr,),
            in_specs=[pl.BlockSpec((tr, d), lambda i, a, b: (i, 0)),
                      pl.BlockSpec((tr, 2), lambda i, a, b: (i, 0)), gate,
                      pl.BlockSpec(memory_space=pl.ANY)],
            out_specs=pl.BlockSpec((tr, d), lambda i, a, b: (i, 0)),
            scratch_shapes=[pltpu.VMEM((2, 2, tr, nch, 128), jnp.uint32), pltpu.SemaphoreType.DMA((2,))]),
        name="moe_combine",
        compiler_params=_cparams("arbitrary"),
    )(d1, d2, x, w12, mod, ys)


def _moe(geo, x, g, mod, layer, wr_stack, b_router, wg, wu, wd):
    tm = 256
    hfp, ridx, rwgt, rank, cnt = _ffn_pre(geo, x, g, mod, layer, wr_stack, b_router)
    n_tiles = (2 * geo.m) // tm + N_EXPERTS
    counts = cnt[:, 0].astype(jnp.int32)
    tiles_per = (counts + tm - 1) // tm
    tile_start = jnp.cumsum(tiles_per) - tiles_per
    dest = tile_start[ridx] * tm + rank
    n_valid = jnp.sum(tiles_per).reshape(1)
    tile_expert = jnp.sum(jnp.arange(n_tiles)[:, None] >= tile_start[None, :], axis=1).astype(jnp.int32) - 1
    xs = _dispatch(hfp, dest.reshape(-1), n_tiles * tm)
    ys = _gmm(xs, tile_expert, n_valid, wg, wu, wd, layer, tm)
    return _combine(geo, x, ys, dest[0], dest[1], rwgt.T, mod, layer, 5)


def _final_kernel(x_ref, g_ref, o_ref):
    o_ref[...] = _rms(x_ref[...], g_ref[...])


def _final_norm(geo, x, g):
    tr, d = geo.tr, geo.d
    x3 = x.reshape(geo.b, geo.n, d)
    return pl.pallas_call(
        _final_kernel,
        out_shape=jax.ShapeDtypeStruct((geo.b, geo.s, d), F32),
        grid=(geo.b, geo.s // tr),
        in_specs=[pl.BlockSpec((None, tr, d), lambda b, i: (b, i + geo.nct, 0)),
                  pl.BlockSpec((1, d), lambda b, i: (0, 0))],
        out_specs=pl.BlockSpec((None, tr, d), lambda b, i: (b, i, 0)),
        name="final_norm",
        compiler_params=_cparams("parallel", "parallel"),
    )(x3, g.reshape(1, d))


def _pad_cols(w, width):
    return jnp.pad(w, [(0, 0)] * (w.ndim - 1) + [(0, width - w.shape[-1])])


def _prep_w_in(w_in, d):
    src = _src_layout(d)
    off, total = _proj_layout(d)
    cut = lambda name: w_in[..., src[name][0]:src[name][1]]
    cf, sc = cut("cf"), cut("sc")
    gate = jnp.concatenate([cut("gf"), cut("gb")], axis=-1)
    pieces = {"merge": cut("merge"), "sc_b": sc[..., :SC_WIDTH], "sc_c": sc[..., SC_WIDTH:2 * SC_WIDTH],
              "sc_h": sc[..., 2 * SC_WIDTH:], "cf_a": cf[..., :CF_WIDTH], "cf_b": cf[..., CF_WIDTH:],
              "cq": cut("cq"), "gla_v": cut("gla_v"), "gla_r": cut("gla_r"), "ckv": cut("ckv"),
              "gla_q": cut("gla_q"), "gla_k": cut("gla_k"), "kr": _pad_cols(cut("kr"), 128),
              "gate": _pad_cols(gate, 128)}
    order = sorted(off, key=off.get)
    out = jnp.concatenate([pieces[name] for name in order], axis=-1).astype(BF16)
    assert out.shape[-1] == total
    return out, off


def _rope_swap_perm():
    q = MLA_ROPE // 4
    return np.concatenate([np.arange(q, 2 * q), np.arange(0, q), np.arange(3 * q, 4 * q), np.arange(2 * q, 3 * q)])


def _prep_mla(w_uq, w_ukv):
    n_l = w_uq.shape[0]
    perm = _rope_swap_perm()
    wq = w_uq.reshape(n_l, MLA_Q_RANK, MLA_HEADS, MLA_NOPE + MLA_ROPE)
    qn, qr = wq[..., :MLA_NOPE], wq[..., MLA_NOPE:]
    zeros = lambda *s: jnp.zeros((n_l,) + s, w_uq.dtype)
    wqa = jnp.concatenate([qn, qr, qr], axis=-1)
    wqb = jnp.concatenate([zeros(MLA_Q_RANK, MLA_HEADS, MLA_NOPE), qr[..., perm],
                           zeros(MLA_Q_RANK, MLA_HEADS, MLA_ROPE)], axis=-1)
    wkv = w_ukv.reshape(n_l, MLA_KV_RANK, MLA_HEADS, MLA_NOPE + MLA_V)
    wka = jnp.concatenate([wkv[..., :MLA_NOPE], zeros(MLA_KV_RANK, MLA_HEADS, MLA_SLOT - MLA_NOPE)], axis=-1)
    wv = wkv[..., MLA_NOPE:]
    flat = lambda w: w.reshape(n_l, w.shape[1], -1).astype(BF16)
    sela = np.zeros((128, MLA_HEADS, MLA_SLOT), np.float32)
    selb = np.zeros((128, MLA_HEADS, MLA_SLOT), np.float32)
    for r in range(MLA_ROPE):
        sela[r, :, MLA_NOPE + r] = 1.0
        sela[r, :, MLA_NOPE + MLA_ROPE + r] = 1.0
        selb[perm[r], :, MLA_NOPE + r] = 1.0
    sel = lambda s: jnp.asarray(s.reshape(128, -1), BF16)
    return flat(wqa), flat(wqb), flat(wka), sel(sela), sel(selb), flat(wv)


def _rope_tables(geo):
    rows = geo.s // GRID_W
    pos = jnp.arange(rows * GRID_W)
    row = (pos // GRID_W).astype(F32)
    col = (pos % GRID_W).astype(F32)
    n_freq = MLA_ROPE // 4
    freqs = ROPE_THETA ** (-jnp.arange(n_freq, dtype=F32) / n_freq)
    cr, sr = jnp.cos(row[:, None] * freqs), jnp.sin(row[:, None] * freqs)
    cc, sn = jnp.cos(col[:, None] * freqs), jnp.sin(col[:, None] * freqs)
    cos32 = jnp.concatenate([cr, cr, cc, cc], axis=-1)
    sin32 = jnp.concatenate([-sr, sr, -sn, sn], axis=-1)
    s, c = geo.s, geo.c
    one = lambda n, w: jnp.ones((n, w), F32)
    zero = lambda n, w: jnp.zeros((n, w), F32)
    scale = (MLA_NOPE + MLA_ROPE) ** -0.5
    qca = jnp.concatenate([jnp.concatenate([one(c, 64), zero(c, 32), one(c, 32)], 1),
                           jnp.concatenate([one(s, 64), cos32, one(s, 32)], 1)], 0) * scale
    qsb = jnp.concatenate([zero(c, 128), jnp.concatenate([zero(s, 64), sin32, zero(s, 32)], 1)], 0) * scale
    kca = jnp.concatenate([jnp.concatenate([one(c, 64), zero(c, 32), one(c, 32)], 1),
                           jnp.concatenate([one(s, 64), cos32, zero(s, 32)], 1)], 0)
    ksb = jnp.concatenate([zero(c, 128), jnp.concatenate([zero(s, 64), sin32, zero(s, 32)], 1)], 0)
    return qca, qsb, kca, ksb


def kernel(x, c, ctx, c_ctx, w_mod, b_mod, g_mix, g_ffn, w_in, g_q, w_uq, g_kv, w_ukv, w_mla_o, w_cf_dw, b_cf_dw, g_cf_ln, b_cf_ln, w_cf_o, w_gla_gf, b_gla_gf, w_gla_gb, b_gla_gb, g_gla_norm, w_gla_o, w_sc_conv, w_sc_o, w_mix_o, w_router, b_router, w_e_gate, w_e_up, w_e_down, g_final):
    bsz, seq, d = x.shape
    n_ctx = ctx.shape[1]
    depth = w_in.shape[0]
    geo = _Geom(bsz, n_ctx, seq, d)
    assert bsz + 1 <= 8 and seq % GRID_W == 0 and n_ctx % GLA_CHUNK == 0 and seq % GLA_CHUNK == 0

    w_in_p, off = _prep_w_in(w_in, d)
    wqa, wqb, wka, sela, selb, wv = _prep_mla(w_uq, w_ukv)
    tabs = _rope_tables(geo)
    w_br = jnp.stack([w_mla_o, w_cf_o, w_gla_o, w_sc_o], axis=1).astype(BF16)
    w_mix = w_mix_o.astype(BF16)
    gate_w = jnp.zeros((depth, 2, 128, GLA_HEADS * GLA_DK), F32)
    gate_w = gate_w.at[:, 0, :GLA_GATE_RANK].set(w_gla_gf).at[:, 1, GLA_GATE_RANK:2 * GLA_GATE_RANK].set(w_gla_gb)
    gate_w = gate_w.astype(BF16)
    gate_b = jnp.stack([b_gla_gf, b_gla_gb], axis=1)[:, :, None, :]
    mat3, msk4 = _gla_tables()
    wr_t = w_router.T
    wr_hi = wr_t.astype(BF16)
    wr_stack = jnp.concatenate([wr_hi, (wr_t - wr_hi.astype(F32)).astype(BF16)], axis=0)
    vec = lambda a: a.reshape(a.shape[0], 1, a.shape[-1])

    cc = jnp.zeros((8, d), F32).at[:bsz].set(c).at[bsz].set(c_ctx)
    mod = _modulation(cc, w_mod, b_mod).reshape(depth, 8, 1, 6 * d)

    xa = jnp.concatenate([ctx, x], axis=1).reshape(geo.m, d)
    for l in range(depth):
        h = _norm_mod(geo, xa, vec(g_mix), mod, l, 0, 1)
        u = _matmul(h, w_in_p, l, BF16)
        q, k, v = _mla_qkv(geo, u, off, l, vec(g_q), vec(g_kv), wqa, wqb, wka, sela, selb, wv, tabs)
        attn = _mla_attention(geo, q, k, v)
        conf = _conformer(geo, u, off, l, w_cf_dw, vec(b_cf_dw), vec(g_cf_ln), vec(b_cf_ln))
        sconv = _shortconv(geo, u, off, l, w_sc_conv)
        gla2 = _gla_scan(geo, u, off, l, gate_w, gate_b, mat3, msk4)
        y = _merge(geo, attn, conf, gla2, u, off, vec(g_gla_norm), sconv, w_br, l)
        xa = _matmul_residual(geo, y, w_mix, l, xa, mod, 2)
        xa = _moe(geo, xa, vec(g_ffn), mod, l, wr_stack, b_router.reshape(N_EXPERTS, 1), w_e_gate, w_e_up, w_e_down)
    return _final_norm(geo, xa, g_final)
```

```python
import functools

import numpy as np
import jax
import jax.numpy as jnp
from jax import lax
from jax.experimental import pallas as pl
from jax.experimental.pallas import tpu as pltpu

F32 = jnp.float32
BF16 = jnp.bfloat16
EPS = 1e-6

GRID_W = 64
ROPE_THETA = 10000.0
MLA_HEADS = 8
MLA_NOPE = 64
MLA_ROPE = 32
MLA_V = 64
MLA_Q_RANK = 512
MLA_KV_RANK = 256
MLA_SLOT = 128
CF_WIDTH = 512
CF_KERNEL = 31
GLA_HEADS = 4
GLA_DK = 64
GLA_DV = 128
GLA_GATE_RANK = 16
GLA_GATE_TAU = 16.0
GLA_CHUNK = 64
GLA_LEVELS = 6
SC_WIDTH = 512
SC_KERNEL = 3
N_BRANCHES = 4
N_EXPERTS = 16
N_GROUPS = 4
EXPERT_FF = 512
CONV_HALO = 16
VMEM_LIMIT = 56 * 1024 * 1024


def _cparams(*sem):
    return pltpu.CompilerParams(dimension_semantics=sem, vmem_limit_bytes=VMEM_LIMIT)


def _pick(n, cands):
    for c in cands:
        if n % c == 0:
            return c
    raise ValueError(f"no tile for {n}")


def _dot(a, b):
    return jnp.dot(a, b, preferred_element_type=F32)


def _dot_nt(a, b):
    return lax.dot_general(a, b, (((1,), (1,)), ((), ())), preferred_element_type=F32)


def _sigmoid(x):
    return 0.5 * jnp.tanh(0.5 * x) + 0.5


def _proj_layout(d):
    parts = [("merge", N_BRANCHES * d), ("sc_b", SC_WIDTH), ("sc_c", SC_WIDTH), ("sc_h", SC_WIDTH),
             ("cf_a", CF_WIDTH), ("cf_b", CF_WIDTH), ("cq", MLA_Q_RANK), ("gla_v", GLA_HEADS * GLA_DV),
             ("gla_r", GLA_HEADS * GLA_DV), ("ckv", MLA_KV_RANK), ("gla_q", GLA_HEADS * GLA_DK),
             ("gla_k", GLA_HEADS * GLA_DK), ("kr", 128), ("gate", 128)]
    off, start = {}, 0
    for name, width in parts:
        assert start % width == 0 or name == "merge"
        off[name] = start
        start += width
    return off, start


def _src_layout(d):
    parts = (("cq", MLA_Q_RANK), ("ckv", MLA_KV_RANK), ("kr", MLA_ROPE), ("cf", 2 * CF_WIDTH),
             ("gla_q", GLA_HEADS * GLA_DK), ("gla_k", GLA_HEADS * GLA_DK), ("gla_v", GLA_HEADS * GLA_DV),
             ("gf", GLA_GATE_RANK), ("gb", GLA_GATE_RANK), ("gla_r", GLA_HEADS * GLA_DV),
             ("sc", 3 * SC_WIDTH), ("merge", N_BRANCHES * d))
    off, start = {}, 0
    for name, width in parts:
        off[name] = (start, start + width)
        start += width
    return off


def _mod_kernel(c_ref, w_ref, b_ref, o_ref):
    c = c_ref[...]
    a = (c * jax.nn.sigmoid(c)).astype(BF16)
    o_ref[...] = _dot(a, w_ref[...].astype(BF16)) + b_ref[...]


def _modulation(cc, w_mod, b_mod):
    n_l, d, n6 = w_mod.shape
    tn = _pick(n6, (1024, 512, 256, 128))
    return pl.pallas_call(
        _mod_kernel,
        out_shape=jax.ShapeDtypeStruct((n_l, 8, n6), F32),
        grid=(n_l, n6 // tn),
        in_specs=[pl.BlockSpec((8, d), lambda l, j: (0, 0)),
                  pl.BlockSpec((None, d, tn), lambda l, j: (l, 0, j)),
                  pl.BlockSpec((None, 1, tn), lambda l, j: (l, 0, j))],
        out_specs=pl.BlockSpec((None, 8, tn), lambda l, j: (l, 0, j)),
        name="adaln_mod",
        compiler_params=_cparams("parallel", "parallel"),
    )(cc, w_mod, b_mod.reshape(n_l, 1, n6))


class _Geom:
    def __init__(self, bsz, n_ctx, seq, d):
        self.b, self.c, self.s, self.d = bsz, n_ctx, seq, d
        self.n = n_ctx + seq
        self.m = bsz * self.n
        self.tr = _pick(int(np.gcd(n_ctx, seq)), (256, 128, 64))
        self.nt = self.n // self.tr
        self.nct = n_ctx // self.tr

    def mod_row(self, i):
        b, j = i // self.nt, i % self.nt
        return jnp.where(j < self.nct, self.b, b)

    def mod_spec(self, layer, part):
        return pl.BlockSpec((None, None, 1, self.d), lambda i, *_: (layer, self.mod_row(i), 0, part))


def _norm_mod_kernel(x_ref, g_ref, sh_ref, sc_ref, o_ref):
    x = x_ref[...]
    y = x * lax.rsqrt(jnp.mean(x * x, axis=-1, keepdims=True) + EPS) * g_ref[...]
    o_ref[...] = (y * (1.0 + sc_ref[...]) + sh_ref[...]).astype(o_ref.dtype)


def _norm_mod(geo, x, g, mod, layer, sh_part, sc_part):
    tr, d = geo.tr, geo.d
    return pl.pallas_call(
        _norm_mod_kernel,
        out_shape=jax.ShapeDtypeStruct((geo.m, d), BF16),
        grid=(geo.m // tr,),
        in_specs=[pl.BlockSpec((tr, d), lambda i: (i, 0)),
                  pl.BlockSpec((None, 1, d), lambda i: (layer, 0, 0)),
                  geo.mod_spec(layer, sh_part), geo.mod_spec(layer, sc_part)],
        out_specs=pl.BlockSpec((tr, d), lambda i: (i, 0)),
        name="norm_mod",
        compiler_params=_cparams("parallel"),
    )(x, g, mod, mod)


def _mm_kernel(a_ref, w_ref, o_ref):
    o_ref[...] = _dot(a_ref[...], w_ref[...]).astype(o_ref.dtype)


def _matmul(a, w, layer, out_dtype):
    m, k = a.shape
    n = w.shape[-1]
    tm = _pick(m, (512, 256, 128))
    tn = _pick(n, (1024, 512, 256, 128))
    return pl.pallas_call(
        _mm_kernel,
        out_shape=jax.ShapeDtypeStruct((m, n), out_dtype),
        grid=(n // tn, m // tm),
        in_specs=[pl.BlockSpec((tm, k), lambda j, i: (i, 0)),
                  pl.BlockSpec((None, k, tn), lambda j, i: (layer, 0, j))],
        out_specs=pl.BlockSpec((tm, tn), lambda j, i: (i, j)),
        name="proj_in",
        compiler_params=_cparams("parallel", "parallel"),
    )(a, w)


def _mm_res_kernel(a_ref, w_ref, x_ref, gt_ref, o_ref):
    o_ref[...] = x_ref[...] + gt_ref[...] * _dot(a_ref[...], w_ref[...])


def _matmul_residual(geo, a, w, layer, x, mod, gate_part):
    m, k = a.shape
    d = geo.d
    tm = geo.tr
    tn = _pick(d, (1024, 512, 256, 128))
    nj = d // tn
    gate_spec = pl.BlockSpec((None, None, 1, tn), lambda j, i: (layer, geo.mod_row(i), 0, gate_part * nj + j))
    return pl.pallas_call(
        _mm_res_kernel,
        out_shape=jax.ShapeDtypeStruct((m, d), F32),
        grid=(nj, m // tm),
        in_specs=[pl.BlockSpec((tm, k), lambda j, i: (i, 0)),
                  pl.BlockSpec((None, k, tn), lambda j, i: (layer, 0, j)),
                  pl.BlockSpec((tm, tn), lambda j, i: (i, j)),
                  gate_spec],
        out_specs=pl.BlockSpec((tm, tn), lambda j, i: (i, j)),
        name="mix_out",
        compiler_params=_cparams("parallel", "parallel"),
    )(a, w, x, mod)


def _rms(x, g):
    return x * lax.rsqrt(jnp.mean(x * x, axis=-1, keepdims=True) + EPS) * g


def _qkv_kernel(cq_ref, ckv_ref, kr_ref, gq_ref, gkv_ref, wqa_ref, wqb_ref, wka_ref, sela_ref, selb_ref,
                wv_ref, qca_ref, qsb_ref, kca_ref, ksb_ref, q_ref, k_ref, v_ref):
    cqn = _rms(cq_ref[...].astype(F32), gq_ref[...]).astype(BF16)
    ckvn = _rms(ckv_ref[...].astype(F32), gkv_ref[...]).astype(BF16)
    kr = kr_ref[...]
    qa = _dot(cqn, wqa_ref[...])
    qb = _dot(cqn, wqb_ref[...])
    ka = _dot(ckvn, wka_ref[...]) + _dot(kr, sela_ref[...])
    kb = _dot(kr, selb_ref[...])
    qca, qsb, kca, ksb = qca_ref[...], qsb_ref[...], kca_ref[...], ksb_ref[...]
    for h in range(MLA_HEADS):
        sl = slice(h * MLA_SLOT, (h + 1) * MLA_SLOT)
        q_ref[:, sl] = (qa[:, sl] * qca + qb[:, sl] * qsb).astype(BF16)
        k_ref[:, sl] = (ka[:, sl] * kca + kb[:, sl] * ksb).astype(BF16)
    v_ref[...] = _dot(ckvn, wv_ref[...]).astype(BF16)


def _mla_qkv(geo, u, off, layer, gq, gkv, wqa, wqb, wka, sela, selb, wv, tabs):
    tr, m = geo.tr, geo.m
    hq = MLA_HEADS * MLA_SLOT
    hv = MLA_HEADS * MLA_V
    row = lambda w, o: pl.BlockSpec((tr, w), lambda i: (i, o // w))
    lw = lambda a: pl.BlockSpec((None,) + a.shape[1:], lambda i: (layer,) + (0,) * (a.ndim - 1))
    cw = lambda a: pl.BlockSpec(a.shape, lambda i: (0,) * a.ndim)
    tab = pl.BlockSpec((tr, MLA_SLOT), lambda i: (i % geo.nt, 0))
    return pl.pallas_call(
        _qkv_kernel,
        out_shape=(jax.ShapeDtypeStruct((m, hq), BF16), jax.ShapeDtypeStruct((m, hq), BF16),
                   jax.ShapeDtypeStruct((m, hv), BF16)),
        grid=(m // tr,),
        in_specs=[row(MLA_Q_RANK, off["cq"]), row(MLA_KV_RANK, off["ckv"]), row(128, off["kr"]),
                  lw(gq), lw(gkv), lw(wqa), lw(wqb), lw(wka), cw(sela), cw(selb), lw(wv), tab, tab, tab, tab],
        out_specs=(pl.BlockSpec((tr, hq), lambda i: (i, 0)), pl.BlockSpec((tr, hq), lambda i: (i, 0)),
                   pl.BlockSpec((tr, hv), lambda i: (i, 0))),
        name="mla_qkv",
        compiler_params=_cparams("parallel"),
    )(u, u, u, gq, gkv, wqa, wqb, wka, sela, selb, wv, *tabs)


def _attn_kernel(q_ref, k_ref, v_ref, o_ref, *, n_ctx, n_all, n_ctx_tiles):
    qi = pl.program_id(2)
    tq = q_ref.shape[0]
    lane = lax.broadcasted_iota(jnp.int32, (tq, 2 * MLA_V), 1)

    def run(nk):
        outs = []
        for hh in range(2):
            sl = slice(hh * MLA_SLOT, (hh + 1) * MLA_SLOT)
            s = _dot_nt(q_ref[:, sl], k_ref[0:nk, sl])
            p = jnp.exp(s - jnp.max(s, axis=-1, keepdims=True))
            inv = 1.0 / jnp.sum(p, axis=-1, keepdims=True)
            outs.append(_dot(p.astype(BF16), v_ref[0:nk, :]) * inv)
        o_ref[...] = jnp.where(lane < MLA_V, outs[0], outs[1]).astype(o_ref.dtype)

    @pl.when(qi < n_ctx_tiles)
    def _():
        run(n_ctx)

    @pl.when(qi >= n_ctx_tiles)
    def _():
        run(n_all)


def _mla_attention(geo, q, k, v):
    tq, n, bsz = geo.tr, geo.n, geo.b
    q3, k3, v3 = (t.reshape(bsz, n, t.shape[-1]) for t in (q, k, v))
    kern = functools.partial(_attn_kernel, n_ctx=geo.c, n_all=n, n_ctx_tiles=geo.nct)
    out = pl.pallas_call(
        kern,
        out_shape=jax.ShapeDtypeStruct((bsz, n, MLA_HEADS * MLA_V), BF16),
        grid=(bsz, MLA_HEADS // 2, geo.nt),
        in_specs=[pl.BlockSpec((None, tq, 2 * MLA_SLOT), lambda b, hp, i: (b, i, hp)),
                  pl.BlockSpec((None, n, 2 * MLA_SLOT), lambda b, hp, i: (b, 0, hp)),
                  pl.BlockSpec((None, n, 2 * MLA_V), lambda b, hp, i: (b, 0, hp))],
        out_specs=pl.BlockSpec((None, tq, 2 * MLA_V), lambda b, hp, i: (b, i, hp)),
        name="mla_attn",
        compiler_params=_cparams("parallel", "parallel", "parallel"),
    )(q3, k3, v3)
    return out.reshape(geo.m, MLA_HEADS * MLA_V)


def _halo_flags(geo, i):
    j = i % geo.nt
    left = jnp.logical_and(j != 0, j != geo.nct)
    right = jnp.logical_and(j != geo.nct - 1, j != geo.nt - 1)
    return left.astype(F32), right.astype(F32)


def _conv_taps(pad_ref, w_ref, n_taps, rows):
    base = CONV_HALO - n_taps // 2
    acc = w_ref[0:1, :] * pad_ref[base:base + rows, :]
    for kk in range(1, n_taps):
        acc = acc + w_ref[kk:kk + 1, :] * pad_ref[base + kk:base + kk + rows, :]
    return acc


def _conf_kernel(a_ref, b_ref, ap_ref, bp_ref, an_ref, bn_ref, w_ref, bias_ref, g_ref, beta_ref, o_ref,
                 pad_ref, *, geo):
    left, right = _halo_flags(geo, pl.program_id(0))
    rows = a_ref.shape[0]
    glu = lambda a, b: a[...].astype(F32) * jax.nn.sigmoid(b[...].astype(F32))
    pad_ref[0:CONV_HALO, :] = glu(ap_ref, bp_ref) * left
    pad_ref[CONV_HALO:CONV_HALO + rows, :] = glu(a_ref, b_ref)
    pad_ref[CONV_HALO + rows:2 * CONV_HALO + rows, :] = glu(an_ref, bn_ref) * right
    h = _conv_taps(pad_ref, w_ref, CF_KERNEL, rows) + bias_ref[...]
    hc = h - jnp.mean(h, axis=-1, keepdims=True)
    y = hc * lax.rsqrt(jnp.mean(hc * hc, axis=-1, keepdims=True) + EPS) * g_ref[...] + beta_ref[...]
    o_ref[...] = (y * jax.nn.sigmoid(y)).astype(o_ref.dtype)


def _sconv_kernel(gb_ref, gc_ref, h_ref, gcp_ref, hp_ref, gcn_ref, hn_ref, w_ref, o_ref, pad_ref, *, geo):
    left, right = _halo_flags(geo, pl.program_id(0))
    rows = h_ref.shape[0]
    prod = lambda a, b: a[...].astype(F32) * b[...].astype(F32)
    pad_ref[0:CONV_HALO, :] = prod(gcp_ref, hp_ref) * left
    pad_ref[CONV_HALO:CONV_HALO + rows, :] = prod(gc_ref, h_ref)
    pad_ref[CONV_HALO + rows:2 * CONV_HALO + rows, :] = prod(gcn_ref, hn_ref) * right
    y = _conv_taps(pad_ref, w_ref, SC_KERNEL, rows)
    o_ref[...] = (gb_ref[...].astype(F32) * y).astype(o_ref.dtype)


def _halo_specs(geo, width, offset):
    tr = geo.tr
    per = tr // CONV_HALO
    last = geo.m // CONV_HALO - 1
    cur = pl.BlockSpec((tr, width), lambda i: (i, offset // width))
    prev = pl.BlockSpec((CONV_HALO, width), lambda i: (jnp.maximum(i * per - 1, 0), offset // width))
    nxt = pl.BlockSpec((CONV_HALO, width), lambda i: (jnp.minimum((i + 1) * per, last), offset // width))
    return cur, prev, nxt


def _conformer(geo, u, off, layer, w_dw, b_dw, g_ln, b_ln):
    tr = geo.tr
    a_c, a_p, a_n = _halo_specs(geo, CF_WIDTH, off["cf_a"])
    b_c, b_p, b_n = _halo_specs(geo, CF_WIDTH, off["cf_b"])
    lw = lambda a: pl.BlockSpec((None,) + a.shape[1:], lambda i: (layer,) + (0,) * (a.ndim - 1))
    return pl.pallas_call(
        functools.partial(_conf_kernel, geo=geo),
        out_shape=jax.ShapeDtypeStruct((geo.m, CF_WIDTH), BF16),
        grid=(geo.m // tr,),
        in_specs=[a_c, b_c, a_p, b_p, a_n, b_n, lw(w_dw), lw(b_dw), lw(g_ln), lw(b_ln)],
        out_specs=pl.BlockSpec((tr, CF_WIDTH), lambda i: (i, 0)),
        scratch_shapes=[pltpu.VMEM((tr + 2 * CONV_HALO, CF_WIDTH), F32)],
        name="conformer",
        compiler_params=_cparams("parallel"),
    )(u, u, u, u, u, u, w_dw, b_dw, g_ln, b_ln)


def _shortconv(geo, u, off, layer, w_conv):
    tr = geo.tr
    gb_c, _, _ = _halo_specs(geo, SC_WIDTH, off["sc_b"])
    gc_c, gc_p, gc_n = _halo_specs(geo, SC_WIDTH, off["sc_c"])
    h_c, h_p, h_n = _halo_specs(geo, SC_WIDTH, off["sc_h"])
    lw = lambda a: pl.BlockSpec((None,) + a.shape[1:], lambda i: (layer,) + (0,) * (a.ndim - 1))
    return pl.pallas_call(
        functools.partial(_sconv_kernel, geo=geo),
        out_shape=jax.ShapeDtypeStruct((geo.m, SC_WIDTH), BF16),
        grid=(geo.m // tr,),
        in_specs=[gb_c, gc_c, h_c, gc_p, h_p, gc_n, h_n, lw(w_conv)],
        out_specs=pl.BlockSpec((tr, SC_WIDTH), lambda i: (i, 0)),
        scratch_shapes=[pltpu.VMEM((tr + 2 * CONV_HALO, SC_WIDTH), F32)],
        name="shortconv",
        compiler_params=_cparams("parallel"),
    )(u, u, u, u, u, u, u, w_conv)


def _gla_tables():
    c = GLA_CHUNK
    n_rows = (GLA_LEVELS + 2) * c + 8
    mat = np.zeros((2, n_rows, c), np.float32)
    msk = np.zeros((2, GLA_LEVELS + 1, c, c), np.float32)
    for lvl in range(GLA_LEVELS):
        hs = (c // 2) >> lvl
        for t in range(c):
            mid = (t // (2 * hs)) * 2 * hs + hs
            if t >= mid:
                mat[0, lvl * c + t, mid + 1:t + 1] = 1.0
                msk[0, lvl, t, mid - hs:mid] = 1.0
            else:
                mat[0, lvl * c + t, t + 1:mid + 1] = 1.0
    msk[0, GLA_LEVELS] = np.eye(c, dtype=np.float32)
    for t in range(c):
        mat[0, GLA_LEVELS * c + t, :t + 1] = 1.0
        mat[0, (GLA_LEVELS + 1) * c + t, t + 1:] = 1.0
    mat[0, (GLA_LEVELS + 2) * c:, :] = 1.0
    n_blk = GLA_LEVELS + 2
    mat[1, :n_blk * c] = mat[0, :n_blk * c].reshape(n_blk, c, c)[:, ::-1, ::-1].reshape(n_blk * c, c)
    mat[1, n_blk * c:] = 1.0
    msk[1] = msk[0][:, ::-1, ::-1]
    mat3 = np.concatenate([mat, mat, mat], axis=2)
    msk4 = np.tile(msk, (1, 1, GLA_HEADS, 1))
    return jnp.asarray(mat3, BF16), jnp.asarray(msk4, F32)


def _gla_kernel(q_ref, k_ref, v_ref, g_ref, w2_ref, b2_ref, mat_ref, msk_ref, o_ref, st_ref):
    @pl.when(pl.program_id(1) == 0)
    def _():
        st_ref[...] = jnp.zeros_like(st_ref)

    for b in range(q_ref.shape[0]):
        _gla_chunk(q_ref.at[b], k_ref.at[b], v_ref.at[b], g_ref.at[b], w2_ref, b2_ref, mat_ref, msk_ref,
                   o_ref.at[b], st_ref.at[b])


def _gla_chunk(q_ref, k_ref, v_ref, g_ref, w2_ref, b2_ref, mat_ref, msk_ref, o_ref, st_ref):
    c = GLA_CHUNK
    z = _dot(g_ref[...], w2_ref[...]) + b2_ref[...]
    la = (jnp.minimum(z, 0.0) - jnp.log1p(jnp.exp(-jnp.abs(z)))) * (1.0 / GLA_GATE_TAU)
    hi = la.astype(BF16)
    r1 = la - hi.astype(F32)
    lo = r1.astype(BF16)
    lo2 = (r1 - lo.astype(F32)).astype(BF16)
    ea = jnp.exp(_dot(mat_ref[...], jnp.concatenate([hi, lo, lo2], axis=0)))

    head = lax.broadcasted_iota(jnp.int32, (1, GLA_HEADS * GLA_DK), 1) // GLA_DK
    zero = jnp.zeros((), BF16)

    def stack(x):
        xb = x.astype(BF16)
        return jnp.concatenate([jnp.where(head == h, xb, zero) for h in range(GLA_HEADS)], axis=0)

    q = q_ref[...].astype(F32) * (GLA_DK ** -0.5)
    k = k_ref[...].astype(F32)
    vb = v_ref[...]

    attn = _dot_nt(stack(q), k.astype(BF16)) * msk_ref[GLA_LEVELS]
    for lvl in range(GLA_LEVELS):
        e = ea[lvl * c:(lvl + 1) * c]
        attn = attn + _dot_nt(stack(q * e), (k * e).astype(BF16)) * msk_ref[lvl]
    o_intra = _dot(attn.astype(BF16), vb)

    st = st_ref[...]
    o_inter = _dot_nt(stack(q * ea[GLA_LEVELS * c:(GLA_LEVELS + 1) * c]), st.astype(BF16))
    for h in range(GLA_HEADS):
        rows = slice(h * c, (h + 1) * c)
        cols = slice(h * GLA_DV, (h + 1) * GLA_DV)
        o_ref[:, cols] = o_intra[rows, cols] + o_inter[rows, :]

    k_dec = stack(k * ea[(GLA_LEVELS + 1) * c:(GLA_LEVELS + 2) * c])
    v_rows = jnp.concatenate([vb[:, h * GLA_DV:(h + 1) * GLA_DV] for h in range(GLA_HEADS)], axis=0)
    eye = (lax.broadcasted_iota(jnp.int32, (GLA_DV, GLA_DV), 0)
           == lax.broadcasted_iota(jnp.int32, (GLA_DV, GLA_DV), 1)).astype(BF16)
    v_t = _dot_nt(eye, v_rows).astype(BF16)
    tot = ea[(GLA_LEVELS + 2) * c:(GLA_LEVELS + 2) * c + 1]
    st_ref[...] = st * tot + _dot(v_t, k_dec)


def _gla_scan(geo, u, off, layer, w2, b2, mat3, msk4):
    c = GLA_CHUNK
    ncc = geo.c // c
    nc = geo.n // c
    dk, dv = GLA_HEADS * GLA_DK, GLA_HEADS * GLA_DV

    def chunk(d, i):
        bwd = jnp.where(i < ncc, ncc - 1 - i, nc - 1 - (i - ncc))
        return jnp.where(d == 0, i, bwd)

    u3 = u.reshape(geo.b, geo.n, u.shape[-1])
    row = lambda w, o: pl.BlockSpec((geo.b, c, w), lambda d, i: (0, chunk(d, i), o // w))
    out = pl.pallas_call(
        _gla_kernel,
        out_shape=jax.ShapeDtypeStruct((2, geo.b, geo.n, dv), F32),
        grid=(2, nc),
        in_specs=[row(dk, off["gla_q"]), row(dk, off["gla_k"]), row(dv, off["gla_v"]), row(128, off["gate"]),
                  pl.BlockSpec((None, None, 128, dk), lambda d, i: (layer, d, 0, 0)),
                  pl.BlockSpec((None, None, 1, dk), lambda d, i: (layer, d, 0, 0)),
                  pl.BlockSpec((None,) + mat3.shape[1:], lambda d, i: (d, 0, 0)),
                  pl.BlockSpec((None,) + msk4.shape[1:], lambda d, i: (d, 0, 0, 0))],
        out_specs=pl.BlockSpec((None, geo.b, c, dv), lambda d, i: (d, 0, chunk(d, i), 0)),
        scratch_shapes=[pltpu.VMEM((geo.b, GLA_DV, dk), F32)],
        name="gla_scan",
        compiler_params=_cparams("parallel", "arbitrary"),
    )(u3, u3, u3, u3, w2, b2, mat3, msk4)
    return out.reshape(2, geo.m, dv)


def _merge_kernel(attn_ref, conf_ref, gla_ref, r_ref, gn_ref, sc_ref, g0_ref, g1_ref, g2_ref, g3_ref, w_ref,
                  y_ref):
    o = gla_ref[0] + gla_ref[1]
    r = r_ref[...].astype(F32)
    gn = gn_ref[...]
    normed = [_rms(o[:, h * GLA_DV:(h + 1) * GLA_DV], gn) for h in range(GLA_HEADS)]
    gla = (jnp.concatenate(normed, axis=1) * (r * jax.nn.sigmoid(r))).astype(BF16)
    acts = (attn_ref[...], conf_ref[...], gla, sc_ref[...])
    gates = (g0_ref, g1_ref, g2_ref, g3_ref)
    y = None
    for i in range(N_BRANCHES):
        term = _sigmoid(gates[i][...].astype(F32)) * _dot(acts[i], w_ref[i])
        y = term if y is None else y + term
    y_ref[...] = y.astype(y_ref.dtype)


def _merge(geo, attn, conf, gla2, u, off, g_norm, sconv, w_br, layer):
    tm, d, m = geo.tr, geo.d, geo.m
    tn = _pick(d, (1024, 512, 256, 128))
    nj = d // tn
    w512 = GLA_HEADS * GLA_DV
    act = pl.BlockSpec((tm, w512), lambda j, i: (i, 0))
    gate = lambda b: pl.BlockSpec((tm, tn), lambda j, i: (i, (off["merge"] + b * d) // tn + j))
    return pl.pallas_call(
        _merge_kernel,
        out_shape=jax.ShapeDtypeStruct((m, d), BF16),
        grid=(nj, m // tm),
        in_specs=[act, act, pl.BlockSpec((2, tm, w512), lambda j, i: (0, i, 0)),
                  pl.BlockSpec((tm, w512), lambda j, i: (i, off["gla_r"] // w512)),
                  pl.BlockSpec((None, 1, GLA_DV), lambda j, i: (layer, 0, 0)),
                  act, gate(0), gate(1), gate(2), gate(3),
                  pl.BlockSpec((None, N_BRANCHES, w512, tn), lambda j, i: (layer, 0, 0, j))],
        out_specs=pl.BlockSpec((tm, tn), lambda j, i: (i, j)),
        name="branch_merge",
        compiler_params=_cparams("parallel", "parallel"),
    )(attn, conf, gla2, u, g_norm, sconv, u, u, u, u, w_br)


def _second_largest(a, b, c, d):
    return jnp.maximum(jnp.maximum(jnp.minimum(a, b), jnp.minimum(c, d)),
                       jnp.minimum(jnp.maximum(a, b), jnp.maximum(c, d)))


def _chunk_rows(rows, n, cc):
    return pl.ds(cc, rows, stride=n) if n > 1 else pl.ds(0, rows)


def _pack_rows(ref, y):
    rows = y.shape[0]
    n = ref.shape[0] // rows
    half = n * 128
    bits = lambda v: pltpu.bitcast(v.astype(BF16).astype(F32), jnp.uint32)
    packed = (bits(y[:, :half]) >> 16) | bits(y[:, half:])
    for cc in range(n):
        ref[_chunk_rows(rows, n, cc), :] = packed[:, cc * 128:(cc + 1) * 128]


def _unpack_cols(words):
    lo = pltpu.bitcast(words << 16, F32)
    hi = pltpu.bitcast(words & jnp.uint32(0xFFFF0000), F32)
    return lo, hi


def _ffn_pre_kernel(x_ref, g_ref, sh_ref, sc_ref, wr_ref, br_ref, tri_ref, h_ref, idx_ref, wgt_ref, rank_ref,
                    cnt_ref, carry_ref):
    @pl.when(pl.program_id(0) == 0)
    def _():
        carry_ref[...] = jnp.zeros_like(carry_ref)

    x = x_ref[...]
    y = _rms(x, g_ref[...]) * (1.0 + sc_ref[...]) + sh_ref[...]
    hi = y.astype(BF16)
    lo = (y - hi.astype(F32)).astype(BF16)
    _pack_rows(h_ref, y)
    ne = N_EXPERTS
    two = _dot_nt(wr_ref[...], hi)
    logits = two[0:ne] + two[ne:2 * ne] + _dot_nt(wr_ref[0:ne, :], lo)
    aff = 1.0 / (1.0 + jnp.exp(-logits))
    sel = aff + br_ref[...]
    per = ne // N_GROUPS
    gs = []
    for g in range(N_GROUPS):
        a, b, c, d = (sel[g * per + r:g * per + r + 1] for r in range(per))
        top1 = jnp.maximum(jnp.maximum(a, b), jnp.maximum(c, d))
        gs.append(top1 + _second_largest(a, b, c, d))
    best = gs[0]
    for g in range(1, N_GROUPS):
        best = jnp.maximum(best, gs[g])
    grp = jnp.full(best.shape, N_GROUPS, jnp.int32)
    for g in reversed(range(N_GROUPS)):
        grp = jnp.where(gs[g] == best, g, grp)
    erow = lax.broadcasted_iota(jnp.int32, sel.shape, 0)
    masked = jnp.where(erow // per == grp, sel, -jnp.inf)
    m1 = jnp.max(masked, axis=0, keepdims=True)
    i1 = jnp.min(jnp.where(masked == m1, erow, ne), axis=0, keepdims=True)
    one1 = erow == i1
    masked2 = jnp.where(one1, -jnp.inf, masked)
    m2 = jnp.max(masked2, axis=0, keepdims=True)
    i2 = jnp.min(jnp.where(masked2 == m2, erow, ne), axis=0, keepdims=True)
    one2 = erow == i2
    w1 = jnp.sum(jnp.where(one1, aff, 0.0), axis=0, keepdims=True)
    w2 = jnp.sum(jnp.where(one2, aff, 0.0), axis=0, keepdims=True)
    inv = 1.0 / (w1 + w2)
    idx_ref[0:1, :] = i1
    idx_ref[1:2, :] = i2
    wgt_ref[0:1, :] = w1 * inv
    wgt_ref[1:2, :] = w2 * inv
    o1 = jnp.where(one1, 1.0, 0.0)
    o2 = jnp.where(one2, 1.0, 0.0)
    tri = tri_ref[...]
    c0 = carry_ref[...]
    c1 = c0 + jnp.sum(o1, axis=1, keepdims=True)
    rank_ref[0:1, :] = jnp.sum(o1 * (c0 + _dot(o1.astype(BF16), tri)), axis=0, keepdims=True).astype(jnp.int32)
    rank_ref[1:2, :] = jnp.sum(o2 * (c1 + _dot(o2.astype(BF16), tri)), axis=0, keepdims=True).astype(jnp.int32)
    c2 = c1 + jnp.sum(o2, axis=1, keepdims=True)
    carry_ref[...] = c2
    cnt_ref[...] = c2


def _ffn_pre(geo, x, g, mod, layer, wr_stack, b_router):
    tr, d, m = geo.tr, geo.d, geo.m
    nch = d // 256
    tri = jnp.asarray(np.triu(np.ones((tr, tr), np.float32), 1), BF16)
    pair = lambda dt: jax.ShapeDtypeStruct((2, m), dt)
    pair_spec = pl.BlockSpec((2, tr), lambda i: (0, i))
    return pl.pallas_call(
        _ffn_pre_kernel,
        out_shape=(jax.ShapeDtypeStruct((m * nch, 128), jnp.uint32), pair(jnp.int32), pair(F32), pair(jnp.int32),
                   jax.ShapeDtypeStruct((N_EXPERTS, 1), F32)),
        grid=(m // tr,),
        in_specs=[pl.BlockSpec((tr, d), lambda i: (i, 0)),
                  pl.BlockSpec((None, 1, d), lambda i: (layer, 0, 0)),
                  geo.mod_spec(layer, 3), geo.mod_spec(layer, 4),
                  pl.BlockSpec(wr_stack.shape, lambda i: (0, 0)),
                  pl.BlockSpec(b_router.shape, lambda i: (0, 0)),
                  pl.BlockSpec((tr, tr), lambda i: (0, 0))],
        out_specs=(pl.BlockSpec((tr * nch, 128), lambda i: (i, 0)), pair_spec, pair_spec, pair_spec,
                   pl.BlockSpec((N_EXPERTS, 1), lambda i: (0, 0))),
        scratch_shapes=[pltpu.VMEM((N_EXPERTS, 1), F32)],
        name="ffn_pre",
        compiler_params=_cparams("arbitrary"),
    )(x, g, mod, mod, wr_stack, b_router, tri)


def _sorted_row(idx_ref, rank_ref, base_ref, a, n):
    return pl.multiple_of((base_ref[idx_ref[a]] + rank_ref[a]) * n, n)


def _dispatch_kernel(idx_ref, rank_ref, base_ref, h_ref, z_hbm, xs_hbm, sem, *, m, n):
    del z_hbm
    i = pl.program_id(0)
    tr = h_ref.shape[0] // n

    def issue(r, carry):
        src = h_ref.at[pl.ds(pl.multiple_of(r * n, n), n), :]
        for k in range(2):
            row = _sorted_row(idx_ref, rank_ref, base_ref, k * m + i * tr + r, n)
            pltpu.make_async_copy(src, xs_hbm.at[pl.ds(row, n), :], sem).start()
        return carry

    lax.fori_loop(0, tr, issue, 0, unroll=4)
    everything = xs_hbm.at[pl.ds(0, 2 * tr * n), :]
    pltpu.make_async_copy(everything, everything, sem).wait()


def _dispatch(hfp, ridx, rank, base, n_rows, tr):
    n = hfp.shape[0] * 2 // ridx.shape[0]
    m = hfp.shape[0] // n
    zeros = jnp.zeros((n_rows * n, 128), jnp.uint32)
    any_spec = pl.BlockSpec(memory_space=pl.ANY)
    return pl.pallas_call(
        functools.partial(_dispatch_kernel, m=m, n=n),
        out_shape=jax.ShapeDtypeStruct(zeros.shape, jnp.uint32),
        grid_spec=pltpu.PrefetchScalarGridSpec(
            num_scalar_prefetch=3, grid=(m // tr,),
            in_specs=[pl.BlockSpec((tr * n, 128), lambda i, *_: (i, 0)), any_spec], out_specs=any_spec,
            scratch_shapes=[pltpu.SemaphoreType.DMA(())]),
        input_output_aliases={4: 0},
        name="moe_dispatch",
        compiler_params=_cparams("arbitrary"),
    )(ridx, rank, base, hfp, zeros)


def _gmm_kernel(te_ref, nv_ref, xs_ref, wg_ref, wu_ref, wd_ref, ys_ref, wgb_ref, wub_ref, wdb_ref, prev_ref):
    j = pl.program_id(0)

    @pl.when(j == 0)
    def _():
        prev_ref[0] = -1

    @pl.when(j < nv_ref[0])
    def _():
        e = te_ref[j]

        @pl.when(e != prev_ref[0])
        def _():
            wgb_ref[...] = wg_ref[...].astype(BF16)
            wub_ref[...] = wu_ref[...].astype(BF16)
            wdb_ref[...] = wd_ref[...].astype(BF16)
            prev_ref[0] = e

        half = wgb_ref.shape[0] // 2
        n = half // 128
        tm = xs_ref.shape[0] // n
        parts = [_unpack_cols(xs_ref[_chunk_rows(tm, n, cc), :]) for cc in range(n)]
        xa = jnp.concatenate([p[0].astype(BF16) for p in parts], axis=1)
        xb = jnp.concatenate([p[1].astype(BF16) for p in parts], axis=1)
        g = _dot(xa, wgb_ref[0:half, :]) + _dot(xb, wgb_ref[half:2 * half, :])
        u = _dot(xa, wub_ref[0:half, :]) + _dot(xb, wub_ref[half:2 * half, :])
        act = (g * jax.nn.sigmoid(g) * u).astype(BF16)
        _pack_rows(ys_ref, _dot(act, wdb_ref[...]))

    @pl.when(j >= nv_ref[0])
    def _():
        ys_ref[...] = jnp.zeros_like(ys_ref)


def _gmm(xs, tile_expert, n_valid, wg, wu, wd, layer, tm):
    d, ff = wg.shape[-2], wg.shape[-1]
    nch = d // 256
    blk = tm * nch
    row = lambda j, te, nv: (jnp.minimum(j, nv[0] - 1), 0)
    wsel = lambda j, te, nv: (layer, te[jnp.minimum(j, nv[0] - 1)], 0, 0)
    return pl.pallas_call(
        _gmm_kernel,
        out_shape=jax.ShapeDtypeStruct(xs.shape, jnp.uint32),
        grid_spec=pltpu.PrefetchScalarGridSpec(
            num_scalar_prefetch=2, grid=(xs.shape[0] // blk,),
            in_specs=[pl.BlockSpec((blk, 128), row),
                      pl.BlockSpec((None, None, d, ff), wsel), pl.BlockSpec((None, None, d, ff), wsel),
                      pl.BlockSpec((None, None, ff, d), wsel)],
            out_specs=pl.BlockSpec((blk, 128), lambda j, te, nv: (j, 0)),
            scratch_shapes=[pltpu.VMEM((d, ff), BF16), pltpu.VMEM((d, ff), BF16), pltpu.VMEM((ff, d), BF16),
                            pltpu.SMEM((1,), jnp.int32)]),
        name="moe_experts",
        compiler_params=_cparams("arbitrary"),
    )(tile_expert, n_valid, xs, wg, wu, wd)


def _combine_kernel(idx_ref, rank_ref, base_ref, x_ref, w_ref, gt_ref, ys_hbm, o_ref, buf_ref, sem_ref, *, m):
    i = pl.program_id(0)
    n_tiles = pl.num_programs(0)
    tr = x_ref.shape[0]
    n = buf_ref.shape[2] // tr

    def issue(tile, slot):
        def body(r, carry):
            for k in range(2):
                row = _sorted_row(idx_ref, rank_ref, base_ref, k * m + tile * tr + r, n)
                pltpu.make_async_copy(ys_hbm.at[pl.ds(row, n), :],
                                      buf_ref.at[slot, k, pl.ds(pl.multiple_of(r * n, n), n), :],
                                      sem_ref.at[slot]).start()
            return carry
        lax.fori_loop(0, tr, body, 0, unroll=4)

    @pl.when(i == 0)
    def _():
        issue(0, 0)

    @pl.when(i + 1 < n_tiles)
    def _():
        issue(i + 1, (i + 1) % 2)

    slot = i % 2
    pltpu.make_async_copy(buf_ref.at[slot], buf_ref.at[slot], sem_ref.at[slot]).wait()
    w1, w2 = w_ref[:, 0:1], w_ref[:, 1:2]
    half = n * 128
    for cc in range(n):
        lo1, hi1 = _unpack_cols(buf_ref[slot, 0, _chunk_rows(tr, n, cc), :])
        lo2, hi2 = _unpack_cols(buf_ref[slot, 1, _chunk_rows(tr, n, cc), :])
        ca = slice(cc * 128, (cc + 1) * 128)
        cb = slice(half + cc * 128, half + (cc + 1) * 128)
        o_ref[:, ca] = x_ref[:, ca] + gt_ref[:, ca] * (w1 * lo1 + w2 * lo2)
        o_ref[:, cb] = x_ref[:, cb] + gt_ref[:, cb] * (w1 * hi1 + w2 * hi2)


def _combine(geo, x, ys, ridx, rank, base, w12, mod, layer, gate_part):
    tr, d = geo.tr, geo.d
    nch = d // 256
    gate = pl.BlockSpec((None, None, 1, d), lambda i, *_: (layer, geo.mod_row(i), 0, gate_part))
    return pl.pallas_call(
        functools.partial(_combine_kernel, m=geo.m),
        out_shape=jax.ShapeDtypeStruct((geo.m, d), F32),
        grid_spec=pltpu.PrefetchScalarGridSpec(
            num_scalar_prefetch=3, grid=(geo.m // tr,),
            in_specs=[pl.BlockSpec((tr, d), lambda i, *_: (i, 0)),
                      pl.BlockSpec((tr, 2), lambda i, *_: (i, 0)), gate,
                      pl.BlockSpec(memory_space=pl.ANY)],
            out_specs=pl.BlockSpec((tr, d), lambda i, *_: (i, 0)),
            scratch_shapes=[pltpu.VMEM((2, 2, tr * nch, 128), jnp.uint32), pltpu.SemaphoreType.DMA((2,))]),
        name="moe_combine",
        compiler_params=_cparams("arbitrary"),
    )(ridx, rank, base, x, w12, mod, ys)


def _moe(geo, x, g, mod, layer, wr_stack, b_router, wg, wu, wd):
    tm = 256
    hfp, ridx, rwgt, rank, cnt = _ffn_pre(geo, x, g, mod, layer, wr_stack, b_router)
    n_tiles = (2 * geo.m) // tm + N_EXPERTS
    counts = cnt[:, 0].astype(jnp.int32)
    tiles_per = (counts + tm - 1) // tm
    tile_start = jnp.cumsum(tiles_per) - tiles_per
    base = tile_start * tm
    n_valid = jnp.sum(tiles_per).reshape(1)
    tile_expert = jnp.sum(jnp.arange(n_tiles)[:, None] >= tile_start[None, :], axis=1).astype(jnp.int32) - 1
    ridx, rank = ridx.reshape(-1), rank.reshape(-1)
    xs = _dispatch(hfp, ridx, rank, base, n_tiles * tm, geo.tr)
    ys = _gmm(xs, tile_expert, n_valid, wg, wu, wd, layer, tm)
    return _combine(geo, x, ys, ridx, rank, base, rwgt.T, mod, layer, 5)


def _final_kernel(x_ref, g_ref, o_ref):
    o_ref[...] = _rms(x_ref[...], g_ref[...])


def _final_norm(geo, x, g):
    tr, d = geo.tr, geo.d
    x3 = x.reshape(geo.b, geo.n, d)
    return pl.pallas_call(
        _final_kernel,
        out_shape=jax.ShapeDtypeStruct((geo.b, geo.s, d), F32),
        grid=(geo.b, geo.s // tr),
        in_specs=[pl.BlockSpec((None, tr, d), lambda b, i: (b, i + geo.nct, 0)),
                  pl.BlockSpec((1, d), lambda b, i: (0, 0))],
        out_specs=pl.BlockSpec((None, tr, d), lambda b, i: (b, i, 0)),
        name="final_norm",
        compiler_params=_cparams("parallel", "parallel"),
    )(x3, g.reshape(1, d))


def _pad_cols(w, width):
    return jnp.pad(w, [(0, 0)] * (w.ndim - 1) + [(0, width - w.shape[-1])])


def _prep_w_in(w_in, d):
    src = _src_layout(d)
    off, total = _proj_layout(d)
    cut = lambda name: w_in[..., src[name][0]:src[name][1]]
    cf, sc = cut("cf"), cut("sc")
    gate = jnp.concatenate([cut("gf"), cut("gb")], axis=-1)
    pieces = {"merge": cut("merge"), "sc_b": sc[..., :SC_WIDTH], "sc_c": sc[..., SC_WIDTH:2 * SC_WIDTH],
              "sc_h": sc[..., 2 * SC_WIDTH:], "cf_a": cf[..., :CF_WIDTH], "cf_b": cf[..., CF_WIDTH:],
              "cq": cut("cq"), "gla_v": cut("gla_v"), "gla_r": cut("gla_r"), "ckv": cut("ckv"),
              "gla_q": cut("gla_q"), "gla_k": cut("gla_k"), "kr": _pad_cols(cut("kr"), 128),
              "gate": _pad_cols(gate, 128)}
    order = sorted(off, key=off.get)
    out = jnp.concatenate([pieces[name] for name in order], axis=-1).astype(BF16)
    assert out.shape[-1] == total
    return out, off


def _rope_swap_perm():
    q = MLA_ROPE // 4
    return np.concatenate([np.arange(q, 2 * q), np.arange(0, q), np.arange(3 * q, 4 * q), np.arange(2 * q, 3 * q)])


def _prep_mla(w_uq, w_ukv):
    n_l = w_uq.shape[0]
    perm = _rope_swap_perm()
    wq = w_uq.reshape(n_l, MLA_Q_RANK, MLA_HEADS, MLA_NOPE + MLA_ROPE)
    qn, qr = wq[..., :MLA_NOPE], wq[..., MLA_NOPE:]
    zeros = lambda *s: jnp.zeros((n_l,) + s, w_uq.dtype)
    wqa = jnp.concatenate([qn, qr, qr], axis=-1)
    wqb = jnp.concatenate([zeros(MLA_Q_RANK, MLA_HEADS, MLA_NOPE), qr[..., perm],
                           zeros(MLA_Q_RANK, MLA_HEADS, MLA_ROPE)], axis=-1)
    wkv = w_ukv.reshape(n_l, MLA_KV_RANK, MLA_HEADS, MLA_NOPE + MLA_V)
    wka = jnp.concatenate([wkv[..., :MLA_NOPE], zeros(MLA_KV_RANK, MLA_HEADS, MLA_SLOT - MLA_NOPE)], axis=-1)
    wv = wkv[..., MLA_NOPE:]
    flat = lambda w: w.reshape(n_l, w.shape[1], -1).astype(BF16)
    sela = np.zeros((128, MLA_HEADS, MLA_SLOT), np.float32)
    selb = np.zeros((128, MLA_HEADS, MLA_SLOT), np.float32)
    for r in range(MLA_ROPE):
        sela[r, :, MLA_NOPE + r] = 1.0
        sela[r, :, MLA_NOPE + MLA_ROPE + r] = 1.0
        selb[perm[r], :, MLA_NOPE + r] = 1.0
    sel = lambda s: jnp.asarray(s.reshape(128, -1), BF16)
    return flat(wqa), flat(wqb), flat(wka), sel(sela), sel(selb), flat(wv)


def _rope_tables(geo):
    rows = geo.s // GRID_W
    pos = jnp.arange(rows * GRID_W)
    row = (pos // GRID_W).astype(F32)
    col = (pos % GRID_W).astype(F32)
    n_freq = MLA_ROPE // 4
    freqs = ROPE_THETA ** (-jnp.arange(n_freq, dtype=F32) / n_freq)
    cr, sr = jnp.cos(row[:, None] * freqs), jnp.sin(row[:, None] * freqs)
    cc, sn = jnp.cos(col[:, None] * freqs), jnp.sin(col[:, None] * freqs)
    cos32 = jnp.concatenate([cr, cr, cc, cc], axis=-1)
    sin32 = jnp.concatenate([-sr, sr, -sn, sn], axis=-1)
    s, c = geo.s, geo.c
    one = lambda n, w: jnp.ones((n, w), F32)
    zero = lambda n, w: jnp.zeros((n, w), F32)
    scale = (MLA_NOPE + MLA_ROPE) ** -0.5
    qca = jnp.concatenate([jnp.concatenate([one(c, 64), zero(c, 32), one(c, 32)], 1),
                           jnp.concatenate([one(s, 64), cos32, one(s, 32)], 1)], 0) * scale
    qsb = jnp.concatenate([zero(c, 128), jnp.concatenate([zero(s, 64), sin32, zero(s, 32)], 1)], 0) * scale
    kca = jnp.concatenate([jnp.concatenate([one(c, 64), zero(c, 32), one(c, 32)], 1),
                           jnp.concatenate([one(s, 64), cos32, zero(s, 32)], 1)], 0)
    ksb = jnp.concatenate([zero(c, 128), jnp.concatenate([zero(s, 64), sin32, zero(s, 32)], 1)], 0)
    return qca, qsb, kca, ksb


def kernel(x, c, ctx, c_ctx, w_mod, b_mod, g_mix, g_ffn, w_in, g_q, w_uq, g_kv, w_ukv, w_mla_o, w_cf_dw, b_cf_dw, g_cf_ln, b_cf_ln, w_cf_o, w_gla_gf, b_gla_gf, w_gla_gb, b_gla_gb, g_gla_norm, w_gla_o, w_sc_conv, w_sc_o, w_mix_o, w_router, b_router, w_e_gate, w_e_up, w_e_down, g_final):
    bsz, seq, d = x.shape
    n_ctx = ctx.shape[1]
    depth = w_in.shape[0]
    geo = _Geom(bsz, n_ctx, seq, d)
    assert bsz + 1 <= 8 and seq % GRID_W == 0 and n_ctx % GLA_CHUNK == 0 and seq % GLA_CHUNK == 0

    w_in_p, off = _prep_w_in(w_in, d)
    wqa, wqb, wka, sela, selb, wv = _prep_mla(w_uq, w_ukv)
    tabs = _rope_tables(geo)
    w_br = jnp.stack([w_mla_o, w_cf_o, w_gla_o, w_sc_o], axis=1).astype(BF16)
    w_mix = w_mix_o.astype(BF16)
    gate_w = jnp.zeros((depth, 2, 128, GLA_HEADS * GLA_DK), F32)
    gate_w = gate_w.at[:, 0, :GLA_GATE_RANK].set(w_gla_gf).at[:, 1, GLA_GATE_RANK:2 * GLA_GATE_RANK].set(w_gla_gb)
    gate_w = gate_w.astype(BF16)
    gate_b = jnp.stack([b_gla_gf, b_gla_gb], axis=1)[:, :, None, :]
    mat3, msk4 = _gla_tables()
    wr_t = w_router.T
    wr_hi = wr_t.astype(BF16)
    wr_stack = jnp.concatenate([wr_hi, (wr_t - wr_hi.astype(F32)).astype(BF16)], axis=0)
    vec = lambda a: a.reshape(a.shape[0], 1, a.shape[-1])

    cc = jnp.zeros((8, d), F32).at[:bsz].set(c).at[bsz].set(c_ctx)
    mod = _modulation(cc, w_mod, b_mod).reshape(depth, 8, 1, 6 * d)

    xa = jnp.concatenate([ctx, x], axis=1).reshape(geo.m, d)
    for l in range(depth):
        h = _norm_mod(geo, xa, vec(g_mix), mod, l, 0, 1)
        u = _matmul(h, w_in_p, l, BF16)
        q, k, v = _mla_qkv(geo, u, off, l, vec(g_q), vec(g_kv), wqa, wqb, wka, sela, selb, wv, tabs)
        attn = _mla_attention(geo, q, k, v)
        conf = _conformer(geo, u, off, l, w_cf_dw, vec(b_cf_dw), vec(g_cf_ln), vec(b_cf_ln))
        sconv = _shortconv(geo, u, off, l, w_sc_conv)
        gla2 = _gla_scan(geo, u, off, l, gate_w, gate_b, mat3, msk4)
        y = _merge(geo, attn, conf, gla2, u, off, vec(g_gla_norm), sconv, w_br, l)
        xa = _matmul_residual(geo, y, w_mix, l, xa, mod, 2)
        xa = _moe(geo, xa, vec(g_ffn), mod, l, wr_stack, b_router.reshape(N_EXPERTS, 1), w_e_gate, w_e_up, w_e_down)
    return _final_norm(geo, xa, g_final)
```

```python
import functools

import numpy as np
import jax
import jax.numpy as jnp
from jax import lax
from jax.experimental import pallas as pl
from jax.experimental.pallas import tpu as pltpu

F32 = jnp.float32
BF16 = jnp.bfloat16
EPS = 1e-6

GRID_W = 64
ROPE_THETA = 10000.0
MLA_HEADS = 8
MLA_NOPE = 64
MLA_ROPE = 32
MLA_V = 64
MLA_Q_RANK = 512
MLA_KV_RANK = 256
MLA_SLOT = 128
CF_WIDTH = 512
CF_KERNEL = 31
GLA_HEADS = 4
GLA_DK = 64
GLA_DV = 128
GLA_GATE_RANK = 16
GLA_GATE_TAU = 16.0
GLA_CHUNK = 64
GLA_LEVELS = 6
SC_WIDTH = 512
SC_KERNEL = 3
N_BRANCHES = 4
N_EXPERTS = 16
N_GROUPS = 4
EXPERT_FF = 512
CONV_HALO = 16
SUBLANES = 8
MERGE_COLS = 512
VMEM_LIMIT = 56 * 1024 * 1024


def _cparams(*sem):
    return pltpu.CompilerParams(dimension_semantics=sem, vmem_limit_bytes=VMEM_LIMIT)


def _pick(n, cands):
    for c in cands:
        if n % c == 0:
            return c
    raise ValueError(f"no tile for {n}")


def _dot(a, b):
    return jnp.dot(a, b, preferred_element_type=F32)


def _dot_nt(a, b):
    return lax.dot_general(a, b, (((1,), (1,)), ((), ())), preferred_element_type=F32)


def _proj_layout(d):
    parts = [("merge", N_BRANCHES * d), ("sc_b", SC_WIDTH), ("sc_c", SC_WIDTH), ("sc_h", SC_WIDTH),
             ("cf_a", CF_WIDTH), ("cf_b", CF_WIDTH), ("cq", MLA_Q_RANK), ("gla_v", GLA_HEADS * GLA_DV),
             ("gla_r", GLA_HEADS * GLA_DV), ("ckv", MLA_KV_RANK), ("gla_q", GLA_HEADS * GLA_DK),
             ("gla_k", GLA_HEADS * GLA_DK), ("kr", 128), ("gate", 128)]
    off, start = {}, 0
    for name, width in parts:
        assert start % width == 0 or name == "merge"
        off[name] = start
        start += width
    return off, start


def _src_layout(d):
    parts = (("cq", MLA_Q_RANK), ("ckv", MLA_KV_RANK), ("kr", MLA_ROPE), ("cf", 2 * CF_WIDTH),
             ("gla_q", GLA_HEADS * GLA_DK), ("gla_k", GLA_HEADS * GLA_DK), ("gla_v", GLA_HEADS * GLA_DV),
             ("gf", GLA_GATE_RANK), ("gb", GLA_GATE_RANK), ("gla_r", GLA_HEADS * GLA_DV),
             ("sc", 3 * SC_WIDTH), ("merge", N_BRANCHES * d))
    off, start = {}, 0
    for name, width in parts:
        off[name] = (start, start + width)
        start += width
    return off


def _mod_kernel(c_ref, w_ref, b_ref, o_ref):
    c = c_ref[...]
    a = (c * jax.nn.sigmoid(c)).astype(BF16)
    o_ref[...] = _dot(a, w_ref[...].astype(BF16)) + b_ref[...]


def _modulation(cc, w_mod, b_mod):
    n_l, d, n6 = w_mod.shape
    tn = _pick(n6, (1024, 512, 256, 128))
    return pl.pallas_call(
        _mod_kernel,
        out_shape=jax.ShapeDtypeStruct((n_l, 8, n6), F32),
        grid=(n_l, n6 // tn),
        in_specs=[pl.BlockSpec((8, d), lambda l, j: (0, 0)),
                  pl.BlockSpec((None, d, tn), lambda l, j: (l, 0, j)),
                  pl.BlockSpec((None, 1, tn), lambda l, j: (l, 0, j))],
        out_specs=pl.BlockSpec((None, 8, tn), lambda l, j: (l, 0, j)),
        name="adaln_mod",
        compiler_params=_cparams("parallel", "parallel"),
    )(cc, w_mod, b_mod.reshape(n_l, 1, n6))


class _Geom:
    def __init__(self, bsz, n_ctx, seq, d):
        self.b, self.c, self.s, self.d = bsz, n_ctx, seq, d
        self.n = n_ctx + seq
        self.m = bsz * self.n
        self.tr = _pick(int(np.gcd(n_ctx, seq)), (256, 128, 64))
        self.nt = self.n // self.tr
        self.nct = n_ctx // self.tr

    def mod_row(self, i):
        b, j = i // self.nt, i % self.nt
        return jnp.where(j < self.nct, self.b, b)

    def mod_spec(self, layer, part):
        return pl.BlockSpec((None, None, 1, self.d), lambda i, *_: (layer, self.mod_row(i), 0, part))


def _norm_mod_kernel(x_ref, g_ref, sh_ref, sc_ref, o_ref):
    x = x_ref[...]
    y = x * lax.rsqrt(jnp.mean(x * x, axis=-1, keepdims=True) + EPS) * g_ref[...]
    o_ref[...] = (y * (1.0 + sc_ref[...]) + sh_ref[...]).astype(o_ref.dtype)


def _norm_mod(geo, x, g, mod, layer, sh_part, sc_part):
    tr, d = geo.tr, geo.d
    return pl.pallas_call(
        _norm_mod_kernel,
        out_shape=jax.ShapeDtypeStruct((geo.m, d), BF16),
        grid=(geo.m // tr,),
        in_specs=[pl.BlockSpec((tr, d), lambda i: (i, 0)),
                  pl.BlockSpec((None, 1, d), lambda i: (layer, 0, 0)),
                  geo.mod_spec(layer, sh_part), geo.mod_spec(layer, sc_part)],
        out_specs=pl.BlockSpec((tr, d), lambda i: (i, 0)),
        name="norm_mod",
        compiler_params=_cparams("parallel"),
    )(x, g, mod, mod)


def _mm_kernel(a_ref, w_ref, o_ref):
    o_ref[...] = _dot(a_ref[...], w_ref[...]).astype(o_ref.dtype)


def _matmul(a, w, layer, out_dtype):
    m, k = a.shape
    n = w.shape[-1]
    tm = _pick(m, (1152, 1024, 512, 256, 128))
    tn = _pick(n, (1024, 512, 256, 128))
    return pl.pallas_call(
        _mm_kernel,
        out_shape=jax.ShapeDtypeStruct((m, n), out_dtype),
        grid=(n // tn, m // tm),
        in_specs=[pl.BlockSpec((tm, k), lambda j, i: (i, 0)),
                  pl.BlockSpec((None, k, tn), lambda j, i: (layer, 0, j))],
        out_specs=pl.BlockSpec((tm, tn), lambda j, i: (i, j)),
        name="proj_in",
        compiler_params=_cparams("parallel", "parallel"),
    )(a, w)


def _rms(x, g):
    return x * lax.rsqrt(jnp.mean(x * x, axis=-1, keepdims=True) + EPS) * g


def _qkv_kernel(cq_ref, ckv_ref, kr_ref, gq_ref, gkv_ref, wqa_ref, wqb_ref, wka_ref, sela_ref, selb_ref,
                wv_ref, qca_ref, qsb_ref, kca_ref, ksb_ref, q_ref, k_ref, v_ref):
    cqn = _rms(cq_ref[...].astype(F32), gq_ref[...]).astype(BF16)
    ckvn = _rms(ckv_ref[...].astype(F32), gkv_ref[...]).astype(BF16)
    kr = kr_ref[...]
    qa = _dot(cqn, wqa_ref[...])
    qb = _dot(cqn, wqb_ref[...])
    ka = _dot(ckvn, wka_ref[...]) + _dot(kr, sela_ref[...])
    kb = _dot(kr, selb_ref[...])
    qca, qsb, kca, ksb = qca_ref[...], qsb_ref[...], kca_ref[...], ksb_ref[...]
    for h in range(MLA_HEADS):
        sl = slice(h * MLA_SLOT, (h + 1) * MLA_SLOT)
        q_ref[:, sl] = (qa[:, sl] * qca + qb[:, sl] * qsb).astype(BF16)
        k_ref[:, sl] = (ka[:, sl] * kca + kb[:, sl] * ksb).astype(BF16)
    v_ref[...] = _dot(ckvn, wv_ref[...]).astype(BF16)


def _mla_qkv(geo, u, off, layer, gq, gkv, wqa, wqb, wka, sela, selb, wv, tabs):
    tr, m = geo.tr, geo.m
    hq = MLA_HEADS * MLA_SLOT
    hv = MLA_HEADS * MLA_V
    row = lambda w, o: pl.BlockSpec((tr, w), lambda i: (i, o // w))
    lw = lambda a: pl.BlockSpec((None,) + a.shape[1:], lambda i: (layer,) + (0,) * (a.ndim - 1))
    cw = lambda a: pl.BlockSpec(a.shape, lambda i: (0,) * a.ndim)
    tab = pl.BlockSpec((tr, MLA_SLOT), lambda i: (i % geo.nt, 0))
    return pl.pallas_call(
        _qkv_kernel,
        out_shape=(jax.ShapeDtypeStruct((m, hq), BF16), jax.ShapeDtypeStruct((m, hq), BF16),
                   jax.ShapeDtypeStruct((m, hv), BF16)),
        grid=(m // tr,),
        in_specs=[row(MLA_Q_RANK, off["cq"]), row(MLA_KV_RANK, off["ckv"]), row(128, off["kr"]),
                  lw(gq), lw(gkv), lw(wqa), lw(wqb), lw(wka), cw(sela), cw(selb), lw(wv), tab, tab, tab, tab],
        out_specs=(pl.BlockSpec((tr, hq), lambda i: (i, 0)), pl.BlockSpec((tr, hq), lambda i: (i, 0)),
                   pl.BlockSpec((tr, hv), lambda i: (i, 0))),
        name="mla_qkv",
        compiler_params=_cparams("parallel"),
    )(u, u, u, gq, gkv, wqa, wqb, wka, sela, selb, wv, *tabs)


def _attn_kernel(q_ref, k_ref, v_ref, o_ref, *, n_ctx, n_all, n_ctx_tiles):
    qi = pl.program_id(2)
    tq = q_ref.shape[0]
    lane = lax.broadcasted_iota(jnp.int32, (tq, 2 * MLA_V), 1)

    def run(nk):
        outs = []
        for hh in range(2):
            sl = slice(hh * MLA_SLOT, (hh + 1) * MLA_SLOT)
            s = _dot_nt(q_ref[:, sl], k_ref[0:nk, sl])
            p = jnp.exp(s - jnp.max(s, axis=-1, keepdims=True))
            inv = 1.0 / jnp.sum(p, axis=-1, keepdims=True)
            outs.append(_dot(p.astype(BF16), v_ref[0:nk, :]) * inv)
        o_ref[...] = jnp.where(lane < MLA_V, outs[0], outs[1]).astype(o_ref.dtype)

    @pl.when(qi < n_ctx_tiles)
    def _():
        run(n_ctx)

    @pl.when(qi >= n_ctx_tiles)
    def _():
        run(n_all)


def _mla_attention(geo, q, k, v):
    tq, n, bsz = geo.tr, geo.n, geo.b
    q3, k3, v3 = (t.reshape(bsz, n, t.shape[-1]) for t in (q, k, v))
    kern = functools.partial(_attn_kernel, n_ctx=geo.c, n_all=n, n_ctx_tiles=geo.nct)
    out = pl.pallas_call(
        kern,
        out_shape=jax.ShapeDtypeStruct((bsz, n, MLA_HEADS * MLA_V), BF16),
        grid=(bsz, MLA_HEADS // 2, geo.nt),
        in_specs=[pl.BlockSpec((None, tq, 2 * MLA_SLOT), lambda b, hp, i: (b, i, hp)),
                  pl.BlockSpec((None, n, 2 * MLA_SLOT), lambda b, hp, i: (b, 0, hp)),
                  pl.BlockSpec((None, n, 2 * MLA_V), lambda b, hp, i: (b, 0, hp))],
        out_specs=pl.BlockSpec((None, tq, 2 * MLA_V), lambda b, hp, i: (b, i, hp)),
        name="mla_attn",
        compiler_params=_cparams("parallel", "parallel", "parallel"),
    )(q3, k3, v3)
    return out.reshape(geo.m, MLA_HEADS * MLA_V)


def _halo_flags(geo, i):
    j = i % geo.nt
    left = jnp.logical_and(j != 0, j != geo.nct)
    right = jnp.logical_and(j != geo.nct - 1, j != geo.nt - 1)
    return left.astype(F32), right.astype(F32)


def _conv_shifts(n_taps):
    base = CONV_HALO - n_taps // 2
    return sorted({(base + kk) % SUBLANES for kk in range(n_taps)} - {0})


def _conv_taps(pad_ref, sh_ref, w_ref, n_taps, rows):
    base = CONV_HALO - n_taps // 2
    shifts = _conv_shifts(n_taps)
    span = sh_ref.shape[1]
    for s, r in enumerate(shifts):
        sh_ref[s, :, :] = pad_ref[r:r + span, :]

    def window(o):
        a, r = o - o % SUBLANES, o % SUBLANES
        return pad_ref[a:a + rows, :] if r == 0 else sh_ref[shifts.index(r), a:a + rows, :]

    acc = w_ref[0:1, :] * window(base)
    for kk in range(1, n_taps):
        acc = acc + w_ref[kk:kk + 1, :] * window(base + kk)
    return acc


def _conv_scratch(tr, width, n_taps):
    return [pltpu.VMEM((tr + 2 * CONV_HALO, width), F32),
            pltpu.VMEM((len(_conv_shifts(n_taps)), tr + 2 * CONV_HALO - SUBLANES, width), F32)]


def _conf_kernel(a_ref, b_ref, ap_ref, bp_ref, an_ref, bn_ref, w_ref, bias_ref, g_ref, beta_ref, o_ref,
                 pad_ref, sh_ref, *, geo):
    left, right = _halo_flags(geo, pl.program_id(0))
    rows = a_ref.shape[0]
    glu = lambda a, b: a[...].astype(F32) * jax.nn.sigmoid(b[...].astype(F32))
    pad_ref[0:CONV_HALO, :] = glu(ap_ref, bp_ref) * left
    pad_ref[CONV_HALO:CONV_HALO + rows, :] = glu(a_ref, b_ref)
    pad_ref[CONV_HALO + rows:2 * CONV_HALO + rows, :] = glu(an_ref, bn_ref) * right
    h = _conv_taps(pad_ref, sh_ref, w_ref, CF_KERNEL, rows) + bias_ref[...]
    hc = h - jnp.mean(h, axis=-1, keepdims=True)
    y = hc * lax.rsqrt(jnp.mean(hc * hc, axis=-1, keepdims=True) + EPS) * g_ref[...] + beta_ref[...]
    o_ref[...] = (y * jax.nn.sigmoid(y)).astype(o_ref.dtype)


def _sconv_kernel(gb_ref, gc_ref, h_ref, gcp_ref, hp_ref, gcn_ref, hn_ref, w_ref, o_ref, pad_ref, sh_ref, *, geo):
    left, right = _halo_flags(geo, pl.program_id(0))
    rows = h_ref.shape[0]
    prod = lambda a, b: a[...].astype(F32) * b[...].astype(F32)
    pad_ref[0:CONV_HALO, :] = prod(gcp_ref, hp_ref) * left
    pad_ref[CONV_HALO:CONV_HALO + rows, :] = prod(gc_ref, h_ref)
    pad_ref[CONV_HALO + rows:2 * CONV_HALO + rows, :] = prod(gcn_ref, hn_ref) * right
    y = _conv_taps(pad_ref, sh_ref, w_ref, SC_KERNEL, rows)
    o_ref[...] = (gb_ref[...].astype(F32) * y).astype(o_ref.dtype)


def _halo_specs(geo, width, offset):
    tr = geo.tr
    per = tr // CONV_HALO
    last = geo.m // CONV_HALO - 1
    cur = pl.BlockSpec((tr, width), lambda i: (i, offset // width))
    prev = pl.BlockSpec((CONV_HALO, width), lambda i: (jnp.maximum(i * per - 1, 0), offset // width))
    nxt = pl.BlockSpec((CONV_HALO, width), lambda i: (jnp.minimum((i + 1) * per, last), offset // width))
    return cur, prev, nxt


def _conformer(geo, u, off, layer, w_dw, b_dw, g_ln, b_ln):
    tr = geo.tr
    a_c, a_p, a_n = _halo_specs(geo, CF_WIDTH, off["cf_a"])
    b_c, b_p, b_n = _halo_specs(geo, CF_WIDTH, off["cf_b"])
    lw = lambda a: pl.BlockSpec((None,) + a.shape[1:], lambda i: (layer,) + (0,) * (a.ndim - 1))
    return pl.pallas_call(
        functools.partial(_conf_kernel, geo=geo),
        out_shape=jax.ShapeDtypeStruct((geo.m, CF_WIDTH), BF16),
        grid=(geo.m // tr,),
        in_specs=[a_c, b_c, a_p, b_p, a_n, b_n, lw(w_dw), lw(b_dw), lw(g_ln), lw(b_ln)],
        out_specs=pl.BlockSpec((tr, CF_WIDTH), lambda i: (i, 0)),
        scratch_shapes=_conv_scratch(tr, CF_WIDTH, CF_KERNEL),
        name="conformer",
        compiler_params=_cparams("parallel"),
    )(u, u, u, u, u, u, w_dw, b_dw, g_ln, b_ln)


def _shortconv(geo, u, off, layer, w_conv):
    tr = geo.tr
    gb_c, _, _ = _halo_specs(geo, SC_WIDTH, off["sc_b"])
    gc_c, gc_p, gc_n = _halo_specs(geo, SC_WIDTH, off["sc_c"])
    h_c, h_p, h_n = _halo_specs(geo, SC_WIDTH, off["sc_h"])
    lw = lambda a: pl.BlockSpec((None,) + a.shape[1:], lambda i: (layer,) + (0,) * (a.ndim - 1))
    return pl.pallas_call(
        functools.partial(_sconv_kernel, geo=geo),
        out_shape=jax.ShapeDtypeStruct((geo.m, SC_WIDTH), BF16),
        grid=(geo.m // tr,),
        in_specs=[gb_c, gc_c, h_c, gc_p, h_p, gc_n, h_n, lw(w_conv)],
        out_specs=pl.BlockSpec((tr, SC_WIDTH), lambda i: (i, 0)),
        scratch_shapes=_conv_scratch(tr, SC_WIDTH, SC_KERNEL),
        name="shortconv",
        compiler_params=_cparams("parallel"),
    )(u, u, u, u, u, u, u, w_conv)


def _gla_tables():
    c = GLA_CHUNK
    n_rows = (GLA_LEVELS + 2) * c + 8
    mat = np.zeros((2, n_rows, c), np.float32)
    msk = np.zeros((2, GLA_LEVELS + 1, c, c), np.float32)
    for lvl in range(GLA_LEVELS):
        hs = (c // 2) >> lvl
        for t in range(c):
            mid = (t // (2 * hs)) * 2 * hs + hs
            if t >= mid:
                mat[0, lvl * c + t, mid + 1:t + 1] = 1.0
                msk[0, lvl, t, mid - hs:mid] = 1.0
            else:
                mat[0, lvl * c + t, t + 1:mid + 1] = 1.0
    msk[0, GLA_LEVELS] = np.eye(c, dtype=np.float32)
    for t in range(c):
        mat[0, GLA_LEVELS * c + t, :t + 1] = 1.0
        mat[0, (GLA_LEVELS + 1) * c + t, t + 1:] = 1.0
    mat[0, (GLA_LEVELS + 2) * c:, :] = 1.0
    n_blk = GLA_LEVELS + 2
    mat[1, :n_blk * c] = mat[0, :n_blk * c].reshape(n_blk, c, c)[:, ::-1, ::-1].reshape(n_blk * c, c)
    mat[1, n_blk * c:] = 1.0
    msk[1] = msk[0][:, ::-1, ::-1]
    mat3 = np.concatenate([mat, mat, mat], axis=2)
    msk4 = np.tile(msk, (1, 1, GLA_HEADS, 1))
    return jnp.asarray(mat3, BF16), jnp.asarray(msk4, F32)


def _gla_kernel(q_ref, k_ref, v_ref, g_ref, w2_ref, b2_ref, mat_ref, msk_ref, o_ref, st_ref):
    @pl.when(pl.program_id(1) == 0)
    def _():
        st_ref[...] = jnp.zeros_like(st_ref)

    for b in range(q_ref.shape[0]):
        _gla_chunk(q_ref.at[b], k_ref.at[b], v_ref.at[b], g_ref.at[b], w2_ref, b2_ref, mat_ref, msk_ref,
                   o_ref.at[b], st_ref.at[b])


def _gla_chunk(q_ref, k_ref, v_ref, g_ref, w2_ref, b2_ref, mat_ref, msk_ref, o_ref, st_ref):
    c = GLA_CHUNK
    z = _dot(g_ref[...], w2_ref[...]) + b2_ref[...]
    la = (jnp.minimum(z, 0.0) - jnp.log1p(jnp.exp(-jnp.abs(z)))) * (1.0 / GLA_GATE_TAU)
    hi = la.astype(BF16)
    r1 = la - hi.astype(F32)
    lo = r1.astype(BF16)
    lo2 = (r1 - lo.astype(F32)).astype(BF16)
    ea = jnp.exp(_dot(mat_ref[...], jnp.concatenate([hi, lo, lo2], axis=0)))

    head = lax.broadcasted_iota(jnp.int32, (1, GLA_HEADS * GLA_DK), 1) // GLA_DK
    zero = jnp.zeros((), BF16)

    def stack(x):
        xb = x.astype(BF16)
        return jnp.concatenate([jnp.where(head == h, xb, zero) for h in range(GLA_HEADS)], axis=0)

    q = q_ref[...].astype(F32) * (GLA_DK ** -0.5)
    k = k_ref[...].astype(F32)
    vb = v_ref[...]

    attn = _dot_nt(stack(q), k.astype(BF16)) * msk_ref[GLA_LEVELS]
    for lvl in range(GLA_LEVELS):
        e = ea[lvl * c:(lvl + 1) * c]
        attn = attn + _dot_nt(stack(q * e), (k * e).astype(BF16)) * msk_ref[lvl]
    o_intra = _dot(attn.astype(BF16), vb)

    st = st_ref[...]
    o_inter = _dot_nt(stack(q * ea[GLA_LEVELS * c:(GLA_LEVELS + 1) * c]), st.astype(BF16))
    for h in range(GLA_HEADS):
        rows = slice(h * c, (h + 1) * c)
        cols = slice(h * GLA_DV, (h + 1) * GLA_DV)
        o_ref[:, cols] = o_intra[rows, cols] + o_inter[rows, :]

    k_dec = stack(k * ea[(GLA_LEVELS + 1) * c:(GLA_LEVELS + 2) * c])
    v_rows = jnp.concatenate([vb[:, h * GLA_DV:(h + 1) * GLA_DV] for h in range(GLA_HEADS)], axis=0)
    eye = (lax.broadcasted_iota(jnp.int32, (GLA_DV, GLA_DV), 0)
           == lax.broadcasted_iota(jnp.int32, (GLA_DV, GLA_DV), 1)).astype(BF16)
    v_t = _dot_nt(eye, v_rows).astype(BF16)
    tot = ea[(GLA_LEVELS + 2) * c:(GLA_LEVELS + 2) * c + 1]
    st_ref[...] = st * tot + _dot(v_t, k_dec)


def _gla_scan(geo, u, off, layer, w2, b2, mat3, msk4):
    c = GLA_CHUNK
    ncc = geo.c // c
    nc = geo.n // c
    dk, dv = GLA_HEADS * GLA_DK, GLA_HEADS * GLA_DV

    def chunk(d, i):
        bwd = jnp.where(i < ncc, ncc - 1 - i, nc - 1 - (i - ncc))
        return jnp.where(d == 0, i, bwd)

    u3 = u.reshape(geo.b, geo.n, u.shape[-1])
    row = lambda w, o: pl.BlockSpec((geo.b, c, w), lambda d, i: (0, chunk(d, i), o // w))
    out = pl.pallas_call(
        _gla_kernel,
        out_shape=jax.ShapeDtypeStruct((2, geo.b, geo.n, dv), F32),
        grid=(2, nc),
        in_specs=[row(dk, off["gla_q"]), row(dk, off["gla_k"]), row(dv, off["gla_v"]), row(128, off["gate"]),
                  pl.BlockSpec((None, None, 128, dk), lambda d, i: (layer, d, 0, 0)),
                  pl.BlockSpec((None, None, 1, dk), lambda d, i: (layer, d, 0, 0)),
                  pl.BlockSpec((None,) + mat3.shape[1:], lambda d, i: (d, 0, 0)),
                  pl.BlockSpec((None,) + msk4.shape[1:], lambda d, i: (d, 0, 0, 0))],
        out_specs=pl.BlockSpec((None, geo.b, c, dv), lambda d, i: (d, 0, chunk(d, i), 0)),
        scratch_shapes=[pltpu.VMEM((geo.b, GLA_DV, dk), F32)],
        name="gla_scan",
        compiler_params=_cparams("parallel", "arbitrary"),
    )(u3, u3, u3, u3, w2, b2, mat3, msk4)
    return out.reshape(2, geo.m, dv)


def _merge_mix_kernel(attn_ref, conf_ref, gla_ref, r_ref, gn_ref, sc_ref, gate_ref, wbr_ref, wmix_ref, x_ref,
                      gt_ref, o_ref, y_ref):
    o = gla_ref[0] + gla_ref[1]
    r = r_ref[...].astype(F32)
    gn = gn_ref[...]
    normed = [_rms(o[:, h * GLA_DV:(h + 1) * GLA_DV], gn) for h in range(GLA_HEADS)]
    gla = (jnp.concatenate(normed, axis=1) * (r * jax.nn.sigmoid(r))).astype(BF16)
    acts = (attn_ref[...], conf_ref[...], gla, sc_ref[...])
    d = x_ref.shape[1]
    cw = min(MERGE_COLS, d)
    for c0 in range(0, d, cw):
        y = None
        for i in range(N_BRANCHES):
            half_b = _dot(acts[i], wbr_ref[i, :, c0:c0 + cw])
            t = jnp.tanh(gate_ref[:, i * d + c0:i * d + c0 + cw].astype(F32))
            term = t * half_b + half_b
            y = term if y is None else y + term
        y_ref[:, c0:c0 + cw] = y.astype(BF16)
    y_all = y_ref[...]
    for c0 in range(0, d, cw):
        cols = slice(c0, c0 + cw)
        o_ref[:, cols] = x_ref[:, cols] + gt_ref[:, cols] * _dot(y_all, wmix_ref[:, cols])


def _merge_mix(geo, attn, conf, gla2, u, off, g_norm, sconv, w_br_half, w_mix, x, mod, layer):
    tm, d, m = geo.tr, geo.d, geo.m
    assert off["merge"] == 0
    w512 = GLA_HEADS * GLA_DV
    act = pl.BlockSpec((tm, w512), lambda i: (i, 0))
    once = pl.Buffered(1)
    return pl.pallas_call(
        _merge_mix_kernel,
        out_shape=jax.ShapeDtypeStruct((m, d), F32),
        grid=(m // tm,),
        in_specs=[act, act, pl.BlockSpec((2, tm, w512), lambda i: (0, i, 0)),
                  pl.BlockSpec((tm, w512), lambda i: (i, off["gla_r"] // w512)),
                  pl.BlockSpec((None, 1, GLA_DV), lambda i: (layer, 0, 0)),
                  act, pl.BlockSpec((tm, N_BRANCHES * d), lambda i: (i, 0)),
                  pl.BlockSpec((None, N_BRANCHES, w512, d), lambda i: (layer, 0, 0, 0), pipeline_mode=once),
                  pl.BlockSpec((None, d, d), lambda i: (layer, 0, 0), pipeline_mode=once),
                  pl.BlockSpec((tm, d), lambda i: (i, 0)), geo.mod_spec(layer, 2)],
        out_specs=pl.BlockSpec((tm, d), lambda i: (i, 0)),
        scratch_shapes=[pltpu.VMEM((tm, d), BF16)],
        name="merge_mix",
        compiler_params=_cparams("parallel"),
    )(attn, conf, gla2, u, g_norm, sconv, u, w_br_half, w_mix, x, mod)


def _second_largest(a, b, c, d):
    return jnp.maximum(jnp.maximum(jnp.minimum(a, b), jnp.minimum(c, d)),
                       jnp.minimum(jnp.maximum(a, b), jnp.maximum(c, d)))


def _chunk_rows(rows, n, cc):
    return pl.ds(cc, rows, stride=n) if n > 1 else pl.ds(0, rows)


def _pack_rows(ref, y):
    rows = y.shape[0]
    n = ref.shape[0] // rows
    half = n * 128
    bits = lambda v: pltpu.bitcast(v.astype(BF16).astype(F32), jnp.uint32)
    packed = (bits(y[:, :half]) >> 16) | bits(y[:, half:])
    for cc in range(n):
        ref[_chunk_rows(rows, n, cc), :] = packed[:, cc * 128:(cc + 1) * 128]


def _unpack_cols(words):
    lo = pltpu.bitcast(words << 16, F32)
    hi = pltpu.bitcast(words & jnp.uint32(0xFFFF0000), F32)
    return lo, hi


def _ffn_pre_kernel(x_ref, g_ref, sh_ref, sc_ref, wr_ref, br_ref, tri_ref, h_ref, idx_ref, wgt_ref, rank_ref,
                    cnt_ref, carry_ref):
    @pl.when(pl.program_id(0) == 0)
    def _():
        carry_ref[...] = jnp.zeros_like(carry_ref)

    x = x_ref[...]
    y = _rms(x, g_ref[...]) * (1.0 + sc_ref[...]) + sh_ref[...]
    hi = y.astype(BF16)
    lo = (y - hi.astype(F32)).astype(BF16)
    _pack_rows(h_ref, y)
    ne = N_EXPERTS
    two = _dot_nt(wr_ref[...], hi)
    logits = two[0:ne] + two[ne:2 * ne] + _dot_nt(wr_ref[0:ne, :], lo)
    aff = 1.0 / (1.0 + jnp.exp(-logits))
    sel = aff + br_ref[...]
    per = ne // N_GROUPS
    gs = []
    for g in range(N_GROUPS):
        a, b, c, d = (sel[g * per + r:g * per + r + 1] for r in range(per))
        top1 = jnp.maximum(jnp.maximum(a, b), jnp.maximum(c, d))
        gs.append(top1 + _second_largest(a, b, c, d))
    best = gs[0]
    for g in range(1, N_GROUPS):
        best = jnp.maximum(best, gs[g])
    grp = jnp.full(best.shape, N_GROUPS, jnp.int32)
    for g in reversed(range(N_GROUPS)):
        grp = jnp.where(gs[g] == best, g, grp)
    erow = lax.broadcasted_iota(jnp.int32, sel.shape, 0)
    masked = jnp.where(erow // per == grp, sel, -jnp.inf)
    m1 = jnp.max(masked, axis=0, keepdims=True)
    i1 = jnp.min(jnp.where(masked == m1, erow, ne), axis=0, keepdims=True)
    one1 = erow == i1
    masked2 = jnp.where(one1, -jnp.inf, masked)
    m2 = jnp.max(masked2, axis=0, keepdims=True)
    i2 = jnp.min(jnp.where(masked2 == m2, erow, ne), axis=0, keepdims=True)
    one2 = erow == i2
    w1 = jnp.sum(jnp.where(one1, aff, 0.0), axis=0, keepdims=True)
    w2 = jnp.sum(jnp.where(one2, aff, 0.0), axis=0, keepdims=True)
    inv = 1.0 / (w1 + w2)
    idx_ref[0:1, :] = i1
    idx_ref[1:2, :] = i2
    wgt_ref[0:1, :] = w1 * inv
    wgt_ref[1:2, :] = w2 * inv
    o1 = jnp.where(one1, 1.0, 0.0)
    o2 = jnp.where(one2, 1.0, 0.0)
    tri = tri_ref[...]
    c0 = carry_ref[...]
    c1 = c0 + jnp.sum(o1, axis=1, keepdims=True)
    rank_ref[0:1, :] = jnp.sum(o1 * (c0 + _dot(o1.astype(BF16), tri)), axis=0, keepdims=True).astype(jnp.int32)
    rank_ref[1:2, :] = jnp.sum(o2 * (c1 + _dot(o2.astype(BF16), tri)), axis=0, keepdims=True).astype(jnp.int32)
    c2 = c1 + jnp.sum(o2, axis=1, keepdims=True)
    carry_ref[...] = c2
    cnt_ref[...] = c2


def _ffn_pre(geo, x, g, mod, layer, wr_stack, b_router):
    tr, d, m = geo.tr, geo.d, geo.m
    nch = d // 256
    tri = jnp.asarray(np.triu(np.ones((tr, tr), np.float32), 1), BF16)
    pair = lambda dt: jax.ShapeDtypeStruct((2, m), dt)
    pair_spec = pl.BlockSpec((2, tr), lambda i: (0, i))
    return pl.pallas_call(
        _ffn_pre_kernel,
        out_shape=(jax.ShapeDtypeStruct((m * nch, 128), jnp.uint32), pair(jnp.int32), pair(F32), pair(jnp.int32),
                   jax.ShapeDtypeStruct((N_EXPERTS, 1), F32)),
        grid=(m // tr,),
        in_specs=[pl.BlockSpec((tr, d), lambda i: (i, 0)),
                  pl.BlockSpec((None, 1, d), lambda i: (layer, 0, 0)),
                  geo.mod_spec(layer, 3), geo.mod_spec(layer, 4),
                  pl.BlockSpec(wr_stack.shape, lambda i: (0, 0)),
                  pl.BlockSpec(b_router.shape, lambda i: (0, 0)),
                  pl.BlockSpec((tr, tr), lambda i: (0, 0))],
        out_specs=(pl.BlockSpec((tr * nch, 128), lambda i: (i, 0)), pair_spec, pair_spec, pair_spec,
                   pl.BlockSpec((N_EXPERTS, 1), lambda i: (0, 0))),
        scratch_shapes=[pltpu.VMEM((N_EXPERTS, 1), F32)],
        name="ffn_pre",
        compiler_params=_cparams("arbitrary"),
    )(x, g, mod, mod, wr_stack, b_router, tri)


def _sorted_row(idx_ref, rank_ref, base_ref, a, n):
    return pl.multiple_of((base_ref[idx_ref[a]] + rank_ref[a]) * n, n)


def _dispatch_kernel(idx_ref, rank_ref, base_ref, h_ref, z_hbm, xs_hbm, sem, *, m, n):
    del z_hbm
    i = pl.program_id(0)
    tr = h_ref.shape[0] // n

    def issue(r, carry):
        src = h_ref.at[pl.ds(pl.multiple_of(r * n, n), n), :]
        for k in range(2):
            row = _sorted_row(idx_ref, rank_ref, base_ref, k * m + i * tr + r, n)
            pltpu.make_async_copy(src, xs_hbm.at[pl.ds(row, n), :], sem).start()
        return carry

    lax.fori_loop(0, tr, issue, 0, unroll=4)
    everything = xs_hbm.at[pl.ds(0, 2 * tr * n), :]
    pltpu.make_async_copy(everything, everything, sem).wait()


def _dispatch(hfp, ridx, rank, base, n_rows, tr):
    n = hfp.shape[0] * 2 // ridx.shape[0]
    m = hfp.shape[0] // n
    zeros = jnp.zeros((n_rows * n, 128), jnp.uint32)
    any_spec = pl.BlockSpec(memory_space=pl.ANY)
    return pl.pallas_call(
        functools.partial(_dispatch_kernel, m=m, n=n),
        out_shape=jax.ShapeDtypeStruct(zeros.shape, jnp.uint32),
        grid_spec=pltpu.PrefetchScalarGridSpec(
            num_scalar_prefetch=3, grid=(m // tr,),
            in_specs=[pl.BlockSpec((tr * n, 128), lambda i, *_: (i, 0)), any_spec], out_specs=any_spec,
            scratch_shapes=[pltpu.SemaphoreType.DMA(())]),
        input_output_aliases={4: 0},
        name="moe_dispatch",
        compiler_params=_cparams("arbitrary"),
    )(ridx, rank, base, hfp, zeros)


def _gmm_kernel(te_ref, nv_ref, xs_ref, wg_ref, wu_ref, wd_ref, ys_ref, wgb_ref, wub_ref, wdb_ref, prev_ref):
    j = pl.program_id(0)

    @pl.when(j == 0)
    def _():
        prev_ref[0] = -1

    @pl.when(j < nv_ref[0])
    def _():
        e = te_ref[j]

        @pl.when(e != prev_ref[0])
        def _():
            wgb_ref[...] = wg_ref[...].astype(BF16)
            wub_ref[...] = wu_ref[...].astype(BF16)
            wdb_ref[...] = wd_ref[...].astype(BF16)
            prev_ref[0] = e

        half = wgb_ref.shape[0] // 2
        n = half // 128
        tm = xs_ref.shape[0] // n
        parts = [_unpack_cols(xs_ref[_chunk_rows(tm, n, cc), :]) for cc in range(n)]
        xa = jnp.concatenate([p[0].astype(BF16) for p in parts], axis=1)
        xb = jnp.concatenate([p[1].astype(BF16) for p in parts], axis=1)
        g = _dot(xa, wgb_ref[0:half, :]) + _dot(xb, wgb_ref[half:2 * half, :])
        u = _dot(xa, wub_ref[0:half, :]) + _dot(xb, wub_ref[half:2 * half, :])
        act = (g * jax.nn.sigmoid(g) * u).astype(BF16)
        _pack_rows(ys_ref, _dot(act, wdb_ref[...]))

    @pl.when(j >= nv_ref[0])
    def _():
        ys_ref[...] = jnp.zeros_like(ys_ref)


def _gmm(xs, tile_expert, n_valid, wg, wu, wd, layer, tm):
    d, ff = wg.shape[-2], wg.shape[-1]
    nch = d // 256
    blk = tm * nch
    row = lambda j, te, nv: (jnp.minimum(j, nv[0] - 1), 0)
    wsel = lambda j, te, nv: (layer, te[jnp.minimum(j, nv[0] - 1)], 0, 0)
    return pl.pallas_call(
        _gmm_kernel,
        out_shape=jax.ShapeDtypeStruct(xs.shape, jnp.uint32),
        grid_spec=pltpu.PrefetchScalarGridSpec(
            num_scalar_prefetch=2, grid=(xs.shape[0] // blk,),
            in_specs=[pl.BlockSpec((blk, 128), row),
                      pl.BlockSpec((None, None, d, ff), wsel), pl.BlockSpec((None, None, d, ff), wsel),
                      pl.BlockSpec((None, None, ff, d), wsel)],
            out_specs=pl.BlockSpec((blk, 128), lambda j, te, nv: (j, 0)),
            scratch_shapes=[pltpu.VMEM((d, ff), BF16), pltpu.VMEM((d, ff), BF16), pltpu.VMEM((ff, d), BF16),
                            pltpu.SMEM((1,), jnp.int32)]),
        name="moe_experts",
        compiler_params=_cparams("arbitrary"),
    )(tile_expert, n_valid, xs, wg, wu, wd)


def _combine_kernel(idx_ref, rank_ref, base_ref, x_ref, w_ref, gt_ref, ys_hbm, o_ref, buf_ref, sem_ref, *, m):
    i = pl.program_id(0)
    n_tiles = pl.num_programs(0)
    tr = x_ref.shape[0]
    n = buf_ref.shape[2] // tr

    def issue(tile, slot):
        def body(r, carry):
            for k in range(2):
                row = _sorted_row(idx_ref, rank_ref, base_ref, k * m + tile * tr + r, n)
                pltpu.make_async_copy(ys_hbm.at[pl.ds(row, n), :],
                                      buf_ref.at[slot, k, pl.ds(pl.multiple_of(r * n, n), n), :],
                                      sem_ref.at[slot]).start()
            return carry
        lax.fori_loop(0, tr, body, 0, unroll=4)

    @pl.when(i == 0)
    def _():
        issue(0, 0)

    @pl.when(i + 1 < n_tiles)
    def _():
        issue(i + 1, (i + 1) % 2)

    slot = i % 2
    pltpu.make_async_copy(buf_ref.at[slot], buf_ref.at[slot], sem_ref.at[slot]).wait()
    w1, w2 = w_ref[:, 0:1], w_ref[:, 1:2]
    half = n * 128
    for cc in range(n):
        lo1, hi1 = _unpack_cols(buf_ref[slot, 0, _chunk_rows(tr, n, cc), :])
        lo2, hi2 = _unpack_cols(buf_ref[slot, 1, _chunk_rows(tr, n, cc), :])
        ca = slice(cc * 128, (cc + 1) * 128)
        cb = slice(half + cc * 128, half + (cc + 1) * 128)
        o_ref[:, ca] = x_ref[:, ca] + gt_ref[:, ca] * (w1 * lo1 + w2 * lo2)
        o_ref[:, cb] = x_ref[:, cb] + gt_ref[:, cb] * (w1 * hi1 + w2 * hi2)


def _combine(geo, x, ys, ridx, rank, base, w12, mod, layer, gate_part):
    tr, d = geo.tr, geo.d
    nch = d // 256
    gate = pl.BlockSpec((None, None, 1, d), lambda i, *_: (layer, geo.mod_row(i), 0, gate_part))
    return pl.pallas_call(
        functools.partial(_combine_kernel, m=geo.m),
        out_shape=jax.ShapeDtypeStruct((geo.m, d), F32),
        grid_spec=pltpu.PrefetchScalarGridSpec(
            num_scalar_prefetch=3, grid=(geo.m // tr,),
            in_specs=[pl.BlockSpec((tr, d), lambda i, *_: (i, 0)),
                      pl.BlockSpec((tr, 2), lambda i, *_: (i, 0)), gate,
                      pl.BlockSpec(memory_space=pl.ANY)],
            out_specs=pl.BlockSpec((tr, d), lambda i, *_: (i, 0)),
            scratch_shapes=[pltpu.VMEM((2, 2, tr * nch, 128), jnp.uint32), pltpu.SemaphoreType.DMA((2,))]),
        name="moe_combine",
        compiler_params=_cparams("arbitrary"),
    )(ridx, rank, base, x, w12, mod, ys)


def _moe(geo, x, g, mod, layer, wr_stack, b_router, wg, wu, wd):
    tm = 256
    hfp, ridx, rwgt, rank, cnt = _ffn_pre(geo, x, g, mod, layer, wr_stack, b_router)
    n_tiles = (2 * geo.m) // tm + N_EXPERTS
    counts = cnt[:, 0].astype(jnp.int32)
    tiles_per = (counts + tm - 1) // tm
    tile_start = jnp.cumsum(tiles_per) - tiles_per
    base = tile_start * tm
    n_valid = jnp.sum(tiles_per).reshape(1)
    tile_expert = jnp.sum(jnp.arange(n_tiles)[:, None] >= tile_start[None, :], axis=1).astype(jnp.int32) - 1
    ridx, rank = ridx.reshape(-1), rank.reshape(-1)
    xs = _dispatch(hfp, ridx, rank, base, n_tiles * tm, geo.tr)
    ys = _gmm(xs, tile_expert, n_valid, wg, wu, wd, layer, tm)
    return _combine(geo, x, ys, ridx, rank, base, rwgt.T, mod, layer, 5)


def _final_kernel(x_ref, g_ref, o_ref):
    o_ref[...] = _rms(x_ref[...], g_ref[...])


def _final_norm(geo, x, g):
    tr, d = geo.tr, geo.d
    x3 = x.reshape(geo.b, geo.n, d)
    return pl.pallas_call(
        _final_kernel,
        out_shape=jax.ShapeDtypeStruct((geo.b, geo.s, d), F32),
        grid=(geo.b, geo.s // tr),
        in_specs=[pl.BlockSpec((None, tr, d), lambda b, i: (b, i + geo.nct, 0)),
                  pl.BlockSpec((1, d), lambda b, i: (0, 0))],
        out_specs=pl.BlockSpec((None, tr, d), lambda b, i: (b, i, 0)),
        name="final_norm",
        compiler_params=_cparams("parallel", "parallel"),
    )(x3, g.reshape(1, d))


def _pad_cols(w, width):
    return jnp.pad(w, [(0, 0)] * (w.ndim - 1) + [(0, width - w.shape[-1])])


def _prep_w_in(w_in, d):
    src = _src_layout(d)
    off, total = _proj_layout(d)
    cut = lambda name: w_in[..., src[name][0]:src[name][1]]
    cf, sc = cut("cf"), cut("sc")
    gate = jnp.concatenate([cut("gf"), cut("gb")], axis=-1)
    pieces = {"merge": cut("merge") * 0.5, "sc_b": sc[..., :SC_WIDTH], "sc_c": sc[..., SC_WIDTH:2 * SC_WIDTH],
              "sc_h": sc[..., 2 * SC_WIDTH:], "cf_a": cf[..., :CF_WIDTH], "cf_b": cf[..., CF_WIDTH:],
              "cq": cut("cq"), "gla_v": cut("gla_v"), "gla_r": cut("gla_r"), "ckv": cut("ckv"),
              "gla_q": cut("gla_q"), "gla_k": cut("gla_k"), "kr": _pad_cols(cut("kr"), 128),
              "gate": _pad_cols(gate, 128)}
    order = sorted(off, key=off.get)
    out = jnp.concatenate([pieces[name] for name in order], axis=-1).astype(BF16)
    assert out.shape[-1] == total
    return out, off


def _rope_swap_perm():
    q = MLA_ROPE // 4
    return np.concatenate([np.arange(q, 2 * q), np.arange(0, q), np.arange(3 * q, 4 * q), np.arange(2 * q, 3 * q)])


def _prep_mla(w_uq, w_ukv):
    n_l = w_uq.shape[0]
    perm = _rope_swap_perm()
    wq = w_uq.reshape(n_l, MLA_Q_RANK, MLA_HEADS, MLA_NOPE + MLA_ROPE)
    qn, qr = wq[..., :MLA_NOPE], wq[..., MLA_NOPE:]
    zeros = lambda *s: jnp.zeros((n_l,) + s, w_uq.dtype)
    wqa = jnp.concatenate([qn, qr, qr], axis=-1)
    wqb = jnp.concatenate([zeros(MLA_Q_RANK, MLA_HEADS, MLA_NOPE), qr[..., perm],
                           zeros(MLA_Q_RANK, MLA_HEADS, MLA_ROPE)], axis=-1)
    wkv = w_ukv.reshape(n_l, MLA_KV_RANK, MLA_HEADS, MLA_NOPE + MLA_V)
    wka = jnp.concatenate([wkv[..., :MLA_NOPE], zeros(MLA_KV_RANK, MLA_HEADS, MLA_SLOT - MLA_NOPE)], axis=-1)
    wv = wkv[..., MLA_NOPE:]
    flat = lambda w: w.reshape(n_l, w.shape[1], -1).astype(BF16)
    sela = np.zeros((128, MLA_HEADS, MLA_SLOT), np.float32)
    selb = np.zeros((128, MLA_HEADS, MLA_SLOT), np.float32)
    for r in range(MLA_ROPE):
        sela[r, :, MLA_NOPE + r] = 1.0
        sela[r, :, MLA_NOPE + MLA_ROPE + r] = 1.0
        selb[perm[r], :, MLA_NOPE + r] = 1.0
    sel = lambda s: jnp.asarray(s.reshape(128, -1), BF16)
    return flat(wqa), flat(wqb), flat(wka), sel(sela), sel(selb), flat(wv)


def _rope_tables(geo):
    rows = geo.s // GRID_W
    pos = jnp.arange(rows * GRID_W)
    row = (pos // GRID_W).astype(F32)
    col = (pos % GRID_W).astype(F32)
    n_freq = MLA_ROPE // 4
    freqs = ROPE_THETA ** (-jnp.arange(n_freq, dtype=F32) / n_freq)
    cr, sr = jnp.cos(row[:, None] * freqs), jnp.sin(row[:, None] * freqs)
    cc, sn = jnp.cos(col[:, None] * freqs), jnp.sin(col[:, None] * freqs)
    cos32 = jnp.concatenate([cr, cr, cc, cc], axis=-1)
    sin32 = jnp.concatenate([-sr, sr, -sn, sn], axis=-1)
    s, c = geo.s, geo.c
    one = lambda n, w: jnp.ones((n, w), F32)
    zero = lambda n, w: jnp.zeros((n, w), F32)
    scale = (MLA_NOPE + MLA_ROPE) ** -0.5
    qca = jnp.concatenate([jnp.concatenate([one(c, 64), zero(c, 32), one(c, 32)], 1),
                           jnp.concatenate([one(s, 64), cos32, one(s, 32)], 1)], 0) * scale
    qsb = jnp.concatenate([zero(c, 128), jnp.concatenate([zero(s, 64), sin32, zero(s, 32)], 1)], 0) * scale
    kca = jnp.concatenate([jnp.concatenate([one(c, 64), zero(c, 32), one(c, 32)], 1),
                           jnp.concatenate([one(s, 64), cos32, zero(s, 32)], 1)], 0)
    ksb = jnp.concatenate([zero(c, 128), jnp.concatenate([zero(s, 64), sin32, zero(s, 32)], 1)], 0)
    return qca, qsb, kca, ksb


def kernel(x, c, ctx, c_ctx, w_mod, b_mod, g_mix, g_ffn, w_in, g_q, w_uq, g_kv, w_ukv, w_mla_o, w_cf_dw, b_cf_dw, g_cf_ln, b_cf_ln, w_cf_o, w_gla_gf, b_gla_gf, w_gla_gb, b_gla_gb, g_gla_norm, w_gla_o, w_sc_conv, w_sc_o, w_mix_o, w_router, b_router, w_e_gate, w_e_up, w_e_down, g_final):
    bsz, seq, d = x.shape
    n_ctx = ctx.shape[1]
    depth = w_in.shape[0]
    geo = _Geom(bsz, n_ctx, seq, d)
    assert bsz + 1 <= 8 and seq % GRID_W == 0 and n_ctx % GLA_CHUNK == 0 and seq % GLA_CHUNK == 0

    w_in_p, off = _prep_w_in(w_in, d)
    wqa, wqb, wka, sela, selb, wv = _prep_mla(w_uq, w_ukv)
    tabs = _rope_tables(geo)
    w_br_half = (jnp.stack([w_mla_o, w_cf_o, w_gla_o, w_sc_o], axis=1) * 0.5).astype(BF16)
    w_mix = w_mix_o.astype(BF16)
    gate_w = jnp.zeros((depth, 2, 128, GLA_HEADS * GLA_DK), F32)
    gate_w = gate_w.at[:, 0, :GLA_GATE_RANK].set(w_gla_gf).at[:, 1, GLA_GATE_RANK:2 * GLA_GATE_RANK].set(w_gla_gb)
    gate_w = gate_w.astype(BF16)
    gate_b = jnp.stack([b_gla_gf, b_gla_gb], axis=1)[:, :, None, :]
    mat3, msk4 = _gla_tables()
    wr_t = w_router.T
    wr_hi = wr_t.astype(BF16)
    wr_stack = jnp.concatenate([wr_hi, (wr_t - wr_hi.astype(F32)).astype(BF16)], axis=0)
    vec = lambda a: a.reshape(a.shape[0], 1, a.shape[-1])

    cc = jnp.zeros((8, d), F32).at[:bsz].set(c).at[bsz].set(c_ctx)
    mod = _modulation(cc, w_mod, b_mod).reshape(depth, 8, 1, 6 * d)

    xa = jnp.concatenate([ctx, x], axis=1).reshape(geo.m, d)
    for l in range(depth):
        h = _norm_mod(geo, xa, vec(g_mix), mod, l, 0, 1)
        u = _matmul(h, w_in_p, l, BF16)
        q, k, v = _mla_qkv(geo, u, off, l, vec(g_q), vec(g_kv), wqa, wqb, wka, sela, selb, wv, tabs)
        attn = _mla_attention(geo, q, k, v)
        conf = _conformer(geo, u, off, l, w_cf_dw, vec(b_cf_dw), vec(g_cf_ln), vec(b_cf_ln))
        sconv = _shortconv(geo, u, off, l, w_sc_conv)
        gla2 = _gla_scan(geo, u, off, l, gate_w, gate_b, mat3, msk4)
        xa = _merge_mix(geo, attn, conf, gla2, u, off, vec(g_gla_norm), sconv, w_br_half, w_mix, xa, mod, l)
        xa = _moe(geo, xa, vec(g_ffn), mod, l, wr_stack, b_router.reshape(N_EXPERTS, 1), w_e_gate, w_e_up, w_e_down)
    return _final_norm(geo, xa, g_final)
```

```python
import functools

import numpy as np
import jax
import jax.numpy as jnp
from jax import lax
from jax.experimental import pallas as pl
from jax.experimental.pallas import tpu as pltpu

F32 = jnp.float32
BF16 = jnp.bfloat16
EPS = 1e-6

GRID_W = 64
ROPE_THETA = 10000.0
MLA_HEADS = 8
MLA_NOPE = 64
MLA_ROPE = 32
MLA_V = 64
MLA_Q_RANK = 512
MLA_KV_RANK = 256
MLA_SLOT = 128
CF_WIDTH = 512
CF_KERNEL = 31
GLA_HEADS = 4
GLA_DK = 64
GLA_DV = 128
GLA_GATE_RANK = 16
GLA_GATE_TAU = 16.0
GLA_CHUNK = 64
GLA_LEVELS = 6
SC_WIDTH = 512
SC_KERNEL = 3
N_BRANCHES = 4
N_EXPERTS = 16
N_GROUPS = 4
EXPERT_FF = 512
CONV_HALO = 16
SUBLANES = 8
MERGE_COLS = 512
VMEM_LIMIT = 56 * 1024 * 1024


def _cparams(*sem):
    return pltpu.CompilerParams(dimension_semantics=sem, vmem_limit_bytes=VMEM_LIMIT)


def _pick(n, cands):
    for c in cands:
        if n % c == 0:
            return c
    raise ValueError(f"no tile for {n}")


def _dot(a, b):
    return jnp.dot(a, b, preferred_element_type=F32)


def _dot_nt(a, b):
    return lax.dot_general(a, b, (((1,), (1,)), ((), ())), preferred_element_type=F32)


def _proj_layout(d):
    parts = [("merge", N_BRANCHES * d), ("sc_b", SC_WIDTH), ("sc_c", SC_WIDTH), ("sc_h", SC_WIDTH),
             ("cf_a", CF_WIDTH), ("cf_b", CF_WIDTH), ("cq", MLA_Q_RANK), ("gla_v", GLA_HEADS * GLA_DV),
             ("gla_r", GLA_HEADS * GLA_DV), ("ckv", MLA_KV_RANK), ("gla_q", GLA_HEADS * GLA_DK),
             ("gla_k", GLA_HEADS * GLA_DK), ("kr", 128), ("gate", 128)]
    off, start = {}, 0
    for name, width in parts:
        assert start % width == 0 or name == "merge"
        off[name] = start
        start += width
    return off, start


def _src_layout(d):
    parts = (("cq", MLA_Q_RANK), ("ckv", MLA_KV_RANK), ("kr", MLA_ROPE), ("cf", 2 * CF_WIDTH),
             ("gla_q", GLA_HEADS * GLA_DK), ("gla_k", GLA_HEADS * GLA_DK), ("gla_v", GLA_HEADS * GLA_DV),
             ("gf", GLA_GATE_RANK), ("gb", GLA_GATE_RANK), ("gla_r", GLA_HEADS * GLA_DV),
             ("sc", 3 * SC_WIDTH), ("merge", N_BRANCHES * d))
    off, start = {}, 0
    for name, width in parts:
        off[name] = (start, start + width)
        start += width
    return off


def _mod_kernel(c_ref, w_ref, b_ref, o_ref):
    c = c_ref[...]
    a = (c * jax.nn.sigmoid(c)).astype(BF16)
    o_ref[...] = _dot(a, w_ref[...].astype(BF16)) + b_ref[...]


def _modulation(cc, w_mod, b_mod):
    n_l, d, n6 = w_mod.shape
    tn = _pick(n6, (1024, 512, 256, 128))
    return pl.pallas_call(
        _mod_kernel,
        out_shape=jax.ShapeDtypeStruct((n_l, 8, n6), F32),
        grid=(n_l, n6 // tn),
        in_specs=[pl.BlockSpec((8, d), lambda l, j: (0, 0)),
                  pl.BlockSpec((None, d, tn), lambda l, j: (l, 0, j)),
                  pl.BlockSpec((None, 1, tn), lambda l, j: (l, 0, j))],
        out_specs=pl.BlockSpec((None, 8, tn), lambda l, j: (l, 0, j)),
        name="adaln_mod",
        compiler_params=_cparams("parallel", "parallel"),
    )(cc, w_mod, b_mod.reshape(n_l, 1, n6))


class _Geom:
    def __init__(self, bsz, n_ctx, seq, d):
        self.b, self.c, self.s, self.d = bsz, n_ctx, seq, d
        self.n = n_ctx + seq
        self.m = bsz * self.n
        self.tr = _pick(int(np.gcd(n_ctx, seq)), (256, 128, 64))
        self.nt = self.n // self.tr
        self.nct = n_ctx // self.tr

    def mod_row(self, i):
        b, j = i // self.nt, i % self.nt
        return jnp.where(j < self.nct, self.b, b)

    def mod_spec(self, layer, part):
        return pl.BlockSpec((None, None, 1, self.d), lambda i, *_: (layer, self.mod_row(i), 0, part))


def _norm_mod_kernel(x_ref, g_ref, sh_ref, sc_ref, o_ref):
    x = x_ref[...]
    y = x * lax.rsqrt(jnp.mean(x * x, axis=-1, keepdims=True) + EPS) * g_ref[...]
    o_ref[...] = (y * (1.0 + sc_ref[...]) + sh_ref[...]).astype(o_ref.dtype)


def _norm_mod(geo, x, g, mod, layer, sh_part, sc_part):
    tr, d = geo.tr, geo.d
    return pl.pallas_call(
        _norm_mod_kernel,
        out_shape=jax.ShapeDtypeStruct((geo.m, d), BF16),
        grid=(geo.m // tr,),
        in_specs=[pl.BlockSpec((tr, d), lambda i: (i, 0)),
                  pl.BlockSpec((None, 1, d), lambda i: (layer, 0, 0)),
                  geo.mod_spec(layer, sh_part), geo.mod_spec(layer, sc_part)],
        out_specs=pl.BlockSpec((tr, d), lambda i: (i, 0)),
        name="norm_mod",
        compiler_params=_cparams("parallel"),
    )(x, g, mod, mod)


def _mm_kernel(a_ref, w_ref, o_ref):
    o_ref[...] = _dot(a_ref[...], w_ref[...]).astype(o_ref.dtype)


def _matmul(a, w, layer, out_dtype):
    m, k = a.shape
    n = w.shape[-1]
    tm = _pick(m, (1152, 1024, 512, 256, 128))
    tn = _pick(n, (1024, 512, 256, 128))
    return pl.pallas_call(
        _mm_kernel,
        out_shape=jax.ShapeDtypeStruct((m, n), out_dtype),
        grid=(n // tn, m // tm),
        in_specs=[pl.BlockSpec((tm, k), lambda j, i: (i, 0)),
                  pl.BlockSpec((None, k, tn), lambda j, i: (layer, 0, j))],
        out_specs=pl.BlockSpec((tm, tn), lambda j, i: (i, j)),
        name="proj_in",
        compiler_params=_cparams("parallel", "parallel"),
    )(a, w)


def _rms(x, g):
    return x * lax.rsqrt(jnp.mean(x * x, axis=-1, keepdims=True) + EPS) * g


def _qkv_kernel(cq_ref, ckv_ref, kr_ref, gq_ref, gkv_ref, wqa_ref, wqb_ref, wka_ref, sela_ref, selb_ref,
                wv_ref, qca_ref, qsb_ref, kca_ref, ksb_ref, q_ref, k_ref, v_ref):
    cqn = _rms(cq_ref[...].astype(F32), gq_ref[...]).astype(BF16)
    ckvn = _rms(ckv_ref[...].astype(F32), gkv_ref[...]).astype(BF16)
    kr = kr_ref[...]
    qa = _dot(cqn, wqa_ref[...])
    qb = _dot(cqn, wqb_ref[...])
    ka = _dot(ckvn, wka_ref[...]) + _dot(kr, sela_ref[...])
    kb = _dot(kr, selb_ref[...])
    qca, qsb, kca, ksb = qca_ref[...], qsb_ref[...], kca_ref[...], ksb_ref[...]
    for h in range(MLA_HEADS):
        sl = slice(h * MLA_SLOT, (h + 1) * MLA_SLOT)
        q_ref[:, sl] = (qa[:, sl] * qca + qb[:, sl] * qsb).astype(BF16)
        k_ref[:, sl] = (ka[:, sl] * kca + kb[:, sl] * ksb).astype(BF16)
    v_ref[...] = _dot(ckvn, wv_ref[...]).astype(BF16)


def _mla_qkv(geo, u, off, layer, gq, gkv, wqa, wqb, wka, sela, selb, wv, tabs):
    tr, m = geo.tr, geo.m
    hq = MLA_HEADS * MLA_SLOT
    hv = MLA_HEADS * MLA_V
    row = lambda w, o: pl.BlockSpec((tr, w), lambda i: (i, o // w))
    lw = lambda a: pl.BlockSpec((None,) + a.shape[1:], lambda i: (layer,) + (0,) * (a.ndim - 1))
    cw = lambda a: pl.BlockSpec(a.shape, lambda i: (0,) * a.ndim)
    tab = pl.BlockSpec((tr, MLA_SLOT), lambda i: (i % geo.nt, 0))
    return pl.pallas_call(
        _qkv_kernel,
        out_shape=(jax.ShapeDtypeStruct((m, hq), BF16), jax.ShapeDtypeStruct((m, hq), BF16),
                   jax.ShapeDtypeStruct((m, hv), BF16)),
        grid=(m // tr,),
        in_specs=[row(MLA_Q_RANK, off["cq"]), row(MLA_KV_RANK, off["ckv"]), row(128, off["kr"]),
                  lw(gq), lw(gkv), lw(wqa), lw(wqb), lw(wka), cw(sela), cw(selb), lw(wv), tab, tab, tab, tab],
        out_specs=(pl.BlockSpec((tr, hq), lambda i: (i, 0)), pl.BlockSpec((tr, hq), lambda i: (i, 0)),
                   pl.BlockSpec((tr, hv), lambda i: (i, 0))),
        name="mla_qkv",
        compiler_params=_cparams("parallel"),
    )(u, u, u, gq, gkv, wqa, wqb, wka, sela, selb, wv, *tabs)


def _attn_kernel(q_ref, k_ref, v_ref, o_ref, *, n_ctx, n_all, n_ctx_tiles):
    qi = pl.program_id(2)
    tq = q_ref.shape[0]
    lane = lax.broadcasted_iota(jnp.int32, (tq, 2 * MLA_V), 1)

    def run(nk):
        outs = []
        for hh in range(2):
            sl = slice(hh * MLA_SLOT, (hh + 1) * MLA_SLOT)
            s = _dot_nt(q_ref[:, sl], k_ref[0:nk, sl])
            p = jnp.exp(s - jnp.max(s, axis=-1, keepdims=True))
            inv = 1.0 / jnp.sum(p, axis=-1, keepdims=True)
            outs.append(_dot(p.astype(BF16), v_ref[0:nk, :]) * inv)
        o_ref[...] = jnp.where(lane < MLA_V, outs[0], outs[1]).astype(o_ref.dtype)

    @pl.when(qi < n_ctx_tiles)
    def _():
        run(n_ctx)

    @pl.when(qi >= n_ctx_tiles)
    def _():
        run(n_all)


def _mla_attention(geo, q, k, v):
    tq, n, bsz = geo.tr, geo.n, geo.b
    q3, k3, v3 = (t.reshape(bsz, n, t.shape[-1]) for t in (q, k, v))
    kern = functools.partial(_attn_kernel, n_ctx=geo.c, n_all=n, n_ctx_tiles=geo.nct)
    out = pl.pallas_call(
        kern,
        out_shape=jax.ShapeDtypeStruct((bsz, n, MLA_HEADS * MLA_V), BF16),
        grid=(bsz, MLA_HEADS // 2, geo.nt),
        in_specs=[pl.BlockSpec((None, tq, 2 * MLA_SLOT), lambda b, hp, i: (b, i, hp)),
                  pl.BlockSpec((None, n, 2 * MLA_SLOT), lambda b, hp, i: (b, 0, hp)),
                  pl.BlockSpec((None, n, 2 * MLA_V), lambda b, hp, i: (b, 0, hp))],
        out_specs=pl.BlockSpec((None, tq, 2 * MLA_V), lambda b, hp, i: (b, i, hp)),
        name="mla_attn",
        compiler_params=_cparams("parallel", "parallel", "parallel"),
    )(q3, k3, v3)
    return out.reshape(geo.m, MLA_HEADS * MLA_V)


def _halo_flags(geo, i):
    j = i % geo.nt
    left = jnp.logical_and(j != 0, j != geo.nct)
    right = jnp.logical_and(j != geo.nct - 1, j != geo.nt - 1)
    return left.astype(F32), right.astype(F32)


def _conv_shifts(n_taps):
    base = CONV_HALO - n_taps // 2
    return sorted({(base + kk) % SUBLANES for kk in range(n_taps)} - {0})


def _conv_taps(pad_ref, sh_ref, w_ref, n_taps, rows):
    base = CONV_HALO - n_taps // 2
    shifts = _conv_shifts(n_taps)
    span = sh_ref.shape[1]
    for s, r in enumerate(shifts):
        sh_ref[s, :, :] = pad_ref[r:r + span, :]

    def window(o):
        a, r = o - o % SUBLANES, o % SUBLANES
        return pad_ref[a:a + rows, :] if r == 0 else sh_ref[shifts.index(r), a:a + rows, :]

    acc = w_ref[0:1, :] * window(base)
    for kk in range(1, n_taps):
        acc = acc + w_ref[kk:kk + 1, :] * window(base + kk)
    return acc


def _conv_scratch(tr, width, n_taps):
    return [pltpu.VMEM((tr + 2 * CONV_HALO, width), F32),
            pltpu.VMEM((len(_conv_shifts(n_taps)), tr + 2 * CONV_HALO - SUBLANES, width), F32)]


def _conf_kernel(a_ref, b_ref, ap_ref, bp_ref, an_ref, bn_ref, w_ref, bias_ref, g_ref, beta_ref, o_ref,
                 pad_ref, sh_ref, *, geo):
    left, right = _halo_flags(geo, pl.program_id(0))
    rows = a_ref.shape[0]
    glu = lambda a, b: a[...].astype(F32) * jax.nn.sigmoid(b[...].astype(F32))
    pad_ref[0:CONV_HALO, :] = glu(ap_ref, bp_ref) * left
    pad_ref[CONV_HALO:CONV_HALO + rows, :] = glu(a_ref, b_ref)
    pad_ref[CONV_HALO + rows:2 * CONV_HALO + rows, :] = glu(an_ref, bn_ref) * right
    h = _conv_taps(pad_ref, sh_ref, w_ref, CF_KERNEL, rows) + bias_ref[...]
    hc = h - jnp.mean(h, axis=-1, keepdims=True)
    y = hc * lax.rsqrt(jnp.mean(hc * hc, axis=-1, keepdims=True) + EPS) * g_ref[...] + beta_ref[...]
    o_ref[...] = (y * jax.nn.sigmoid(y)).astype(o_ref.dtype)


def _sconv_kernel(gb_ref, gc_ref, h_ref, gcp_ref, hp_ref, gcn_ref, hn_ref, w_ref, o_ref, pad_ref, sh_ref, *, geo):
    left, right = _halo_flags(geo, pl.program_id(0))
    rows = h_ref.shape[0]
    prod = lambda a, b: a[...].astype(F32) * b[...].astype(F32)
    pad_ref[0:CONV_HALO, :] = prod(gcp_ref, hp_ref) * left
    pad_ref[CONV_HALO:CONV_HALO + rows, :] = prod(gc_ref, h_ref)
    pad_ref[CONV_HALO + rows:2 * CONV_HALO + rows, :] = prod(gcn_ref, hn_ref) * right
    y = _conv_taps(pad_ref, sh_ref, w_ref, SC_KERNEL, rows)
    o_ref[...] = (gb_ref[...].astype(F32) * y).astype(o_ref.dtype)


def _halo_specs(geo, width, offset):
    tr = geo.tr
    per = tr // CONV_HALO
    last = geo.m // CONV_HALO - 1
    cur = pl.BlockSpec((tr, width), lambda i: (i, offset // width))
    prev = pl.BlockSpec((CONV_HALO, width), lambda i: (jnp.maximum(i * per - 1, 0), offset // width))
    nxt = pl.BlockSpec((CONV_HALO, width), lambda i: (jnp.minimum((i + 1) * per, last), offset // width))
    return cur, prev, nxt


def _conformer(geo, u, off, layer, w_dw, b_dw, g_ln, b_ln):
    tr = geo.tr
    a_c, a_p, a_n = _halo_specs(geo, CF_WIDTH, off["cf_a"])
    b_c, b_p, b_n = _halo_specs(geo, CF_WIDTH, off["cf_b"])
    lw = lambda a: pl.BlockSpec((None,) + a.shape[1:], lambda i: (layer,) + (0,) * (a.ndim - 1))
    return pl.pallas_call(
        functools.partial(_conf_kernel, geo=geo),
        out_shape=jax.ShapeDtypeStruct((geo.m, CF_WIDTH), BF16),
        grid=(geo.m // tr,),
        in_specs=[a_c, b_c, a_p, b_p, a_n, b_n, lw(w_dw), lw(b_dw), lw(g_ln), lw(b_ln)],
        out_specs=pl.BlockSpec((tr, CF_WIDTH), lambda i: (i, 0)),
        scratch_shapes=_conv_scratch(tr, CF_WIDTH, CF_KERNEL),
        name="conformer",
        compiler_params=_cparams("parallel"),
    )(u, u, u, u, u, u, w_dw, b_dw, g_ln, b_ln)


def _shortconv(geo, u, off, layer, w_conv):
    tr = geo.tr
    gb_c, _, _ = _halo_specs(geo, SC_WIDTH, off["sc_b"])
    gc_c, gc_p, gc_n = _halo_specs(geo, SC_WIDTH, off["sc_c"])
    h_c, h_p, h_n = _halo_specs(geo, SC_WIDTH, off["sc_h"])
    lw = lambda a: pl.BlockSpec((None,) + a.shape[1:], lambda i: (layer,) + (0,) * (a.ndim - 1))
    return pl.pallas_call(
        functools.partial(_sconv_kernel, geo=geo),
        out_shape=jax.ShapeDtypeStruct((geo.m, SC_WIDTH), BF16),
        grid=(geo.m // tr,),
        in_specs=[gb_c, gc_c, h_c, gc_p, h_p, gc_n, h_n, lw(w_conv)],
        out_specs=pl.BlockSpec((tr, SC_WIDTH), lambda i: (i, 0)),
        scratch_shapes=_conv_scratch(tr, SC_WIDTH, SC_KERNEL),
        name="shortconv",
        compiler_params=_cparams("parallel"),
    )(u, u, u, u, u, u, u, w_conv)


def _gla_tables():
    c = GLA_CHUNK
    n_rows = (GLA_LEVELS + 2) * c + 8
    mat = np.zeros((2, n_rows, c), np.float32)
    msk = np.zeros((2, GLA_LEVELS + 1, c, c), np.float32)
    for lvl in range(GLA_LEVELS):
        hs = (c // 2) >> lvl
        for t in range(c):
            mid = (t // (2 * hs)) * 2 * hs + hs
            if t >= mid:
                mat[0, lvl * c + t, mid + 1:t + 1] = 1.0
                msk[0, lvl, t, mid - hs:mid] = 1.0
            else:
                mat[0, lvl * c + t, t + 1:mid + 1] = 1.0
    msk[0, GLA_LEVELS] = np.eye(c, dtype=np.float32)
    for t in range(c):
        mat[0, GLA_LEVELS * c + t, :t + 1] = 1.0
        mat[0, (GLA_LEVELS + 1) * c + t, t + 1:] = 1.0
    mat[0, (GLA_LEVELS + 2) * c:, :] = 1.0
    n_blk = GLA_LEVELS + 2
    mat[1, :n_blk * c] = mat[0, :n_blk * c].reshape(n_blk, c, c)[:, ::-1, ::-1].reshape(n_blk * c, c)
    mat[1, n_blk * c:] = 1.0
    msk[1] = msk[0][:, ::-1, ::-1]
    mat3 = np.concatenate([mat, mat, mat], axis=2)
    msk4 = np.tile(msk, (1, 1, GLA_HEADS, 1))
    return jnp.asarray(mat3, BF16), jnp.asarray(msk4, F32)


def _gla_kernel(qf_ref, kf_ref, vf_ref, gf_ref, qb_ref, kb_ref, vb_ref, gb_ref, w2_ref, b2_ref, mat_ref, msk_ref,
                of_ref, ob_ref, st_ref):
    @pl.when(pl.program_id(0) == 0)
    def _():
        st_ref[...] = jnp.zeros_like(st_ref)

    ins = ((qf_ref, kf_ref, vf_ref, gf_ref, of_ref), (qb_ref, kb_ref, vb_ref, gb_ref, ob_ref))
    for b in range(qf_ref.shape[0]):
        for d, (q_ref, k_ref, v_ref, g_ref, o_ref) in enumerate(ins):
            _gla_chunk(q_ref.at[b], k_ref.at[b], v_ref.at[b], g_ref.at[b], w2_ref.at[d], b2_ref.at[d],
                       mat_ref.at[d], msk_ref.at[d], o_ref.at[b], st_ref.at[d, b])


def _gla_chunk(q_ref, k_ref, v_ref, g_ref, w2_ref, b2_ref, mat_ref, msk_ref, o_ref, st_ref):
    c = GLA_CHUNK
    z = _dot(g_ref[...], w2_ref[...]) + b2_ref[...]
    la = (jnp.minimum(z, 0.0) - jnp.log1p(jnp.exp(-jnp.abs(z)))) * (1.0 / GLA_GATE_TAU)
    hi = la.astype(BF16)
    r1 = la - hi.astype(F32)
    lo = r1.astype(BF16)
    lo2 = (r1 - lo.astype(F32)).astype(BF16)
    ea = jnp.exp(_dot(mat_ref[...], jnp.concatenate([hi, lo, lo2], axis=0)))

    head = lax.broadcasted_iota(jnp.int32, (1, GLA_HEADS * GLA_DK), 1) // GLA_DK
    zero = jnp.zeros((), BF16)

    def stack(x):
        xb = x.astype(BF16)
        return jnp.concatenate([jnp.where(head == h, xb, zero) for h in range(GLA_HEADS)], axis=0)

    q = q_ref[...].astype(F32) * (GLA_DK ** -0.5)
    k = k_ref[...].astype(F32)
    vb = v_ref[...]

    attn = _dot_nt(stack(q), k.astype(BF16)) * msk_ref[GLA_LEVELS]
    for lvl in range(GLA_LEVELS):
        e = ea[lvl * c:(lvl + 1) * c]
        attn = attn + _dot_nt(stack(q * e), (k * e).astype(BF16)) * msk_ref[lvl]
    o_intra = _dot(attn.astype(BF16), vb)

    st = st_ref[...]
    o_inter = _dot_nt(stack(q * ea[GLA_LEVELS * c:(GLA_LEVELS + 1) * c]), st.astype(BF16))
    for h in range(GLA_HEADS):
        rows = slice(h * c, (h + 1) * c)
        cols = slice(h * GLA_DV, (h + 1) * GLA_DV)
        o_ref[:, cols] = o_intra[rows, cols] + o_inter[rows, :]

    k_dec = stack(k * ea[(GLA_LEVELS + 1) * c:(GLA_LEVELS + 2) * c])
    v_rows = jnp.concatenate([vb[:, h * GLA_DV:(h + 1) * GLA_DV] for h in range(GLA_HEADS)], axis=0)
    eye = (lax.broadcasted_iota(jnp.int32, (GLA_DV, GLA_DV), 0)
           == lax.broadcasted_iota(jnp.int32, (GLA_DV, GLA_DV), 1)).astype(BF16)
    v_t = _dot_nt(eye, v_rows).astype(BF16)
    tot = ea[(GLA_LEVELS + 2) * c:(GLA_LEVELS + 2) * c + 1]
    st_ref[...] = st * tot + _dot(v_t, k_dec)


def _gla_scan(geo, u, off, layer, w2, b2, mat3, msk4):
    c = GLA_CHUNK
    ncc = geo.c // c
    nc = geo.n // c
    dk, dv = GLA_HEADS * GLA_DK, GLA_HEADS * GLA_DV

    fwd = lambda i: i
    bwd = lambda i: jnp.where(i < ncc, ncc - 1 - i, nc - 1 - (i - ncc))

    u3 = u.reshape(geo.b, geo.n, u.shape[-1])
    row = lambda w, o, chunk: pl.BlockSpec((geo.b, c, w), lambda i: (0, chunk(i), o // w))
    rows = lambda chunk: [row(dk, off["gla_q"], chunk), row(dk, off["gla_k"], chunk), row(dv, off["gla_v"], chunk),
                          row(128, off["gate"], chunk)]
    out_sds = jax.ShapeDtypeStruct((geo.b, geo.n, dv), F32)
    o_f, o_b = pl.pallas_call(
        _gla_kernel,
        out_shape=(out_sds, out_sds),
        grid=(nc,),
        in_specs=rows(fwd) + rows(bwd) + [
            pl.BlockSpec((None, 2, 128, dk), lambda i: (layer, 0, 0, 0)),
            pl.BlockSpec((None, 2, 1, dk), lambda i: (layer, 0, 0, 0)),
            pl.BlockSpec(mat3.shape, lambda i: (0, 0, 0)),
            pl.BlockSpec(msk4.shape, lambda i: (0, 0, 0, 0))],
        out_specs=(pl.BlockSpec((geo.b, c, dv), lambda i: (0, fwd(i), 0)),
                   pl.BlockSpec((geo.b, c, dv), lambda i: (0, bwd(i), 0))),
        scratch_shapes=[pltpu.VMEM((2, geo.b, GLA_DV, dk), F32)],
        name="gla_scan",
        compiler_params=_cparams("arbitrary"),
    )(u3, u3, u3, u3, u3, u3, u3, u3, w2, b2, mat3, msk4)
    return o_f.reshape(geo.m, dv), o_b.reshape(geo.m, dv)


def _merge_mix_kernel(attn_ref, conf_ref, glaf_ref, glab_ref, r_ref, gn_ref, sc_ref, gate_ref, w0_ref, w1_ref,
                      w2_ref, w3_ref, wmix_ref, x_ref, gt_ref, o_ref, y_ref):
    wbr_ref = (w0_ref, w1_ref, w2_ref, w3_ref)
    o = glaf_ref[...] + glab_ref[...]
    r = r_ref[...].astype(F32)
    gn = gn_ref[...]
    normed = [_rms(o[:, h * GLA_DV:(h + 1) * GLA_DV], gn) for h in range(GLA_HEADS)]
    gla = (jnp.concatenate(normed, axis=1) * (r * jax.nn.sigmoid(r))).astype(BF16)
    acts = (attn_ref[...], conf_ref[...], gla, sc_ref[...])
    d = x_ref.shape[1]
    cw = min(MERGE_COLS, d)
    for c0 in range(0, d, cw):
        y = None
        for i in range(N_BRANCHES):
            half_b = _dot(acts[i], wbr_ref[i][:, c0:c0 + cw])
            t = jnp.tanh(gate_ref[:, i * d + c0:i * d + c0 + cw].astype(F32))
            term = t * half_b + half_b
            y = term if y is None else y + term
        y_ref[:, c0:c0 + cw] = y.astype(BF16)
    y_all = y_ref[...]
    for c0 in range(0, d, cw):
        cols = slice(c0, c0 + cw)
        o_ref[:, cols] = x_ref[:, cols] + gt_ref[:, cols] * _dot(y_all, wmix_ref[:, cols])


def _merge_mix(geo, attn, conf, gla_f, gla_b, u, off, g_norm, sconv, w_br_half, w_mix, x, mod, layer):
    tm, d, m = geo.tr, geo.d, geo.m
    assert off["merge"] == 0
    w512 = GLA_HEADS * GLA_DV
    act = pl.BlockSpec((tm, w512), lambda i: (i, 0))
    once = pl.Buffered(1)
    w_br = pl.BlockSpec((None, w512, d), lambda i: (layer, 0, 0), pipeline_mode=once)
    return pl.pallas_call(
        _merge_mix_kernel,
        out_shape=jax.ShapeDtypeStruct((m, d), F32),
        grid=(m // tm,),
        in_specs=[act, act, act, act,
                  pl.BlockSpec((tm, w512), lambda i: (i, off["gla_r"] // w512)),
                  pl.BlockSpec((None, 1, GLA_DV), lambda i: (layer, 0, 0)),
                  act, pl.BlockSpec((tm, N_BRANCHES * d), lambda i: (i, 0)),
                  w_br, w_br, w_br, w_br,
                  pl.BlockSpec((None, d, d), lambda i: (layer, 0, 0), pipeline_mode=once),
                  pl.BlockSpec((tm, d), lambda i: (i, 0)), geo.mod_spec(layer, 2)],
        out_specs=pl.BlockSpec((tm, d), lambda i: (i, 0)),
        scratch_shapes=[pltpu.VMEM((tm, d), BF16)],
        name="merge_mix",
        compiler_params=_cparams("parallel"),
    )(attn, conf, gla_f, gla_b, u, g_norm, sconv, u, *w_br_half, w_mix, x, mod)


def _cast_kernel(w_ref, o_ref, *, scale):
    o_ref[...] = (w_ref[...] * scale).astype(o_ref.dtype)


def _to_bf16(w, scale=1.0):
    n_l, rows, cols = w.shape
    tr = _pick(rows, (512, 256, 128))
    return pl.pallas_call(
        functools.partial(_cast_kernel, scale=scale),
        out_shape=jax.ShapeDtypeStruct(w.shape, BF16),
        grid=(n_l, rows // tr),
        in_specs=[pl.BlockSpec((None, tr, cols), lambda l, i: (l, i, 0))],
        out_specs=pl.BlockSpec((None, tr, cols), lambda l, i: (l, i, 0)),
        name="weight_cast",
        compiler_params=_cparams("parallel", "parallel"),
    )(w)


def _second_largest(a, b, c, d):
    return jnp.maximum(jnp.maximum(jnp.minimum(a, b), jnp.minimum(c, d)),
                       jnp.minimum(jnp.maximum(a, b), jnp.maximum(c, d)))


def _chunk_rows(rows, n, cc):
    return pl.ds(cc, rows, stride=n) if n > 1 else pl.ds(0, rows)


def _pack_rows(ref, y):
    rows = y.shape[0]
    n = ref.shape[0] // rows
    half = n * 128
    bits = lambda v: pltpu.bitcast(v.astype(BF16).astype(F32), jnp.uint32)
    packed = (bits(y[:, :half]) >> 16) | bits(y[:, half:])
    for cc in range(n):
        ref[_chunk_rows(rows, n, cc), :] = packed[:, cc * 128:(cc + 1) * 128]


def _unpack_cols(words):
    lo = pltpu.bitcast(words << 16, F32)
    hi = pltpu.bitcast(words & jnp.uint32(0xFFFF0000), F32)
    return lo, hi


def _ffn_pre_kernel(x_ref, g_ref, sh_ref, sc_ref, wr_ref, br_ref, tri_ref, h_ref, idx_ref, wgt_ref, rank_ref,
                    cnt_ref, carry_ref):
    @pl.when(pl.program_id(0) == 0)
    def _():
        carry_ref[...] = jnp.zeros_like(carry_ref)

    x = x_ref[...]
    y = _rms(x, g_ref[...]) * (1.0 + sc_ref[...]) + sh_ref[...]
    hi = y.astype(BF16)
    lo = (y - hi.astype(F32)).astype(BF16)
    _pack_rows(h_ref, y)
    ne = N_EXPERTS
    two = _dot_nt(wr_ref[...], hi)
    logits = two[0:ne] + two[ne:2 * ne] + _dot_nt(wr_ref[0:ne, :], lo)
    aff = 1.0 / (1.0 + jnp.exp(-logits))
    sel = aff + br_ref[...]
    per = ne // N_GROUPS
    gs = []
    for g in range(N_GROUPS):
        a, b, c, d = (sel[g * per + r:g * per + r + 1] for r in range(per))
        top1 = jnp.maximum(jnp.maximum(a, b), jnp.maximum(c, d))
        gs.append(top1 + _second_largest(a, b, c, d))
    best = gs[0]
    for g in range(1, N_GROUPS):
        best = jnp.maximum(best, gs[g])
    grp = jnp.full(best.shape, N_GROUPS, jnp.int32)
    for g in reversed(range(N_GROUPS)):
        grp = jnp.where(gs[g] == best, g, grp)
    erow = lax.broadcasted_iota(jnp.int32, sel.shape, 0)
    masked = jnp.where(erow // per == grp, sel, -jnp.inf)
    m1 = jnp.max(masked, axis=0, keepdims=True)
    i1 = jnp.min(jnp.where(masked == m1, erow, ne), axis=0, keepdims=True)
    one1 = erow == i1
    masked2 = jnp.where(one1, -jnp.inf, masked)
    m2 = jnp.max(masked2, axis=0, keepdims=True)
    i2 = jnp.min(jnp.where(masked2 == m2, erow, ne), axis=0, keepdims=True)
    one2 = erow == i2
    w1 = jnp.sum(jnp.where(one1, aff, 0.0), axis=0, keepdims=True)
    w2 = jnp.sum(jnp.where(one2, aff, 0.0), axis=0, keepdims=True)
    inv = 1.0 / (w1 + w2)
    idx_ref[0:1, :] = i1
    idx_ref[1:2, :] = i2
    wgt_ref[0:1, :] = w1 * inv
    wgt_ref[1:2, :] = w2 * inv
    o1 = jnp.where(one1, 1.0, 0.0)
    o2 = jnp.where(one2, 1.0, 0.0)
    tri = tri_ref[...]
    c0 = carry_ref[...]
    c1 = c0 + jnp.sum(o1, axis=1, keepdims=True)
    rank_ref[0:1, :] = jnp.sum(o1 * (c0 + _dot(o1.astype(BF16), tri)), axis=0, keepdims=True).astype(jnp.int32)
    rank_ref[1:2, :] = jnp.sum(o2 * (c1 + _dot(o2.astype(BF16), tri)), axis=0, keepdims=True).astype(jnp.int32)
    c2 = c1 + jnp.sum(o2, axis=1, keepdims=True)
    carry_ref[...] = c2
    cnt_ref[...] = c2


def _ffn_pre(geo, x, g, mod, layer, wr_stack, b_router):
    tr, d, m = geo.tr, geo.d, geo.m
    nch = d // 256
    tri = jnp.asarray(np.triu(np.ones((tr, tr), np.float32), 1), BF16)
    pair = lambda dt: jax.ShapeDtypeStruct((2, m), dt)
    pair_spec = pl.BlockSpec((2, tr), lambda i: (0, i))
    return pl.pallas_call(
        _ffn_pre_kernel,
        out_shape=(jax.ShapeDtypeStruct((m * nch, 128), jnp.uint32), pair(jnp.int32), pair(F32), pair(jnp.int32),
                   jax.ShapeDtypeStruct((N_EXPERTS, 1), F32)),
        grid=(m // tr,),
        in_specs=[pl.BlockSpec((tr, d), lambda i: (i, 0)),
                  pl.BlockSpec((None, 1, d), lambda i: (layer, 0, 0)),
                  geo.mod_spec(layer, 3), geo.mod_spec(layer, 4),
                  pl.BlockSpec(wr_stack.shape, lambda i: (0, 0)),
                  pl.BlockSpec(b_router.shape, lambda i: (0, 0)),
                  pl.BlockSpec((tr, tr), lambda i: (0, 0))],
        out_specs=(pl.BlockSpec((tr * nch, 128), lambda i: (i, 0)), pair_spec, pair_spec, pair_spec,
                   pl.BlockSpec((N_EXPERTS, 1), lambda i: (0, 0))),
        scratch_shapes=[pltpu.VMEM((N_EXPERTS, 1), F32)],
        name="ffn_pre",
        compiler_params=_cparams("arbitrary"),
    )(x, g, mod, mod, wr_stack, b_router, tri)


def _sorted_row(idx_ref, rank_ref, base_ref, a, n):
    return pl.multiple_of((base_ref[idx_ref[a]] + rank_ref[a]) * n, n)


def _dispatch_kernel(idx_ref, rank_ref, base_ref, last_ref, h_ref, xs_hbm, zero_ref, sem, zero_sem, *, m, n):
    i = pl.program_id(0)
    tr = h_ref.shape[0] // n

    @pl.when(i == 0)
    def _():
        zero_ref[...] = jnp.zeros_like(zero_ref)

        def fills(e):
            rows = zero_ref.shape[0]
            start = pl.multiple_of(last_ref[e] * n, rows)
            return pltpu.make_async_copy(zero_ref, xs_hbm.at[pl.ds(start, rows), :], zero_sem)

        for e in range(2 * N_EXPERTS):
            @pl.when(last_ref[e] >= 0)
            def _():
                fills(e).start()
        for e in range(2 * N_EXPERTS):
            @pl.when(last_ref[e] >= 0)
            def _():
                fills(e).wait()

    def issue(r, carry):
        src = h_ref.at[pl.ds(pl.multiple_of(r * n, n), n), :]
        for k in range(2):
            row = _sorted_row(idx_ref, rank_ref, base_ref, k * m + i * tr + r, n)
            pltpu.make_async_copy(src, xs_hbm.at[pl.ds(row, n), :], sem).start()
        return carry

    lax.fori_loop(0, tr, issue, 0, unroll=4)
    everything = xs_hbm.at[pl.ds(0, 2 * tr * n), :]
    pltpu.make_async_copy(everything, everything, sem).wait()


def _dispatch(hfp, ridx, rank, base, last_tile_row, n_rows, tr, tm):
    n = hfp.shape[0] * 2 // ridx.shape[0]
    m = hfp.shape[0] // n
    return pl.pallas_call(
        functools.partial(_dispatch_kernel, m=m, n=n),
        out_shape=jax.ShapeDtypeStruct((n_rows * n, 128), jnp.uint32),
        grid_spec=pltpu.PrefetchScalarGridSpec(
            num_scalar_prefetch=4, grid=(m // tr,),
            in_specs=[pl.BlockSpec((tr * n, 128), lambda i, *_: (i, 0))],
            out_specs=pl.BlockSpec(memory_space=pl.ANY),
            scratch_shapes=[pltpu.VMEM((tm * n, 128), jnp.uint32), pltpu.SemaphoreType.DMA(()),
                            pltpu.SemaphoreType.DMA(())]),
        name="moe_dispatch",
        compiler_params=_cparams("arbitrary"),
    )(ridx, rank, base, last_tile_row, hfp)


def _gmm_kernel(te_ref, nv_ref, xs_ref, wg_ref, wu_ref, wd_ref, ys_ref, wgb_ref, wub_ref, wdb_ref, prev_ref):
    j = pl.program_id(0)

    @pl.when(j == 0)
    def _():
        prev_ref[0] = -1

    @pl.when(j < nv_ref[0])
    def _():
        e = te_ref[j]

        @pl.when(e != prev_ref[0])
        def _():
            wgb_ref[...] = wg_ref[...].astype(BF16)
            wub_ref[...] = wu_ref[...].astype(BF16)
            wdb_ref[...] = wd_ref[...].astype(BF16)
            prev_ref[0] = e

        half = wgb_ref.shape[0] // 2
        n = half // 128
        tm = xs_ref.shape[0] // n
        parts = [_unpack_cols(xs_ref[_chunk_rows(tm, n, cc), :]) for cc in range(n)]
        xa = jnp.concatenate([p[0].astype(BF16) for p in parts], axis=1)
        xb = jnp.concatenate([p[1].astype(BF16) for p in parts], axis=1)
        g = _dot(xa, wgb_ref[0:half, :]) + _dot(xb, wgb_ref[half:2 * half, :])
        u = _dot(xa, wub_ref[0:half, :]) + _dot(xb, wub_ref[half:2 * half, :])
        act = (g * jax.nn.sigmoid(g) * u).astype(BF16)
        _pack_rows(ys_ref, _dot(act, wdb_ref[...]))

    @pl.when(j >= nv_ref[0])
    def _():
        ys_ref[...] = jnp.zeros_like(ys_ref)


def _gmm(xs, tile_expert, n_valid, wg, wu, wd, layer, tm):
    d, ff = wg.shape[-2], wg.shape[-1]
    nch = d // 256
    blk = tm * nch
    row = lambda j, te, nv: (jnp.minimum(j, nv[0] - 1), 0)
    wsel = lambda j, te, nv: (layer, te[jnp.minimum(j, nv[0] - 1)], 0, 0)
    return pl.pallas_call(
        _gmm_kernel,
        out_shape=jax.ShapeDtypeStruct(xs.shape, jnp.uint32),
        grid_spec=pltpu.PrefetchScalarGridSpec(
            num_scalar_prefetch=2, grid=(xs.shape[0] // blk,),
            in_specs=[pl.BlockSpec((blk, 128), row),
                      pl.BlockSpec((None, None, d, ff), wsel), pl.BlockSpec((None, None, d, ff), wsel),
                      pl.BlockSpec((None, None, ff, d), wsel)],
            out_specs=pl.BlockSpec((blk, 128), lambda j, te, nv: (j, 0)),
            scratch_shapes=[pltpu.VMEM((d, ff), BF16), pltpu.VMEM((d, ff), BF16), pltpu.VMEM((ff, d), BF16),
                            pltpu.SMEM((1,), jnp.int32)]),
        name="moe_experts",
        compiler_params=_cparams("arbitrary"),
    )(tile_expert, n_valid, xs, wg, wu, wd)


def _combine_kernel(idx_ref, rank_ref, base_ref, x_ref, w_ref, gt_ref, ys_hbm, o_ref, buf_ref, sem_ref, *, m):
    i = pl.program_id(0)
    n_tiles = pl.num_programs(0)
    tr = x_ref.shape[0]
    n = buf_ref.shape[2] // tr

    def issue(tile, slot):
        def body(r, carry):
            for k in range(2):
                row = _sorted_row(idx_ref, rank_ref, base_ref, k * m + tile * tr + r, n)
                pltpu.make_async_copy(ys_hbm.at[pl.ds(row, n), :],
                                      buf_ref.at[slot, k, pl.ds(pl.multiple_of(r * n, n), n), :],
                                      sem_ref.at[slot]).start()
            return carry
        lax.fori_loop(0, tr, body, 0, unroll=4)

    @pl.when(i == 0)
    def _():
        issue(0, 0)

    @pl.when(i + 1 < n_tiles)
    def _():
        issue(i + 1, (i + 1) % 2)

    slot = i % 2
    pltpu.make_async_copy(buf_ref.at[slot], buf_ref.at[slot], sem_ref.at[slot]).wait()
    w1, w2 = w_ref[:, 0:1], w_ref[:, 1:2]
    half = n * 128
    for cc in range(n):
        lo1, hi1 = _unpack_cols(buf_ref[slot, 0, _chunk_rows(tr, n, cc), :])
        lo2, hi2 = _unpack_cols(buf_ref[slot, 1, _chunk_rows(tr, n, cc), :])
        ca = slice(cc * 128, (cc + 1) * 128)
        cb = slice(half + cc * 128, half + (cc + 1) * 128)
        o_ref[:, ca] = x_ref[:, ca] + gt_ref[:, ca] * (w1 * lo1 + w2 * lo2)
        o_ref[:, cb] = x_ref[:, cb] + gt_ref[:, cb] * (w1 * hi1 + w2 * hi2)


def _combine(geo, x, ys, ridx, rank, base, w12, mod, layer, gate_part):
    tr, d = geo.tr, geo.d
    nch = d // 256
    gate = pl.BlockSpec((None, None, 1, d), lambda i, *_: (layer, geo.mod_row(i), 0, gate_part))
    return pl.pallas_call(
        functools.partial(_combine_kernel, m=geo.m),
        out_shape=jax.ShapeDtypeStruct((geo.m, d), F32),
        grid_spec=pltpu.PrefetchScalarGridSpec(
            num_scalar_prefetch=3, grid=(geo.m // tr,),
            in_specs=[pl.BlockSpec((tr, d), lambda i, *_: (i, 0)),
                      pl.BlockSpec((tr, 2), lambda i, *_: (i, 0)), gate,
                      pl.BlockSpec(memory_space=pl.ANY)],
            out_specs=pl.BlockSpec((tr, d), lambda i, *_: (i, 0)),
            scratch_shapes=[pltpu.VMEM((2, 2, tr * nch, 128), jnp.uint32), pltpu.SemaphoreType.DMA((2,))]),
        name="moe_combine",
        compiler_params=_cparams("arbitrary"),
    )(ridx, rank, base, x, w12, mod, ys)


def _moe(geo, x, g, mod, layer, wr_stack, b_router, wg, wu, wd):
    tm = 256
    hfp, ridx, rwgt, rank, cnt = _ffn_pre(geo, x, g, mod, layer, wr_stack, b_router)
    n_tiles = (2 * geo.m) // tm + N_EXPERTS
    counts = cnt[:, 0].astype(jnp.int32)
    tiles_per = (counts + tm - 1) // tm
    tile_start = jnp.cumsum(tiles_per) - tiles_per
    base = tile_start * tm
    n_valid = jnp.sum(tiles_per).reshape(1)
    tile_expert = jnp.sum(jnp.arange(n_tiles)[:, None] >= tile_start[None, :], axis=1).astype(jnp.int32) - 1
    ridx, rank = ridx.reshape(-1), rank.reshape(-1)
    tail = n_valid + jnp.arange(N_EXPERTS)
    fill_rows = jnp.concatenate([jnp.where(tiles_per > 0, base + (tiles_per - 1) * tm, -1),
                                 jnp.where(tail < n_tiles, tail * tm, -1)])
    xs = _dispatch(hfp, ridx, rank, base, fill_rows, n_tiles * tm, geo.tr, tm)
    ys = _gmm(xs, tile_expert, n_valid, wg, wu, wd, layer, tm)
    return _combine(geo, x, ys, ridx, rank, base, rwgt.T, mod, layer, 5)


def _final_kernel(x_ref, g_ref, o_ref):
    o_ref[...] = _rms(x_ref[...], g_ref[...])


def _final_norm(geo, x, g):
    tr, d = geo.tr, geo.d
    x3 = x.reshape(geo.b, geo.n, d)
    return pl.pallas_call(
        _final_kernel,
        out_shape=jax.ShapeDtypeStruct((geo.b, geo.s, d), F32),
        grid=(geo.b, geo.s // tr),
        in_specs=[pl.BlockSpec((None, tr, d), lambda b, i: (b, i + geo.nct, 0)),
                  pl.BlockSpec((1, d), lambda b, i: (0, 0))],
        out_specs=pl.BlockSpec((None, tr, d), lambda b, i: (b, i, 0)),
        name="final_norm",
        compiler_params=_cparams("parallel", "parallel"),
    )(x3, g.reshape(1, d))


def _pad_cols(w, width):
    return jnp.pad(w, [(0, 0)] * (w.ndim - 1) + [(0, width - w.shape[-1])])


def _prep_w_in(w_in, d):
    src = _src_layout(d)
    off, total = _proj_layout(d)
    cut = lambda name: w_in[..., src[name][0]:src[name][1]]
    cf, sc = cut("cf"), cut("sc")
    gate = jnp.concatenate([cut("gf"), cut("gb")], axis=-1)
    pieces = {"merge": cut("merge") * 0.5, "sc_b": sc[..., :SC_WIDTH], "sc_c": sc[..., SC_WIDTH:2 * SC_WIDTH],
              "sc_h": sc[..., 2 * SC_WIDTH:], "cf_a": cf[..., :CF_WIDTH], "cf_b": cf[..., CF_WIDTH:],
              "cq": cut("cq"), "gla_v": cut("gla_v"), "gla_r": cut("gla_r"), "ckv": cut("ckv"),
              "gla_q": cut("gla_q"), "gla_k": cut("gla_k"), "kr": _pad_cols(cut("kr"), 128),
              "gate": _pad_cols(gate, 128)}
    order = sorted(off, key=off.get)
    out = jnp.concatenate([pieces[name] for name in order], axis=-1).astype(BF16)
    assert out.shape[-1] == total
    return out, off


def _rope_swap_perm():
    q = MLA_ROPE // 4
    return np.concatenate([np.arange(q, 2 * q), np.arange(0, q), np.arange(3 * q, 4 * q), np.arange(2 * q, 3 * q)])


def _prep_mla(w_uq, w_ukv):
    n_l = w_uq.shape[0]
    perm = _rope_swap_perm()
    wq = w_uq.reshape(n_l, MLA_Q_RANK, MLA_HEADS, MLA_NOPE + MLA_ROPE)
    qn, qr = wq[..., :MLA_NOPE], wq[..., MLA_NOPE:]
    zeros = lambda *s: jnp.zeros((n_l,) + s, w_uq.dtype)
    wqa = jnp.concatenate([qn, qr, qr], axis=-1)
    wqb = jnp.concatenate([zeros(MLA_Q_RANK, MLA_HEADS, MLA_NOPE), qr[..., perm],
                           zeros(MLA_Q_RANK, MLA_HEADS, MLA_ROPE)], axis=-1)
    wkv = w_ukv.reshape(n_l, MLA_KV_RANK, MLA_HEADS, MLA_NOPE + MLA_V)
    wka = jnp.concatenate([wkv[..., :MLA_NOPE], zeros(MLA_KV_RANK, MLA_HEADS, MLA_SLOT - MLA_NOPE)], axis=-1)
    wv = wkv[..., MLA_NOPE:]
    flat = lambda w: w.reshape(n_l, w.shape[1], -1).astype(BF16)
    sela = np.zeros((128, MLA_HEADS, MLA_SLOT), np.float32)
    selb = np.zeros((128, MLA_HEADS, MLA_SLOT), np.float32)
    for r in range(MLA_ROPE):
        sela[r, :, MLA_NOPE + r] = 1.0
        sela[r, :, MLA_NOPE + MLA_ROPE + r] = 1.0
        selb[perm[r], :, MLA_NOPE + r] = 1.0
    sel = lambda s: jnp.asarray(s.reshape(128, -1), BF16)
    return flat(wqa), flat(wqb), flat(wka), sel(sela), sel(selb), flat(wv)


def _rope_tables(geo):
    rows = geo.s // GRID_W
    pos = jnp.arange(rows * GRID_W)
    row = (pos // GRID_W).astype(F32)
    col = (pos % GRID_W).astype(F32)
    n_freq = MLA_ROPE // 4
    freqs = ROPE_THETA ** (-jnp.arange(n_freq, dtype=F32) / n_freq)
    cr, sr = jnp.cos(row[:, None] * freqs), jnp.sin(row[:, None] * freqs)
    cc, sn = jnp.cos(col[:, None] * freqs), jnp.sin(col[:, None] * freqs)
    cos32 = jnp.concatenate([cr, cr, cc, cc], axis=-1)
    sin32 = jnp.concatenate([-sr, sr, -sn, sn], axis=-1)
    s, c = geo.s, geo.c
    one = lambda n, w: jnp.ones((n, w), F32)
    zero = lambda n, w: jnp.zeros((n, w), F32)
    scale = (MLA_NOPE + MLA_ROPE) ** -0.5
    qca = jnp.concatenate([jnp.concatenate([one(c, 64), zero(c, 32), one(c, 32)], 1),
                           jnp.concatenate([one(s, 64), cos32, one(s, 32)], 1)], 0) * scale
    qsb = jnp.concatenate([zero(c, 128), jnp.concatenate([zero(s, 64), sin32, zero(s, 32)], 1)], 0) * scale
    kca = jnp.concatenate([jnp.concatenate([one(c, 64), zero(c, 32), one(c, 32)], 1),
                           jnp.concatenate([one(s, 64), cos32, zero(s, 32)], 1)], 0)
    ksb = jnp.concatenate([zero(c, 128), jnp.concatenate([zero(s, 64), sin32, zero(s, 32)], 1)], 0)
    return qca, qsb, kca, ksb


def kernel(x, c, ctx, c_ctx, w_mod, b_mod, g_mix, g_ffn, w_in, g_q, w_uq, g_kv, w_ukv, w_mla_o, w_cf_dw, b_cf_dw, g_cf_ln, b_cf_ln, w_cf_o, w_gla_gf, b_gla_gf, w_gla_gb, b_gla_gb, g_gla_norm, w_gla_o, w_sc_conv, w_sc_o, w_mix_o, w_router, b_router, w_e_gate, w_e_up, w_e_down, g_final):
    bsz, seq, d = x.shape
    n_ctx = ctx.shape[1]
    depth = w_in.shape[0]
    geo = _Geom(bsz, n_ctx, seq, d)
    assert bsz + 1 <= 8 and seq % GRID_W == 0 and n_ctx % GLA_CHUNK == 0 and seq % GLA_CHUNK == 0

    w_in_p, off = _prep_w_in(w_in, d)
    wqa, wqb, wka, sela, selb, wv = _prep_mla(w_uq, w_ukv)
    tabs = _rope_tables(geo)
    w_br_half = [_to_bf16(w, 0.5) for w in (w_mla_o, w_cf_o, w_gla_o, w_sc_o)]
    w_mix = _to_bf16(w_mix_o)
    gate_w = jnp.zeros((depth, 2, 128, GLA_HEADS * GLA_DK), F32)
    gate_w = gate_w.at[:, 0, :GLA_GATE_RANK].set(w_gla_gf).at[:, 1, GLA_GATE_RANK:2 * GLA_GATE_RANK].set(w_gla_gb)
    gate_w = gate_w.astype(BF16)
    gate_b = jnp.stack([b_gla_gf, b_gla_gb], axis=1)[:, :, None, :]
    mat3, msk4 = _gla_tables()
    wr_t = w_router.T
    wr_hi = wr_t.astype(BF16)
    wr_stack = jnp.concatenate([wr_hi, (wr_t - wr_hi.astype(F32)).astype(BF16)], axis=0)
    vec = lambda a: a.reshape(a.shape[0], 1, a.shape[-1])

    cc = jnp.zeros((8, d), F32).at[:bsz].set(c).at[bsz].set(c_ctx)
    mod = _modulation(cc, w_mod, b_mod).reshape(depth, 8, 1, 6 * d)

    xa = jnp.concatenate([ctx, x], axis=1).reshape(geo.m, d)
    for l in range(depth):
        h = _norm_mod(geo, xa, vec(g_mix), mod, l, 0, 1)
        u = _matmul(h, w_in_p, l, BF16)
        q, k, v = _mla_qkv(geo, u, off, l, vec(g_q), vec(g_kv), wqa, wqb, wka, sela, selb, wv, tabs)
        attn = _mla_attention(geo, q, k, v)
        conf = _conformer(geo, u, off, l, w_cf_dw, vec(b_cf_dw), vec(g_cf_ln), vec(b_cf_ln))
        sconv = _shortconv(geo, u, off, l, w_sc_conv)
        gla_f, gla_b = _gla_scan(geo, u, off, l, gate_w, gate_b, mat3, msk4)
        xa = _merge_mix(geo, attn, conf, gla_f, gla_b, u, off, vec(g_gla_norm), sconv, w_br_half, w_mix, xa, mod, l)
        xa = _moe(geo, xa, vec(g_ffn), mod, l, wr_stack, b_router.reshape(N_EXPERTS, 1), w_e_gate, w_e_up, w_e_down)
    return _final_norm(geo, xa, g_final)
```

```python
import functools

import numpy as np
import jax
import jax.numpy as jnp
from jax import lax
from jax.experimental import pallas as pl
from jax.experimental.pallas import tpu as pltpu

F32 = jnp.float32
BF16 = jnp.bfloat16
EPS = 1e-6

GRID_W = 64
ROPE_THETA = 10000.0
MLA_HEADS = 8
MLA_NOPE = 64
MLA_ROPE = 32
MLA_V = 64
MLA_Q_RANK = 512
MLA_KV_RANK = 256
MLA_SLOT = 128
CF_WIDTH = 512
CF_KERNEL = 31
GLA_HEADS = 4
GLA_DK = 64
GLA_DV = 128
GLA_GATE_RANK = 16
GLA_GATE_TAU = 16.0
GLA_CHUNK = 64
GLA_LEVELS = 6
SC_WIDTH = 512
SC_KERNEL = 3
N_BRANCHES = 4
N_EXPERTS = 16
N_GROUPS = 4
EXPERT_FF = 512
CONV_HALO = 16
SUBLANES = 8
MERGE_COLS = 512
VMEM_LIMIT = 56 * 1024 * 1024


def _cparams(*sem):
    return pltpu.CompilerParams(dimension_semantics=sem, vmem_limit_bytes=VMEM_LIMIT)


def _pick(n, cands):
    for c in cands:
        if n % c == 0:
            return c
    raise ValueError(f"no tile for {n}")


def _dot(a, b):
    return jnp.dot(a, b, preferred_element_type=F32)


def _dot_nt(a, b):
    return lax.dot_general(a, b, (((1,), (1,)), ((), ())), preferred_element_type=F32)


def _proj_layout():
    parts = [("sc_b", SC_WIDTH), ("sc_c", SC_WIDTH), ("sc_h", SC_WIDTH),
             ("cf_a", CF_WIDTH), ("cf_b", CF_WIDTH), ("cq", MLA_Q_RANK), ("gla_v", GLA_HEADS * GLA_DV),
             ("gla_r", GLA_HEADS * GLA_DV), ("ckv", MLA_KV_RANK), ("gla_q", GLA_HEADS * GLA_DK),
             ("gla_k", GLA_HEADS * GLA_DK), ("kr", 128), ("gate", 128)]
    off, start = {}, 0
    for name, width in parts:
        assert start % width == 0
        off[name] = start
        start += width
    return off, start


def _src_layout(d):
    parts = (("cq", MLA_Q_RANK), ("ckv", MLA_KV_RANK), ("kr", MLA_ROPE), ("cf", 2 * CF_WIDTH),
             ("gla_q", GLA_HEADS * GLA_DK), ("gla_k", GLA_HEADS * GLA_DK), ("gla_v", GLA_HEADS * GLA_DV),
             ("gf", GLA_GATE_RANK), ("gb", GLA_GATE_RANK), ("gla_r", GLA_HEADS * GLA_DV),
             ("sc", 3 * SC_WIDTH), ("merge", N_BRANCHES * d))
    off, start = {}, 0
    for name, width in parts:
        off[name] = (start, start + width)
        start += width
    return off


def _mod_kernel(c_ref, w_ref, b_ref, o_ref):
    c = c_ref[...]
    a = (c * jax.nn.sigmoid(c)).astype(BF16)
    o_ref[...] = _dot(a, w_ref[...].astype(BF16)) + b_ref[...]


def _modulation(cc, w_mod, b_mod):
    n_l, d, n6 = w_mod.shape
    tn = _pick(n6, (1024, 512, 256, 128))
    return pl.pallas_call(
        _mod_kernel,
        out_shape=jax.ShapeDtypeStruct((n_l, 8, n6), F32),
        grid=(n_l, n6 // tn),
        in_specs=[pl.BlockSpec((8, d), lambda l, j: (0, 0)),
                  pl.BlockSpec((None, d, tn), lambda l, j: (l, 0, j)),
                  pl.BlockSpec((None, 1, tn), lambda l, j: (l, 0, j))],
        out_specs=pl.BlockSpec((None, 8, tn), lambda l, j: (l, 0, j)),
        name="adaln_mod",
        compiler_params=_cparams("parallel", "parallel"),
    )(cc, w_mod, b_mod.reshape(n_l, 1, n6))


class _Geom:
    def __init__(self, bsz, n_ctx, seq, d):
        self.b, self.c, self.s, self.d = bsz, n_ctx, seq, d
        self.n = n_ctx + seq
        self.m = bsz * self.n
        self.tr = _pick(int(np.gcd(n_ctx, seq)), (256, 128, 64))
        self.nt = self.n // self.tr
        self.nct = n_ctx // self.tr

    def mod_row(self, i):
        b, j = i // self.nt, i % self.nt
        return jnp.where(j < self.nct, self.b, b)

    def mod_spec(self, layer, part):
        return pl.BlockSpec((None, None, 1, self.d), lambda i, *_: (layer, self.mod_row(i), 0, part))


def _norm_mod_kernel(x_ref, g_ref, sh_ref, sc_ref, o_ref):
    x = x_ref[...]
    y = x * lax.rsqrt(jnp.mean(x * x, axis=-1, keepdims=True) + EPS) * g_ref[...]
    o_ref[...] = (y * (1.0 + sc_ref[...]) + sh_ref[...]).astype(o_ref.dtype)


def _norm_mod(geo, x, g, mod, layer, sh_part, sc_part):
    tr, d = geo.tr, geo.d
    return pl.pallas_call(
        _norm_mod_kernel,
        out_shape=jax.ShapeDtypeStruct((geo.m, d), BF16),
        grid=(geo.m // tr,),
        in_specs=[pl.BlockSpec((tr, d), lambda i: (i, 0)),
                  pl.BlockSpec((None, 1, d), lambda i: (layer, 0, 0)),
                  geo.mod_spec(layer, sh_part), geo.mod_spec(layer, sc_part)],
        out_specs=pl.BlockSpec((tr, d), lambda i: (i, 0)),
        name="norm_mod",
        compiler_params=_cparams("parallel"),
    )(x, g, mod, mod)


def _mm_kernel(a_ref, w_ref, o_ref):
    o_ref[...] = _dot(a_ref[...], w_ref[...]).astype(o_ref.dtype)


def _matmul(a, w, layer, out_dtype):
    m, k = a.shape
    n = w.shape[-1]
    tm = _pick(m, (1152, 1024, 512, 256, 128))
    tn = _pick(n, (1024, 512, 256, 128))
    return pl.pallas_call(
        _mm_kernel,
        out_shape=jax.ShapeDtypeStruct((m, n), out_dtype),
        grid=(n // tn, m // tm),
        in_specs=[pl.BlockSpec((tm, k), lambda j, i: (i, 0)),
                  pl.BlockSpec((None, k, tn), lambda j, i: (layer, 0, j))],
        out_specs=pl.BlockSpec((tm, tn), lambda j, i: (i, j)),
        name="proj_in",
        compiler_params=_cparams("parallel", "parallel"),
    )(a, w)


def _rms(x, g):
    return x * lax.rsqrt(jnp.mean(x * x, axis=-1, keepdims=True) + EPS) * g


def _qkv_kernel(cq_ref, ckv_ref, kr_ref, gq_ref, gkv_ref, wqa_ref, wqb_ref, wka_ref, sela_ref, selb_ref,
                wv_ref, qca_ref, qsb_ref, kca_ref, ksb_ref, q_ref, k_ref, v_ref):
    cqn = _rms(cq_ref[...].astype(F32), gq_ref[...]).astype(BF16)
    ckvn = _rms(ckv_ref[...].astype(F32), gkv_ref[...]).astype(BF16)
    kr = kr_ref[...]
    qa = _dot(cqn, wqa_ref[...])
    qb = _dot(cqn, wqb_ref[...])
    ka = _dot(ckvn, wka_ref[...]) + _dot(kr, sela_ref[...])
    kb = _dot(kr, selb_ref[...])
    qca, qsb, kca, ksb = qca_ref[...], qsb_ref[...], kca_ref[...], ksb_ref[...]
    for h in range(MLA_HEADS):
        sl = slice(h * MLA_SLOT, (h + 1) * MLA_SLOT)
        q_ref[:, sl] = (qa[:, sl] * qca + qb[:, sl] * qsb).astype(BF16)
        k_ref[:, sl] = (ka[:, sl] * kca + kb[:, sl] * ksb).astype(BF16)
    v_ref[...] = _dot(ckvn, wv_ref[...]).astype(BF16)


def _mla_qkv(geo, u, off, layer, gq, gkv, wqa, wqb, wka, sela, selb, wv, tabs):
    tr, m = geo.tr, geo.m
    hq = MLA_HEADS * MLA_SLOT
    hv = MLA_HEADS * MLA_V
    row = lambda w, o: pl.BlockSpec((tr, w), lambda i: (i, o // w))
    lw = lambda a: pl.BlockSpec((None,) + a.shape[1:], lambda i: (layer,) + (0,) * (a.ndim - 1))
    cw = lambda a: pl.BlockSpec(a.shape, lambda i: (0,) * a.ndim)
    tab = pl.BlockSpec((tr, MLA_SLOT), lambda i: (i % geo.nt, 0))
    return pl.pallas_call(
        _qkv_kernel,
        out_shape=(jax.ShapeDtypeStruct((m, hq), BF16), jax.ShapeDtypeStruct((m, hq), BF16),
                   jax.ShapeDtypeStruct((m, hv), BF16)),
        grid=(m // tr,),
        in_specs=[row(MLA_Q_RANK, off["cq"]), row(MLA_KV_RANK, off["ckv"]), row(128, off["kr"]),
                  lw(gq), lw(gkv), lw(wqa), lw(wqb), lw(wka), cw(sela), cw(selb), lw(wv), tab, tab, tab, tab],
        out_specs=(pl.BlockSpec((tr, hq), lambda i: (i, 0)), pl.BlockSpec((tr, hq), lambda i: (i, 0)),
                   pl.BlockSpec((tr, hv), lambda i: (i, 0))),
        name="mla_qkv",
        compiler_params=_cparams("parallel"),
    )(u, u, u, gq, gkv, wqa, wqb, wka, sela, selb, wv, *tabs)


def _attn_kernel(q_ref, k_ref, v_ref, o_ref, *, n_ctx, n_all, n_ctx_tiles):
    qi = pl.program_id(2)
    tq = q_ref.shape[0]
    lane = lax.broadcasted_iota(jnp.int32, (tq, 2 * MLA_V), 1)

    def run(nk):
        outs = []
        for hh in range(2):
            sl = slice(hh * MLA_SLOT, (hh + 1) * MLA_SLOT)
            s = _dot_nt(q_ref[:, sl], k_ref[0:nk, sl])
            p = jnp.exp(s - jnp.max(s, axis=-1, keepdims=True))
            inv = 1.0 / jnp.sum(p, axis=-1, keepdims=True)
            outs.append(_dot(p.astype(BF16), v_ref[0:nk, :]) * inv)
        o_ref[...] = jnp.where(lane < MLA_V, outs[0], outs[1]).astype(o_ref.dtype)

    @pl.when(qi < n_ctx_tiles)
    def _():
        run(n_ctx)

    @pl.when(qi >= n_ctx_tiles)
    def _():
        run(n_all)


def _mla_attention(geo, q, k, v):
    tq, n, bsz = geo.tr, geo.n, geo.b
    q3, k3, v3 = (t.reshape(bsz, n, t.shape[-1]) for t in (q, k, v))
    kern = functools.partial(_attn_kernel, n_ctx=geo.c, n_all=n, n_ctx_tiles=geo.nct)
    out = pl.pallas_call(
        kern,
        out_shape=jax.ShapeDtypeStruct((bsz, n, MLA_HEADS * MLA_V), BF16),
        grid=(bsz, MLA_HEADS // 2, geo.nt),
        in_specs=[pl.BlockSpec((None, tq, 2 * MLA_SLOT), lambda b, hp, i: (b, i, hp)),
                  pl.BlockSpec((None, n, 2 * MLA_SLOT), lambda b, hp, i: (b, 0, hp)),
                  pl.BlockSpec((None, n, 2 * MLA_V), lambda b, hp, i: (b, 0, hp))],
        out_specs=pl.BlockSpec((None, tq, 2 * MLA_V), lambda b, hp, i: (b, i, hp)),
        name="mla_attn",
        compiler_params=_cparams("parallel", "parallel", "parallel"),
    )(q3, k3, v3)
    return out.reshape(geo.m, MLA_HEADS * MLA_V)


def _halo_flags(geo, i):
    j = i % geo.nt
    left = jnp.logical_and(j != 0, j != geo.nct)
    right = jnp.logical_and(j != geo.nct - 1, j != geo.nt - 1)
    return left.astype(F32), right.astype(F32)


def _conv_shifts(n_taps):
    base = CONV_HALO - n_taps // 2
    return sorted({(base + kk) % SUBLANES for kk in range(n_taps)} - {0})


def _conv_taps(pad_ref, sh_ref, w_ref, n_taps, rows):
    base = CONV_HALO - n_taps // 2
    shifts = _conv_shifts(n_taps)
    span = sh_ref.shape[1]
    for s, r in enumerate(shifts):
        sh_ref[s, :, :] = pad_ref[r:r + span, :]

    def window(o):
        a, r = o - o % SUBLANES, o % SUBLANES
        return pad_ref[a:a + rows, :] if r == 0 else sh_ref[shifts.index(r), a:a + rows, :]

    acc = w_ref[0:1, :] * window(base)
    for kk in range(1, n_taps):
        acc = acc + w_ref[kk:kk + 1, :] * window(base + kk)
    return acc


def _conv_scratch(tr, width, n_taps):
    return [pltpu.VMEM((tr + 2 * CONV_HALO, width), F32),
            pltpu.VMEM((len(_conv_shifts(n_taps)), tr + 2 * CONV_HALO - SUBLANES, width), F32)]


def _conf_kernel(a_ref, b_ref, ap_ref, bp_ref, an_ref, bn_ref, w_ref, bias_ref, g_ref, beta_ref, o_ref,
                 pad_ref, sh_ref, *, geo):
    left, right = _halo_flags(geo, pl.program_id(0))
    rows = a_ref.shape[0]
    glu = lambda a, b: a[...].astype(F32) * jax.nn.sigmoid(b[...].astype(F32))
    pad_ref[0:CONV_HALO, :] = glu(ap_ref, bp_ref) * left
    pad_ref[CONV_HALO:CONV_HALO + rows, :] = glu(a_ref, b_ref)
    pad_ref[CONV_HALO + rows:2 * CONV_HALO + rows, :] = glu(an_ref, bn_ref) * right
    h = _conv_taps(pad_ref, sh_ref, w_ref, CF_KERNEL, rows) + bias_ref[...]
    hc = h - jnp.mean(h, axis=-1, keepdims=True)
    y = hc * lax.rsqrt(jnp.mean(hc * hc, axis=-1, keepdims=True) + EPS) * g_ref[...] + beta_ref[...]
    o_ref[...] = (y * jax.nn.sigmoid(y)).astype(o_ref.dtype)


def _sconv_kernel(gb_ref, gc_ref, h_ref, gcp_ref, hp_ref, gcn_ref, hn_ref, w_ref, o_ref, pad_ref, sh_ref, *, geo):
    left, right = _halo_flags(geo, pl.program_id(0))
    rows = h_ref.shape[0]
    prod = lambda a, b: a[...].astype(F32) * b[...].astype(F32)
    pad_ref[0:CONV_HALO, :] = prod(gcp_ref, hp_ref) * left
    pad_ref[CONV_HALO:CONV_HALO + rows, :] = prod(gc_ref, h_ref)
    pad_ref[CONV_HALO + rows:2 * CONV_HALO + rows, :] = prod(gcn_ref, hn_ref) * right
    y = _conv_taps(pad_ref, sh_ref, w_ref, SC_KERNEL, rows)
    o_ref[...] = (gb_ref[...].astype(F32) * y).astype(o_ref.dtype)


def _halo_specs(geo, width, offset):
    tr = geo.tr
    per = tr // CONV_HALO
    last = geo.m // CONV_HALO - 1
    cur = pl.BlockSpec((tr, width), lambda i: (i, offset // width))
    prev = pl.BlockSpec((CONV_HALO, width), lambda i: (jnp.maximum(i * per - 1, 0), offset // width))
    nxt = pl.BlockSpec((CONV_HALO, width), lambda i: (jnp.minimum((i + 1) * per, last), offset // width))
    return cur, prev, nxt


def _conformer(geo, u, off, layer, w_dw, b_dw, g_ln, b_ln):
    tr = geo.tr
    a_c, a_p, a_n = _halo_specs(geo, CF_WIDTH, off["cf_a"])
    b_c, b_p, b_n = _halo_specs(geo, CF_WIDTH, off["cf_b"])
    lw = lambda a: pl.BlockSpec((None,) + a.shape[1:], lambda i: (layer,) + (0,) * (a.ndim - 1))
    return pl.pallas_call(
        functools.partial(_conf_kernel, geo=geo),
        out_shape=jax.ShapeDtypeStruct((geo.m, CF_WIDTH), BF16),
        grid=(geo.m // tr,),
        in_specs=[a_c, b_c, a_p, b_p, a_n, b_n, lw(w_dw), lw(b_dw), lw(g_ln), lw(b_ln)],
        out_specs=pl.BlockSpec((tr, CF_WIDTH), lambda i: (i, 0)),
        scratch_shapes=_conv_scratch(tr, CF_WIDTH, CF_KERNEL),
        name="conformer",
        compiler_params=_cparams("parallel"),
    )(u, u, u, u, u, u, w_dw, b_dw, g_ln, b_ln)


def _shortconv(geo, u, off, layer, w_conv):
    tr = geo.tr
    gb_c, _, _ = _halo_specs(geo, SC_WIDTH, off["sc_b"])
    gc_c, gc_p, gc_n = _halo_specs(geo, SC_WIDTH, off["sc_c"])
    h_c, h_p, h_n = _halo_specs(geo, SC_WIDTH, off["sc_h"])
    lw = lambda a: pl.BlockSpec((None,) + a.shape[1:], lambda i: (layer,) + (0,) * (a.ndim - 1))
    return pl.pallas_call(
        functools.partial(_sconv_kernel, geo=geo),
        out_shape=jax.ShapeDtypeStruct((geo.m, SC_WIDTH), BF16),
        grid=(geo.m // tr,),
        in_specs=[gb_c, gc_c, h_c, gc_p, h_p, gc_n, h_n, lw(w_conv)],
        out_specs=pl.BlockSpec((tr, SC_WIDTH), lambda i: (i, 0)),
        scratch_shapes=_conv_scratch(tr, SC_WIDTH, SC_KERNEL),
        name="shortconv",
        compiler_params=_cparams("parallel"),
    )(u, u, u, u, u, u, u, w_conv)


def _gla_tables():
    c = GLA_CHUNK
    n_rows = (GLA_LEVELS + 2) * c + 8
    mat = np.zeros((2, n_rows, c), np.float32)
    msk = np.zeros((2, GLA_LEVELS + 1, c, c), np.float32)
    for lvl in range(GLA_LEVELS):
        hs = (c // 2) >> lvl
        for t in range(c):
            mid = (t // (2 * hs)) * 2 * hs + hs
            if t >= mid:
                mat[0, lvl * c + t, mid + 1:t + 1] = 1.0
                msk[0, lvl, t, mid - hs:mid] = 1.0
            else:
                mat[0, lvl * c + t, t + 1:mid + 1] = 1.0
    msk[0, GLA_LEVELS] = np.eye(c, dtype=np.float32)
    for t in range(c):
        mat[0, GLA_LEVELS * c + t, :t + 1] = 1.0
        mat[0, (GLA_LEVELS + 1) * c + t, t + 1:] = 1.0
    mat[0, (GLA_LEVELS + 2) * c:, :] = 1.0
    n_blk = GLA_LEVELS + 2
    mat[1, :n_blk * c] = mat[0, :n_blk * c].reshape(n_blk, c, c)[:, ::-1, ::-1].reshape(n_blk * c, c)
    mat[1, n_blk * c:] = 1.0
    msk[1] = msk[0][:, ::-1, ::-1]
    mat3 = np.concatenate([mat, mat, mat], axis=2)
    msk4 = np.tile(msk, (1, 1, GLA_HEADS, 1))
    return jnp.asarray(mat3, BF16), jnp.asarray(msk4, F32)


def _gla_kernel(qf_ref, kf_ref, vf_ref, gf_ref, qb_ref, kb_ref, vb_ref, gb_ref, w2_ref, b2_ref, mat_ref, msk_ref,
                of_ref, ob_ref, st_ref):
    @pl.when(pl.program_id(0) == 0)
    def _():
        st_ref[...] = jnp.zeros_like(st_ref)

    ins = ((qf_ref, kf_ref, vf_ref, gf_ref, of_ref), (qb_ref, kb_ref, vb_ref, gb_ref, ob_ref))
    for b in range(qf_ref.shape[0]):
        for d, (q_ref, k_ref, v_ref, g_ref, o_ref) in enumerate(ins):
            _gla_chunk(q_ref.at[b], k_ref.at[b], v_ref.at[b], g_ref.at[b], w2_ref.at[d], b2_ref.at[d],
                       mat_ref.at[d], msk_ref.at[d], o_ref.at[b], st_ref.at[d, b])


def _gla_chunk(q_ref, k_ref, v_ref, g_ref, w2_ref, b2_ref, mat_ref, msk_ref, o_ref, st_ref):
    c = GLA_CHUNK
    z = _dot(g_ref[...], w2_ref[...]) + b2_ref[...]
    la = (jnp.minimum(z, 0.0) - jnp.log1p(jnp.exp(-jnp.abs(z)))) * (1.0 / GLA_GATE_TAU)
    hi = la.astype(BF16)
    r1 = la - hi.astype(F32)
    lo = r1.astype(BF16)
    lo2 = (r1 - lo.astype(F32)).astype(BF16)
    ea = jnp.exp(_dot(mat_ref[...], jnp.concatenate([hi, lo, lo2], axis=0)))

    head = lax.broadcasted_iota(jnp.int32, (1, GLA_HEADS * GLA_DK), 1) // GLA_DK
    zero = jnp.zeros((), BF16)

    def stack(x):
        xb = x.astype(BF16)
        return jnp.concatenate([jnp.where(head == h, xb, zero) for h in range(GLA_HEADS)], axis=0)

    q = q_ref[...].astype(F32) * (GLA_DK ** -0.5)
    k = k_ref[...].astype(F32)
    vb = v_ref[...]

    attn = _dot_nt(stack(q), k.astype(BF16)) * msk_ref[GLA_LEVELS]
    for lvl in range(GLA_LEVELS):
        e = ea[lvl * c:(lvl + 1) * c]
        attn = attn + _dot_nt(stack(q * e), (k * e).astype(BF16)) * msk_ref[lvl]
    o_intra = _dot(attn.astype(BF16), vb)

    st = st_ref[...]
    o_inter = _dot_nt(stack(q * ea[GLA_LEVELS * c:(GLA_LEVELS + 1) * c]), st.astype(BF16))
    for h in range(GLA_HEADS):
        rows = slice(h * c, (h + 1) * c)
        cols = slice(h * GLA_DV, (h + 1) * GLA_DV)
        o_ref[:, cols] = o_intra[rows, cols] + o_inter[rows, :]

    k_dec = stack(k * ea[(GLA_LEVELS + 1) * c:(GLA_LEVELS + 2) * c])
    v_rows = jnp.concatenate([vb[:, h * GLA_DV:(h + 1) * GLA_DV] for h in range(GLA_HEADS)], axis=0)
    eye = (lax.broadcasted_iota(jnp.int32, (GLA_DV, GLA_DV), 0)
           == lax.broadcasted_iota(jnp.int32, (GLA_DV, GLA_DV), 1)).astype(BF16)
    v_t = _dot_nt(eye, v_rows).astype(BF16)
    tot = ea[(GLA_LEVELS + 2) * c:(GLA_LEVELS + 2) * c + 1]
    st_ref[...] = st * tot + _dot(v_t, k_dec)


def _gla_scan(geo, u, off, layer, w2, b2, mat3, msk4):
    c = GLA_CHUNK
    ncc = geo.c // c
    nc = geo.n // c
    dk, dv = GLA_HEADS * GLA_DK, GLA_HEADS * GLA_DV

    fwd = lambda i: i
    bwd = lambda i: jnp.where(i < ncc, ncc - 1 - i, nc - 1 - (i - ncc))

    u3 = u.reshape(geo.b, geo.n, u.shape[-1])
    row = lambda w, o, chunk: pl.BlockSpec((geo.b, c, w), lambda i: (0, chunk(i), o // w))
    rows = lambda chunk: [row(dk, off["gla_q"], chunk), row(dk, off["gla_k"], chunk), row(dv, off["gla_v"], chunk),
                          row(128, off["gate"], chunk)]
    out_sds = jax.ShapeDtypeStruct((geo.b, geo.n, dv), F32)
    o_f, o_b = pl.pallas_call(
        _gla_kernel,
        out_shape=(out_sds, out_sds),
        grid=(nc,),
        in_specs=rows(fwd) + rows(bwd) + [
            pl.BlockSpec((None, 2, 128, dk), lambda i: (layer, 0, 0, 0)),
            pl.BlockSpec((None, 2, 1, dk), lambda i: (layer, 0, 0, 0)),
            pl.BlockSpec(mat3.shape, lambda i: (0, 0, 0)),
            pl.BlockSpec(msk4.shape, lambda i: (0, 0, 0, 0))],
        out_specs=(pl.BlockSpec((geo.b, c, dv), lambda i: (0, fwd(i), 0)),
                   pl.BlockSpec((geo.b, c, dv), lambda i: (0, bwd(i), 0))),
        scratch_shapes=[pltpu.VMEM((2, geo.b, GLA_DV, dk), F32)],
        name="gla_scan",
        compiler_params=_cparams("arbitrary"),
    )(u3, u3, u3, u3, u3, u3, u3, u3, w2, b2, mat3, msk4)
    return o_f.reshape(geo.m, dv), o_b.reshape(geo.m, dv)


def _merge_mix_kernel(attn_ref, conf_ref, glaf_ref, glab_ref, r_ref, gn_ref, sc_ref, gate_ref, w0_ref, w1_ref,
                      w2_ref, w3_ref, wmix_ref, x_ref, gt_ref, o_ref, y_ref):
    wbr_ref = (w0_ref, w1_ref, w2_ref, w3_ref)
    o = glaf_ref[...] + glab_ref[...]
    r = r_ref[...].astype(F32)
    gn = gn_ref[...]
    normed = [_rms(o[:, h * GLA_DV:(h + 1) * GLA_DV], gn) for h in range(GLA_HEADS)]
    gla = (jnp.concatenate(normed, axis=1) * (r * jax.nn.sigmoid(r))).astype(BF16)
    acts = (attn_ref[...], conf_ref[...], gla, sc_ref[...])
    d = x_ref.shape[1]
    cw = min(MERGE_COLS, d)
    for c0 in range(0, d, cw):
        y = None
        for i in range(N_BRANCHES):
            half_b = _dot(acts[i], wbr_ref[i][:, c0:c0 + cw])
            t = jnp.tanh(gate_ref[:, i * d + c0:i * d + c0 + cw].astype(F32))
            term = t * half_b + half_b
            y = term if y is None else y + term
        y_ref[:, c0:c0 + cw] = y.astype(BF16)
    y_all = y_ref[...]
    for c0 in range(0, d, cw):
        cols = slice(c0, c0 + cw)
        o_ref[:, cols] = x_ref[:, cols] + gt_ref[:, cols] * _dot(y_all, wmix_ref[:, cols])


def _merge_mix(geo, attn, conf, gla_f, gla_b, u, u_gates, off, g_norm, sconv, w_br_half, w_mix, x, mod, layer):
    tm, d, m = geo.tr, geo.d, geo.m
    w512 = GLA_HEADS * GLA_DV
    act = pl.BlockSpec((tm, w512), lambda i: (i, 0))
    once = pl.Buffered(1)
    w_br = pl.BlockSpec((None, w512, d), lambda i: (layer, 0, 0), pipeline_mode=once)
    return pl.pallas_call(
        _merge_mix_kernel,
        out_shape=jax.ShapeDtypeStruct((m, d), F32),
        grid=(m // tm,),
        in_specs=[act, act, act, act,
                  pl.BlockSpec((tm, w512), lambda i: (i, off["gla_r"] // w512)),
                  pl.BlockSpec((None, 1, GLA_DV), lambda i: (layer, 0, 0)),
                  act, pl.BlockSpec((tm, N_BRANCHES * d), lambda i: (i, 0)),
                  w_br, w_br, w_br, w_br,
                  pl.BlockSpec((None, d, d), lambda i: (layer, 0, 0), pipeline_mode=once),
                  pl.BlockSpec((tm, d), lambda i: (i, 0)), geo.mod_spec(layer, 2)],
        out_specs=pl.BlockSpec((tm, d), lambda i: (i, 0)),
        scratch_shapes=[pltpu.VMEM((tm, d), BF16)],
        name="merge_mix",
        compiler_params=_cparams("parallel"),
    )(attn, conf, gla_f, gla_b, u, g_norm, sconv, u_gates, *w_br_half, w_mix, x, mod)


def _cast_kernel(w_ref, o_ref, *, scale):
    o_ref[...] = (w_ref[...] * scale).astype(o_ref.dtype)


def _to_bf16(w, scale=1.0):
    n_l, rows, cols = w.shape
    tr = _pick(rows, (512, 256, 128))
    return pl.pallas_call(
        functools.partial(_cast_kernel, scale=scale),
        out_shape=jax.ShapeDtypeStruct(w.shape, BF16),
        grid=(n_l, rows // tr),
        in_specs=[pl.BlockSpec((None, tr, cols), lambda l, i: (l, i, 0))],
        out_specs=pl.BlockSpec((None, tr, cols), lambda l, i: (l, i, 0)),
        name="weight_cast",
        compiler_params=_cparams("parallel", "parallel"),
    )(w)


def _second_largest(a, b, c, d):
    return jnp.maximum(jnp.maximum(jnp.minimum(a, b), jnp.minimum(c, d)),
                       jnp.minimum(jnp.maximum(a, b), jnp.maximum(c, d)))


def _chunk_rows(rows, n, cc):
    return pl.ds(cc, rows, stride=n) if n > 1 else pl.ds(0, rows)


def _pack_rows(ref, y):
    rows = y.shape[0]
    n = ref.shape[0] // rows
    half = n * 128
    bits = lambda v: pltpu.bitcast(v.astype(BF16).astype(F32), jnp.uint32)
    packed = (bits(y[:, :half]) >> 16) | bits(y[:, half:])
    for cc in range(n):
        ref[_chunk_rows(rows, n, cc), :] = packed[:, cc * 128:(cc + 1) * 128]


def _unpack_cols(words):
    lo = pltpu.bitcast(words << 16, F32)
    hi = pltpu.bitcast(words & jnp.uint32(0xFFFF0000), F32)
    return lo, hi


def _ffn_pre_kernel(x_ref, g_ref, sh_ref, sc_ref, wr_ref, br_ref, tri_ref, h_ref, idx_ref, wgt_ref, rank_ref,
                    cnt_ref, carry_ref):
    @pl.when(pl.program_id(0) == 0)
    def _():
        carry_ref[...] = jnp.zeros_like(carry_ref)

    x = x_ref[...]
    y = _rms(x, g_ref[...]) * (1.0 + sc_ref[...]) + sh_ref[...]
    hi = y.astype(BF16)
    lo = (y - hi.astype(F32)).astype(BF16)
    _pack_rows(h_ref, y)
    ne = N_EXPERTS
    two = _dot_nt(wr_ref[...], hi)
    logits = two[0:ne] + two[ne:2 * ne] + _dot_nt(wr_ref[0:ne, :], lo)
    aff = 1.0 / (1.0 + jnp.exp(-logits))
    sel = aff + br_ref[...]
    per = ne // N_GROUPS
    gs = []
    for g in range(N_GROUPS):
        a, b, c, d = (sel[g * per + r:g * per + r + 1] for r in range(per))
        top1 = jnp.maximum(jnp.maximum(a, b), jnp.maximum(c, d))
        gs.append(top1 + _second_largest(a, b, c, d))
    best = gs[0]
    for g in range(1, N_GROUPS):
        best = jnp.maximum(best, gs[g])
    grp = jnp.full(best.shape, N_GROUPS, jnp.int32)
    for g in reversed(range(N_GROUPS)):
        grp = jnp.where(gs[g] == best, g, grp)
    erow = lax.broadcasted_iota(jnp.int32, sel.shape, 0)
    masked = jnp.where(erow // per == grp, sel, -jnp.inf)
    m1 = jnp.max(masked, axis=0, keepdims=True)
    i1 = jnp.min(jnp.where(masked == m1, erow, ne), axis=0, keepdims=True)
    one1 = erow == i1
    masked2 = jnp.where(one1, -jnp.inf, masked)
    m2 = jnp.max(masked2, axis=0, keepdims=True)
    i2 = jnp.min(jnp.where(masked2 == m2, erow, ne), axis=0, keepdims=True)
    one2 = erow == i2
    w1 = jnp.sum(jnp.where(one1, aff, 0.0), axis=0, keepdims=True)
    w2 = jnp.sum(jnp.where(one2, aff, 0.0), axis=0, keepdims=True)
    inv = 1.0 / (w1 + w2)
    idx_ref[0:1, :] = i1
    idx_ref[1:2, :] = i2
    wgt_ref[0:1, :] = w1 * inv
    wgt_ref[1:2, :] = w2 * inv
    o1 = jnp.where(one1, 1.0, 0.0)
    o2 = jnp.where(one2, 1.0, 0.0)
    tri = tri_ref[...]
    c0 = carry_ref[...]
    c1 = c0 + jnp.sum(o1, axis=1, keepdims=True)
    rank_ref[0:1, :] = jnp.sum(o1 * (c0 + _dot(o1.astype(BF16), tri)), axis=0, keepdims=True).astype(jnp.int32)
    rank_ref[1:2, :] = jnp.sum(o2 * (c1 + _dot(o2.astype(BF16), tri)), axis=0, keepdims=True).astype(jnp.int32)
    c2 = c1 + jnp.sum(o2, axis=1, keepdims=True)
    carry_ref[...] = c2
    cnt_ref[...] = c2


def _ffn_pre(geo, x, g, mod, layer, wr_stack, b_router):
    tr, d, m = geo.tr, geo.d, geo.m
    nch = d // 256
    tri = jnp.asarray(np.triu(np.ones((tr, tr), np.float32), 1), BF16)
    pair = lambda dt: jax.ShapeDtypeStruct((2, m), dt)
    pair_spec = pl.BlockSpec((2, tr), lambda i: (0, i))
    return pl.pallas_call(
        _ffn_pre_kernel,
        out_shape=(jax.ShapeDtypeStruct((m * nch, 128), jnp.uint32), pair(jnp.int32), pair(F32), pair(jnp.int32),
                   jax.ShapeDtypeStruct((N_EXPERTS, 1), F32)),
        grid=(m // tr,),
        in_specs=[pl.BlockSpec((tr, d), lambda i: (i, 0)),
                  pl.BlockSpec((None, 1, d), lambda i: (layer, 0, 0)),
                  geo.mod_spec(layer, 3), geo.mod_spec(layer, 4),
                  pl.BlockSpec(wr_stack.shape, lambda i: (0, 0)),
                  pl.BlockSpec(b_router.shape, lambda i: (0, 0)),
                  pl.BlockSpec((tr, tr), lambda i: (0, 0))],
        out_specs=(pl.BlockSpec((tr * nch, 128), lambda i: (i, 0)), pair_spec, pair_spec, pair_spec,
                   pl.BlockSpec((N_EXPERTS, 1), lambda i: (0, 0))),
        scratch_shapes=[pltpu.VMEM((N_EXPERTS, 1), F32)],
        name="ffn_pre",
        compiler_params=_cparams("arbitrary"),
    )(x, g, mod, mod, wr_stack, b_router, tri)


def _sorted_row(idx_ref, rank_ref, base_ref, a, n):
    return pl.multiple_of((base_ref[idx_ref[a]] + rank_ref[a]) * n, n)


def _dispatch_kernel(idx_ref, rank_ref, base_ref, last_ref, h_ref, xs_hbm, zero_ref, sem, zero_sem, *, m, n):
    i = pl.program_id(0)
    tr = h_ref.shape[0] // n

    @pl.when(i == 0)
    def _():
        zero_ref[...] = jnp.zeros_like(zero_ref)

        def fills(e):
            rows = zero_ref.shape[0]
            start = pl.multiple_of(last_ref[e] * n, rows)
            return pltpu.make_async_copy(zero_ref, xs_hbm.at[pl.ds(start, rows), :], zero_sem)

        for e in range(2 * N_EXPERTS):
            @pl.when(last_ref[e] >= 0)
            def _():
                fills(e).start()
        for e in range(2 * N_EXPERTS):
            @pl.when(last_ref[e] >= 0)
            def _():
                fills(e).wait()

    def issue(r, carry):
        src = h_ref.at[pl.ds(pl.multiple_of(r * n, n), n), :]
        for k in range(2):
            row = _sorted_row(idx_ref, rank_ref, base_ref, k * m + i * tr + r, n)
            pltpu.make_async_copy(src, xs_hbm.at[pl.ds(row, n), :], sem).start()
        return carry

    lax.fori_loop(0, tr, issue, 0, unroll=4)
    everything = xs_hbm.at[pl.ds(0, 2 * tr * n), :]
    pltpu.make_async_copy(everything, everything, sem).wait()


def _dispatch(hfp, ridx, rank, base, last_tile_row, n_rows, tr, tm):
    n = hfp.shape[0] * 2 // ridx.shape[0]
    m = hfp.shape[0] // n
    return pl.pallas_call(
        functools.partial(_dispatch_kernel, m=m, n=n),
        out_shape=jax.ShapeDtypeStruct((n_rows * n, 128), jnp.uint32),
        grid_spec=pltpu.PrefetchScalarGridSpec(
            num_scalar_prefetch=4, grid=(m // tr,),
            in_specs=[pl.BlockSpec((tr * n, 128), lambda i, *_: (i, 0))],
            out_specs=pl.BlockSpec(memory_space=pl.ANY),
            scratch_shapes=[pltpu.VMEM((tm * n, 128), jnp.uint32), pltpu.SemaphoreType.DMA(()),
                            pltpu.SemaphoreType.DMA(())]),
        name="moe_dispatch",
        compiler_params=_cparams("arbitrary"),
    )(ridx, rank, base, last_tile_row, hfp)


def _gmm_kernel(te_ref, nv_ref, xs_ref, wg_ref, wu_ref, wd_ref, ys_ref, wgb_ref, wub_ref, wdb_ref, prev_ref):
    j = pl.program_id(0)

    @pl.when(j == 0)
    def _():
        prev_ref[0] = -1

    @pl.when(j < nv_ref[0])
    def _():
        e = te_ref[j]

        @pl.when(e != prev_ref[0])
        def _():
            wgb_ref[...] = wg_ref[...].astype(BF16)
            wub_ref[...] = wu_ref[...].astype(BF16)
            wdb_ref[...] = wd_ref[...].astype(BF16)
            prev_ref[0] = e

        half = wgb_ref.shape[0] // 2
        n = half // 128
        tm = xs_ref.shape[0] // n
        parts = [_unpack_cols(xs_ref[_chunk_rows(tm, n, cc), :]) for cc in range(n)]
        xa = jnp.concatenate([p[0].astype(BF16) for p in parts], axis=1)
        xb = jnp.concatenate([p[1].astype(BF16) for p in parts], axis=1)
        g = _dot(xa, wgb_ref[0:half, :]) + _dot(xb, wgb_ref[half:2 * half, :])
        u = _dot(xa, wub_ref[0:half, :]) + _dot(xb, wub_ref[half:2 * half, :])
        act = (g * jax.nn.sigmoid(g) * u).astype(BF16)
        _pack_rows(ys_ref, _dot(act, wdb_ref[...]))

    @pl.when(j >= nv_ref[0])
    def _():
        ys_ref[...] = jnp.zeros_like(ys_ref)


def _gmm(xs, tile_expert, n_valid, wg, wu, wd, layer, tm):
    d, ff = wg.shape[-2], wg.shape[-1]
    nch = d // 256
    blk = tm * nch
    row = lambda j, te, nv: (jnp.minimum(j, nv[0] - 1), 0)
    wsel = lambda j, te, nv: (layer, te[jnp.minimum(j, nv[0] - 1)], 0, 0)
    return pl.pallas_call(
        _gmm_kernel,
        out_shape=jax.ShapeDtypeStruct(xs.shape, jnp.uint32),
        grid_spec=pltpu.PrefetchScalarGridSpec(
            num_scalar_prefetch=2, grid=(xs.shape[0] // blk,),
            in_specs=[pl.BlockSpec((blk, 128), row),
                      pl.BlockSpec((None, None, d, ff), wsel), pl.BlockSpec((None, None, d, ff), wsel),
                      pl.BlockSpec((None, None, ff, d), wsel)],
            out_specs=pl.BlockSpec((blk, 128), lambda j, te, nv: (j, 0)),
            scratch_shapes=[pltpu.VMEM((d, ff), BF16), pltpu.VMEM((d, ff), BF16), pltpu.VMEM((ff, d), BF16),
                            pltpu.SMEM((1,), jnp.int32)]),
        name="moe_experts",
        compiler_params=_cparams("arbitrary"),
    )(tile_expert, n_valid, xs, wg, wu, wd)


def _combine_kernel(idx_ref, rank_ref, base_ref, x_ref, w_ref, gt_ref, ys_hbm, *rest, m, with_norm):
    if with_norm:
        g_ref, sh_ref, sc_ref, o_ref, h_ref, buf_ref, sem_ref = rest
    else:
        o_ref, buf_ref, sem_ref = rest
    i = pl.program_id(0)
    n_tiles = pl.num_programs(0)
    tr = x_ref.shape[0]
    n = buf_ref.shape[2] // tr

    def issue(tile, slot):
        def body(r, carry):
            for k in range(2):
                row = _sorted_row(idx_ref, rank_ref, base_ref, k * m + tile * tr + r, n)
                pltpu.make_async_copy(ys_hbm.at[pl.ds(row, n), :],
                                      buf_ref.at[slot, k, pl.ds(pl.multiple_of(r * n, n), n), :],
                                      sem_ref.at[slot]).start()
            return carry
        lax.fori_loop(0, tr, body, 0, unroll=4)

    @pl.when(i == 0)
    def _():
        issue(0, 0)

    @pl.when(i + 1 < n_tiles)
    def _():
        issue(i + 1, (i + 1) % 2)

    slot = i % 2
    pltpu.make_async_copy(buf_ref.at[slot], buf_ref.at[slot], sem_ref.at[slot]).wait()
    w1, w2 = w_ref[:, 0:1], w_ref[:, 1:2]
    half = n * 128
    for cc in range(n):
        lo1, hi1 = _unpack_cols(buf_ref[slot, 0, _chunk_rows(tr, n, cc), :])
        lo2, hi2 = _unpack_cols(buf_ref[slot, 1, _chunk_rows(tr, n, cc), :])
        ca = slice(cc * 128, (cc + 1) * 128)
        cb = slice(half + cc * 128, half + (cc + 1) * 128)
        o_ref[:, ca] = x_ref[:, ca] + gt_ref[:, ca] * (w1 * lo1 + w2 * lo2)
        o_ref[:, cb] = x_ref[:, cb] + gt_ref[:, cb] * (w1 * hi1 + w2 * hi2)
    if with_norm:
        y = _rms(o_ref[...], g_ref[...]) * (1.0 + sc_ref[...]) + sh_ref[...]
        h_ref[...] = y.astype(h_ref.dtype)


def _combine(geo, x, ys, ridx, rank, base, w12, mod, layer, g_next):
    tr, d = geo.tr, geo.d
    nch = d // 256
    with_norm = g_next is not None
    tile = pl.BlockSpec((tr, d), lambda i, *_: (i, 0))
    in_specs = [tile, pl.BlockSpec((tr, 2), lambda i, *_: (i, 0)), geo.mod_spec(layer, 5),
                pl.BlockSpec(memory_space=pl.ANY)]
    operands = [x, w12, mod, ys]
    out_shape = [jax.ShapeDtypeStruct((geo.m, d), F32)]
    if with_norm:
        in_specs += [pl.BlockSpec((None, 1, d), lambda i, *_: (layer + 1, 0, 0)),
                     geo.mod_spec(layer + 1, 0), geo.mod_spec(layer + 1, 1)]
        operands += [g_next, mod, mod]
        out_shape.append(jax.ShapeDtypeStruct((geo.m, d), BF16))
    return pl.pallas_call(
        functools.partial(_combine_kernel, m=geo.m, with_norm=with_norm),
        out_shape=out_shape,
        grid_spec=pltpu.PrefetchScalarGridSpec(
            num_scalar_prefetch=3, grid=(geo.m // tr,),
            in_specs=in_specs, out_specs=[tile] * len(out_shape),
            scratch_shapes=[pltpu.VMEM((2, 2, tr * nch, 128), jnp.uint32), pltpu.SemaphoreType.DMA((2,))]),
        name="moe_combine",
        compiler_params=_cparams("arbitrary"),
    )(ridx, rank, base, *operands)


def _moe(geo, x, g, mod, layer, wr_stack, b_router, wg, wu, wd, g_next):
    tm = 256
    hfp, ridx, rwgt, rank, cnt = _ffn_pre(geo, x, g, mod, layer, wr_stack, b_router)
    n_tiles = (2 * geo.m) // tm + N_EXPERTS
    counts = cnt[:, 0].astype(jnp.int32)
    tiles_per = (counts + tm - 1) // tm
    tile_start = jnp.cumsum(tiles_per) - tiles_per
    base = tile_start * tm
    n_valid = jnp.sum(tiles_per).reshape(1)
    tile_expert = jnp.sum(jnp.arange(n_tiles)[:, None] >= tile_start[None, :], axis=1).astype(jnp.int32) - 1
    ridx, rank = ridx.reshape(-1), rank.reshape(-1)
    tail = n_valid + jnp.arange(N_EXPERTS)
    fill_rows = jnp.concatenate([jnp.where(tiles_per > 0, base + (tiles_per - 1) * tm, -1),
                                 jnp.where(tail < n_tiles, tail * tm, -1)])
    xs = _dispatch(hfp, ridx, rank, base, fill_rows, n_tiles * tm, geo.tr, tm)
    ys = _gmm(xs, tile_expert, n_valid, wg, wu, wd, layer, tm)
    return _combine(geo, x, ys, ridx, rank, base, rwgt.T, mod, layer, g_next)


def _final_kernel(x_ref, g_ref, o_ref):
    o_ref[...] = _rms(x_ref[...], g_ref[...])


def _final_norm(geo, x, g):
    tr, d = geo.tr, geo.d
    x3 = x.reshape(geo.b, geo.n, d)
    return pl.pallas_call(
        _final_kernel,
        out_shape=jax.ShapeDtypeStruct((geo.b, geo.s, d), F32),
        grid=(geo.b, geo.s // tr),
        in_specs=[pl.BlockSpec((None, tr, d), lambda b, i: (b, i + geo.nct, 0)),
                  pl.BlockSpec((1, d), lambda b, i: (0, 0))],
        out_specs=pl.BlockSpec((None, tr, d), lambda b, i: (b, i, 0)),
        name="final_norm",
        compiler_params=_cparams("parallel", "parallel"),
    )(x3, g.reshape(1, d))


def _pad_cols(w, width):
    return jnp.pad(w, [(0, 0)] * (w.ndim - 1) + [(0, width - w.shape[-1])])


def _prep_w_in(w_in, d):
    src = _src_layout(d)
    off, total = _proj_layout()
    cut = lambda name: w_in[..., src[name][0]:src[name][1]].astype(BF16)
    cf, sc = cut("cf"), cut("sc")
    gate = jnp.concatenate([cut("gf"), cut("gb")], axis=-1)
    pieces = {"sc_b": sc[..., :SC_WIDTH], "sc_c": sc[..., SC_WIDTH:2 * SC_WIDTH],
              "sc_h": sc[..., 2 * SC_WIDTH:], "cf_a": cf[..., :CF_WIDTH], "cf_b": cf[..., CF_WIDTH:],
              "cq": cut("cq"), "gla_v": cut("gla_v"), "gla_r": cut("gla_r"), "ckv": cut("ckv"),
              "gla_q": cut("gla_q"), "gla_k": cut("gla_k"), "kr": _pad_cols(cut("kr"), 128),
              "gate": _pad_cols(gate, 128)}
    order = sorted(off, key=off.get)
    rest = jnp.concatenate([pieces[name] for name in order], axis=-1)
    assert rest.shape[-1] == total
    half_gates = (w_in[..., src["merge"][0]:src["merge"][1]] * 0.5).astype(BF16)
    return half_gates, rest, off


def _rope_swap_perm():
    q = MLA_ROPE // 4
    return np.concatenate([np.arange(q, 2 * q), np.arange(0, q), np.arange(3 * q, 4 * q), np.arange(2 * q, 3 * q)])


def _prep_mla(w_uq, w_ukv):
    n_l = w_uq.shape[0]
    perm = _rope_swap_perm()
    wq = w_uq.reshape(n_l, MLA_Q_RANK, MLA_HEADS, MLA_NOPE + MLA_ROPE)
    qn, qr = wq[..., :MLA_NOPE], wq[..., MLA_NOPE:]
    zeros = lambda *s: jnp.zeros((n_l,) + s, w_uq.dtype)
    wqa = jnp.concatenate([qn, qr, qr], axis=-1)
    wqb = jnp.concatenate([zeros(MLA_Q_RANK, MLA_HEADS, MLA_NOPE), qr[..., perm],
                           zeros(MLA_Q_RANK, MLA_HEADS, MLA_ROPE)], axis=-1)
    wkv = w_ukv.reshape(n_l, MLA_KV_RANK, MLA_HEADS, MLA_NOPE + MLA_V)
    wka = jnp.concatenate([wkv[..., :MLA_NOPE], zeros(MLA_KV_RANK, MLA_HEADS, MLA_SLOT - MLA_NOPE)], axis=-1)
    wv = wkv[..., MLA_NOPE:]
    flat = lambda w: w.reshape(n_l, w.shape[1], -1).astype(BF16)
    sela = np.zeros((128, MLA_HEADS, MLA_SLOT), np.float32)
    selb = np.zeros((128, MLA_HEADS, MLA_SLOT), np.float32)
    for r in range(MLA_ROPE):
        sela[r, :, MLA_NOPE + r] = 1.0
        sela[r, :, MLA_NOPE + MLA_ROPE + r] = 1.0
        selb[perm[r], :, MLA_NOPE + r] = 1.0
    sel = lambda s: jnp.asarray(s.reshape(128, -1), BF16)
    return flat(wqa), flat(wqb), flat(wka), sel(sela), sel(selb), flat(wv)


def _rope_tables(geo):
    rows = geo.s // GRID_W
    pos = jnp.arange(rows * GRID_W)
    row = (pos // GRID_W).astype(F32)
    col = (pos % GRID_W).astype(F32)
    n_freq = MLA_ROPE // 4
    freqs = ROPE_THETA ** (-jnp.arange(n_freq, dtype=F32) / n_freq)
    cr, sr = jnp.cos(row[:, None] * freqs), jnp.sin(row[:, None] * freqs)
    cc, sn = jnp.cos(col[:, None] * freqs), jnp.sin(col[:, None] * freqs)
    cos32 = jnp.concatenate([cr, cr, cc, cc], axis=-1)
    sin32 = jnp.concatenate([-sr, sr, -sn, sn], axis=-1)
    s, c = geo.s, geo.c
    one = lambda n, w: jnp.ones((n, w), F32)
    zero = lambda n, w: jnp.zeros((n, w), F32)
    scale = (MLA_NOPE + MLA_ROPE) ** -0.5
    qca = jnp.concatenate([jnp.concatenate([one(c, 64), zero(c, 32), one(c, 32)], 1),
                           jnp.concatenate([one(s, 64), cos32, one(s, 32)], 1)], 0) * scale
    qsb = jnp.concatenate([zero(c, 128), jnp.concatenate([zero(s, 64), sin32, zero(s, 32)], 1)], 0) * scale
    kca = jnp.concatenate([jnp.concatenate([one(c, 64), zero(c, 32), one(c, 32)], 1),
                           jnp.concatenate([one(s, 64), cos32, zero(s, 32)], 1)], 0)
    ksb = jnp.concatenate([zero(c, 128), jnp.concatenate([zero(s, 64), sin32, zero(s, 32)], 1)], 0)
    return qca, qsb, kca, ksb


def kernel(x, c, ctx, c_ctx, w_mod, b_mod, g_mix, g_ffn, w_in, g_q, w_uq, g_kv, w_ukv, w_mla_o, w_cf_dw, b_cf_dw, g_cf_ln, b_cf_ln, w_cf_o, w_gla_gf, b_gla_gf, w_gla_gb, b_gla_gb, g_gla_norm, w_gla_o, w_sc_conv, w_sc_o, w_mix_o, w_router, b_router, w_e_gate, w_e_up, w_e_down, g_final):
    bsz, seq, d = x.shape
    n_ctx = ctx.shape[1]
    depth = w_in.shape[0]
    geo = _Geom(bsz, n_ctx, seq, d)
    assert bsz + 1 <= 8 and seq % GRID_W == 0 and n_ctx % GLA_CHUNK == 0 and seq % GLA_CHUNK == 0

    w_gates, w_rest, off = _prep_w_in(w_in, d)
    wqa, wqb, wka, sela, selb, wv = _prep_mla(w_uq, w_ukv)
    tabs = _rope_tables(geo)
    w_br_half = [_to_bf16(w, 0.5) for w in (w_mla_o, w_cf_o, w_gla_o, w_sc_o)]
    w_mix = _to_bf16(w_mix_o)
    gate_w = jnp.zeros((depth, 2, 128, GLA_HEADS * GLA_DK), F32)
    gate_w = gate_w.at[:, 0, :GLA_GATE_RANK].set(w_gla_gf).at[:, 1, GLA_GATE_RANK:2 * GLA_GATE_RANK].set(w_gla_gb)
    gate_w = gate_w.astype(BF16)
    gate_b = jnp.stack([b_gla_gf, b_gla_gb], axis=1)[:, :, None, :]
    mat3, msk4 = _gla_tables()
    wr_t = w_router.T
    wr_hi = wr_t.astype(BF16)
    wr_stack = jnp.concatenate([wr_hi, (wr_t - wr_hi.astype(F32)).astype(BF16)], axis=0)
    vec = lambda a: a.reshape(a.shape[0], 1, a.shape[-1])

    cc = jnp.zeros((8, d), F32).at[:bsz].set(c).at[bsz].set(c_ctx)
    mod = _modulation(cc, w_mod, b_mod).reshape(depth, 8, 1, 6 * d)

    xa = jnp.concatenate([ctx, x], axis=1).reshape(geo.m, d)
    h = _norm_mod(geo, xa, vec(g_mix), mod, 0, 0, 1)
    for l in range(depth):
        u = _matmul(h, w_rest, l, BF16)
        u_gates = _matmul(h, w_gates, l, BF16)
        q, k, v = _mla_qkv(geo, u, off, l, vec(g_q), vec(g_kv), wqa, wqb, wka, sela, selb, wv, tabs)
        attn = _mla_attention(geo, q, k, v)
        conf = _conformer(geo, u, off, l, w_cf_dw, vec(b_cf_dw), vec(g_cf_ln), vec(b_cf_ln))
        sconv = _shortconv(geo, u, off, l, w_sc_conv)
        gla_f, gla_b = _gla_scan(geo, u, off, l, gate_w, gate_b, mat3, msk4)
        xa = _merge_mix(geo, attn, conf, gla_f, gla_b, u, u_gates, off, vec(g_gla_norm), sconv, w_br_half, w_mix,
                        xa, mod, l)
        outs = _moe(geo, xa, vec(g_ffn), mod, l, wr_stack, b_router.reshape(N_EXPERTS, 1), w_e_gate, w_e_up,
                    w_e_down, vec(g_mix) if l + 1 < depth else None)
        xa = outs[0]
        if l + 1 < depth:
            h = outs[1]
    return _final_norm(geo, xa, g_final)
```

```python
import functools

import numpy as np
import jax
import jax.numpy as jnp
from jax import lax
from jax.experimental import pallas as pl
from jax.experimental.pallas import tpu as pltpu

F32 = jnp.float32
BF16 = jnp.bfloat16
EPS = 1e-6

GRID_W = 64
ROPE_THETA = 10000.0
MLA_HEADS = 8
MLA_NOPE = 64
MLA_ROPE = 32
MLA_V = 64
MLA_Q_RANK = 512
MLA_KV_RANK = 256
MLA_SLOT = 128
CF_WIDTH = 512
CF_KERNEL = 31
GLA_HEADS = 4
GLA_DK = 64
GLA_DV = 128
GLA_GATE_RANK = 16
GLA_GATE_TAU = 16.0
GLA_CHUNK = 64
GLA_LEVELS = 6
SC_WIDTH = 512
SC_KERNEL = 3
N_BRANCHES = 4
N_EXPERTS = 16
N_GROUPS = 4
EXPERT_FF = 512
CONV_HALO = 16
SUBLANES = 8
MERGE_COLS = 512
VMEM_LIMIT = 56 * 1024 * 1024


def _cparams(*sem):
    return pltpu.CompilerParams(dimension_semantics=sem, vmem_limit_bytes=VMEM_LIMIT)


def _pick(n, cands):
    for c in cands:
        if n % c == 0:
            return c
    raise ValueError(f"no tile for {n}")


def _dot(a, b):
    return jnp.dot(a, b, preferred_element_type=F32)


def _dot_nt(a, b):
    return lax.dot_general(a, b, (((1,), (1,)), ((), ())), preferred_element_type=F32)


def _proj_layout():
    parts = [("sc_b", SC_WIDTH), ("sc_c", SC_WIDTH), ("sc_h", SC_WIDTH),
             ("cf_a", CF_WIDTH), ("cf_b", CF_WIDTH), ("cq", MLA_Q_RANK), ("gla_v", GLA_HEADS * GLA_DV),
             ("gla_r", GLA_HEADS * GLA_DV), ("ckv", MLA_KV_RANK), ("gla_q", GLA_HEADS * GLA_DK),
             ("gla_k", GLA_HEADS * GLA_DK), ("kr", 128), ("gate", 128)]
    off, start = {}, 0
    for name, width in parts:
        assert start % width == 0
        off[name] = start
        start += width
    return off, start


def _src_layout(d):
    parts = (("cq", MLA_Q_RANK), ("ckv", MLA_KV_RANK), ("kr", MLA_ROPE), ("cf", 2 * CF_WIDTH),
             ("gla_q", GLA_HEADS * GLA_DK), ("gla_k", GLA_HEADS * GLA_DK), ("gla_v", GLA_HEADS * GLA_DV),
             ("gf", GLA_GATE_RANK), ("gb", GLA_GATE_RANK), ("gla_r", GLA_HEADS * GLA_DV),
             ("sc", 3 * SC_WIDTH), ("merge", N_BRANCHES * d))
    off, start = {}, 0
    for name, width in parts:
        off[name] = (start, start + width)
        start += width
    return off


def _mod_kernel(c_ref, w_ref, b_ref, o_ref):
    c = c_ref[...]
    a = (c * jax.nn.sigmoid(c)).astype(BF16)
    o_ref[...] = _dot(a, w_ref[...].astype(BF16)) + b_ref[...]


def _modulation(cc, w_mod, b_mod):
    n_l, d, n6 = w_mod.shape
    tn = _pick(n6, (1024, 512, 256, 128))
    return pl.pallas_call(
        _mod_kernel,
        out_shape=jax.ShapeDtypeStruct((n_l, 8, n6), F32),
        grid=(n_l, n6 // tn),
        in_specs=[pl.BlockSpec((8, d), lambda l, j: (0, 0)),
                  pl.BlockSpec((None, d, tn), lambda l, j: (l, 0, j)),
                  pl.BlockSpec((None, 1, tn), lambda l, j: (l, 0, j))],
        out_specs=pl.BlockSpec((None, 8, tn), lambda l, j: (l, 0, j)),
        name="adaln_mod",
        compiler_params=_cparams("parallel", "parallel"),
    )(cc, w_mod, b_mod.reshape(n_l, 1, n6))


class _Geom:
    def __init__(self, bsz, n_ctx, seq, d):
        self.b, self.c, self.s, self.d = bsz, n_ctx, seq, d
        self.n = n_ctx + seq
        self.m = bsz * self.n
        self.tr = _pick(int(np.gcd(n_ctx, seq)), (256, 128, 64))
        self.nt = self.n // self.tr
        self.nct = n_ctx // self.tr

    def mod_row(self, i):
        b, j = i // self.nt, i % self.nt
        return jnp.where(j < self.nct, self.b, b)

    def mod_spec(self, layer, part):
        return pl.BlockSpec((None, None, 1, self.d), lambda i, *_: (layer, self.mod_row(i), 0, part))


def _norm_mod_kernel(x_ref, g_ref, sh_ref, sc_ref, o_ref):
    x = x_ref[...]
    y = x * lax.rsqrt(jnp.mean(x * x, axis=-1, keepdims=True) + EPS) * g_ref[...]
    o_ref[...] = (y * (1.0 + sc_ref[...]) + sh_ref[...]).astype(o_ref.dtype)


def _norm_mod(geo, x, g, mod, layer, sh_part, sc_part):
    tr, d = geo.tr, geo.d
    return pl.pallas_call(
        _norm_mod_kernel,
        out_shape=jax.ShapeDtypeStruct((geo.m, d), BF16),
        grid=(geo.m // tr,),
        in_specs=[pl.BlockSpec((tr, d), lambda i: (i, 0)),
                  pl.BlockSpec((None, 1, d), lambda i: (layer, 0, 0)),
                  geo.mod_spec(layer, sh_part), geo.mod_spec(layer, sc_part)],
        out_specs=pl.BlockSpec((tr, d), lambda i: (i, 0)),
        name="norm_mod",
        compiler_params=_cparams("parallel"),
    )(x, g, mod, mod)


def _mm_kernel(a_ref, w_ref, o_ref):
    o_ref[...] = _dot(a_ref[...], w_ref[...]).astype(o_ref.dtype)


def _matmul(a, w, layer, out_dtype):
    m, k = a.shape
    n = w.shape[-1]
    tm = _pick(m, (1152, 1024, 512, 256, 128))
    tn = _pick(n, (1024, 512, 256, 128))
    return pl.pallas_call(
        _mm_kernel,
        out_shape=jax.ShapeDtypeStruct((m, n), out_dtype),
        grid=(n // tn, m // tm),
        in_specs=[pl.BlockSpec((tm, k), lambda j, i: (i, 0)),
                  pl.BlockSpec((None, k, tn), lambda j, i: (layer, 0, j))],
        out_specs=pl.BlockSpec((tm, tn), lambda j, i: (i, j)),
        name="proj_in",
        compiler_params=_cparams("parallel", "parallel"),
    )(a, w)


def _rms(x, g):
    return x * lax.rsqrt(jnp.mean(x * x, axis=-1, keepdims=True) + EPS) * g


def _qkv_kernel(cq_ref, ckv_ref, kr_ref, gq_ref, gkv_ref, wqa_ref, wqb_ref, wka_ref, sela_ref, selb_ref,
                wv_ref, qca_ref, qsb_ref, kca_ref, ksb_ref, q_ref, k_ref, v_ref):
    cqn = _rms(cq_ref[...].astype(F32), gq_ref[...]).astype(BF16)
    ckvn = _rms(ckv_ref[...].astype(F32), gkv_ref[...]).astype(BF16)
    kr = kr_ref[...]
    qa = _dot(cqn, wqa_ref[...])
    qb = _dot(cqn, wqb_ref[...])
    ka = _dot(ckvn, wka_ref[...]) + _dot(kr, sela_ref[...])
    kb = _dot(kr, selb_ref[...])
    qca, qsb, kca, ksb = qca_ref[...], qsb_ref[...], kca_ref[...], ksb_ref[...]
    for h in range(MLA_HEADS):
        sl = slice(h * MLA_SLOT, (h + 1) * MLA_SLOT)
        q_ref[:, sl] = (qa[:, sl] * qca + qb[:, sl] * qsb).astype(BF16)
        k_ref[:, sl] = (ka[:, sl] * kca + kb[:, sl] * ksb).astype(BF16)
    v_ref[...] = _dot(ckvn, wv_ref[...]).astype(BF16)


def _mla_qkv(geo, u, off, layer, gq, gkv, wqa, wqb, wka, sela, selb, wv, tabs):
    tr, m = geo.tr, geo.m
    hq = MLA_HEADS * MLA_SLOT
    hv = MLA_HEADS * MLA_V
    row = lambda w, o: pl.BlockSpec((tr, w), lambda i: (i, o // w))
    lw = lambda a: pl.BlockSpec((None,) + a.shape[1:], lambda i: (layer,) + (0,) * (a.ndim - 1))
    cw = lambda a: pl.BlockSpec(a.shape, lambda i: (0,) * a.ndim)
    tab = pl.BlockSpec((tr, MLA_SLOT), lambda i: (i % geo.nt, 0))
    return pl.pallas_call(
        _qkv_kernel,
        out_shape=(jax.ShapeDtypeStruct((m, hq), BF16), jax.ShapeDtypeStruct((m, hq), BF16),
                   jax.ShapeDtypeStruct((m, hv), BF16)),
        grid=(m // tr,),
        in_specs=[row(MLA_Q_RANK, off["cq"]), row(MLA_KV_RANK, off["ckv"]), row(128, off["kr"]),
                  lw(gq), lw(gkv), lw(wqa), lw(wqb), lw(wka), cw(sela), cw(selb), lw(wv), tab, tab, tab, tab],
        out_specs=(pl.BlockSpec((tr, hq), lambda i: (i, 0)), pl.BlockSpec((tr, hq), lambda i: (i, 0)),
                   pl.BlockSpec((tr, hv), lambda i: (i, 0))),
        name="mla_qkv",
        compiler_params=_cparams("parallel"),
    )(u, u, u, gq, gkv, wqa, wqb, wka, sela, selb, wv, *tabs)


def _attn_kernel(q_ref, k_ref, v_ref, o_ref, *, n_ctx, n_all, n_ctx_tiles):
    qi = pl.program_id(2)
    tq = q_ref.shape[0]
    lane = lax.broadcasted_iota(jnp.int32, (tq, 2 * MLA_V), 1)

    def run(nk):
        outs = []
        for hh in range(2):
            sl = slice(hh * MLA_SLOT, (hh + 1) * MLA_SLOT)
            s = _dot_nt(q_ref[:, sl], k_ref[0:nk, sl])
            p = jnp.exp(s - jnp.max(s, axis=-1, keepdims=True))
            inv = 1.0 / jnp.sum(p, axis=-1, keepdims=True)
            outs.append(_dot(p.astype(BF16), v_ref[0:nk, :]) * inv)
        o_ref[...] = jnp.where(lane < MLA_V, outs[0], outs[1]).astype(o_ref.dtype)

    @pl.when(qi < n_ctx_tiles)
    def _():
        run(n_ctx)

    @pl.when(qi >= n_ctx_tiles)
    def _():
        run(n_all)


def _mla_attention(geo, q, k, v):
    tq, n, bsz = geo.tr, geo.n, geo.b
    q3, k3, v3 = (t.reshape(bsz, n, t.shape[-1]) for t in (q, k, v))
    kern = functools.partial(_attn_kernel, n_ctx=geo.c, n_all=n, n_ctx_tiles=geo.nct)
    out = pl.pallas_call(
        kern,
        out_shape=jax.ShapeDtypeStruct((bsz, n, MLA_HEADS * MLA_V), BF16),
        grid=(bsz, MLA_HEADS // 2, geo.nt),
        in_specs=[pl.BlockSpec((None, tq, 2 * MLA_SLOT), lambda b, hp, i: (b, i, hp)),
                  pl.BlockSpec((None, n, 2 * MLA_SLOT), lambda b, hp, i: (b, 0, hp)),
                  pl.BlockSpec((None, n, 2 * MLA_V), lambda b, hp, i: (b, 0, hp))],
        out_specs=pl.BlockSpec((None, tq, 2 * MLA_V), lambda b, hp, i: (b, i, hp)),
        name="mla_attn",
        compiler_params=_cparams("parallel", "parallel", "parallel"),
    )(q3, k3, v3)
    return out.reshape(geo.m, MLA_HEADS * MLA_V)


def _halo_flags(geo, i):
    j = i % geo.nt
    left = jnp.logical_and(j != 0, j != geo.nct)
    right = jnp.logical_and(j != geo.nct - 1, j != geo.nt - 1)
    return left.astype(F32), right.astype(F32)


def _conv_shifts(n_taps):
    base = CONV_HALO - n_taps // 2
    return sorted({(base + kk) % SUBLANES for kk in range(n_taps)} - {0})


def _conv_taps(pad_ref, sh_ref, w_ref, n_taps, rows):
    base = CONV_HALO - n_taps // 2
    shifts = _conv_shifts(n_taps)
    span = sh_ref.shape[1]
    for s, r in enumerate(shifts):
        sh_ref[s, :, :] = pad_ref[r:r + span, :]

    def window(o):
        a, r = o - o % SUBLANES, o % SUBLANES
        return pad_ref[a:a + rows, :] if r == 0 else sh_ref[shifts.index(r), a:a + rows, :]

    acc = w_ref[0:1, :] * window(base)
    for kk in range(1, n_taps):
        acc = acc + w_ref[kk:kk + 1, :] * window(base + kk)
    return acc


def _conv_scratch(tr, width, n_taps):
    return [pltpu.VMEM((tr + 2 * CONV_HALO, width), F32),
            pltpu.VMEM((len(_conv_shifts(n_taps)), tr + 2 * CONV_HALO - SUBLANES, width), F32)]


def _conf_kernel(a_ref, b_ref, ap_ref, bp_ref, an_ref, bn_ref, w_ref, bias_ref, g_ref, beta_ref, o_ref,
                 pad_ref, sh_ref, *, geo):
    left, right = _halo_flags(geo, pl.program_id(0))
    rows = a_ref.shape[0]
    glu = lambda a, b: a[...].astype(F32) * jax.nn.sigmoid(b[...].astype(F32))
    pad_ref[0:CONV_HALO, :] = glu(ap_ref, bp_ref) * left
    pad_ref[CONV_HALO:CONV_HALO + rows, :] = glu(a_ref, b_ref)
    pad_ref[CONV_HALO + rows:2 * CONV_HALO + rows, :] = glu(an_ref, bn_ref) * right
    h = _conv_taps(pad_ref, sh_ref, w_ref, CF_KERNEL, rows) + bias_ref[...]
    hc = h - jnp.mean(h, axis=-1, keepdims=True)
    y = hc * lax.rsqrt(jnp.mean(hc * hc, axis=-1, keepdims=True) + EPS) * g_ref[...] + beta_ref[...]
    o_ref[...] = (y * jax.nn.sigmoid(y)).astype(o_ref.dtype)


def _sconv_kernel(gb_ref, gc_ref, h_ref, gcp_ref, hp_ref, gcn_ref, hn_ref, w_ref, o_ref, pad_ref, sh_ref, *, geo):
    left, right = _halo_flags(geo, pl.program_id(0))
    rows = h_ref.shape[0]
    prod = lambda a, b: a[...].astype(F32) * b[...].astype(F32)
    pad_ref[0:CONV_HALO, :] = prod(gcp_ref, hp_ref) * left
    pad_ref[CONV_HALO:CONV_HALO + rows, :] = prod(gc_ref, h_ref)
    pad_ref[CONV_HALO + rows:2 * CONV_HALO + rows, :] = prod(gcn_ref, hn_ref) * right
    y = _conv_taps(pad_ref, sh_ref, w_ref, SC_KERNEL, rows)
    o_ref[...] = (gb_ref[...].astype(F32) * y).astype(o_ref.dtype)


def _halo_specs(geo, width, offset):
    tr = geo.tr
    per = tr // CONV_HALO
    last = geo.m // CONV_HALO - 1
    cur = pl.BlockSpec((tr, width), lambda i: (i, offset // width))
    prev = pl.BlockSpec((CONV_HALO, width), lambda i: (jnp.maximum(i * per - 1, 0), offset // width))
    nxt = pl.BlockSpec((CONV_HALO, width), lambda i: (jnp.minimum((i + 1) * per, last), offset // width))
    return cur, prev, nxt


def _conformer(geo, u, off, layer, w_dw, b_dw, g_ln, b_ln):
    tr = geo.tr
    a_c, a_p, a_n = _halo_specs(geo, CF_WIDTH, off["cf_a"])
    b_c, b_p, b_n = _halo_specs(geo, CF_WIDTH, off["cf_b"])
    lw = lambda a: pl.BlockSpec((None,) + a.shape[1:], lambda i: (layer,) + (0,) * (a.ndim - 1))
    return pl.pallas_call(
        functools.partial(_conf_kernel, geo=geo),
        out_shape=jax.ShapeDtypeStruct((geo.m, CF_WIDTH), BF16),
        grid=(geo.m // tr,),
        in_specs=[a_c, b_c, a_p, b_p, a_n, b_n, lw(w_dw), lw(b_dw), lw(g_ln), lw(b_ln)],
        out_specs=pl.BlockSpec((tr, CF_WIDTH), lambda i: (i, 0)),
        scratch_shapes=_conv_scratch(tr, CF_WIDTH, CF_KERNEL),
        name="conformer",
        compiler_params=_cparams("parallel"),
    )(u, u, u, u, u, u, w_dw, b_dw, g_ln, b_ln)


def _shortconv(geo, u, off, layer, w_conv):
    tr = geo.tr
    gb_c, _, _ = _halo_specs(geo, SC_WIDTH, off["sc_b"])
    gc_c, gc_p, gc_n = _halo_specs(geo, SC_WIDTH, off["sc_c"])
    h_c, h_p, h_n = _halo_specs(geo, SC_WIDTH, off["sc_h"])
    lw = lambda a: pl.BlockSpec((None,) + a.shape[1:], lambda i: (layer,) + (0,) * (a.ndim - 1))
    return pl.pallas_call(
        functools.partial(_sconv_kernel, geo=geo),
        out_shape=jax.ShapeDtypeStruct((geo.m, SC_WIDTH), BF16),
        grid=(geo.m // tr,),
        in_specs=[gb_c, gc_c, h_c, gc_p, h_p, gc_n, h_n, lw(w_conv)],
        out_specs=pl.BlockSpec((tr, SC_WIDTH), lambda i: (i, 0)),
        scratch_shapes=_conv_scratch(tr, SC_WIDTH, SC_KERNEL),
        name="shortconv",
        compiler_params=_cparams("parallel"),
    )(u, u, u, u, u, u, u, w_conv)


def _gla_tables():
    c = GLA_CHUNK
    n_rows = (GLA_LEVELS + 2) * c + 8
    mat = np.zeros((2, n_rows, c), np.float32)
    msk = np.zeros((2, GLA_LEVELS + 1, c, c), np.float32)
    for lvl in range(GLA_LEVELS):
        hs = (c // 2) >> lvl
        for t in range(c):
            mid = (t // (2 * hs)) * 2 * hs + hs
            if t >= mid:
                mat[0, lvl * c + t, mid + 1:t + 1] = 1.0
                msk[0, lvl, t, mid - hs:mid] = 1.0
            else:
                mat[0, lvl * c + t, t + 1:mid + 1] = 1.0
    msk[0, GLA_LEVELS] = np.eye(c, dtype=np.float32)
    for t in range(c):
        mat[0, GLA_LEVELS * c + t, :t + 1] = 1.0
        mat[0, (GLA_LEVELS + 1) * c + t, t + 1:] = 1.0
    mat[0, (GLA_LEVELS + 2) * c:, :] = 1.0
    n_blk = GLA_LEVELS + 2
    mat[1, :n_blk * c] = mat[0, :n_blk * c].reshape(n_blk, c, c)[:, ::-1, ::-1].reshape(n_blk * c, c)
    mat[1, n_blk * c:] = 1.0
    msk[1] = msk[0][:, ::-1, ::-1]
    mat3 = np.concatenate([mat, mat, mat], axis=2)
    msk4 = np.tile(msk, (1, 1, GLA_HEADS, 1))
    return jnp.asarray(mat3, BF16), jnp.asarray(msk4, F32)


def _gla_kernel(qf_ref, kf_ref, vf_ref, gf_ref, qb_ref, kb_ref, vb_ref, gb_ref, w2_ref, b2_ref, mat_ref, msk_ref,
                of_ref, ob_ref, st_ref):
    @pl.when(pl.program_id(0) == 0)
    def _():
        st_ref[...] = jnp.zeros_like(st_ref)

    ins = ((qf_ref, kf_ref, vf_ref, gf_ref, of_ref), (qb_ref, kb_ref, vb_ref, gb_ref, ob_ref))
    for b in range(qf_ref.shape[0]):
        for d, (q_ref, k_ref, v_ref, g_ref, o_ref) in enumerate(ins):
            _gla_chunk(q_ref.at[b], k_ref.at[b], v_ref.at[b], g_ref.at[b], w2_ref.at[d], b2_ref.at[d],
                       mat_ref.at[d], msk_ref.at[d], o_ref.at[b], st_ref.at[d, b])


def _gla_chunk(q_ref, k_ref, v_ref, g_ref, w2_ref, b2_ref, mat_ref, msk_ref, o_ref, st_ref):
    c = GLA_CHUNK
    z = _dot(g_ref[...], w2_ref[...]) + b2_ref[...]
    la = (jnp.minimum(z, 0.0) - jnp.log1p(jnp.exp(-jnp.abs(z)))) * (1.0 / GLA_GATE_TAU)
    hi = la.astype(BF16)
    r1 = la - hi.astype(F32)
    lo = r1.astype(BF16)
    lo2 = (r1 - lo.astype(F32)).astype(BF16)
    ea = jnp.exp(_dot(mat_ref[...], jnp.concatenate([hi, lo, lo2], axis=0)))

    head = lax.broadcasted_iota(jnp.int32, (1, GLA_HEADS * GLA_DK), 1) // GLA_DK
    zero = jnp.zeros((), BF16)

    def stack(x):
        xb = x.astype(BF16)
        return jnp.concatenate([jnp.where(head == h, xb, zero) for h in range(GLA_HEADS)], axis=0)

    q = q_ref[...].astype(F32) * (GLA_DK ** -0.5)
    k = k_ref[...].astype(F32)
    vb = v_ref[...]

    attn = _dot_nt(stack(q), k.astype(BF16)) * msk_ref[GLA_LEVELS]
    for lvl in range(GLA_LEVELS):
        e = ea[lvl * c:(lvl + 1) * c]
        attn = attn + _dot_nt(stack(q * e), (k * e).astype(BF16)) * msk_ref[lvl]
    o_intra = _dot(attn.astype(BF16), vb)

    st = st_ref[...]
    o_inter = _dot_nt(stack(q * ea[GLA_LEVELS * c:(GLA_LEVELS + 1) * c]), st.astype(BF16))
    for h in range(GLA_HEADS):
        rows = slice(h * c, (h + 1) * c)
        cols = slice(h * GLA_DV, (h + 1) * GLA_DV)
        o_ref[:, cols] = o_intra[rows, cols] + o_inter[rows, :]

    k_dec = stack(k * ea[(GLA_LEVELS + 1) * c:(GLA_LEVELS + 2) * c])
    v_rows = jnp.concatenate([vb[:, h * GLA_DV:(h + 1) * GLA_DV] for h in range(GLA_HEADS)], axis=0)
    eye = (lax.broadcasted_iota(jnp.int32, (GLA_DV, GLA_DV), 0)
           == lax.broadcasted_iota(jnp.int32, (GLA_DV, GLA_DV), 1)).astype(BF16)
    v_t = _dot_nt(eye, v_rows).astype(BF16)
    tot = ea[(GLA_LEVELS + 2) * c:(GLA_LEVELS + 2) * c + 1]
    st_ref[...] = st * tot + _dot(v_t, k_dec)


def _gla_scan(geo, u, off, layer, w2, b2, mat3, msk4):
    c = GLA_CHUNK
    ncc = geo.c // c
    nc = geo.n // c
    dk, dv = GLA_HEADS * GLA_DK, GLA_HEADS * GLA_DV

    fwd = lambda i: i
    bwd = lambda i: jnp.where(i < ncc, ncc - 1 - i, nc - 1 - (i - ncc))

    u3 = u.reshape(geo.b, geo.n, u.shape[-1])
    row = lambda w, o, chunk: pl.BlockSpec((geo.b, c, w), lambda i: (0, chunk(i), o // w))
    rows = lambda chunk: [row(dk, off["gla_q"], chunk), row(dk, off["gla_k"], chunk), row(dv, off["gla_v"], chunk),
                          row(128, off["gate"], chunk)]
    out_sds = jax.ShapeDtypeStruct((geo.b, geo.n, dv), F32)
    o_f, o_b = pl.pallas_call(
        _gla_kernel,
        out_shape=(out_sds, out_sds),
        grid=(nc,),
        in_specs=rows(fwd) + rows(bwd) + [
            pl.BlockSpec((None, 2, 128, dk), lambda i: (layer, 0, 0, 0)),
            pl.BlockSpec((None, 2, 1, dk), lambda i: (layer, 0, 0, 0)),
            pl.BlockSpec(mat3.shape, lambda i: (0, 0, 0)),
            pl.BlockSpec(msk4.shape, lambda i: (0, 0, 0, 0))],
        out_specs=(pl.BlockSpec((geo.b, c, dv), lambda i: (0, fwd(i), 0)),
                   pl.BlockSpec((geo.b, c, dv), lambda i: (0, bwd(i), 0))),
        scratch_shapes=[pltpu.VMEM((2, geo.b, GLA_DV, dk), F32)],
        name="gla_scan",
        compiler_params=_cparams("arbitrary"),
    )(u3, u3, u3, u3, u3, u3, u3, u3, w2, b2, mat3, msk4)
    return o_f.reshape(geo.m, dv), o_b.reshape(geo.m, dv)


def _merge_mix_kernel(attn_ref, conf_ref, glaf_ref, glab_ref, r_ref, gn_ref, sc_ref, gate_ref, w0_ref, w1_ref,
                      w2_ref, w3_ref, wmix_ref, x_ref, gt_ref, gf_ref, shf_ref, scf_ref, wr_ref, br_ref, tri_ref,
                      o_ref, h_ref, idx_ref, wgt_ref, rank_ref, cnt_ref, y_ref, carry_ref):
    @pl.when(pl.program_id(0) == 0)
    def _():
        carry_ref[...] = jnp.zeros_like(carry_ref)

    wbr_ref = (w0_ref, w1_ref, w2_ref, w3_ref)
    o = glaf_ref[...] + glab_ref[...]
    r = r_ref[...].astype(F32)
    gn = gn_ref[...]
    normed = [_rms(o[:, h * GLA_DV:(h + 1) * GLA_DV], gn) for h in range(GLA_HEADS)]
    gla = (jnp.concatenate(normed, axis=1) * (r * jax.nn.sigmoid(r))).astype(BF16)
    acts = (attn_ref[...], conf_ref[...], gla, sc_ref[...])
    d = x_ref.shape[1]
    cw = min(MERGE_COLS, d)
    for c0 in range(0, d, cw):
        y = None
        for i in range(N_BRANCHES):
            half_b = _dot(acts[i], wbr_ref[i][:, c0:c0 + cw])
            t = jnp.tanh(gate_ref[:, i * d + c0:i * d + c0 + cw].astype(F32))
            term = t * half_b + half_b
            y = term if y is None else y + term
        y_ref[:, c0:c0 + cw] = y.astype(BF16)
    y_all = y_ref[...]
    for c0 in range(0, d, cw):
        cols = slice(c0, c0 + cw)
        o_ref[:, cols] = x_ref[:, cols] + gt_ref[:, cols] * _dot(y_all, wmix_ref[:, cols])
    _route_rows(o_ref[...], gf_ref, shf_ref, scf_ref, wr_ref, br_ref, tri_ref, h_ref, idx_ref, wgt_ref, rank_ref,
                cnt_ref, carry_ref)


def _merge_mix(geo, attn, conf, gla_f, gla_b, u, u_gates, off, g_norm, sconv, w_br_half, w_mix, x, mod, layer,
               g_ffn, wr_stack, b_router):
    tm, d, m = geo.tr, geo.d, geo.m
    w512 = GLA_HEADS * GLA_DV
    nch = d // 256
    act = pl.BlockSpec((tm, w512), lambda i: (i, 0))
    once = pl.Buffered(1)
    w_br = pl.BlockSpec((None, w512, d), lambda i: (layer, 0, 0), pipeline_mode=once)
    tri = jnp.asarray(np.triu(np.ones((tm, tm), np.float32), 1), BF16)
    pair = lambda dt: jax.ShapeDtypeStruct((2, m), dt)
    pair_spec = pl.BlockSpec((2, tm), lambda i: (0, i))
    return pl.pallas_call(
        _merge_mix_kernel,
        out_shape=(jax.ShapeDtypeStruct((m, d), F32), jax.ShapeDtypeStruct((m * nch, 128), jnp.uint32),
                   pair(jnp.int32), pair(F32), pair(jnp.int32), jax.ShapeDtypeStruct((N_EXPERTS, 1), F32)),
        grid=(m // tm,),
        in_specs=[act, act, act, act,
                  pl.BlockSpec((tm, w512), lambda i: (i, off["gla_r"] // w512)),
                  pl.BlockSpec((None, 1, GLA_DV), lambda i: (layer, 0, 0)),
                  act, pl.BlockSpec((tm, N_BRANCHES * d), lambda i: (i, 0)),
                  w_br, w_br, w_br, w_br,
                  pl.BlockSpec((None, d, d), lambda i: (layer, 0, 0), pipeline_mode=once),
                  pl.BlockSpec((tm, d), lambda i: (i, 0)), geo.mod_spec(layer, 2),
                  pl.BlockSpec((None, 1, d), lambda i: (layer, 0, 0)),
                  geo.mod_spec(layer, 3), geo.mod_spec(layer, 4),
                  pl.BlockSpec(wr_stack.shape, lambda i: (0, 0)),
                  pl.BlockSpec(b_router.shape, lambda i: (0, 0)),
                  pl.BlockSpec((tm, tm), lambda i: (0, 0))],
        out_specs=(pl.BlockSpec((tm, d), lambda i: (i, 0)), pl.BlockSpec((tm * nch, 128), lambda i: (i, 0)),
                   pair_spec, pair_spec, pair_spec, pl.BlockSpec((N_EXPERTS, 1), lambda i: (0, 0))),
        scratch_shapes=[pltpu.VMEM((tm, d), BF16), pltpu.VMEM((N_EXPERTS, 1), F32)],
        name="merge_mix",
        compiler_params=_cparams("arbitrary"),
    )(attn, conf, gla_f, gla_b, u, g_norm, sconv, u_gates, *w_br_half, w_mix, x, mod, g_ffn, mod, mod, wr_stack,
      b_router, tri)


def _cast_kernel(w_ref, o_ref, *, scale):
    o_ref[...] = (w_ref[...] * scale).astype(o_ref.dtype)


def _to_bf16(w, scale=1.0):
    n_l, rows, cols = w.shape
    tr = _pick(rows, (512, 256, 128))
    return pl.pallas_call(
        functools.partial(_cast_kernel, scale=scale),
        out_shape=jax.ShapeDtypeStruct(w.shape, BF16),
        grid=(n_l, rows // tr),
        in_specs=[pl.BlockSpec((None, tr, cols), lambda l, i: (l, i, 0))],
        out_specs=pl.BlockSpec((None, tr, cols), lambda l, i: (l, i, 0)),
        name="weight_cast",
        compiler_params=_cparams("parallel", "parallel"),
    )(w)


def _second_largest(a, b, c, d):
    return jnp.maximum(jnp.maximum(jnp.minimum(a, b), jnp.minimum(c, d)),
                       jnp.minimum(jnp.maximum(a, b), jnp.maximum(c, d)))


def _chunk_rows(rows, n, cc):
    return pl.ds(cc, rows, stride=n) if n > 1 else pl.ds(0, rows)


def _pack_rows(ref, y):
    rows = y.shape[0]
    n = ref.shape[0] // rows
    half = n * 128
    bits = lambda v: pltpu.bitcast(v.astype(BF16).astype(F32), jnp.uint32)
    packed = (bits(y[:, :half]) >> 16) | bits(y[:, half:])
    for cc in range(n):
        ref[_chunk_rows(rows, n, cc), :] = packed[:, cc * 128:(cc + 1) * 128]


def _unpack_cols(words):
    lo = pltpu.bitcast(words << 16, F32)
    hi = pltpu.bitcast(words & jnp.uint32(0xFFFF0000), F32)
    return lo, hi


def _route_rows(x, g_ref, sh_ref, sc_ref, wr_ref, br_ref, tri_ref, h_ref, idx_ref, wgt_ref, rank_ref, cnt_ref,
                carry_ref):
    y = _rms(x, g_ref[...]) * (1.0 + sc_ref[...]) + sh_ref[...]
    hi = y.astype(BF16)
    lo = (y - hi.astype(F32)).astype(BF16)
    _pack_rows(h_ref, y)
    ne = N_EXPERTS
    two = _dot_nt(wr_ref[...], hi)
    logits = two[0:ne] + two[ne:2 * ne] + _dot_nt(wr_ref[0:ne, :], lo)
    aff = 1.0 / (1.0 + jnp.exp(-logits))
    sel = aff + br_ref[...]
    per = ne // N_GROUPS
    gs = []
    for g in range(N_GROUPS):
        a, b, c, d = (sel[g * per + r:g * per + r + 1] for r in range(per))
        top1 = jnp.maximum(jnp.maximum(a, b), jnp.maximum(c, d))
        gs.append(top1 + _second_largest(a, b, c, d))
    best = gs[0]
    for g in range(1, N_GROUPS):
        best = jnp.maximum(best, gs[g])
    grp = jnp.full(best.shape, N_GROUPS, jnp.int32)
    for g in reversed(range(N_GROUPS)):
        grp = jnp.where(gs[g] == best, g, grp)
    erow = lax.broadcasted_iota(jnp.int32, sel.shape, 0)
    masked = jnp.where(erow // per == grp, sel, -jnp.inf)
    m1 = jnp.max(masked, axis=0, keepdims=True)
    i1 = jnp.min(jnp.where(masked == m1, erow, ne), axis=0, keepdims=True)
    one1 = erow == i1
    masked2 = jnp.where(one1, -jnp.inf, masked)
    m2 = jnp.max(masked2, axis=0, keepdims=True)
    i2 = jnp.min(jnp.where(masked2 == m2, erow, ne), axis=0, keepdims=True)
    one2 = erow == i2
    w1 = jnp.sum(jnp.where(one1, aff, 0.0), axis=0, keepdims=True)
    w2 = jnp.sum(jnp.where(one2, aff, 0.0), axis=0, keepdims=True)
    inv = 1.0 / (w1 + w2)
    idx_ref[0:1, :] = i1
    idx_ref[1:2, :] = i2
    wgt_ref[0:1, :] = w1 * inv
    wgt_ref[1:2, :] = w2 * inv
    o1 = jnp.where(one1, 1.0, 0.0)
    o2 = jnp.where(one2, 1.0, 0.0)
    tri = tri_ref[...]
    c0 = carry_ref[...]
    c1 = c0 + jnp.sum(o1, axis=1, keepdims=True)
    rank_ref[0:1, :] = jnp.sum(o1 * (c0 + _dot(o1.astype(BF16), tri)), axis=0, keepdims=True).astype(jnp.int32)
    rank_ref[1:2, :] = jnp.sum(o2 * (c1 + _dot(o2.astype(BF16), tri)), axis=0, keepdims=True).astype(jnp.int32)
    c2 = c1 + jnp.sum(o2, axis=1, keepdims=True)
    carry_ref[...] = c2
    cnt_ref[...] = c2


def _sorted_row(idx_ref, rank_ref, base_ref, a, n):
    return pl.multiple_of((base_ref[idx_ref[a]] + rank_ref[a]) * n, n)


def _dispatch_kernel(idx_ref, rank_ref, base_ref, last_ref, h_ref, xs_hbm, zero_ref, sem, zero_sem, *, m, n):
    i = pl.program_id(0)
    tr = h_ref.shape[0] // n

    @pl.when(i == 0)
    def _():
        zero_ref[...] = jnp.zeros_like(zero_ref)

        def fills(e):
            rows = zero_ref.shape[0]
            start = pl.multiple_of(last_ref[e] * n, rows)
            return pltpu.make_async_copy(zero_ref, xs_hbm.at[pl.ds(start, rows), :], zero_sem)

        for e in range(2 * N_EXPERTS):
            @pl.when(last_ref[e] >= 0)
            def _():
                fills(e).start()
        for e in range(2 * N_EXPERTS):
            @pl.when(last_ref[e] >= 0)
            def _():
                fills(e).wait()

    def issue(r, carry):
        src = h_ref.at[pl.ds(pl.multiple_of(r * n, n), n), :]
        for k in range(2):
            row = _sorted_row(idx_ref, rank_ref, base_ref, k * m + i * tr + r, n)
            pltpu.make_async_copy(src, xs_hbm.at[pl.ds(row, n), :], sem).start()
        return carry

    lax.fori_loop(0, tr, issue, 0, unroll=4)
    everything = xs_hbm.at[pl.ds(0, 2 * tr * n), :]
    pltpu.make_async_copy(everything, everything, sem).wait()


def _dispatch(hfp, ridx, rank, base, last_tile_row, n_rows, tr, tm):
    n = hfp.shape[0] * 2 // ridx.shape[0]
    m = hfp.shape[0] // n
    return pl.pallas_call(
        functools.partial(_dispatch_kernel, m=m, n=n),
        out_shape=jax.ShapeDtypeStruct((n_rows * n, 128), jnp.uint32),
        grid_spec=pltpu.PrefetchScalarGridSpec(
            num_scalar_prefetch=4, grid=(m // tr,),
            in_specs=[pl.BlockSpec((tr * n, 128), lambda i, *_: (i, 0))],
            out_specs=pl.BlockSpec(memory_space=pl.ANY),
            scratch_shapes=[pltpu.VMEM((tm * n, 128), jnp.uint32), pltpu.SemaphoreType.DMA(()),
                            pltpu.SemaphoreType.DMA(())]),
        name="moe_dispatch",
        compiler_params=_cparams("arbitrary"),
    )(ridx, rank, base, last_tile_row, hfp)


def _gmm_kernel(te_ref, nv_ref, xs_ref, wg_ref, wu_ref, wd_ref, ys_ref, wgb_ref, wub_ref, wdb_ref, prev_ref):
    j = pl.program_id(0)

    @pl.when(j == 0)
    def _():
        prev_ref[0] = -1

    @pl.when(j < nv_ref[0])
    def _():
        e = te_ref[j]

        @pl.when(e != prev_ref[0])
        def _():
            wgb_ref[...] = wg_ref[...].astype(BF16)
            wub_ref[...] = wu_ref[...].astype(BF16)
            wdb_ref[...] = wd_ref[...].astype(BF16)
            prev_ref[0] = e

        half = wgb_ref.shape[0] // 2
        n = half // 128
        tm = xs_ref.shape[0] // n
        parts = [_unpack_cols(xs_ref[_chunk_rows(tm, n, cc), :]) for cc in range(n)]
        xa = jnp.concatenate([p[0].astype(BF16) for p in parts], axis=1)
        xb = jnp.concatenate([p[1].astype(BF16) for p in parts], axis=1)
        g = _dot(xa, wgb_ref[0:half, :]) + _dot(xb, wgb_ref[half:2 * half, :])
        u = _dot(xa, wub_ref[0:half, :]) + _dot(xb, wub_ref[half:2 * half, :])
        act = (g * jax.nn.sigmoid(g) * u).astype(BF16)
        _pack_rows(ys_ref, _dot(act, wdb_ref[...]))

    @pl.when(j >= nv_ref[0])
    def _():
        ys_ref[...] = jnp.zeros_like(ys_ref)


def _gmm(xs, tile_expert, n_valid, wg, wu, wd, layer, tm):
    d, ff = wg.shape[-2], wg.shape[-1]
    nch = d // 256
    blk = tm * nch
    row = lambda j, te, nv: (jnp.minimum(j, nv[0] - 1), 0)
    wsel = lambda j, te, nv: (layer, te[jnp.minimum(j, nv[0] - 1)], 0, 0)
    return pl.pallas_call(
        _gmm_kernel,
        out_shape=jax.ShapeDtypeStruct(xs.shape, jnp.uint32),
        grid_spec=pltpu.PrefetchScalarGridSpec(
            num_scalar_prefetch=2, grid=(xs.shape[0] // blk,),
            in_specs=[pl.BlockSpec((blk, 128), row),
                      pl.BlockSpec((None, None, d, ff), wsel), pl.BlockSpec((None, None, d, ff), wsel),
                      pl.BlockSpec((None, None, ff, d), wsel)],
            out_specs=pl.BlockSpec((blk, 128), lambda j, te, nv: (j, 0)),
            scratch_shapes=[pltpu.VMEM((d, ff), BF16), pltpu.VMEM((d, ff), BF16), pltpu.VMEM((ff, d), BF16),
                            pltpu.SMEM((1,), jnp.int32)]),
        name="moe_experts",
        compiler_params=_cparams("arbitrary"),
    )(tile_expert, n_valid, xs, wg, wu, wd)


def _combine_kernel(idx_ref, rank_ref, base_ref, x_ref, w_ref, gt_ref, ys_hbm, *rest, m, last):
    if last:
        g_ref, h_ref, o_ref, buf_ref, sem_ref = rest
    else:
        g_ref, sh_ref, sc_ref, o_ref, h_ref, buf_ref, sem_ref = rest
    i = pl.program_id(0)
    n_tiles = pl.num_programs(0)
    tr = x_ref.shape[0]
    n = buf_ref.shape[2] // tr

    def issue(tile, slot):
        def body(r, carry):
            for k in range(2):
                row = _sorted_row(idx_ref, rank_ref, base_ref, k * m + tile * tr + r, n)
                pltpu.make_async_copy(ys_hbm.at[pl.ds(row, n), :],
                                      buf_ref.at[slot, k, pl.ds(pl.multiple_of(r * n, n), n), :],
                                      sem_ref.at[slot]).start()
            return carry
        lax.fori_loop(0, tr, body, 0, unroll=4)

    @pl.when(i == 0)
    def _():
        issue(0, 0)

    @pl.when(i + 1 < n_tiles)
    def _():
        issue(i + 1, (i + 1) % 2)

    slot = i % 2
    pltpu.make_async_copy(buf_ref.at[slot], buf_ref.at[slot], sem_ref.at[slot]).wait()
    w1, w2 = w_ref[:, 0:1], w_ref[:, 1:2]
    half = n * 128
    for cc in range(n):
        lo1, hi1 = _unpack_cols(buf_ref[slot, 0, _chunk_rows(tr, n, cc), :])
        lo2, hi2 = _unpack_cols(buf_ref[slot, 1, _chunk_rows(tr, n, cc), :])
        ca = slice(cc * 128, (cc + 1) * 128)
        cb = slice(half + cc * 128, half + (cc + 1) * 128)
        o_ref[:, ca] = x_ref[:, ca] + gt_ref[:, ca] * (w1 * lo1 + w2 * lo2)
        o_ref[:, cb] = x_ref[:, cb] + gt_ref[:, cb] * (w1 * hi1 + w2 * hi2)
    if last:
        h_ref[...] = _rms(o_ref[...], g_ref[...])
    else:
        y = _rms(o_ref[...], g_ref[...]) * (1.0 + sc_ref[...]) + sh_ref[...]
        h_ref[...] = y.astype(h_ref.dtype)


def _combine(geo, x, ys, ridx, rank, base, w12, mod, layer, g_next, g_final):
    tr, d = geo.tr, geo.d
    nch = d // 256
    last = g_next is None
    tile = pl.BlockSpec((tr, d), lambda i, *_: (i, 0))
    in_specs = [tile, pl.BlockSpec((tr, 2), lambda i, *_: (i, 0)), geo.mod_spec(layer, 5),
                pl.BlockSpec(memory_space=pl.ANY)]
    operands = [x, w12, mod, ys]
    scratch = [pltpu.VMEM((2, 2, tr * nch, 128), jnp.uint32), pltpu.SemaphoreType.DMA((2,))]
    if last:
        in_specs.append(pl.BlockSpec((1, d), lambda i, *_: (0, 0)))
        operands.append(g_final.reshape(1, d))
        out_shape = jax.ShapeDtypeStruct((geo.b, geo.s, d), F32)
        out_specs = pl.BlockSpec((None, tr, d), lambda i, *_: (i // geo.nt, jnp.maximum(i % geo.nt - geo.nct, 0), 0))
        scratch = [pltpu.VMEM((tr, d), F32)] + scratch
    else:
        in_specs += [pl.BlockSpec((None, 1, d), lambda i, *_: (layer + 1, 0, 0)),
                     geo.mod_spec(layer + 1, 0), geo.mod_spec(layer + 1, 1)]
        operands += [g_next, mod, mod]
        out_shape = [jax.ShapeDtypeStruct((geo.m, d), F32), jax.ShapeDtypeStruct((geo.m, d), BF16)]
        out_specs = [tile, tile]
    return pl.pallas_call(
        functools.partial(_combine_kernel, m=geo.m, last=last),
        out_shape=out_shape,
        grid_spec=pltpu.PrefetchScalarGridSpec(
            num_scalar_prefetch=3, grid=(geo.m // tr,),
            in_specs=in_specs, out_specs=out_specs, scratch_shapes=scratch),
        name="moe_combine",
        compiler_params=_cparams("arbitrary"),
    )(ridx, rank, base, *operands)


def _moe(geo, x, routed, mod, layer, wg, wu, wd, g_next, g_final):
    tm = 256
    hfp, ridx, rwgt, rank, cnt = routed
    n_tiles = (2 * geo.m) // tm + N_EXPERTS
    counts = cnt[:, 0].astype(jnp.int32)
    tiles_per = (counts + tm - 1) // tm
    tile_start = jnp.cumsum(tiles_per) - tiles_per
    base = tile_start * tm
    n_valid = jnp.sum(tiles_per).reshape(1)
    tile_expert = jnp.sum(jnp.arange(n_tiles)[:, None] >= tile_start[None, :], axis=1).astype(jnp.int32) - 1
    ridx, rank = ridx.reshape(-1), rank.reshape(-1)
    tail = n_valid + jnp.arange(N_EXPERTS)
    fill_rows = jnp.concatenate([jnp.where(tiles_per > 0, base + (tiles_per - 1) * tm, -1),
                                 jnp.where(tail < n_tiles, tail * tm, -1)])
    xs = _dispatch(hfp, ridx, rank, base, fill_rows, n_tiles * tm, geo.tr, tm)
    ys = _gmm(xs, tile_expert, n_valid, wg, wu, wd, layer, tm)
    return _combine(geo, x, ys, ridx, rank, base, rwgt.T, mod, layer, g_next, g_final)


def _take_cols(w_ref, start, width):
    total = w_ref.shape[1]
    a, r = start - start % 128, start % 128
    span = -(-(r + width) // 128) * 128
    if a + span <= total:
        v = w_ref[:, a:a + span]
    else:
        cut = total - total % 128
        v = jnp.concatenate([w_ref[:, a:cut], w_ref[:, cut:total]], axis=1) if cut > a else w_ref[:, cut:total]
    return v[:, r:r + width]


def _w_in_kernel(w_ref, gates_ref, rest_ref, *, moves):
    rows = w_ref.shape[0]
    for dst, dst_col, src_col, width, scale in moves:
        out_ref = gates_ref if dst == "gates" else rest_ref
        for c0 in range(0, width, MERGE_COLS):
            cw = min(MERGE_COLS, width - c0)
            v = _take_cols(w_ref, src_col + c0, cw) * scale
            pad = -cw % 128
            if pad:
                v = jnp.concatenate([v, jnp.zeros((rows, pad), F32)], axis=1)
            out_ref[:, dst_col + c0:dst_col + c0 + cw + pad] = v.astype(BF16)


def _prep_w_in(w_in, d):
    src = _src_layout(d)
    off, total = _proj_layout()
    n_l, rows, width = w_in.shape
    gates_w = N_BRANCHES * d
    one = lambda name, s, w=None: ("rest", off[name], s, w or src_w(name), 1.0)
    src_w = lambda name: src[name][1] - src[name][0]
    cf0, sc0 = src["cf"][0], src["sc"][0]
    moves = [("gates", 0, src["merge"][0], gates_w, 0.5),
             one("cq", src["cq"][0]), one("ckv", src["ckv"][0]), one("kr", src["kr"][0]),
             one("cf_a", cf0, CF_WIDTH), one("cf_b", cf0 + CF_WIDTH, CF_WIDTH),
             one("gla_q", src["gla_q"][0]), one("gla_k", src["gla_k"][0]), one("gla_v", src["gla_v"][0]),
             one("gate", src["gf"][0], 2 * GLA_GATE_RANK), one("gla_r", src["gla_r"][0]),
             one("sc_b", sc0, SC_WIDTH), one("sc_c", sc0 + SC_WIDTH, SC_WIDTH),
             one("sc_h", sc0 + 2 * SC_WIDTH, SC_WIDTH)]
    assert src["gb"][0] == src["gf"][1]
    tr = _pick(rows, (256, 128))
    gates, rest = pl.pallas_call(
        functools.partial(_w_in_kernel, moves=moves),
        out_shape=(jax.ShapeDtypeStruct((n_l, rows, gates_w), BF16), jax.ShapeDtypeStruct((n_l, rows, total), BF16)),
        grid=(n_l, rows // tr),
        in_specs=[pl.BlockSpec((None, tr, width), lambda l, i: (l, i, 0))],
        out_specs=(pl.BlockSpec((None, tr, gates_w), lambda l, i: (l, i, 0)),
                   pl.BlockSpec((None, tr, total), lambda l, i: (l, i, 0))),
        name="w_in_split",
        compiler_params=_cparams("parallel", "parallel"),
    )(w_in)
    return gates, rest, off


def _rope_swap_perm():
    q = MLA_ROPE // 4
    return np.concatenate([np.arange(q, 2 * q), np.arange(0, q), np.arange(3 * q, 4 * q), np.arange(2 * q, 3 * q)])


def _prep_mla(w_uq, w_ukv):
    n_l = w_uq.shape[0]
    perm = _rope_swap_perm()
    wq = w_uq.reshape(n_l, MLA_Q_RANK, MLA_HEADS, MLA_NOPE + MLA_ROPE)
    qn, qr = wq[..., :MLA_NOPE], wq[..., MLA_NOPE:]
    zeros = lambda *s: jnp.zeros((n_l,) + s, w_uq.dtype)
    wqa = jnp.concatenate([qn, qr, qr], axis=-1)
    wqb = jnp.concatenate([zeros(MLA_Q_RANK, MLA_HEADS, MLA_NOPE), qr[..., perm],
                           zeros(MLA_Q_RANK, MLA_HEADS, MLA_ROPE)], axis=-1)
    wkv = w_ukv.reshape(n_l, MLA_KV_RANK, MLA_HEADS, MLA_NOPE + MLA_V)
    wka = jnp.concatenate([wkv[..., :MLA_NOPE], zeros(MLA_KV_RANK, MLA_HEADS, MLA_SLOT - MLA_NOPE)], axis=-1)
    wv = wkv[..., MLA_NOPE:]
    flat = lambda w: w.reshape(n_l, w.shape[1], -1).astype(BF16)
    sela = np.zeros((128, MLA_HEADS, MLA_SLOT), np.float32)
    selb = np.zeros((128, MLA_HEADS, MLA_SLOT), np.float32)
    for r in range(MLA_ROPE):
        sela[r, :, MLA_NOPE + r] = 1.0
        sela[r, :, MLA_NOPE + MLA_ROPE + r] = 1.0
        selb[perm[r], :, MLA_NOPE + r] = 1.0
    sel = lambda s: jnp.asarray(s.reshape(128, -1), BF16)
    return flat(wqa), flat(wqb), flat(wka), sel(sela), sel(selb), flat(wv)


def _rope_tables(geo):
    rows = geo.s // GRID_W
    pos = jnp.arange(rows * GRID_W)
    row = (pos // GRID_W).astype(F32)
    col = (pos % GRID_W).astype(F32)
    n_freq = MLA_ROPE // 4
    freqs = ROPE_THETA ** (-jnp.arange(n_freq, dtype=F32) / n_freq)
    cr, sr = jnp.cos(row[:, None] * freqs), jnp.sin(row[:, None] * freqs)
    cc, sn = jnp.cos(col[:, None] * freqs), jnp.sin(col[:, None] * freqs)
    cos32 = jnp.concatenate([cr, cr, cc, cc], axis=-1)
    sin32 = jnp.concatenate([-sr, sr, -sn, sn], axis=-1)
    s, c = geo.s, geo.c
    one = lambda n, w: jnp.ones((n, w), F32)
    zero = lambda n, w: jnp.zeros((n, w), F32)
    scale = (MLA_NOPE + MLA_ROPE) ** -0.5
    qca = jnp.concatenate([jnp.concatenate([one(c, 64), zero(c, 32), one(c, 32)], 1),
                           jnp.concatenate([one(s, 64), cos32, one(s, 32)], 1)], 0) * scale
    qsb = jnp.concatenate([zero(c, 128), jnp.concatenate([zero(s, 64), sin32, zero(s, 32)], 1)], 0) * scale
    kca = jnp.concatenate([jnp.concatenate([one(c, 64), zero(c, 32), one(c, 32)], 1),
                           jnp.concatenate([one(s, 64), cos32, zero(s, 32)], 1)], 0)
    ksb = jnp.concatenate([zero(c, 128), jnp.concatenate([zero(s, 64), sin32, zero(s, 32)], 1)], 0)
    return qca, qsb, kca, ksb


def kernel(x, c, ctx, c_ctx, w_mod, b_mod, g_mix, g_ffn, w_in, g_q, w_uq, g_kv, w_ukv, w_mla_o, w_cf_dw, b_cf_dw, g_cf_ln, b_cf_ln, w_cf_o, w_gla_gf, b_gla_gf, w_gla_gb, b_gla_gb, g_gla_norm, w_gla_o, w_sc_conv, w_sc_o, w_mix_o, w_router, b_router, w_e_gate, w_e_up, w_e_down, g_final):
    bsz, seq, d = x.shape
    n_ctx = ctx.shape[1]
    depth = w_in.shape[0]
    geo = _Geom(bsz, n_ctx, seq, d)
    assert bsz + 1 <= 8 and seq % GRID_W == 0 and n_ctx % GLA_CHUNK == 0 and seq % GLA_CHUNK == 0

    w_gates, w_rest, off = _prep_w_in(w_in, d)
    wqa, wqb, wka, sela, selb, wv = _prep_mla(w_uq, w_ukv)
    tabs = _rope_tables(geo)
    w_br_half = [_to_bf16(w, 0.5) for w in (w_mla_o, w_cf_o, w_gla_o, w_sc_o)]
    w_mix = _to_bf16(w_mix_o)
    gate_w = jnp.zeros((depth, 2, 128, GLA_HEADS * GLA_DK), F32)
    gate_w = gate_w.at[:, 0, :GLA_GATE_RANK].set(w_gla_gf).at[:, 1, GLA_GATE_RANK:2 * GLA_GATE_RANK].set(w_gla_gb)
    gate_w = gate_w.astype(BF16)
    gate_b = jnp.stack([b_gla_gf, b_gla_gb], axis=1)[:, :, None, :]
    mat3, msk4 = _gla_tables()
    wr_t = w_router.T
    wr_hi = wr_t.astype(BF16)
    wr_stack = jnp.concatenate([wr_hi, (wr_t - wr_hi.astype(F32)).astype(BF16)], axis=0)
    vec = lambda a: a.reshape(a.shape[0], 1, a.shape[-1])

    cc = jnp.zeros((8, d), F32).at[:bsz].set(c).at[bsz].set(c_ctx)
    mod = _modulation(cc, w_mod, b_mod).reshape(depth, 8, 1, 6 * d)

    xa = jnp.concatenate([ctx, x], axis=1).reshape(geo.m, d)
    h = _norm_mod(geo, xa, vec(g_mix), mod, 0, 0, 1)
    for l in range(depth):
        u = _matmul(h, w_rest, l, BF16)
        u_gates = _matmul(h, w_gates, l, BF16)
        q, k, v = _mla_qkv(geo, u, off, l, vec(g_q), vec(g_kv), wqa, wqb, wka, sela, selb, wv, tabs)
        attn = _mla_attention(geo, q, k, v)
        conf = _conformer(geo, u, off, l, w_cf_dw, vec(b_cf_dw), vec(g_cf_ln), vec(b_cf_ln))
        sconv = _shortconv(geo, u, off, l, w_sc_conv)
        gla_f, gla_b = _gla_scan(geo, u, off, l, gate_w, gate_b, mat3, msk4)
        xa, *routed = _merge_mix(geo, attn, conf, gla_f, gla_b, u, u_gates, off, vec(g_gla_norm), sconv, w_br_half,
                                 w_mix, xa, mod, l, vec(g_ffn), wr_stack, b_router.reshape(N_EXPERTS, 1))
        if l + 1 < depth:
            xa, h = _moe(geo, xa, routed, mod, l, w_e_gate, w_e_up, w_e_down, vec(g_mix), None)
        else:
            return _moe(geo, xa, routed, mod, l, w_e_gate, w_e_up, w_e_down, None, g_final)
```

```python
import functools

import numpy as np
import jax
import jax.numpy as jnp
from jax import lax
from jax.experimental import pallas as pl
from jax.experimental.pallas import tpu as pltpu

F32 = jnp.float32
BF16 = jnp.bfloat16
EPS = 1e-6

GRID_W = 64
ROPE_THETA = 10000.0
MLA_HEADS = 8
MLA_NOPE = 64
MLA_ROPE = 32
MLA_V = 64
MLA_Q_RANK = 512
MLA_KV_RANK = 256
MLA_SLOT = 128
CF_WIDTH = 512
CF_KERNEL = 31
GLA_HEADS = 4
GLA_DK = 64
GLA_DV = 128
GLA_GATE_RANK = 16
GLA_GATE_TAU = 16.0
GLA_CHUNK = 64
GLA_LEVELS = 6
SC_WIDTH = 512
SC_KERNEL = 3
N_BRANCHES = 4
N_EXPERTS = 16
N_GROUPS = 4
EXPERT_FF = 512
CONV_HALO = 16
SUBLANES = 8
MERGE_COLS = 512
VMEM_LIMIT = 56 * 1024 * 1024


def _cparams(*sem):
    return pltpu.CompilerParams(dimension_semantics=sem, vmem_limit_bytes=VMEM_LIMIT)


def _pick(n, cands):
    for c in cands:
        if n % c == 0:
            return c
    raise ValueError(f"no tile for {n}")


def _dot(a, b):
    return jnp.dot(a, b, preferred_element_type=F32)


def _dot_nt(a, b):
    return lax.dot_general(a, b, (((1,), (1,)), ((), ())), preferred_element_type=F32)


def _proj_layout():
    parts = [("sc_b", SC_WIDTH), ("sc_c", SC_WIDTH), ("sc_h", SC_WIDTH),
             ("cf_a", CF_WIDTH), ("cf_b", CF_WIDTH), ("cq", MLA_Q_RANK), ("gla_v", GLA_HEADS * GLA_DV),
             ("gla_r", GLA_HEADS * GLA_DV), ("ckv", MLA_KV_RANK), ("gla_q", GLA_HEADS * GLA_DK),
             ("gla_k", GLA_HEADS * GLA_DK), ("kr", 128), ("gate", 128)]
    off, start = {}, 0
    for name, width in parts:
        assert start % width == 0
        off[name] = start
        start += width
    return off, start


def _src_layout(d):
    parts = (("cq", MLA_Q_RANK), ("ckv", MLA_KV_RANK), ("kr", MLA_ROPE), ("cf", 2 * CF_WIDTH),
             ("gla_q", GLA_HEADS * GLA_DK), ("gla_k", GLA_HEADS * GLA_DK), ("gla_v", GLA_HEADS * GLA_DV),
             ("gf", GLA_GATE_RANK), ("gb", GLA_GATE_RANK), ("gla_r", GLA_HEADS * GLA_DV),
             ("sc", 3 * SC_WIDTH), ("merge", N_BRANCHES * d))
    off, start = {}, 0
    for name, width in parts:
        off[name] = (start, start + width)
        start += width
    return off


def _mod_kernel(c_ref, w_ref, b_ref, o_ref):
    c = c_ref[...]
    a = (c * jax.nn.sigmoid(c)).astype(BF16)
    o_ref[...] = _dot(a, w_ref[...].astype(BF16)) + b_ref[...]


def _modulation(cc, w_mod, b_mod):
    n_l, d, n6 = w_mod.shape
    tn = _pick(n6, (1024, 512, 256, 128))
    return pl.pallas_call(
        _mod_kernel,
        out_shape=jax.ShapeDtypeStruct((n_l, 8, n6), F32),
        grid=(n_l, n6 // tn),
        in_specs=[pl.BlockSpec((8, d), lambda l, j: (0, 0)),
                  pl.BlockSpec((None, d, tn), lambda l, j: (l, 0, j)),
                  pl.BlockSpec((None, 1, tn), lambda l, j: (l, 0, j))],
        out_specs=pl.BlockSpec((None, 8, tn), lambda l, j: (l, 0, j)),
        name="adaln_mod",
        compiler_params=_cparams("parallel", "parallel"),
    )(cc, w_mod, b_mod.reshape(n_l, 1, n6))


class _Geom:
    def __init__(self, bsz, n_ctx, seq, d):
        self.b, self.c, self.s, self.d = bsz, n_ctx, seq, d
        self.n = n_ctx + seq
        self.m = bsz * self.n
        self.tr = _pick(int(np.gcd(n_ctx, seq)), (256, 128, 64))
        self.nt = self.n // self.tr
        self.nct = n_ctx // self.tr

    def mod_row(self, i):
        b, j = i // self.nt, i % self.nt
        return jnp.where(j < self.nct, self.b, b)

    def mod_spec(self, layer, part):
        return pl.BlockSpec((None, None, 1, self.d), lambda i, *_: (layer, self.mod_row(i), 0, part))


def _first_norm_kernel(x_ref, c_ref, g_ref, sh_ref, sc_ref, xa_ref, h_ref, *, geo):
    is_ctx = pl.program_id(0) % geo.nt < geo.nct
    rows = jnp.where(is_ctx, c_ref[...], x_ref[...])
    xa_ref[...] = rows
    y = rows * lax.rsqrt(jnp.mean(rows * rows, axis=-1, keepdims=True) + EPS) * g_ref[...]
    h_ref[...] = (y * (1.0 + sc_ref[...]) + sh_ref[...]).astype(h_ref.dtype)


def _first_norm(geo, x, ctx, g, mod):
    tr, d = geo.tr, geo.d
    tile = pl.BlockSpec((tr, d), lambda i: (i, 0))
    return pl.pallas_call(
        functools.partial(_first_norm_kernel, geo=geo),
        out_shape=(jax.ShapeDtypeStruct((geo.m, d), F32), jax.ShapeDtypeStruct((geo.m, d), BF16)),
        grid=(geo.m // tr,),
        in_specs=[pl.BlockSpec((None, tr, d), lambda i: (i // geo.nt, jnp.maximum(i % geo.nt - geo.nct, 0), 0)),
                  pl.BlockSpec((None, tr, d), lambda i: (i // geo.nt, jnp.minimum(i % geo.nt, geo.nct - 1), 0)),
                  pl.BlockSpec((None, 1, d), lambda i: (0, 0, 0)),
                  geo.mod_spec(0, 0), geo.mod_spec(0, 1)],
        out_specs=(tile, tile),
        name="first_norm",
        compiler_params=_cparams("parallel"),
    )(x, ctx, g, mod, mod)


def _mm_kernel(a_ref, w_ref, o_ref):
    o_ref[...] = _dot_nt(a_ref[...], w_ref[...]).astype(o_ref.dtype)


def _matmul_nt(a, w_t, layer, out_dtype):
    m, k = a.shape
    n = w_t.shape[1]
    tm = _pick(m, (1152, 1024, 512, 256, 128))
    tn = _pick(n, (1024, 512, 256, 128))
    return pl.pallas_call(
        _mm_kernel,
        out_shape=jax.ShapeDtypeStruct((m, n), out_dtype),
        grid=(n // tn, m // tm),
        in_specs=[pl.BlockSpec((tm, k), lambda j, i: (i, 0)),
                  pl.BlockSpec((None, tn, k), lambda j, i: (layer, j, 0))],
        out_specs=pl.BlockSpec((tm, tn), lambda j, i: (i, j)),
        name="proj_in",
        compiler_params=_cparams("parallel", "parallel"),
    )(a, w_t)


def _rms(x, g):
    return x * lax.rsqrt(jnp.mean(x * x, axis=-1, keepdims=True) + EPS) * g


def _qkv_kernel(cq_ref, ckv_ref, kr_ref, gq_ref, gkv_ref, wqa_ref, wqb_ref, wka_ref, sela_ref, selb_ref,
                wv_ref, qca_ref, qsb_ref, kca_ref, ksb_ref, q_ref, k_ref, v_ref):
    cqn = _rms(cq_ref[...].astype(F32), gq_ref[...]).astype(BF16)
    ckvn = _rms(ckv_ref[...].astype(F32), gkv_ref[...]).astype(BF16)
    kr = kr_ref[...]
    qa = _dot(cqn, wqa_ref[...])
    qb = _dot(cqn, wqb_ref[...])
    ka = _dot(ckvn, wka_ref[...]) + _dot(kr, sela_ref[...])
    kb = _dot(kr, selb_ref[...])
    qca, qsb, kca, ksb = qca_ref[...], qsb_ref[...], kca_ref[...], ksb_ref[...]
    for h in range(MLA_HEADS):
        sl = slice(h * MLA_SLOT, (h + 1) * MLA_SLOT)
        q_ref[:, sl] = (qa[:, sl] * qca + qb[:, sl] * qsb).astype(BF16)
        k_ref[:, sl] = (ka[:, sl] * kca + kb[:, sl] * ksb).astype(BF16)
    v_ref[...] = _dot(ckvn, wv_ref[...]).astype(BF16)


def _mla_qkv(geo, u, off, layer, gq, gkv, wqa, wqb, wka, sela, selb, wv, tabs):
    tr, m = geo.tr, geo.m
    hq = MLA_HEADS * MLA_SLOT
    hv = MLA_HEADS * MLA_V
    row = lambda w, o: pl.BlockSpec((tr, w), lambda i: (i, o // w))
    lw = lambda a: pl.BlockSpec((None,) + a.shape[1:], lambda i: (layer,) + (0,) * (a.ndim - 1))
    cw = lambda a: pl.BlockSpec(a.shape, lambda i: (0,) * a.ndim)
    tab = pl.BlockSpec((tr, MLA_SLOT), lambda i: (i % geo.nt, 0))
    return pl.pallas_call(
        _qkv_kernel,
        out_shape=(jax.ShapeDtypeStruct((m, hq), BF16), jax.ShapeDtypeStruct((m, hq), BF16),
                   jax.ShapeDtypeStruct((m, hv), BF16)),
        grid=(m // tr,),
        in_specs=[row(MLA_Q_RANK, off["cq"]), row(MLA_KV_RANK, off["ckv"]), row(128, off["kr"]),
                  lw(gq), lw(gkv), lw(wqa), lw(wqb), lw(wka), cw(sela), cw(selb), lw(wv), tab, tab, tab, tab],
        out_specs=(pl.BlockSpec((tr, hq), lambda i: (i, 0)), pl.BlockSpec((tr, hq), lambda i: (i, 0)),
                   pl.BlockSpec((tr, hv), lambda i: (i, 0))),
        name="mla_qkv",
        compiler_params=_cparams("parallel"),
    )(u, u, u, gq, gkv, wqa, wqb, wka, sela, selb, wv, *tabs)


def _attn_kernel(q_ref, k_ref, v_ref, o_ref, *, n_ctx, n_all, n_ctx_tiles):
    qi = pl.program_id(2)
    tq = q_ref.shape[0]
    lane = lax.broadcasted_iota(jnp.int32, (tq, 2 * MLA_V), 1)

    def run(nk):
        outs = []
        for hh in range(2):
            sl = slice(hh * MLA_SLOT, (hh + 1) * MLA_SLOT)
            s = _dot_nt(q_ref[:, sl], k_ref[0:nk, sl])
            p = jnp.exp(s - jnp.max(s, axis=-1, keepdims=True))
            inv = 1.0 / jnp.sum(p, axis=-1, keepdims=True)
            outs.append(_dot(p.astype(BF16), v_ref[0:nk, :]) * inv)
        o_ref[...] = jnp.where(lane < MLA_V, outs[0], outs[1]).astype(o_ref.dtype)

    @pl.when(qi < n_ctx_tiles)
    def _():
        run(n_ctx)

    @pl.when(qi >= n_ctx_tiles)
    def _():
        run(n_all)


def _mla_attention(geo, q, k, v):
    tq, n, bsz = geo.tr, geo.n, geo.b
    q3, k3, v3 = (t.reshape(bsz, n, t.shape[-1]) for t in (q, k, v))
    kern = functools.partial(_attn_kernel, n_ctx=geo.c, n_all=n, n_ctx_tiles=geo.nct)
    out = pl.pallas_call(
        kern,
        out_shape=jax.ShapeDtypeStruct((bsz, n, MLA_HEADS * MLA_V), BF16),
        grid=(bsz, MLA_HEADS // 2, geo.nt),
        in_specs=[pl.BlockSpec((None, tq, 2 * MLA_SLOT), lambda b, hp, i: (b, i, hp)),
                  pl.BlockSpec((None, n, 2 * MLA_SLOT), lambda b, hp, i: (b, 0, hp)),
                  pl.BlockSpec((None, n, 2 * MLA_V), lambda b, hp, i: (b, 0, hp))],
        out_specs=pl.BlockSpec((None, tq, 2 * MLA_V), lambda b, hp, i: (b, i, hp)),
        name="mla_attn",
        compiler_params=_cparams("parallel", "parallel", "parallel"),
    )(q3, k3, v3)
    return out.reshape(geo.m, MLA_HEADS * MLA_V)


def _halo_flags(geo, i):
    j = i % geo.nt
    left = jnp.logical_and(j != 0, j != geo.nct)
    right = jnp.logical_and(j != geo.nct - 1, j != geo.nt - 1)
    return left.astype(F32), right.astype(F32)


def _conv_shifts(n_taps):
    base = CONV_HALO - n_taps // 2
    return sorted({(base + kk) % SUBLANES for kk in range(n_taps)} - {0})


def _conv_taps(pad_ref, sh_ref, w_ref, n_taps, rows):
    base = CONV_HALO - n_taps // 2
    shifts = _conv_shifts(n_taps)
    span = sh_ref.shape[1]
    for s, r in enumerate(shifts):
        sh_ref[s, :, :] = pad_ref[r:r + span, :]

    def window(o):
        a, r = o - o % SUBLANES, o % SUBLANES
        return pad_ref[a:a + rows, :] if r == 0 else sh_ref[shifts.index(r), a:a + rows, :]

    acc = w_ref[0:1, :] * window(base)
    for kk in range(1, n_taps):
        acc = acc + w_ref[kk:kk + 1, :] * window(base + kk)
    return acc


def _conv_scratch(tr, width, n_taps):
    return [pltpu.VMEM((tr + 2 * CONV_HALO, width), F32),
            pltpu.VMEM((len(_conv_shifts(n_taps)), tr + 2 * CONV_HALO - SUBLANES, width), F32)]


def _conf_kernel(a_ref, b_ref, ap_ref, bp_ref, an_ref, bn_ref, w_ref, bias_ref, g_ref, beta_ref, o_ref,
                 pad_ref, sh_ref, *, geo):
    left, right = _halo_flags(geo, pl.program_id(0))
    rows = a_ref.shape[0]
    glu = lambda a, b: a[...].astype(F32) * jax.nn.sigmoid(b[...].astype(F32))
    pad_ref[0:CONV_HALO, :] = glu(ap_ref, bp_ref) * left
    pad_ref[CONV_HALO:CONV_HALO + rows, :] = glu(a_ref, b_ref)
    pad_ref[CONV_HALO + rows:2 * CONV_HALO + rows, :] = glu(an_ref, bn_ref) * right
    h = _conv_taps(pad_ref, sh_ref, w_ref, CF_KERNEL, rows) + bias_ref[...]
    hc = h - jnp.mean(h, axis=-1, keepdims=True)
    y = hc * lax.rsqrt(jnp.mean(hc * hc, axis=-1, keepdims=True) + EPS) * g_ref[...] + beta_ref[...]
    o_ref[...] = (y * jax.nn.sigmoid(y)).astype(o_ref.dtype)


def _sconv_kernel(gb_ref, gc_ref, h_ref, gcp_ref, hp_ref, gcn_ref, hn_ref, w_ref, o_ref, pad_ref, sh_ref, *, geo):
    left, right = _halo_flags(geo, pl.program_id(0))
    rows = h_ref.shape[0]
    prod = lambda a, b: a[...].astype(F32) * b[...].astype(F32)
    pad_ref[0:CONV_HALO, :] = prod(gcp_ref, hp_ref) * left
    pad_ref[CONV_HALO:CONV_HALO + rows, :] = prod(gc_ref, h_ref)
    pad_ref[CONV_HALO + rows:2 * CONV_HALO + rows, :] = prod(gcn_ref, hn_ref) * right
    y = _conv_taps(pad_ref, sh_ref, w_ref, SC_KERNEL, rows)
    o_ref[...] = (gb_ref[...].astype(F32) * y).astype(o_ref.dtype)


def _halo_specs(geo, width, offset):
    tr = geo.tr
    per = tr // CONV_HALO
    last = geo.m // CONV_HALO - 1
    cur = pl.BlockSpec((tr, width), lambda i: (i, offset // width))
    prev = pl.BlockSpec((CONV_HALO, width), lambda i: (jnp.maximum(i * per - 1, 0), offset // width))
    nxt = pl.BlockSpec((CONV_HALO, width), lambda i: (jnp.minimum((i + 1) * per, last), offset // width))
    return cur, prev, nxt


def _conformer(geo, u, off, layer, w_dw, b_dw, g_ln, b_ln):
    tr = geo.tr
    a_c, a_p, a_n = _halo_specs(geo, CF_WIDTH, off["cf_a"])
    b_c, b_p, b_n = _halo_specs(geo, CF_WIDTH, off["cf_b"])
    lw = lambda a: pl.BlockSpec((None,) + a.shape[1:], lambda i: (layer,) + (0,) * (a.ndim - 1))
    return pl.pallas_call(
        functools.partial(_conf_kernel, geo=geo),
        out_shape=jax.ShapeDtypeStruct((geo.m, CF_WIDTH), BF16),
        grid=(geo.m // tr,),
        in_specs=[a_c, b_c, a_p, b_p, a_n, b_n, lw(w_dw), lw(b_dw), lw(g_ln), lw(b_ln)],
        out_specs=pl.BlockSpec((tr, CF_WIDTH), lambda i: (i, 0)),
        scratch_shapes=_conv_scratch(tr, CF_WIDTH, CF_KERNEL),
        name="conformer",
        compiler_params=_cparams("parallel"),
    )(u, u, u, u, u, u, w_dw, b_dw, g_ln, b_ln)


def _shortconv(geo, u, off, layer, w_conv):
    tr = geo.tr
    gb_c, _, _ = _halo_specs(geo, SC_WIDTH, off["sc_b"])
    gc_c, gc_p, gc_n = _halo_specs(geo, SC_WIDTH, off["sc_c"])
    h_c, h_p, h_n = _halo_specs(geo, SC_WIDTH, off["sc_h"])
    lw = lambda a: pl.BlockSpec((None,) + a.shape[1:], lambda i: (layer,) + (0,) * (a.ndim - 1))
    return pl.pallas_call(
        functools.partial(_sconv_kernel, geo=geo),
        out_shape=jax.ShapeDtypeStruct((geo.m, SC_WIDTH), BF16),
        grid=(geo.m // tr,),
        in_specs=[gb_c, gc_c, h_c, gc_p, h_p, gc_n, h_n, lw(w_conv)],
        out_specs=pl.BlockSpec((tr, SC_WIDTH), lambda i: (i, 0)),
        scratch_shapes=_conv_scratch(tr, SC_WIDTH, SC_KERNEL),
        name="shortconv",
        compiler_params=_cparams("parallel"),
    )(u, u, u, u, u, u, u, w_conv)


def _gla_tables():
    c = GLA_CHUNK
    n_rows = (GLA_LEVELS + 2) * c + 8
    mat = np.zeros((2, n_rows, c), np.float32)
    msk = np.zeros((2, GLA_LEVELS + 1, c, c), np.float32)
    for lvl in range(GLA_LEVELS):
        hs = (c // 2) >> lvl
        for t in range(c):
            mid = (t // (2 * hs)) * 2 * hs + hs
            if t >= mid:
                mat[0, lvl * c + t, mid + 1:t + 1] = 1.0
                msk[0, lvl, t, mid - hs:mid] = 1.0
            else:
                mat[0, lvl * c + t, t + 1:mid + 1] = 1.0
    msk[0, GLA_LEVELS] = np.eye(c, dtype=np.float32)
    for t in range(c):
        mat[0, GLA_LEVELS * c + t, :t + 1] = 1.0
        mat[0, (GLA_LEVELS + 1) * c + t, t + 1:] = 1.0
    mat[0, (GLA_LEVELS + 2) * c:, :] = 1.0
    n_blk = GLA_LEVELS + 2
    mat[1, :n_blk * c] = mat[0, :n_blk * c].reshape(n_blk, c, c)[:, ::-1, ::-1].reshape(n_blk * c, c)
    mat[1, n_blk * c:] = 1.0
    msk[1] = msk[0][:, ::-1, ::-1]
    mat3 = np.concatenate([mat, mat, mat], axis=2)
    msk4 = np.tile(msk, (1, 1, GLA_HEADS, 1))
    return jnp.asarray(mat3, BF16), jnp.asarray(msk4, F32)


def _gla_kernel(qf_ref, kf_ref, vf_ref, gf_ref, qb_ref, kb_ref, vb_ref, gb_ref, w2_ref, b2_ref, mat_ref, msk_ref,
                of_ref, ob_ref, st_ref):
    @pl.when(pl.program_id(0) == 0)
    def _():
        st_ref[...] = jnp.zeros_like(st_ref)

    ins = ((qf_ref, kf_ref, vf_ref, gf_ref, of_ref), (qb_ref, kb_ref, vb_ref, gb_ref, ob_ref))
    for b in range(qf_ref.shape[0]):
        for d, (q_ref, k_ref, v_ref, g_ref, o_ref) in enumerate(ins):
            _gla_chunk(q_ref.at[b], k_ref.at[b], v_ref.at[b], g_ref.at[b], w2_ref.at[d], b2_ref.at[d],
                       mat_ref.at[d], msk_ref.at[d], o_ref.at[b], st_ref.at[d, b])


def _gla_chunk(q_ref, k_ref, v_ref, g_ref, w2_ref, b2_ref, mat_ref, msk_ref, o_ref, st_ref):
    c = GLA_CHUNK
    z = _dot(g_ref[...], w2_ref[...]) + b2_ref[...]
    la = (jnp.minimum(z, 0.0) - jnp.log1p(jnp.exp(-jnp.abs(z)))) * (1.0 / GLA_GATE_TAU)
    hi = la.astype(BF16)
    r1 = la - hi.astype(F32)
    lo = r1.astype(BF16)
    lo2 = (r1 - lo.astype(F32)).astype(BF16)
    ea = jnp.exp(_dot(mat_ref[...], jnp.concatenate([hi, lo, lo2], axis=0)))

    head = lax.broadcasted_iota(jnp.int32, (1, GLA_HEADS * GLA_DK), 1) // GLA_DK
    zero = jnp.zeros((), BF16)

    def stack(x):
        xb = x.astype(BF16)
        return jnp.concatenate([jnp.where(head == h, xb, zero) for h in range(GLA_HEADS)], axis=0)

    q = q_ref[...].astype(F32) * (GLA_DK ** -0.5)
    k = k_ref[...].astype(F32)
    vb = v_ref[...]

    attn = _dot_nt(stack(q), k.astype(BF16)) * msk_ref[GLA_LEVELS]
    for lvl in range(GLA_LEVELS):
        e = ea[lvl * c:(lvl + 1) * c]
        attn = attn + _dot_nt(stack(q * e), (k * e).astype(BF16)) * msk_ref[lvl]
    o_intra = _dot(attn.astype(BF16), vb)

    st = st_ref[...]
    o_inter = _dot_nt(stack(q * ea[GLA_LEVELS * c:(GLA_LEVELS + 1) * c]), st.astype(BF16))
    for h in range(GLA_HEADS):
        rows = slice(h * c, (h + 1) * c)
        cols = slice(h * GLA_DV, (h + 1) * GLA_DV)
        o_ref[:, cols] = o_intra[rows, cols] + o_inter[rows, :]

    k_dec = stack(k * ea[(GLA_LEVELS + 1) * c:(GLA_LEVELS + 2) * c])
    v_rows = jnp.concatenate([vb[:, h * GLA_DV:(h + 1) * GLA_DV] for h in range(GLA_HEADS)], axis=0)
    eye = (lax.broadcasted_iota(jnp.int32, (GLA_DV, GLA_DV), 0)
           == lax.broadcasted_iota(jnp.int32, (GLA_DV, GLA_DV), 1)).astype(BF16)
    v_t = _dot_nt(eye, v_rows).astype(BF16)
    tot = ea[(GLA_LEVELS + 2) * c:(GLA_LEVELS + 2) * c + 1]
    st_ref[...] = st * tot + _dot(v_t, k_dec)


def _gla_scan(geo, u, off, layer, w2, b2, mat3, msk4):
    c = GLA_CHUNK
    ncc = geo.c // c
    nc = geo.n // c
    dk, dv = GLA_HEADS * GLA_DK, GLA_HEADS * GLA_DV

    fwd = lambda i: i
    bwd = lambda i: jnp.where(i < ncc, ncc - 1 - i, nc - 1 - (i - ncc))

    u3 = u.reshape(geo.b, geo.n, u.shape[-1])
    row = lambda w, o, chunk: pl.BlockSpec((geo.b, c, w), lambda i: (0, chunk(i), o // w))
    rows = lambda chunk: [row(dk, off["gla_q"], chunk), row(dk, off["gla_k"], chunk), row(dv, off["gla_v"], chunk),
                          row(128, off["gate"], chunk)]
    out_sds = jax.ShapeDtypeStruct((geo.b, geo.n, dv), F32)
    o_f, o_b = pl.pallas_call(
        _gla_kernel,
        out_shape=(out_sds, out_sds),
        grid=(nc,),
        in_specs=rows(fwd) + rows(bwd) + [
            pl.BlockSpec((None, 2, 128, dk), lambda i: (layer, 0, 0, 0)),
            pl.BlockSpec((None, 2, 1, dk), lambda i: (layer, 0, 0, 0)),
            pl.BlockSpec(mat3.shape, lambda i: (0, 0, 0)),
            pl.BlockSpec(msk4.shape, lambda i: (0, 0, 0, 0))],
        out_specs=(pl.BlockSpec((geo.b, c, dv), lambda i: (0, fwd(i), 0)),
                   pl.BlockSpec((geo.b, c, dv), lambda i: (0, bwd(i), 0))),
        scratch_shapes=[pltpu.VMEM((2, geo.b, GLA_DV, dk), F32)],
        name="gla_scan",
        compiler_params=_cparams("arbitrary"),
    )(u3, u3, u3, u3, u3, u3, u3, u3, w2, b2, mat3, msk4)
    return o_f.reshape(geo.m, dv), o_b.reshape(geo.m, dv)


def _merge_mix_kernel(attn_ref, conf_ref, glaf_ref, glab_ref, r_ref, gn_ref, sc_ref, gate_ref, w0_ref, w1_ref,
                      w2_ref, w3_ref, wmix_ref, x_ref, gt_ref, gf_ref, shf_ref, scf_ref, wr_ref, br_ref, tri_ref,
                      o_ref, h_ref, idx_ref, wgt_ref, rank_ref, cnt_ref, y_ref, carry_ref):
    @pl.when(pl.program_id(0) == 0)
    def _():
        carry_ref[...] = jnp.zeros_like(carry_ref)

    wbr_ref = (w0_ref, w1_ref, w2_ref, w3_ref)
    o = glaf_ref[...] + glab_ref[...]
    r = r_ref[...].astype(F32)
    gn = gn_ref[...]
    normed = [_rms(o[:, h * GLA_DV:(h + 1) * GLA_DV], gn) for h in range(GLA_HEADS)]
    gla = (jnp.concatenate(normed, axis=1) * (r * jax.nn.sigmoid(r))).astype(BF16)
    acts = (attn_ref[...], conf_ref[...], gla, sc_ref[...])
    d = x_ref.shape[1]
    cw = min(MERGE_COLS, d)
    for c0 in range(0, d, cw):
        y = None
        for i in range(N_BRANCHES):
            half_b = _dot(acts[i], wbr_ref[i][:, c0:c0 + cw])
            t = jnp.tanh(gate_ref[:, i * d + c0:i * d + c0 + cw].astype(F32))
            term = t * half_b + half_b
            y = term if y is None else y + term
        y_ref[:, c0:c0 + cw] = y.astype(BF16)
    y_all = y_ref[...]
    for c0 in range(0, d, cw):
        cols = slice(c0, c0 + cw)
        o_ref[:, cols] = x_ref[:, cols] + gt_ref[:, cols] * _dot(y_all, wmix_ref[:, cols])
    _route_rows(o_ref[...], gf_ref, shf_ref, scf_ref, wr_ref, br_ref, tri_ref, h_ref, idx_ref, wgt_ref, rank_ref,
                cnt_ref, carry_ref)


def _merge_mix(geo, attn, conf, gla_f, gla_b, u, u_gates, off, g_norm, sconv, w_br_half, w_mix, x, mod, layer,
               g_ffn, wr_stack, b_router):
    tm, d, m = geo.tr, geo.d, geo.m
    w512 = GLA_HEADS * GLA_DV
    nch = d // 256
    act = pl.BlockSpec((tm, w512), lambda i: (i, 0))
    once = pl.Buffered(1)
    w_br = pl.BlockSpec((None, w512, d), lambda i: (layer, 0, 0), pipeline_mode=once)
    tri = jnp.asarray(np.triu(np.ones((tm, tm), np.float32), 1), BF16)
    pair = lambda dt: jax.ShapeDtypeStruct((2, m), dt)
    pair_spec = pl.BlockSpec((2, tm), lambda i: (0, i))
    return pl.pallas_call(
        _merge_mix_kernel,
        out_shape=(jax.ShapeDtypeStruct((m, d), F32), jax.ShapeDtypeStruct((m * nch, 128), jnp.uint32),
                   pair(jnp.int32), pair(F32), pair(jnp.int32), jax.ShapeDtypeStruct((N_EXPERTS, 1), F32)),
        grid=(m // tm,),
        in_specs=[act, act, act, act,
                  pl.BlockSpec((tm, w512), lambda i: (i, off["gla_r"] // w512)),
                  pl.BlockSpec((None, 1, GLA_DV), lambda i: (layer, 0, 0)),
                  act, pl.BlockSpec((tm, N_BRANCHES * d), lambda i: (i, 0)),
                  w_br, w_br, w_br, w_br,
                  pl.BlockSpec((None, d, d), lambda i: (layer, 0, 0), pipeline_mode=once),
                  pl.BlockSpec((tm, d), lambda i: (i, 0)), geo.mod_spec(layer, 2),
                  pl.BlockSpec((None, 1, d), lambda i: (layer, 0, 0)),
                  geo.mod_spec(layer, 3), geo.mod_spec(layer, 4),
                  pl.BlockSpec(wr_stack.shape, lambda i: (0, 0)),
                  pl.BlockSpec(b_router.shape, lambda i: (0, 0)),
                  pl.BlockSpec((tm, tm), lambda i: (0, 0))],
        out_specs=(pl.BlockSpec((tm, d), lambda i: (i, 0)), pl.BlockSpec((tm * nch, 128), lambda i: (i, 0)),
                   pair_spec, pair_spec, pair_spec, pl.BlockSpec((N_EXPERTS, 1), lambda i: (0, 0))),
        scratch_shapes=[pltpu.VMEM((tm, d), BF16), pltpu.VMEM((N_EXPERTS, 1), F32)],
        name="merge_mix",
        compiler_params=_cparams("arbitrary"),
    )(attn, conf, gla_f, gla_b, u, g_norm, sconv, u_gates, *w_br_half, w_mix, x, mod, g_ffn, mod, mod, wr_stack,
      b_router, tri)


def _cast_kernel(w_ref, o_ref, *, scale):
    o_ref[...] = (w_ref[...] * scale).astype(o_ref.dtype)


def _to_bf16(w, scale=1.0):
    n_l, rows, cols = w.shape
    tr = _pick(rows, (512, 256, 128))
    return pl.pallas_call(
        functools.partial(_cast_kernel, scale=scale),
        out_shape=jax.ShapeDtypeStruct(w.shape, BF16),
        grid=(n_l, rows // tr),
        in_specs=[pl.BlockSpec((None, tr, cols), lambda l, i: (l, i, 0))],
        out_specs=pl.BlockSpec((None, tr, cols), lambda l, i: (l, i, 0)),
        name="weight_cast",
        compiler_params=_cparams("parallel", "parallel"),
    )(w)


def _second_largest(a, b, c, d):
    return jnp.maximum(jnp.maximum(jnp.minimum(a, b), jnp.minimum(c, d)),
                       jnp.minimum(jnp.maximum(a, b), jnp.maximum(c, d)))


def _chunk_rows(rows, n, cc):
    return pl.ds(cc, rows, stride=n) if n > 1 else pl.ds(0, rows)


def _pack_rows(ref, y):
    rows = y.shape[0]
    n = ref.shape[0] // rows
    half = n * 128
    bits = lambda v: pltpu.bitcast(v.astype(BF16).astype(F32), jnp.uint32)
    packed = (bits(y[:, :half]) >> 16) | bits(y[:, half:])
    for cc in range(n):
        ref[_chunk_rows(rows, n, cc), :] = packed[:, cc * 128:(cc + 1) * 128]


def _unpack_cols(words):
    lo = pltpu.bitcast(words << 16, F32)
    hi = pltpu.bitcast(words & jnp.uint32(0xFFFF0000), F32)
    return lo, hi


def _route_rows(x, g_ref, sh_ref, sc_ref, wr_ref, br_ref, tri_ref, h_ref, idx_ref, wgt_ref, rank_ref, cnt_ref,
                carry_ref):
    y = _rms(x, g_ref[...]) * (1.0 + sc_ref[...]) + sh_ref[...]
    hi = y.astype(BF16)
    lo = (y - hi.astype(F32)).astype(BF16)
    _pack_rows(h_ref, y)
    ne = N_EXPERTS
    two = _dot_nt(wr_ref[...], hi)
    logits = two[0:ne] + two[ne:2 * ne] + _dot_nt(wr_ref[0:ne, :], lo)
    aff = 1.0 / (1.0 + jnp.exp(-logits))
    sel = aff + br_ref[...]
    per = ne // N_GROUPS
    gs = []
    for g in range(N_GROUPS):
        a, b, c, d = (sel[g * per + r:g * per + r + 1] for r in range(per))
        top1 = jnp.maximum(jnp.maximum(a, b), jnp.maximum(c, d))
        gs.append(top1 + _second_largest(a, b, c, d))
    best = gs[0]
    for g in range(1, N_GROUPS):
        best = jnp.maximum(best, gs[g])
    grp = jnp.full(best.shape, N_GROUPS, jnp.int32)
    for g in reversed(range(N_GROUPS)):
        grp = jnp.where(gs[g] == best, g, grp)
    erow = lax.broadcasted_iota(jnp.int32, sel.shape, 0)
    masked = jnp.where(erow // per == grp, sel, -jnp.inf)
    m1 = jnp.max(masked, axis=0, keepdims=True)
    i1 = jnp.min(jnp.where(masked == m1, erow, ne), axis=0, keepdims=True)
    one1 = erow == i1
    masked2 = jnp.where(one1, -jnp.inf, masked)
    m2 = jnp.max(masked2, axis=0, keepdims=True)
    i2 = jnp.min(jnp.where(masked2 == m2, erow, ne), axis=0, keepdims=True)
    one2 = erow == i2
    w1 = jnp.sum(jnp.where(one1, aff, 0.0), axis=0, keepdims=True)
    w2 = jnp.sum(jnp.where(one2, aff, 0.0), axis=0, keepdims=True)
    inv = 1.0 / (w1 + w2)
    idx_ref[0:1, :] = i1
    idx_ref[1:2, :] = i2
    wgt_ref[0:1, :] = w1 * inv
    wgt_ref[1:2, :] = w2 * inv
    o1 = jnp.where(one1, 1.0, 0.0)
    o2 = jnp.where(one2, 1.0, 0.0)
    tri = tri_ref[...]
    c0 = carry_ref[...]
    c1 = c0 + jnp.sum(o1, axis=1, keepdims=True)
    rank_ref[0:1, :] = jnp.sum(o1 * (c0 + _dot(o1.astype(BF16), tri)), axis=0, keepdims=True).astype(jnp.int32)
    rank_ref[1:2, :] = jnp.sum(o2 * (c1 + _dot(o2.astype(BF16), tri)), axis=0, keepdims=True).astype(jnp.int32)
    c2 = c1 + jnp.sum(o2, axis=1, keepdims=True)
    carry_ref[...] = c2
    cnt_ref[...] = c2


def _sorted_row(idx_ref, rank_ref, base_ref, a, n):
    return pl.multiple_of((base_ref[idx_ref[a]] + rank_ref[a]) * n, n)


def _dispatch_kernel(idx_ref, rank_ref, base_ref, last_ref, h_ref, xs_hbm, zero_ref, sem, zero_sem, *, m, n):
    i = pl.program_id(0)
    tr = h_ref.shape[0] // n

    @pl.when(i == 0)
    def _():
        zero_ref[...] = jnp.zeros_like(zero_ref)

        def fills(e):
            rows = zero_ref.shape[0]
            start = pl.multiple_of(last_ref[e] * n, rows)
            return pltpu.make_async_copy(zero_ref, xs_hbm.at[pl.ds(start, rows), :], zero_sem)

        for e in range(2 * N_EXPERTS):
            @pl.when(last_ref[e] >= 0)
            def _():
                fills(e).start()
        for e in range(2 * N_EXPERTS):
            @pl.when(last_ref[e] >= 0)
            def _():
                fills(e).wait()

    def issue(r, carry):
        src = h_ref.at[pl.ds(pl.multiple_of(r * n, n), n), :]
        for k in range(2):
            row = _sorted_row(idx_ref, rank_ref, base_ref, k * m + i * tr + r, n)
            pltpu.make_async_copy(src, xs_hbm.at[pl.ds(row, n), :], sem).start()
        return carry

    lax.fori_loop(0, tr, issue, 0, unroll=4)
    everything = xs_hbm.at[pl.ds(0, 2 * tr * n), :]
    pltpu.make_async_copy(everything, everything, sem).wait()


def _dispatch(hfp, ridx, rank, base, last_tile_row, n_rows, tr, tm):
    n = hfp.shape[0] * 2 // ridx.shape[0]
    m = hfp.shape[0] // n
    return pl.pallas_call(
        functools.partial(_dispatch_kernel, m=m, n=n),
        out_shape=jax.ShapeDtypeStruct((n_rows * n, 128), jnp.uint32),
        grid_spec=pltpu.PrefetchScalarGridSpec(
            num_scalar_prefetch=4, grid=(m // tr,),
            in_specs=[pl.BlockSpec((tr * n, 128), lambda i, *_: (i, 0))],
            out_specs=pl.BlockSpec(memory_space=pl.ANY),
            scratch_shapes=[pltpu.VMEM((tm * n, 128), jnp.uint32), pltpu.SemaphoreType.DMA(()),
                            pltpu.SemaphoreType.DMA(())]),
        name="moe_dispatch",
        compiler_params=_cparams("arbitrary"),
    )(ridx, rank, base, last_tile_row, hfp)


def _gmm_kernel(te_ref, nv_ref, xs_ref, wg_ref, wu_ref, wd_ref, ys_ref, wgb_ref, wub_ref, wdb_ref, prev_ref):
    j = pl.program_id(0)

    @pl.when(j == 0)
    def _():
        prev_ref[0] = -1

    @pl.when(j < nv_ref[0])
    def _():
        e = te_ref[j]

        @pl.when(e != prev_ref[0])
        def _():
            wgb_ref[...] = wg_ref[...].astype(BF16)
            wub_ref[...] = wu_ref[...].astype(BF16)
            wdb_ref[...] = wd_ref[...].astype(BF16)
            prev_ref[0] = e

        half = wgb_ref.shape[0] // 2
        n = half // 128
        tm = xs_ref.shape[0] // n
        parts = [_unpack_cols(xs_ref[_chunk_rows(tm, n, cc), :]) for cc in range(n)]
        xa = jnp.concatenate([p[0].astype(BF16) for p in parts], axis=1)
        xb = jnp.concatenate([p[1].astype(BF16) for p in parts], axis=1)
        g = _dot(xa, wgb_ref[0:half, :]) + _dot(xb, wgb_ref[half:2 * half, :])
        u = _dot(xa, wub_ref[0:half, :]) + _dot(xb, wub_ref[half:2 * half, :])
        act = (g * jax.nn.sigmoid(g) * u).astype(BF16)
        _pack_rows(ys_ref, _dot(act, wdb_ref[...]))

    @pl.when(j >= nv_ref[0])
    def _():
        ys_ref[...] = jnp.zeros_like(ys_ref)


def _gmm(xs, tile_expert, n_valid, wg, wu, wd, layer, tm):
    d, ff = wg.shape[-2], wg.shape[-1]
    nch = d // 256
    blk = tm * nch
    row = lambda j, te, nv: (jnp.minimum(j, nv[0] - 1), 0)
    wsel = lambda j, te, nv: (layer, te[jnp.minimum(j, nv[0] - 1)], 0, 0)
    return pl.pallas_call(
        _gmm_kernel,
        out_shape=jax.ShapeDtypeStruct(xs.shape, jnp.uint32),
        grid_spec=pltpu.PrefetchScalarGridSpec(
            num_scalar_prefetch=2, grid=(xs.shape[0] // blk,),
            in_specs=[pl.BlockSpec((blk, 128), row),
                      pl.BlockSpec((None, None, d, ff), wsel), pl.BlockSpec((None, None, d, ff), wsel),
                      pl.BlockSpec((None, None, ff, d), wsel)],
            out_specs=pl.BlockSpec((blk, 128), lambda j, te, nv: (j, 0)),
            scratch_shapes=[pltpu.VMEM((d, ff), BF16), pltpu.VMEM((d, ff), BF16), pltpu.VMEM((ff, d), BF16),
                            pltpu.SMEM((1,), jnp.int32)]),
        name="moe_experts",
        compiler_params=_cparams("arbitrary"),
    )(tile_expert, n_valid, xs, wg, wu, wd)


def _combine_kernel(idx_ref, rank_ref, base_ref, x_ref, w_ref, gt_ref, ys_hbm, *rest, m, last):
    if last:
        g_ref, h_ref, o_ref, buf_ref, sem_ref = rest
    else:
        g_ref, sh_ref, sc_ref, o_ref, h_ref, buf_ref, sem_ref = rest
    i = pl.program_id(0)
    n_tiles = pl.num_programs(0)
    tr = x_ref.shape[0]
    n = buf_ref.shape[2] // tr

    def issue(tile, slot):
        def body(r, carry):
            for k in range(2):
                row = _sorted_row(idx_ref, rank_ref, base_ref, k * m + tile * tr + r, n)
                pltpu.make_async_copy(ys_hbm.at[pl.ds(row, n), :],
                                      buf_ref.at[slot, k, pl.ds(pl.multiple_of(r * n, n), n), :],
                                      sem_ref.at[slot]).start()
            return carry
        lax.fori_loop(0, tr, body, 0, unroll=4)

    @pl.when(i == 0)
    def _():
        issue(0, 0)

    @pl.when(i + 1 < n_tiles)
    def _():
        issue(i + 1, (i + 1) % 2)

    slot = i % 2
    pltpu.make_async_copy(buf_ref.at[slot], buf_ref.at[slot], sem_ref.at[slot]).wait()
    w1, w2 = w_ref[:, 0:1], w_ref[:, 1:2]
    half = n * 128
    for cc in range(n):
        lo1, hi1 = _unpack_cols(buf_ref[slot, 0, _chunk_rows(tr, n, cc), :])
        lo2, hi2 = _unpack_cols(buf_ref[slot, 1, _chunk_rows(tr, n, cc), :])
        ca = slice(cc * 128, (cc + 1) * 128)
        cb = slice(half + cc * 128, half + (cc + 1) * 128)
        o_ref[:, ca] = x_ref[:, ca] + gt_ref[:, ca] * (w1 * lo1 + w2 * lo2)
        o_ref[:, cb] = x_ref[:, cb] + gt_ref[:, cb] * (w1 * hi1 + w2 * hi2)
    if last:
        h_ref[...] = _rms(o_ref[...], g_ref[...])
    else:
        y = _rms(o_ref[...], g_ref[...]) * (1.0 + sc_ref[...]) + sh_ref[...]
        h_ref[...] = y.astype(h_ref.dtype)


def _combine(geo, x, ys, ridx, rank, base, w12, mod, layer, g_next, g_final):
    tr, d = geo.tr, geo.d
    nch = d // 256
    last = g_next is None
    tile = pl.BlockSpec((tr, d), lambda i, *_: (i, 0))
    in_specs = [tile, pl.BlockSpec((tr, 2), lambda i, *_: (i, 0)), geo.mod_spec(layer, 5),
                pl.BlockSpec(memory_space=pl.ANY)]
    operands = [x, w12, mod, ys]
    scratch = [pltpu.VMEM((2, 2, tr * nch, 128), jnp.uint32), pltpu.SemaphoreType.DMA((2,))]
    if last:
        in_specs.append(pl.BlockSpec((1, d), lambda i, *_: (0, 0)))
        operands.append(g_final.reshape(1, d))
        out_shape = jax.ShapeDtypeStruct((geo.b, geo.s, d), F32)
        out_specs = pl.BlockSpec((None, tr, d), lambda i, *_: (i // geo.nt, jnp.maximum(i % geo.nt - geo.nct, 0), 0))
        scratch = [pltpu.VMEM((tr, d), F32)] + scratch
    else:
        in_specs += [pl.BlockSpec((None, 1, d), lambda i, *_: (layer + 1, 0, 0)),
                     geo.mod_spec(layer + 1, 0), geo.mod_spec(layer + 1, 1)]
        operands += [g_next, mod, mod]
        out_shape = [jax.ShapeDtypeStruct((geo.m, d), F32), jax.ShapeDtypeStruct((geo.m, d), BF16)]
        out_specs = [tile, tile]
    return pl.pallas_call(
        functools.partial(_combine_kernel, m=geo.m, last=last),
        out_shape=out_shape,
        grid_spec=pltpu.PrefetchScalarGridSpec(
            num_scalar_prefetch=3, grid=(geo.m // tr,),
            in_specs=in_specs, out_specs=out_specs, scratch_shapes=scratch),
        name="moe_combine",
        compiler_params=_cparams("arbitrary"),
    )(ridx, rank, base, *operands)


def _moe(geo, x, routed, mod, layer, wg, wu, wd, g_next, g_final):
    tm = 256
    hfp, ridx, rwgt, rank, cnt = routed
    n_tiles = (2 * geo.m) // tm + N_EXPERTS
    counts = cnt[:, 0].astype(jnp.int32)
    tiles_per = (counts + tm - 1) // tm
    tile_start = jnp.cumsum(tiles_per) - tiles_per
    base = tile_start * tm
    n_valid = jnp.sum(tiles_per).reshape(1)
    tile_expert = jnp.sum(jnp.arange(n_tiles)[:, None] >= tile_start[None, :], axis=1).astype(jnp.int32) - 1
    ridx, rank = ridx.reshape(-1), rank.reshape(-1)
    tail = n_valid + jnp.arange(N_EXPERTS)
    fill_rows = jnp.concatenate([jnp.where(tiles_per > 0, base + (tiles_per - 1) * tm, -1),
                                 jnp.where(tail < n_tiles, tail * tm, -1)])
    xs = _dispatch(hfp, ridx, rank, base, fill_rows, n_tiles * tm, geo.tr, tm)
    ys = _gmm(xs, tile_expert, n_valid, wg, wu, wd, layer, tm)
    return _combine(geo, x, ys, ridx, rank, base, rwgt.T, mod, layer, g_next, g_final)


def _w_in_kernel(w_ref, gates_ref, rest_ref, *, moves):
    cols = w_ref.shape[1]
    for dst, dst_row, src_row, height, scale in moves:
        out_ref = gates_ref if dst == "gates" else rest_ref
        for r0 in range(0, height, MERGE_COLS):
            rh = min(MERGE_COLS, height - r0)
            v = w_ref[src_row + r0:src_row + r0 + rh, :] * scale
            out_ref[dst_row + r0:dst_row + r0 + rh, :] = v.astype(BF16)
        pad = -height % 128
        if pad:
            out_ref[dst_row + height:dst_row + height + pad, :] = jnp.zeros((pad, cols), BF16)


def _prep_w_in(w_in, d):
    src = _src_layout(d)
    off, total = _proj_layout()
    w_t = jnp.swapaxes(w_in, 1, 2)
    n_l, height, cols = w_t.shape
    gates_w = N_BRANCHES * d
    one = lambda name, s, w=None: ("rest", off[name], s, w or src_w(name), 1.0)
    src_w = lambda name: src[name][1] - src[name][0]
    cf0, sc0 = src["cf"][0], src["sc"][0]
    moves = [("gates", 0, src["merge"][0], gates_w, 0.5),
             one("cq", src["cq"][0]), one("ckv", src["ckv"][0]), one("kr", src["kr"][0]),
             one("cf_a", cf0, CF_WIDTH), one("cf_b", cf0 + CF_WIDTH, CF_WIDTH),
             one("gla_q", src["gla_q"][0]), one("gla_k", src["gla_k"][0]), one("gla_v", src["gla_v"][0]),
             one("gate", src["gf"][0], 2 * GLA_GATE_RANK), one("gla_r", src["gla_r"][0]),
             one("sc_b", sc0, SC_WIDTH), one("sc_c", sc0 + SC_WIDTH, SC_WIDTH),
             one("sc_h", sc0 + 2 * SC_WIDTH, SC_WIDTH)]
    assert src["gb"][0] == src["gf"][1]
    kc = _pick(cols, (256, 128))
    gates, rest = pl.pallas_call(
        functools.partial(_w_in_kernel, moves=moves),
        out_shape=(jax.ShapeDtypeStruct((n_l, gates_w, cols), BF16), jax.ShapeDtypeStruct((n_l, total, cols), BF16)),
        grid=(n_l, cols // kc),
        in_specs=[pl.BlockSpec((None, height, kc), lambda l, i: (l, 0, i))],
        out_specs=(pl.BlockSpec((None, gates_w, kc), lambda l, i: (l, 0, i)),
                   pl.BlockSpec((None, total, kc), lambda l, i: (l, 0, i))),
        name="w_in_split",
        compiler_params=_cparams("parallel", "parallel"),
    )(w_t)
    return gates, rest, off


def _rope_swap_perm():
    q = MLA_ROPE // 4
    return np.concatenate([np.arange(q, 2 * q), np.arange(0, q), np.arange(3 * q, 4 * q), np.arange(2 * q, 3 * q)])


def _prep_mla(w_uq, w_ukv):
    n_l = w_uq.shape[0]
    perm = _rope_swap_perm()
    wq = w_uq.reshape(n_l, MLA_Q_RANK, MLA_HEADS, MLA_NOPE + MLA_ROPE)
    qn, qr = wq[..., :MLA_NOPE], wq[..., MLA_NOPE:]
    zeros = lambda *s: jnp.zeros((n_l,) + s, w_uq.dtype)
    wqa = jnp.concatenate([qn, qr, qr], axis=-1)
    wqb = jnp.concatenate([zeros(MLA_Q_RANK, MLA_HEADS, MLA_NOPE), qr[..., perm],
                           zeros(MLA_Q_RANK, MLA_HEADS, MLA_ROPE)], axis=-1)
    wkv = w_ukv.reshape(n_l, MLA_KV_RANK, MLA_HEADS, MLA_NOPE + MLA_V)
    wka = jnp.concatenate([wkv[..., :MLA_NOPE], zeros(MLA_KV_RANK, MLA_HEADS, MLA_SLOT - MLA_NOPE)], axis=-1)
    wv = wkv[..., MLA_NOPE:]
    flat = lambda w: w.reshape(n_l, w.shape[1], -1).astype(BF16)
    sela = np.zeros((128, MLA_HEADS, MLA_SLOT), np.float32)
    selb = np.zeros((128, MLA_HEADS, MLA_SLOT), np.float32)
    for r in range(MLA_ROPE):
        sela[r, :, MLA_NOPE + r] = 1.0
        sela[r, :, MLA_NOPE + MLA_ROPE + r] = 1.0
        selb[perm[r], :, MLA_NOPE + r] = 1.0
    sel = lambda s: jnp.asarray(s.reshape(128, -1), BF16)
    return flat(wqa), flat(wqb), flat(wka), sel(sela), sel(selb), flat(wv)


def _rope_tables(geo):
    rows = geo.s // GRID_W
    pos = jnp.arange(rows * GRID_W)
    row = (pos // GRID_W).astype(F32)
    col = (pos % GRID_W).astype(F32)
    n_freq = MLA_ROPE // 4
    freqs = ROPE_THETA ** (-jnp.arange(n_freq, dtype=F32) / n_freq)
    cr, sr = jnp.cos(row[:, None] * freqs), jnp.sin(row[:, None] * freqs)
    cc, sn = jnp.cos(col[:, None] * freqs), jnp.sin(col[:, None] * freqs)
    cos32 = jnp.concatenate([cr, cr, cc, cc], axis=-1)
    sin32 = jnp.concatenate([-sr, sr, -sn, sn], axis=-1)
    s, c = geo.s, geo.c
    one = lambda n, w: jnp.ones((n, w), F32)
    zero = lambda n, w: jnp.zeros((n, w), F32)
    scale = (MLA_NOPE + MLA_ROPE) ** -0.5
    qca = jnp.concatenate([jnp.concatenate([one(c, 64), zero(c, 32), one(c, 32)], 1),
                           jnp.concatenate([one(s, 64), cos32, one(s, 32)], 1)], 0) * scale
    qsb = jnp.concatenate([zero(c, 128), jnp.concatenate([zero(s, 64), sin32, zero(s, 32)], 1)], 0) * scale
    kca = jnp.concatenate([jnp.concatenate([one(c, 64), zero(c, 32), one(c, 32)], 1),
                           jnp.concatenate([one(s, 64), cos32, zero(s, 32)], 1)], 0)
    ksb = jnp.concatenate([zero(c, 128), jnp.concatenate([zero(s, 64), sin32, zero(s, 32)], 1)], 0)
    return qca, qsb, kca, ksb


def kernel(x, c, ctx, c_ctx, w_mod, b_mod, g_mix, g_ffn, w_in, g_q, w_uq, g_kv, w_ukv, w_mla_o, w_cf_dw, b_cf_dw, g_cf_ln, b_cf_ln, w_cf_o, w_gla_gf, b_gla_gf, w_gla_gb, b_gla_gb, g_gla_norm, w_gla_o, w_sc_conv, w_sc_o, w_mix_o, w_router, b_router, w_e_gate, w_e_up, w_e_down, g_final):
    bsz, seq, d = x.shape
    n_ctx = ctx.shape[1]
    depth = w_in.shape[0]
    geo = _Geom(bsz, n_ctx, seq, d)
    assert bsz + 1 <= 8 and seq % GRID_W == 0 and n_ctx % GLA_CHUNK == 0 and seq % GLA_CHUNK == 0

    w_gates, w_rest, off = _prep_w_in(w_in, d)
    wqa, wqb, wka, sela, selb, wv = _prep_mla(w_uq, w_ukv)
    tabs = _rope_tables(geo)
    w_br_half = [_to_bf16(w, 0.5) for w in (w_mla_o, w_cf_o, w_gla_o, w_sc_o)]
    w_mix = _to_bf16(w_mix_o)
    gate_w = jnp.zeros((depth, 2, 128, GLA_HEADS * GLA_DK), F32)
    gate_w = gate_w.at[:, 0, :GLA_GATE_RANK].set(w_gla_gf).at[:, 1, GLA_GATE_RANK:2 * GLA_GATE_RANK].set(w_gla_gb)
    gate_w = gate_w.astype(BF16)
    gate_b = jnp.stack([b_gla_gf, b_gla_gb], axis=1)[:, :, None, :]
    mat3, msk4 = _gla_tables()
    wr_t = w_router.T
    wr_hi = wr_t.astype(BF16)
    wr_stack = jnp.concatenate([wr_hi, (wr_t - wr_hi.astype(F32)).astype(BF16)], axis=0)
    vec = lambda a: a.reshape(a.shape[0], 1, a.shape[-1])

    cc = jnp.zeros((8, d), F32).at[:bsz].set(c).at[bsz].set(c_ctx)
    mod = _modulation(cc, w_mod, b_mod).reshape(depth, 8, 1, 6 * d)

    xa, h = _first_norm(geo, x, ctx, vec(g_mix), mod)
    for l in range(depth):
        u = _matmul_nt(h, w_rest, l, BF16)
        u_gates = _matmul_nt(h, w_gates, l, BF16)
        q, k, v = _mla_qkv(geo, u, off, l, vec(g_q), vec(g_kv), wqa, wqb, wka, sela, selb, wv, tabs)
        attn = _mla_attention(geo, q, k, v)
        conf = _conformer(geo, u, off, l, w_cf_dw, vec(b_cf_dw), vec(g_cf_ln), vec(b_cf_ln))
        sconv = _shortconv(geo, u, off, l, w_sc_conv)
        gla_f, gla_b = _gla_scan(geo, u, off, l, gate_w, gate_b, mat3, msk4)
        xa, *routed = _merge_mix(geo, attn, conf, gla_f, gla_b, u, u_gates, off, vec(g_gla_norm), sconv, w_br_half,
                                 w_mix, xa, mod, l, vec(g_ffn), wr_stack, b_router.reshape(N_EXPERTS, 1))
        if l + 1 < depth:
            xa, h = _moe(geo, xa, routed, mod, l, w_e_gate, w_e_up, w_e_down, vec(g_mix), None)
        else:
            return _moe(geo, xa, routed, mod, l, w_e_gate, w_e_up, w_e_down, None, g_final)
```

```python
import functools

import numpy as np
import jax
import jax.numpy as jnp
from jax import lax
from jax.experimental import pallas as pl
from jax.experimental.pallas import tpu as pltpu

F32 = jnp.float32
BF16 = jnp.bfloat16
EPS = 1e-6

GRID_W = 64
ROPE_THETA = 10000.0
MLA_HEADS = 8
MLA_NOPE = 64
MLA_ROPE = 32
MLA_V = 64
MLA_Q_RANK = 512
MLA_KV_RANK = 256
MLA_SLOT = 128
CF_WIDTH = 512
CF_KERNEL = 31
GLA_HEADS = 4
GLA_DK = 64
GLA_DV = 128
GLA_GATE_RANK = 16
GLA_GATE_TAU = 16.0
GLA_CHUNK = 64
GLA_LEVELS = 6
SC_WIDTH = 512
SC_KERNEL = 3
N_BRANCHES = 4
N_EXPERTS = 16
N_GROUPS = 4
EXPERT_FF = 512
CONV_HALO = 16
SUBLANES = 8
MERGE_COLS = 512
ATTN_ROWS = 768
PROJ_TILE = 1024
PROJ_CAST_ROWS = 256
VMEM_LIMIT = 56 * 1024 * 1024


def _cparams(*sem):
    return pltpu.CompilerParams(dimension_semantics=sem, vmem_limit_bytes=VMEM_LIMIT)


def _pick(n, cands):
    for c in cands:
        if n % c == 0:
            return c
    raise ValueError(f"no tile for {n}")


def _dot(a, b):
    return jnp.dot(a, b, preferred_element_type=F32)


def _dot_nt(a, b):
    return lax.dot_general(a, b, (((1,), (1,)), ((), ())), preferred_element_type=F32)


def _proj_layout():
    parts = [("sc_b", SC_WIDTH), ("sc_c", SC_WIDTH), ("sc_h", SC_WIDTH),
             ("cf_a", CF_WIDTH), ("cf_b", CF_WIDTH), ("cq", MLA_Q_RANK), ("gla_v", GLA_HEADS * GLA_DV),
             ("gla_r", GLA_HEADS * GLA_DV), ("ckv", MLA_KV_RANK), ("gla_q", GLA_HEADS * GLA_DK),
             ("gla_k", GLA_HEADS * GLA_DK), ("kr", 128), ("gate", 128)]
    off, start = {}, 0
    for name, width in parts:
        assert start % width == 0
        off[name] = start
        start += width
    return off, start


def _src_layout(d):
    parts = (("cq", MLA_Q_RANK), ("ckv", MLA_KV_RANK), ("kr", MLA_ROPE), ("cf", 2 * CF_WIDTH),
             ("gla_q", GLA_HEADS * GLA_DK), ("gla_k", GLA_HEADS * GLA_DK), ("gla_v", GLA_HEADS * GLA_DV),
             ("gf", GLA_GATE_RANK), ("gb", GLA_GATE_RANK), ("gla_r", GLA_HEADS * GLA_DV),
             ("sc", 3 * SC_WIDTH), ("merge", N_BRANCHES * d))
    off, start = {}, 0
    for name, width in parts:
        off[name] = (start, start + width)
        start += width
    return off


def _mod_kernel(c_ref, w_ref, b_ref, o_ref):
    c = c_ref[...]
    a = (c * jax.nn.sigmoid(c)).astype(BF16)
    o_ref[...] = _dot(a, w_ref[...].astype(BF16)) + b_ref[...]


def _modulation(cc, w_mod, b_mod):
    n_l, d, n6 = w_mod.shape
    tn = _pick(n6, (1024, 512, 256, 128))
    return pl.pallas_call(
        _mod_kernel,
        out_shape=jax.ShapeDtypeStruct((n_l, 8, n6), F32),
        grid=(n_l, n6 // tn),
        in_specs=[pl.BlockSpec((8, d), lambda l, j: (0, 0)),
                  pl.BlockSpec((None, d, tn), lambda l, j: (l, 0, j)),
                  pl.BlockSpec((None, 1, tn), lambda l, j: (l, 0, j))],
        out_specs=pl.BlockSpec((None, 8, tn), lambda l, j: (l, 0, j)),
        name="adaln_mod",
        compiler_params=_cparams("parallel", "parallel"),
    )(cc, w_mod, b_mod.reshape(n_l, 1, n6))


class _Geom:
    def __init__(self, bsz, n_ctx, seq, d):
        self.b, self.c, self.s, self.d = bsz, n_ctx, seq, d
        self.n = n_ctx + seq
        self.m = bsz * self.n
        self.tr = _pick(int(np.gcd(n_ctx, seq)), (256, 128, 64))
        self.nt = self.n // self.tr
        self.nct = n_ctx // self.tr

    def mod_row(self, i):
        b, j = i // self.nt, i % self.nt
        return jnp.where(j < self.nct, self.b, b)

    def mod_spec(self, layer, part):
        return pl.BlockSpec((None, None, 1, self.d), lambda i, *_: (layer, self.mod_row(i), 0, part))


def _first_norm_kernel(x_ref, c_ref, g_ref, sh_ref, sc_ref, xa_ref, h_ref, *, geo):
    is_ctx = pl.program_id(0) % geo.nt < geo.nct
    rows = jnp.where(is_ctx, c_ref[...], x_ref[...])
    xa_ref[...] = rows
    y = rows * lax.rsqrt(jnp.mean(rows * rows, axis=-1, keepdims=True) + EPS) * g_ref[...]
    h_ref[...] = (y * (1.0 + sc_ref[...]) + sh_ref[...]).astype(h_ref.dtype)


def _first_norm(geo, x, ctx, g, mod):
    tr, d = geo.tr, geo.d
    tile = pl.BlockSpec((tr, d), lambda i: (i, 0))
    return pl.pallas_call(
        functools.partial(_first_norm_kernel, geo=geo),
        out_shape=(jax.ShapeDtypeStruct((geo.m, d), F32), jax.ShapeDtypeStruct((geo.m, d), BF16)),
        grid=(geo.m // tr,),
        in_specs=[pl.BlockSpec((None, tr, d), lambda i: (i // geo.nt, jnp.maximum(i % geo.nt - geo.nct, 0), 0)),
                  pl.BlockSpec((None, tr, d), lambda i: (i // geo.nt, jnp.minimum(i % geo.nt, geo.nct - 1), 0)),
                  pl.BlockSpec((None, 1, d), lambda i: (0, 0, 0)),
                  geo.mod_spec(0, 0), geo.mod_spec(0, 1)],
        out_specs=(tile, tile),
        name="first_norm",
        compiler_params=_cparams("parallel"),
    )(x, ctx, g, mod, mod)


def _proj_kernel(a_ref, w_hbm, o_ref, wbuf_ref, wb_ref, sem_ref, *, layer, plan, scale):
    j, i = pl.program_id(0), pl.program_id(1)
    n_tiles = pl.num_programs(0)
    tn = wb_ref.shape[0]

    def copies(t, slot):
        return [pltpu.make_async_copy(w_hbm.at[layer, pl.ds(src, rows), :], wbuf_ref.at[slot, pl.ds(dst, rows), :],
                                      sem_ref.at[slot]) for dst, src, rows in plan(t)]

    def on_tile(t, fn):
        if not plan.static:
            fn(t)
            return
        for tt in range(plan.n_tiles):
            pl.when(t == tt)(functools.partial(fn, tt))

    def start(t):
        for cp in copies(t, t % 2):
            cp.start()

    def land(t):
        slot = t % 2
        for cp in copies(t, slot):
            cp.wait()
        covered = 0
        for dst, _, rows in plan(t):
            if dst > covered:
                wb_ref[covered:dst, :] = jnp.zeros((dst - covered, wb_ref.shape[1]), BF16)
            for r0 in range(0, rows, PROJ_CAST_ROWS):
                rr = min(PROJ_CAST_ROWS, rows - r0)
                rows_f32 = wbuf_ref[slot, dst + r0:dst + r0 + rr, :]
                wb_ref[dst + r0:dst + r0 + rr, :] = (rows_f32 * scale).astype(BF16)
            covered = dst + rows
        if covered < tn:
            wb_ref[covered:tn, :] = jnp.zeros((tn - covered, wb_ref.shape[1]), BF16)

    @pl.when(i == 0)
    def _():
        @pl.when(j == 0)
        def _():
            on_tile(j, start)

        on_tile(j, land)

        @pl.when(j + 1 < n_tiles)
        def _():
            on_tile(j + 1, start)

    o_ref[...] = _dot_nt(a_ref[...], wb_ref[...]).astype(o_ref.dtype)


class _ProjPlan:
    def __init__(self, tn, n_tiles, pieces=None, first_row=None):
        self.tn, self.n_tiles, self.static = tn, n_tiles, pieces is not None
        self._pieces, self._first_row = pieces, first_row

    def __call__(self, t):
        if self.static:
            lo = t * self.tn
            return [(dst - lo, src, rows) for dst, src, rows in self._pieces if lo <= dst < lo + self.tn]
        return [(0, pl.multiple_of(self._first_row + t * self.tn, SUBLANES), self.tn)]


def _project(a, w_t, layer, plan, scale):
    m, k = a.shape
    tn = plan.tn
    tm = _pick(m, (1152, 1024, 512, 256, 128))
    return pl.pallas_call(
        functools.partial(_proj_kernel, layer=layer, plan=plan, scale=scale),
        out_shape=jax.ShapeDtypeStruct((m, tn * plan.n_tiles), BF16),
        grid=(plan.n_tiles, m // tm),
        in_specs=[pl.BlockSpec((tm, k), lambda j, i: (i, 0)), pl.BlockSpec(memory_space=pl.ANY)],
        out_specs=pl.BlockSpec((tm, tn), lambda j, i: (i, j)),
        scratch_shapes=[pltpu.VMEM((2, tn, k), F32), pltpu.VMEM((tn, k), BF16), pltpu.SemaphoreType.DMA((2,))],
        name="proj_in",
        compiler_params=_cparams("arbitrary", "arbitrary"),
    )(a, w_t)


def _rms(x, g):
    return x * lax.rsqrt(jnp.mean(x * x, axis=-1, keepdims=True) + EPS) * g


def _qkv_kernel(cq_ref, ckv_ref, kr_ref, gq_ref, gkv_ref, wqa_ref, wqb_ref, wka_ref, sela_ref, selb_ref,
                wv_ref, qca_ref, qsb_ref, kca_ref, ksb_ref, q_ref, k_ref, v_ref):
    cqn = _rms(cq_ref[...].astype(F32), gq_ref[...]).astype(BF16)
    ckvn = _rms(ckv_ref[...].astype(F32), gkv_ref[...]).astype(BF16)
    kr = kr_ref[...]
    qa = _dot(cqn, wqa_ref[...])
    qb = _dot(cqn, wqb_ref[...])
    ka = _dot(ckvn, wka_ref[...]) + _dot(kr, sela_ref[...])
    kb = _dot(kr, selb_ref[...])
    qca, qsb, kca, ksb = qca_ref[...], qsb_ref[...], kca_ref[...], ksb_ref[...]
    for h in range(MLA_HEADS):
        sl = slice(h * MLA_SLOT, (h + 1) * MLA_SLOT)
        q_ref[:, sl] = (qa[:, sl] * qca + qb[:, sl] * qsb).astype(BF16)
        k_ref[:, sl] = (ka[:, sl] * kca + kb[:, sl] * ksb).astype(BF16)
    v_ref[...] = _dot(ckvn, wv_ref[...]).astype(BF16)


def _mla_qkv(geo, u, off, layer, gq, gkv, wqa, wqb, wka, sela, selb, wv, tabs):
    tr, m = geo.tr, geo.m
    hq = MLA_HEADS * MLA_SLOT
    hv = MLA_HEADS * MLA_V
    row = lambda w, o: pl.BlockSpec((tr, w), lambda i: (i, o // w))
    lw = lambda a: pl.BlockSpec((None,) + a.shape[1:], lambda i: (layer,) + (0,) * (a.ndim - 1))
    cw = lambda a: pl.BlockSpec(a.shape, lambda i: (0,) * a.ndim)
    tab = pl.BlockSpec((tr, MLA_SLOT), lambda i: (i % geo.nt, 0))
    return pl.pallas_call(
        _qkv_kernel,
        out_shape=(jax.ShapeDtypeStruct((m, hq), BF16), jax.ShapeDtypeStruct((m, hq), BF16),
                   jax.ShapeDtypeStruct((m, hv), BF16)),
        grid=(m // tr,),
        in_specs=[row(MLA_Q_RANK, off["cq"]), row(MLA_KV_RANK, off["ckv"]), row(128, off["kr"]),
                  lw(gq), lw(gkv), lw(wqa), lw(wqb), lw(wka), cw(sela), cw(selb), lw(wv), tab, tab, tab, tab],
        out_specs=(pl.BlockSpec((tr, hq), lambda i: (i, 0)), pl.BlockSpec((tr, hq), lambda i: (i, 0)),
                   pl.BlockSpec((tr, hv), lambda i: (i, 0))),
        name="mla_qkv",
        compiler_params=_cparams("parallel"),
    )(u, u, u, gq, gkv, wqa, wqb, wka, sela, selb, wv, *tabs)


def _attn_kernel(q_ref, k_ref, v_ref, o_ref, *, n_ctx, n_all):
    tq = q_ref.shape[0]

    def run(r0, rows, nk):
        lane = lax.broadcasted_iota(jnp.int32, (rows, 2 * MLA_V), 1)
        outs = []
        for hh in range(2):
            sl = slice(hh * MLA_SLOT, (hh + 1) * MLA_SLOT)
            s = _dot_nt(q_ref[r0:r0 + rows, sl], k_ref[0:nk, sl])
            p = jnp.exp(s - jnp.max(s, axis=-1, keepdims=True))
            inv = 1.0 / jnp.sum(p, axis=-1, keepdims=True)
            outs.append(_dot(p.astype(BF16), v_ref[0:nk, :]) * inv)
        o_ref[r0:r0 + rows, :] = jnp.where(lane < MLA_V, outs[0], outs[1]).astype(o_ref.dtype)

    @pl.when(pl.program_id(2) == 0)
    def _():
        run(0, n_ctx, n_ctx)
        if tq > n_ctx:
            run(n_ctx, tq - n_ctx, n_all)

    @pl.when(pl.program_id(2) > 0)
    def _():
        run(0, tq, n_all)


def _mla_attention(geo, q, k, v):
    n, bsz = geo.n, geo.b
    groups = [g for g in range(1, geo.nt + 1) if geo.nt % g == 0 and g * geo.tr >= geo.c]
    tq = geo.tr * min(groups, key=lambda g: abs(g * geo.tr - ATTN_ROWS))
    q3, k3, v3 = (t.reshape(bsz, n, t.shape[-1]) for t in (q, k, v))
    kern = functools.partial(_attn_kernel, n_ctx=geo.c, n_all=n)
    out = pl.pallas_call(
        kern,
        out_shape=jax.ShapeDtypeStruct((bsz, n, MLA_HEADS * MLA_V), BF16),
        grid=(bsz, MLA_HEADS // 2, n // tq),
        in_specs=[pl.BlockSpec((None, tq, 2 * MLA_SLOT), lambda b, hp, i: (b, i, hp)),
                  pl.BlockSpec((None, n, 2 * MLA_SLOT), lambda b, hp, i: (b, 0, hp)),
                  pl.BlockSpec((None, n, 2 * MLA_V), lambda b, hp, i: (b, 0, hp))],
        out_specs=pl.BlockSpec((None, tq, 2 * MLA_V), lambda b, hp, i: (b, i, hp)),
        name="mla_attn",
        compiler_params=_cparams("parallel", "parallel", "parallel"),
    )(q3, k3, v3)
    return out.reshape(geo.m, MLA_HEADS * MLA_V)


def _halo_flags(geo, i):
    j = i % geo.nt
    left = jnp.logical_and(j != 0, j != geo.nct)
    right = jnp.logical_and(j != geo.nct - 1, j != geo.nt - 1)
    return left.astype(F32), right.astype(F32)


def _conv_shifts(n_taps):
    base = CONV_HALO - n_taps // 2
    return sorted({(base + kk) % SUBLANES for kk in range(n_taps)} - {0})


def _conv_taps(pad_ref, sh_ref, w_ref, n_taps, rows):
    base = CONV_HALO - n_taps // 2
    shifts = _conv_shifts(n_taps)
    span = sh_ref.shape[1]
    for s, r in enumerate(shifts):
        sh_ref[s, :, :] = pad_ref[r:r + span, :]

    def window(o):
        a, r = o - o % SUBLANES, o % SUBLANES
        return pad_ref[a:a + rows, :] if r == 0 else sh_ref[shifts.index(r), a:a + rows, :]

    acc = w_ref[0:1, :] * window(base)
    for kk in range(1, n_taps):
        acc = acc + w_ref[kk:kk + 1, :] * window(base + kk)
    return acc


def _conv_scratch(tr, width, n_taps):
    return [pltpu.VMEM((tr + 2 * CONV_HALO, width), F32),
            pltpu.VMEM((len(_conv_shifts(n_taps)), tr + 2 * CONV_HALO - SUBLANES, width), F32)]


def _conf_kernel(a_ref, b_ref, ap_ref, bp_ref, an_ref, bn_ref, w_ref, bias_ref, g_ref, beta_ref, o_ref,
                 pad_ref, sh_ref, *, geo):
    left, right = _halo_flags(geo, pl.program_id(0))
    rows = a_ref.shape[0]
    glu = lambda a, b: a[...].astype(F32) * jax.nn.sigmoid(b[...].astype(F32))
    pad_ref[0:CONV_HALO, :] = glu(ap_ref, bp_ref) * left
    pad_ref[CONV_HALO:CONV_HALO + rows, :] = glu(a_ref, b_ref)
    pad_ref[CONV_HALO + rows:2 * CONV_HALO + rows, :] = glu(an_ref, bn_ref) * right
    h = _conv_taps(pad_ref, sh_ref, w_ref, CF_KERNEL, rows) + bias_ref[...]
    hc = h - jnp.mean(h, axis=-1, keepdims=True)
    y = hc * lax.rsqrt(jnp.mean(hc * hc, axis=-1, keepdims=True) + EPS) * g_ref[...] + beta_ref[...]
    o_ref[...] = (y * jax.nn.sigmoid(y)).astype(o_ref.dtype)


def _sconv_kernel(gb_ref, gc_ref, h_ref, gcp_ref, hp_ref, gcn_ref, hn_ref, w_ref, o_ref, pad_ref, sh_ref, *, geo):
    left, right = _halo_flags(geo, pl.program_id(0))
    rows = h_ref.shape[0]
    prod = lambda a, b: a[...].astype(F32) * b[...].astype(F32)
    pad_ref[0:CONV_HALO, :] = prod(gcp_ref, hp_ref) * left
    pad_ref[CONV_HALO:CONV_HALO + rows, :] = prod(gc_ref, h_ref)
    pad_ref[CONV_HALO + rows:2 * CONV_HALO + rows, :] = prod(gcn_ref, hn_ref) * right
    y = _conv_taps(pad_ref, sh_ref, w_ref, SC_KERNEL, rows)
    o_ref[...] = (gb_ref[...].astype(F32) * y).astype(o_ref.dtype)


def _halo_specs(geo, width, offset):
    tr = geo.tr
    per = tr // CONV_HALO
    last = geo.m // CONV_HALO - 1
    cur = pl.BlockSpec((tr, width), lambda i: (i, offset // width))
    prev = pl.BlockSpec((CONV_HALO, width), lambda i: (jnp.maximum(i * per - 1, 0), offset // width))
    nxt = pl.BlockSpec((CONV_HALO, width), lambda i: (jnp.minimum((i + 1) * per, last), offset // width))
    return cur, prev, nxt


def _conformer(geo, u, off, layer, w_dw, b_dw, g_ln, b_ln):
    tr = geo.tr
    a_c, a_p, a_n = _halo_specs(geo, CF_WIDTH, off["cf_a"])
    b_c, b_p, b_n = _halo_specs(geo, CF_WIDTH, off["cf_b"])
    lw = lambda a: pl.BlockSpec((None,) + a.shape[1:], lambda i: (layer,) + (0,) * (a.ndim - 1))
    return pl.pallas_call(
        functools.partial(_conf_kernel, geo=geo),
        out_shape=jax.ShapeDtypeStruct((geo.m, CF_WIDTH), BF16),
        grid=(geo.m // tr,),
        in_specs=[a_c, b_c, a_p, b_p, a_n, b_n, lw(w_dw), lw(b_dw), lw(g_ln), lw(b_ln)],
        out_specs=pl.BlockSpec((tr, CF_WIDTH), lambda i: (i, 0)),
        scratch_shapes=_conv_scratch(tr, CF_WIDTH, CF_KERNEL),
        name="conformer",
        compiler_params=_cparams("parallel"),
    )(u, u, u, u, u, u, w_dw, b_dw, g_ln, b_ln)


def _shortconv(geo, u, off, layer, w_conv):
    tr = geo.tr
    gb_c, _, _ = _halo_specs(geo, SC_WIDTH, off["sc_b"])
    gc_c, gc_p, gc_n = _halo_specs(geo, SC_WIDTH, off["sc_c"])
    h_c, h_p, h_n = _halo_specs(geo, SC_WIDTH, off["sc_h"])
    lw = lambda a: pl.BlockSpec((None,) + a.shape[1:], lambda i: (layer,) + (0,) * (a.ndim - 1))
    return pl.pallas_call(
        functools.partial(_sconv_kernel, geo=geo),
        out_shape=jax.ShapeDtypeStruct((geo.m, SC_WIDTH), BF16),
        grid=(geo.m // tr,),
        in_specs=[gb_c, gc_c, h_c, gc_p, h_p, gc_n, h_n, lw(w_conv)],
        out_specs=pl.BlockSpec((tr, SC_WIDTH), lambda i: (i, 0)),
        scratch_shapes=_conv_scratch(tr, SC_WIDTH, SC_KERNEL),
        name="shortconv",
        compiler_params=_cparams("parallel"),
    )(u, u, u, u, u, u, u, w_conv)


def _gla_tables():
    c = GLA_CHUNK
    n_rows = (GLA_LEVELS + 2) * c + 8
    mat = np.zeros((2, n_rows, c), np.float32)
    msk = np.zeros((2, GLA_LEVELS + 1, c, c), np.float32)
    for lvl in range(GLA_LEVELS):
        hs = (c // 2) >> lvl
        for t in range(c):
            mid = (t // (2 * hs)) * 2 * hs + hs
            if t >= mid:
                mat[0, lvl * c + t, mid + 1:t + 1] = 1.0
                msk[0, lvl, t, mid - hs:mid] = 1.0
            else:
                mat[0, lvl * c + t, t + 1:mid + 1] = 1.0
    msk[0, GLA_LEVELS] = np.eye(c, dtype=np.float32)
    for t in range(c):
        mat[0, GLA_LEVELS * c + t, :t + 1] = 1.0
        mat[0, (GLA_LEVELS + 1) * c + t, t + 1:] = 1.0
    mat[0, (GLA_LEVELS + 2) * c:, :] = 1.0
    n_blk = GLA_LEVELS + 2
    mat[1, :n_blk * c] = mat[0, :n_blk * c].reshape(n_blk, c, c)[:, ::-1, ::-1].reshape(n_blk * c, c)
    mat[1, n_blk * c:] = 1.0
    msk[1] = msk[0][:, ::-1, ::-1]
    mat3 = np.concatenate([mat, mat, mat], axis=2)
    msk4 = np.tile(msk, (1, 1, GLA_HEADS, 1))
    return jnp.asarray(mat3, BF16), jnp.asarray(msk4, F32)


def _gla_kernel(qf_ref, kf_ref, vf_ref, gf_ref, qb_ref, kb_ref, vb_ref, gb_ref, w2_ref, b2_ref, mat_ref, msk_ref,
                of_ref, ob_ref, st_ref):
    @pl.when(pl.program_id(0) == 0)
    def _():
        st_ref[...] = jnp.zeros_like(st_ref)

    ins = ((qf_ref, kf_ref, vf_ref, gf_ref, of_ref), (qb_ref, kb_ref, vb_ref, gb_ref, ob_ref))
    for b in range(qf_ref.shape[0]):
        for d, (q_ref, k_ref, v_ref, g_ref, o_ref) in enumerate(ins):
            _gla_chunk(q_ref.at[b], k_ref.at[b], v_ref.at[b], g_ref.at[b], w2_ref.at[d], b2_ref.at[d],
                       mat_ref.at[d], msk_ref.at[d], o_ref.at[b], st_ref.at[d, b])


def _gla_chunk(q_ref, k_ref, v_ref, g_ref, w2_ref, b2_ref, mat_ref, msk_ref, o_ref, st_ref):
    c = GLA_CHUNK
    z = _dot(g_ref[...], w2_ref[...]) + b2_ref[...]
    la = (jnp.minimum(z, 0.0) - jnp.log1p(jnp.exp(-jnp.abs(z)))) * (1.0 / GLA_GATE_TAU)
    hi = la.astype(BF16)
    r1 = la - hi.astype(F32)
    lo = r1.astype(BF16)
    lo2 = (r1 - lo.astype(F32)).astype(BF16)
    ea = jnp.exp(_dot(mat_ref[...], jnp.concatenate([hi, lo, lo2], axis=0)))

    head = lax.broadcasted_iota(jnp.int32, (1, GLA_HEADS * GLA_DK), 1) // GLA_DK
    zero = jnp.zeros((), BF16)

    def stack(x):
        xb = x.astype(BF16)
        return jnp.concatenate([jnp.where(head == h, xb, zero) for h in range(GLA_HEADS)], axis=0)

    q = q_ref[...].astype(F32) * (GLA_DK ** -0.5)
    k = k_ref[...].astype(F32)
    vb = v_ref[...]

    attn = _dot_nt(stack(q), k.astype(BF16)) * msk_ref[GLA_LEVELS]
    for lvl in range(GLA_LEVELS):
        e = ea[lvl * c:(lvl + 1) * c]
        attn = attn + _dot_nt(stack(q * e), (k * e).astype(BF16)) * msk_ref[lvl]
    o_intra = _dot(attn.astype(BF16), vb)

    st = st_ref[...]
    o_inter = _dot_nt(stack(q * ea[GLA_LEVELS * c:(GLA_LEVELS + 1) * c]), st.astype(BF16))
    for h in range(GLA_HEADS):
        rows = slice(h * c, (h + 1) * c)
        cols = slice(h * GLA_DV, (h + 1) * GLA_DV)
        o_ref[:, cols] = o_intra[rows, cols] + o_inter[rows, :]

    k_dec = stack(k * ea[(GLA_LEVELS + 1) * c:(GLA_LEVELS + 2) * c])
    v_rows = jnp.concatenate([vb[:, h * GLA_DV:(h + 1) * GLA_DV] for h in range(GLA_HEADS)], axis=0)
    eye = (lax.broadcasted_iota(jnp.int32, (GLA_DV, GLA_DV), 0)
           == lax.broadcasted_iota(jnp.int32, (GLA_DV, GLA_DV), 1)).astype(BF16)
    v_t = _dot_nt(eye, v_rows).astype(BF16)
    tot = ea[(GLA_LEVELS + 2) * c:(GLA_LEVELS + 2) * c + 1]
    st_ref[...] = st * tot + _dot(v_t, k_dec)


def _gla_scan(geo, u, off, layer, w2, b2, mat3, msk4):
    c = GLA_CHUNK
    ncc = geo.c // c
    nc = geo.n // c
    dk, dv = GLA_HEADS * GLA_DK, GLA_HEADS * GLA_DV

    fwd = lambda i: i
    bwd = lambda i: jnp.where(i < ncc, ncc - 1 - i, nc - 1 - (i - ncc))

    u3 = u.reshape(geo.b, geo.n, u.shape[-1])
    row = lambda w, o, chunk: pl.BlockSpec((geo.b, c, w), lambda i: (0, chunk(i), o // w))
    rows = lambda chunk: [row(dk, off["gla_q"], chunk), row(dk, off["gla_k"], chunk), row(dv, off["gla_v"], chunk),
                          row(128, off["gate"], chunk)]
    out_sds = jax.ShapeDtypeStruct((geo.b, geo.n, dv), F32)
    o_f, o_b = pl.pallas_call(
        _gla_kernel,
        out_shape=(out_sds, out_sds),
        grid=(nc,),
        in_specs=rows(fwd) + rows(bwd) + [
            pl.BlockSpec((None, 2, 128, dk), lambda i: (layer, 0, 0, 0)),
            pl.BlockSpec((None, 2, 1, dk), lambda i: (layer, 0, 0, 0)),
            pl.BlockSpec(mat3.shape, lambda i: (0, 0, 0)),
            pl.BlockSpec(msk4.shape, lambda i: (0, 0, 0, 0))],
        out_specs=(pl.BlockSpec((geo.b, c, dv), lambda i: (0, fwd(i), 0)),
                   pl.BlockSpec((geo.b, c, dv), lambda i: (0, bwd(i), 0))),
        scratch_shapes=[pltpu.VMEM((2, geo.b, GLA_DV, dk), F32)],
        name="gla_scan",
        compiler_params=_cparams("arbitrary"),
    )(u3, u3, u3, u3, u3, u3, u3, u3, w2, b2, mat3, msk4)
    return o_f.reshape(geo.m, dv), o_b.reshape(geo.m, dv)


def _merge_mix_kernel(attn_ref, conf_ref, glaf_ref, glab_ref, r_ref, gn_ref, sc_ref, gate_ref, w0_ref, w1_ref,
                      w2_ref, w3_ref, wmix_ref, x_ref, gt_ref, gf_ref, shf_ref, scf_ref, wr_ref, br_ref, tri_ref,
                      o_ref, h_ref, idx_ref, wgt_ref, rank_ref, cnt_ref, y_ref, carry_ref):
    @pl.when(pl.program_id(0) == 0)
    def _():
        carry_ref[...] = jnp.zeros_like(carry_ref)

    wbr_ref = (w0_ref, w1_ref, w2_ref, w3_ref)
    o = glaf_ref[...] + glab_ref[...]
    r = r_ref[...].astype(F32)
    gn = gn_ref[...]
    normed = [_rms(o[:, h * GLA_DV:(h + 1) * GLA_DV], gn) for h in range(GLA_HEADS)]
    gla = (jnp.concatenate(normed, axis=1) * (r * jax.nn.sigmoid(r))).astype(BF16)
    acts = (attn_ref[...], conf_ref[...], gla, sc_ref[...])
    d = x_ref.shape[1]
    cw = min(MERGE_COLS, d)
    for c0 in range(0, d, cw):
        y = None
        for i in range(N_BRANCHES):
            half_b = _dot(acts[i], wbr_ref[i][:, c0:c0 + cw])
            t = jnp.tanh(gate_ref[:, i * d + c0:i * d + c0 + cw].astype(F32))
            term = t * half_b + half_b
            y = term if y is None else y + term
        y_ref[:, c0:c0 + cw] = y.astype(BF16)
    y_all = y_ref[...]
    for c0 in range(0, d, cw):
        cols = slice(c0, c0 + cw)
        o_ref[:, cols] = x_ref[:, cols] + gt_ref[:, cols] * _dot(y_all, wmix_ref[:, cols])
    _route_rows(o_ref[...], gf_ref, shf_ref, scf_ref, wr_ref, br_ref, tri_ref, h_ref, idx_ref, wgt_ref, rank_ref,
                cnt_ref, carry_ref)


def _merge_mix(geo, attn, conf, gla_f, gla_b, u, u_gates, off, g_norm, sconv, w_br_half, w_mix, x, mod, layer,
               g_ffn, wr_stack, b_router):
    tm, d, m = geo.tr, geo.d, geo.m
    w512 = GLA_HEADS * GLA_DV
    nch = d // 256
    act = pl.BlockSpec((tm, w512), lambda i: (i, 0))
    once = pl.Buffered(1)
    w_br = pl.BlockSpec((None, w512, d), lambda i: (layer, 0, 0), pipeline_mode=once)
    tri = jnp.asarray(np.triu(np.ones((tm, tm), np.float32), 1), BF16)
    pair = lambda dt: jax.ShapeDtypeStruct((2, m), dt)
    pair_spec = pl.BlockSpec((2, tm), lambda i: (0, i))
    return pl.pallas_call(
        _merge_mix_kernel,
        out_shape=(jax.ShapeDtypeStruct((m, d), F32), jax.ShapeDtypeStruct((m * nch, 128), jnp.uint32),
                   pair(jnp.int32), pair(F32), pair(jnp.int32), jax.ShapeDtypeStruct((N_EXPERTS, 1), F32)),
        grid=(m // tm,),
        in_specs=[act, act, act, act,
                  pl.BlockSpec((tm, w512), lambda i: (i, off["gla_r"] // w512)),
                  pl.BlockSpec((None, 1, GLA_DV), lambda i: (layer, 0, 0)),
                  act, pl.BlockSpec((tm, N_BRANCHES * d), lambda i: (i, 0)),
                  w_br, w_br, w_br, w_br,
                  pl.BlockSpec((None, d, d), lambda i: (layer, 0, 0), pipeline_mode=once),
                  pl.BlockSpec((tm, d), lambda i: (i, 0)), geo.mod_spec(layer, 2),
                  pl.BlockSpec((None, 1, d), lambda i: (layer, 0, 0)),
                  geo.mod_spec(layer, 3), geo.mod_spec(layer, 4),
                  pl.BlockSpec(wr_stack.shape, lambda i: (0, 0)),
                  pl.BlockSpec(b_router.shape, lambda i: (0, 0)),
                  pl.BlockSpec((tm, tm), lambda i: (0, 0))],
        out_specs=(pl.BlockSpec((tm, d), lambda i: (i, 0)), pl.BlockSpec((tm * nch, 128), lambda i: (i, 0)),
                   pair_spec, pair_spec, pair_spec, pl.BlockSpec((N_EXPERTS, 1), lambda i: (0, 0))),
        scratch_shapes=[pltpu.VMEM((tm, d), BF16), pltpu.VMEM((N_EXPERTS, 1), F32)],
        name="merge_mix",
        compiler_params=_cparams("arbitrary"),
    )(attn, conf, gla_f, gla_b, u, g_norm, sconv, u_gates, *w_br_half, w_mix, x, mod, g_ffn, mod, mod, wr_stack,
      b_router, tri)


def _cast_kernel(w_ref, o_ref, *, scale):
    o_ref[...] = (w_ref[...] * scale).astype(o_ref.dtype)


def _to_bf16(w, scale=1.0):
    n_l, rows, cols = w.shape
    tr = _pick(rows, (512, 256, 128))
    return pl.pallas_call(
        functools.partial(_cast_kernel, scale=scale),
        out_shape=jax.ShapeDtypeStruct(w.shape, BF16),
        grid=(n_l, rows // tr),
        in_specs=[pl.BlockSpec((None, tr, cols), lambda l, i: (l, i, 0))],
        out_specs=pl.BlockSpec((None, tr, cols), lambda l, i: (l, i, 0)),
        name="weight_cast",
        compiler_params=_cparams("parallel", "parallel"),
    )(w)


def _second_largest(a, b, c, d):
    return jnp.maximum(jnp.maximum(jnp.minimum(a, b), jnp.minimum(c, d)),
                       jnp.minimum(jnp.maximum(a, b), jnp.maximum(c, d)))


def _chunk_rows(rows, n, cc):
    return pl.ds(cc, rows, stride=n) if n > 1 else pl.ds(0, rows)


def _pack_rows(ref, y):
    rows = y.shape[0]
    n = ref.shape[0] // rows
    half = n * 128
    bits = lambda v: pltpu.bitcast(v.astype(BF16).astype(F32), jnp.uint32)
    packed = (bits(y[:, :half]) >> 16) | bits(y[:, half:])
    for cc in range(n):
        ref[_chunk_rows(rows, n, cc), :] = packed[:, cc * 128:(cc + 1) * 128]


def _unpack_cols(words):
    lo = pltpu.bitcast(words << 16, F32)
    hi = pltpu.bitcast(words & jnp.uint32(0xFFFF0000), F32)
    return lo, hi


def _route_rows(x, g_ref, sh_ref, sc_ref, wr_ref, br_ref, tri_ref, h_ref, idx_ref, wgt_ref, rank_ref, cnt_ref,
                carry_ref):
    y = _rms(x, g_ref[...]) * (1.0 + sc_ref[...]) + sh_ref[...]
    hi = y.astype(BF16)
    lo = (y - hi.astype(F32)).astype(BF16)
    _pack_rows(h_ref, y)
    ne = N_EXPERTS
    two = _dot_nt(wr_ref[...], hi)
    logits = two[0:ne] + two[ne:2 * ne] + _dot_nt(wr_ref[0:ne, :], lo)
    aff = 1.0 / (1.0 + jnp.exp(-logits))
    sel = aff + br_ref[...]
    per = ne // N_GROUPS
    gs = []
    for g in range(N_GROUPS):
        a, b, c, d = (sel[g * per + r:g * per + r + 1] for r in range(per))
        top1 = jnp.maximum(jnp.maximum(a, b), jnp.maximum(c, d))
        gs.append(top1 + _second_largest(a, b, c, d))
    best = gs[0]
    for g in range(1, N_GROUPS):
        best = jnp.maximum(best, gs[g])
    grp = jnp.full(best.shape, N_GROUPS, jnp.int32)
    for g in reversed(range(N_GROUPS)):
        grp = jnp.where(gs[g] == best, g, grp)
    erow = lax.broadcasted_iota(jnp.int32, sel.shape, 0)
    masked = jnp.where(erow // per == grp, sel, -jnp.inf)
    m1 = jnp.max(masked, axis=0, keepdims=True)
    i1 = jnp.min(jnp.where(masked == m1, erow, ne), axis=0, keepdims=True)
    one1 = erow == i1
    masked2 = jnp.where(one1, -jnp.inf, masked)
    m2 = jnp.max(masked2, axis=0, keepdims=True)
    i2 = jnp.min(jnp.where(masked2 == m2, erow, ne), axis=0, keepdims=True)
    one2 = erow == i2
    w1 = jnp.sum(jnp.where(one1, aff, 0.0), axis=0, keepdims=True)
    w2 = jnp.sum(jnp.where(one2, aff, 0.0), axis=0, keepdims=True)
    inv = 1.0 / (w1 + w2)
    idx_ref[0:1, :] = i1
    idx_ref[1:2, :] = i2
    wgt_ref[0:1, :] = w1 * inv
    wgt_ref[1:2, :] = w2 * inv
    o1 = jnp.where(one1, 1.0, 0.0)
    o2 = jnp.where(one2, 1.0, 0.0)
    tri = tri_ref[...]
    c0 = carry_ref[...]
    c1 = c0 + jnp.sum(o1, axis=1, keepdims=True)
    rank_ref[0:1, :] = jnp.sum(o1 * (c0 + _dot(o1.astype(BF16), tri)), axis=0, keepdims=True).astype(jnp.int32)
    rank_ref[1:2, :] = jnp.sum(o2 * (c1 + _dot(o2.astype(BF16), tri)), axis=0, keepdims=True).astype(jnp.int32)
    c2 = c1 + jnp.sum(o2, axis=1, keepdims=True)
    carry_ref[...] = c2
    cnt_ref[...] = c2


def _sorted_row(idx_ref, rank_ref, base_ref, a, n):
    return pl.multiple_of((base_ref[idx_ref[a]] + rank_ref[a]) * n, n)


def _dispatch_kernel(idx_ref, rank_ref, base_ref, last_ref, h_ref, xs_hbm, zero_ref, sem, zero_sem, *, m, n):
    i = pl.program_id(0)
    tr = h_ref.shape[0] // n

    @pl.when(i == 0)
    def _():
        zero_ref[...] = jnp.zeros_like(zero_ref)

        def fills(e):
            rows = zero_ref.shape[0]
            start = pl.multiple_of(last_ref[e] * n, rows)
            return pltpu.make_async_copy(zero_ref, xs_hbm.at[pl.ds(start, rows), :], zero_sem)

        for e in range(2 * N_EXPERTS):
            @pl.when(last_ref[e] >= 0)
            def _():
                fills(e).start()
        for e in range(2 * N_EXPERTS):
            @pl.when(last_ref[e] >= 0)
            def _():
                fills(e).wait()

    def issue(r, carry):
        src = h_ref.at[pl.ds(pl.multiple_of(r * n, n), n), :]
        for k in range(2):
            row = _sorted_row(idx_ref, rank_ref, base_ref, k * m + i * tr + r, n)
            pltpu.make_async_copy(src, xs_hbm.at[pl.ds(row, n), :], sem).start()
        return carry

    lax.fori_loop(0, tr, issue, 0, unroll=4)
    everything = xs_hbm.at[pl.ds(0, 2 * tr * n), :]
    pltpu.make_async_copy(everything, everything, sem).wait()


def _dispatch(hfp, ridx, rank, base, last_tile_row, n_rows, tr, tm):
    n = hfp.shape[0] * 2 // ridx.shape[0]
    m = hfp.shape[0] // n
    return pl.pallas_call(
        functools.partial(_dispatch_kernel, m=m, n=n),
        out_shape=jax.ShapeDtypeStruct((n_rows * n, 128), jnp.uint32),
        grid_spec=pltpu.PrefetchScalarGridSpec(
            num_scalar_prefetch=4, grid=(m // tr,),
            in_specs=[pl.BlockSpec((tr * n, 128), lambda i, *_: (i, 0))],
            out_specs=pl.BlockSpec(memory_space=pl.ANY),
            scratch_shapes=[pltpu.VMEM((tm * n, 128), jnp.uint32), pltpu.SemaphoreType.DMA(()),
                            pltpu.SemaphoreType.DMA(())]),
        name="moe_dispatch",
        compiler_params=_cparams("arbitrary"),
    )(ridx, rank, base, last_tile_row, hfp)


def _gmm_kernel(te_ref, nv_ref, xs_ref, wg_ref, wu_ref, wd_ref, ys_ref, wgb_ref, wub_ref, wdb_ref, prev_ref):
    j = pl.program_id(0)

    @pl.when(j == 0)
    def _():
        prev_ref[0] = -1

    @pl.when(j < nv_ref[0])
    def _():
        e = te_ref[j]

        @pl.when(e != prev_ref[0])
        def _():
            wgb_ref[...] = wg_ref[...].astype(BF16)
            wub_ref[...] = wu_ref[...].astype(BF16)
            wdb_ref[...] = wd_ref[...].astype(BF16)
            prev_ref[0] = e

        half = wgb_ref.shape[0] // 2
        n = half // 128
        tm = xs_ref.shape[0] // n
        parts = [_unpack_cols(xs_ref[_chunk_rows(tm, n, cc), :]) for cc in range(n)]
        xa = jnp.concatenate([p[0].astype(BF16) for p in parts], axis=1)
        xb = jnp.concatenate([p[1].astype(BF16) for p in parts], axis=1)
        g = _dot(xa, wgb_ref[0:half, :]) + _dot(xb, wgb_ref[half:2 * half, :])
        u = _dot(xa, wub_ref[0:half, :]) + _dot(xb, wub_ref[half:2 * half, :])
        act = (g * jax.nn.sigmoid(g) * u).astype(BF16)
        _pack_rows(ys_ref, _dot(act, wdb_ref[...]))

    @pl.when(j >= nv_ref[0])
    def _():
        ys_ref[...] = jnp.zeros_like(ys_ref)


def _gmm(xs, tile_expert, n_valid, wg, wu, wd, layer, tm):
    d, ff = wg.shape[-2], wg.shape[-1]
    nch = d // 256
    blk = tm * nch
    row = lambda j, te, nv: (jnp.minimum(j, nv[0] - 1), 0)
    wsel = lambda j, te, nv: (layer, te[jnp.minimum(j, nv[0] - 1)], 0, 0)
    return pl.pallas_call(
        _gmm_kernel,
        out_shape=jax.ShapeDtypeStruct(xs.shape, jnp.uint32),
        grid_spec=pltpu.PrefetchScalarGridSpec(
            num_scalar_prefetch=2, grid=(xs.shape[0] // blk,),
            in_specs=[pl.BlockSpec((blk, 128), row),
                      pl.BlockSpec((None, None, d, ff), wsel), pl.BlockSpec((None, None, d, ff), wsel),
                      pl.BlockSpec((None, None, ff, d), wsel)],
            out_specs=pl.BlockSpec((blk, 128), lambda j, te, nv: (j, 0)),
            scratch_shapes=[pltpu.VMEM((d, ff), BF16), pltpu.VMEM((d, ff), BF16), pltpu.VMEM((ff, d), BF16),
                            pltpu.SMEM((1,), jnp.int32)]),
        name="moe_experts",
        compiler_params=_cparams("arbitrary"),
    )(tile_expert, n_valid, xs, wg, wu, wd)


def _combine_kernel(idx_ref, rank_ref, base_ref, x_ref, w_ref, gt_ref, ys_hbm, *rest, m, last):
    if last:
        g_ref, h_ref, o_ref, buf_ref, sem_ref = rest
    else:
        g_ref, sh_ref, sc_ref, o_ref, h_ref, buf_ref, sem_ref = rest
    i = pl.program_id(0)
    n_tiles = pl.num_programs(0)
    tr = x_ref.shape[0]
    n = buf_ref.shape[2] // tr

    def issue(tile, slot):
        def body(r, carry):
            for k in range(2):
                row = _sorted_row(idx_ref, rank_ref, base_ref, k * m + tile * tr + r, n)
                pltpu.make_async_copy(ys_hbm.at[pl.ds(row, n), :],
                                      buf_ref.at[slot, k, pl.ds(pl.multiple_of(r * n, n), n), :],
                                      sem_ref.at[slot]).start()
            return carry
        lax.fori_loop(0, tr, body, 0, unroll=4)

    @pl.when(i == 0)
    def _():
        issue(0, 0)

    @pl.when(i + 1 < n_tiles)
    def _():
        issue(i + 1, (i + 1) % 2)

    slot = i % 2
    pltpu.make_async_copy(buf_ref.at[slot], buf_ref.at[slot], sem_ref.at[slot]).wait()
    w1, w2 = w_ref[:, 0:1], w_ref[:, 1:2]
    half = n * 128
    for cc in range(n):
        lo1, hi1 = _unpack_cols(buf_ref[slot, 0, _chunk_rows(tr, n, cc), :])
        lo2, hi2 = _unpack_cols(buf_ref[slot, 1, _chunk_rows(tr, n, cc), :])
        ca = slice(cc * 128, (cc + 1) * 128)
        cb = slice(half + cc * 128, half + (cc + 1) * 128)
        o_ref[:, ca] = x_ref[:, ca] + gt_ref[:, ca] * (w1 * lo1 + w2 * lo2)
        o_ref[:, cb] = x_ref[:, cb] + gt_ref[:, cb] * (w1 * hi1 + w2 * hi2)
    if last:
        h_ref[...] = _rms(o_ref[...], g_ref[...])
    else:
        y = _rms(o_ref[...], g_ref[...]) * (1.0 + sc_ref[...]) + sh_ref[...]
        h_ref[...] = y.astype(h_ref.dtype)


def _combine(geo, x, ys, ridx, rank, base, w12, mod, layer, g_next, g_final):
    tr, d = geo.tr, geo.d
    nch = d // 256
    last = g_next is None
    tile = pl.BlockSpec((tr, d), lambda i, *_: (i, 0))
    in_specs = [tile, pl.BlockSpec((tr, 2), lambda i, *_: (i, 0)), geo.mod_spec(layer, 5),
                pl.BlockSpec(memory_space=pl.ANY)]
    operands = [x, w12, mod, ys]
    scratch = [pltpu.VMEM((2, 2, tr * nch, 128), jnp.uint32), pltpu.SemaphoreType.DMA((2,))]
    if last:
        in_specs.append(pl.BlockSpec((1, d), lambda i, *_: (0, 0)))
        operands.append(g_final.reshape(1, d))
        out_shape = jax.ShapeDtypeStruct((geo.b, geo.s, d), F32)
        out_specs = pl.BlockSpec((None, tr, d), lambda i, *_: (i // geo.nt, jnp.maximum(i % geo.nt - geo.nct, 0), 0))
        scratch = [pltpu.VMEM((tr, d), F32)] + scratch
    else:
        in_specs += [pl.BlockSpec((None, 1, d), lambda i, *_: (layer + 1, 0, 0)),
                     geo.mod_spec(layer + 1, 0), geo.mod_spec(layer + 1, 1)]
        operands += [g_next, mod, mod]
        out_shape = [jax.ShapeDtypeStruct((geo.m, d), F32), jax.ShapeDtypeStruct((geo.m, d), BF16)]
        out_specs = [tile, tile]
    return pl.pallas_call(
        functools.partial(_combine_kernel, m=geo.m, last=last),
        out_shape=out_shape,
        grid_spec=pltpu.PrefetchScalarGridSpec(
            num_scalar_prefetch=3, grid=(geo.m // tr,),
            in_specs=in_specs, out_specs=out_specs, scratch_shapes=scratch),
        name="moe_combine",
        compiler_params=_cparams("arbitrary"),
    )(ridx, rank, base, *operands)


def _moe(geo, x, routed, mod, layer, wg, wu, wd, g_next, g_final):
    tm = 256
    hfp, ridx, rwgt, rank, cnt = routed
    n_tiles = (2 * geo.m) // tm + N_EXPERTS
    counts = cnt[:, 0].astype(jnp.int32)
    tiles_per = (counts + tm - 1) // tm
    tile_start = jnp.cumsum(tiles_per) - tiles_per
    base = tile_start * tm
    n_valid = jnp.sum(tiles_per).reshape(1)
    tile_expert = jnp.sum(jnp.arange(n_tiles)[:, None] >= tile_start[None, :], axis=1).astype(jnp.int32) - 1
    ridx, rank = ridx.reshape(-1), rank.reshape(-1)
    tail = n_valid + jnp.arange(N_EXPERTS)
    fill_rows = jnp.concatenate([jnp.where(tiles_per > 0, base + (tiles_per - 1) * tm, -1),
                                 jnp.where(tail < n_tiles, tail * tm, -1)])
    xs = _dispatch(hfp, ridx, rank, base, fill_rows, n_tiles * tm, geo.tr, tm)
    ys = _gmm(xs, tile_expert, n_valid, wg, wu, wd, layer, tm)
    return _combine(geo, x, ys, ridx, rank, base, rwgt.T, mod, layer, g_next, g_final)


def _proj_plans(d):
    src = _src_layout(d)
    off, total = _proj_layout()
    width = lambda name: src[name][1] - src[name][0]
    cf0, sc0 = src["cf"][0], src["sc"][0]
    assert src["gb"][0] == src["gf"][1]
    pieces = [(off["cq"], src["cq"][0], width("cq")), (off["ckv"], src["ckv"][0], width("ckv")),
              (off["kr"], src["kr"][0], width("kr")),
              (off["cf_a"], cf0, CF_WIDTH), (off["cf_b"], cf0 + CF_WIDTH, CF_WIDTH),
              (off["gla_q"], src["gla_q"][0], width("gla_q")), (off["gla_k"], src["gla_k"][0], width("gla_k")),
              (off["gla_v"], src["gla_v"][0], width("gla_v")), (off["gate"], src["gf"][0], 2 * GLA_GATE_RANK),
              (off["gla_r"], src["gla_r"][0], width("gla_r")),
              (off["sc_b"], sc0, SC_WIDTH), (off["sc_c"], sc0 + SC_WIDTH, SC_WIDTH),
              (off["sc_h"], sc0 + 2 * SC_WIDTH, SC_WIDTH)]
    tn = PROJ_TILE
    assert total % tn == 0 and all(dst // tn == (dst + rows - 1) // tn for dst, _, rows in pieces)
    rest = _ProjPlan(tn, total // tn, pieces=sorted(pieces))
    gates_w = N_BRANCHES * d
    gn = min(tn, gates_w)
    gates = _ProjPlan(gn, gates_w // gn, first_row=src["merge"][0])
    return gates, rest, off


def _rope_swap_perm():
    q = MLA_ROPE // 4
    return np.concatenate([np.arange(q, 2 * q), np.arange(0, q), np.arange(3 * q, 4 * q), np.arange(2 * q, 3 * q)])


def _prep_mla(w_uq, w_ukv):
    n_l = w_uq.shape[0]
    perm = _rope_swap_perm()
    wq = w_uq.reshape(n_l, MLA_Q_RANK, MLA_HEADS, MLA_NOPE + MLA_ROPE)
    qn, qr = wq[..., :MLA_NOPE], wq[..., MLA_NOPE:]
    zeros = lambda *s: jnp.zeros((n_l,) + s, w_uq.dtype)
    wqa = jnp.concatenate([qn, qr, qr], axis=-1)
    wqb = jnp.concatenate([zeros(MLA_Q_RANK, MLA_HEADS, MLA_NOPE), qr[..., perm],
                           zeros(MLA_Q_RANK, MLA_HEADS, MLA_ROPE)], axis=-1)
    wkv = w_ukv.reshape(n_l, MLA_KV_RANK, MLA_HEADS, MLA_NOPE + MLA_V)
    wka = jnp.concatenate([wkv[..., :MLA_NOPE], zeros(MLA_KV_RANK, MLA_HEADS, MLA_SLOT - MLA_NOPE)], axis=-1)
    wv = wkv[..., MLA_NOPE:]
    flat = lambda w: w.reshape(n_l, w.shape[1], -1).astype(BF16)
    sela = np.zeros((128, MLA_HEADS, MLA_SLOT), np.float32)
    selb = np.zeros((128, MLA_HEADS, MLA_SLOT), np.float32)
    for r in range(MLA_ROPE):
        sela[r, :, MLA_NOPE + r] = 1.0
        sela[r, :, MLA_NOPE + MLA_ROPE + r] = 1.0
        selb[perm[r], :, MLA_NOPE + r] = 1.0
    sel = lambda s: jnp.asarray(s.reshape(128, -1), BF16)
    return flat(wqa), flat(wqb), flat(wka), sel(sela), sel(selb), flat(wv)


def _rope_tables(geo):
    rows = geo.s // GRID_W
    pos = jnp.arange(rows * GRID_W)
    row = (pos // GRID_W).astype(F32)
    col = (pos % GRID_W).astype(F32)
    n_freq = MLA_ROPE // 4
    freqs = ROPE_THETA ** (-jnp.arange(n_freq, dtype=F32) / n_freq)
    cr, sr = jnp.cos(row[:, None] * freqs), jnp.sin(row[:, None] * freqs)
    cc, sn = jnp.cos(col[:, None] * freqs), jnp.sin(col[:, None] * freqs)
    cos32 = jnp.concatenate([cr, cr, cc, cc], axis=-1)
    sin32 = jnp.concatenate([-sr, sr, -sn, sn], axis=-1)
    s, c = geo.s, geo.c
    one = lambda n, w: jnp.ones((n, w), F32)
    zero = lambda n, w: jnp.zeros((n, w), F32)
    scale = (MLA_NOPE + MLA_ROPE) ** -0.5
    qca = jnp.concatenate([jnp.concatenate([one(c, 64), zero(c, 32), one(c, 32)], 1),
                           jnp.concatenate([one(s, 64), cos32, one(s, 32)], 1)], 0) * scale
    qsb = jnp.concatenate([zero(c, 128), jnp.concatenate([zero(s, 64), sin32, zero(s, 32)], 1)], 0) * scale
    kca = jnp.concatenate([jnp.concatenate([one(c, 64), zero(c, 32), one(c, 32)], 1),
                           jnp.concatenate([one(s, 64), cos32, zero(s, 32)], 1)], 0)
    ksb = jnp.concatenate([zero(c, 128), jnp.concatenate([zero(s, 64), sin32, zero(s, 32)], 1)], 0)
    return qca, qsb, kca, ksb


def kernel(x, c, ctx, c_ctx, w_mod, b_mod, g_mix, g_ffn, w_in, g_q, w_uq, g_kv, w_ukv, w_mla_o, w_cf_dw, b_cf_dw, g_cf_ln, b_cf_ln, w_cf_o, w_gla_gf, b_gla_gf, w_gla_gb, b_gla_gb, g_gla_norm, w_gla_o, w_sc_conv, w_sc_o, w_mix_o, w_router, b_router, w_e_gate, w_e_up, w_e_down, g_final):
    bsz, seq, d = x.shape
    n_ctx = ctx.shape[1]
    depth = w_in.shape[0]
    geo = _Geom(bsz, n_ctx, seq, d)
    assert bsz + 1 <= 8 and seq % GRID_W == 0 and n_ctx % GLA_CHUNK == 0 and seq % GLA_CHUNK == 0

    plan_gates, plan_rest, off = _proj_plans(d)
    w_in_t = jnp.swapaxes(w_in, 1, 2)
    wqa, wqb, wka, sela, selb, wv = _prep_mla(w_uq, w_ukv)
    tabs = _rope_tables(geo)
    w_br_half = [_to_bf16(w, 0.5) for w in (w_mla_o, w_cf_o, w_gla_o, w_sc_o)]
    w_mix = _to_bf16(w_mix_o)
    gate_w = jnp.zeros((depth, 2, 128, GLA_HEADS * GLA_DK), F32)
    gate_w = gate_w.at[:, 0, :GLA_GATE_RANK].set(w_gla_gf).at[:, 1, GLA_GATE_RANK:2 * GLA_GATE_RANK].set(w_gla_gb)
    gate_w = gate_w.astype(BF16)
    gate_b = jnp.stack([b_gla_gf, b_gla_gb], axis=1)[:, :, None, :]
    mat3, msk4 = _gla_tables()
    wr_t = w_router.T
    wr_hi = wr_t.astype(BF16)
    wr_stack = jnp.concatenate([wr_hi, (wr_t - wr_hi.astype(F32)).astype(BF16)], axis=0)
    vec = lambda a: a.reshape(a.shape[0], 1, a.shape[-1])

    cc = jnp.zeros((8, d), F32).at[:bsz].set(c).at[bsz].set(c_ctx)
    mod = _modulation(cc, w_mod, b_mod).reshape(depth, 8, 1, 6 * d)

    xa, h = _first_norm(geo, x, ctx, vec(g_mix), mod)
    for l in range(depth):
        u = _project(h, w_in_t, l, plan_rest, 1.0)
        u_gates = _project(h, w_in_t, l, plan_gates, 0.5)
        q, k, v = _mla_qkv(geo, u, off, l, vec(g_q), vec(g_kv), wqa, wqb, wka, sela, selb, wv, tabs)
        attn = _mla_attention(geo, q, k, v)
        conf = _conformer(geo, u, off, l, w_cf_dw, vec(b_cf_dw), vec(g_cf_ln), vec(b_cf_ln))
        sconv = _shortconv(geo, u, off, l, w_sc_conv)
        gla_f, gla_b = _gla_scan(geo, u, off, l, gate_w, gate_b, mat3, msk4)
        xa, *routed = _merge_mix(geo, attn, conf, gla_f, gla_b, u, u_gates, off, vec(g_gla_norm), sconv, w_br_half,
                                 w_mix, xa, mod, l, vec(g_ffn), wr_stack, b_router.reshape(N_EXPERTS, 1))
        if l + 1 < depth:
            xa, h = _moe(geo, xa, routed, mod, l, w_e_gate, w_e_up, w_e_down, vec(g_mix), None)
        else:
            return _moe(geo, xa, routed, mod, l, w_e_gate, w_e_up, w_e_down, None, g_final)
```

```python
import functools

import numpy as np
import jax
import jax.numpy as jnp
from jax import lax
from jax.experimental import pallas as pl
from jax.experimental.pallas import tpu as pltpu

F32 = jnp.float32
BF16 = jnp.bfloat16
EPS = 1e-6

GRID_W = 64
ROPE_THETA = 10000.0
MLA_HEADS = 8
MLA_NOPE = 64
MLA_ROPE = 32
MLA_V = 64
MLA_Q_RANK = 512
MLA_KV_RANK = 256
MLA_SLOT = 128
CF_WIDTH = 512
CF_KERNEL = 31
GLA_HEADS = 4
GLA_DK = 64
GLA_DV = 128
GLA_GATE_RANK = 16
GLA_GATE_TAU = 16.0
GLA_CHUNK = 64
GLA_LEVELS = 6
SC_WIDTH = 512
SC_KERNEL = 3
N_BRANCHES = 4
N_EXPERTS = 16
N_GROUPS = 4
EXPERT_FF = 512
CONV_HALO = 16
SUBLANES = 8
MERGE_COLS = 512
ATTN_ROWS = 768
PROJ_TILE = 1024
PROJ_CAST_ROWS = 256
VMEM_LIMIT = 56 * 1024 * 1024


def _cparams(*sem):
    return pltpu.CompilerParams(dimension_semantics=sem, vmem_limit_bytes=VMEM_LIMIT)


def _pick(n, cands):
    for c in cands:
        if n % c == 0:
            return c
    raise ValueError(f"no tile for {n}")


def _dot(a, b):
    return jnp.dot(a, b, preferred_element_type=F32)


def _dot_nt(a, b):
    return lax.dot_general(a, b, (((1,), (1,)), ((), ())), preferred_element_type=F32)


def _proj_layout():
    parts = [("sc_b", SC_WIDTH), ("sc_c", SC_WIDTH), ("sc_h", SC_WIDTH),
             ("cf_a", CF_WIDTH), ("cf_b", CF_WIDTH), ("cq", MLA_Q_RANK), ("gla_v", GLA_HEADS * GLA_DV),
             ("gla_r", GLA_HEADS * GLA_DV), ("ckv", MLA_KV_RANK), ("gla_q", GLA_HEADS * GLA_DK),
             ("gla_k", GLA_HEADS * GLA_DK), ("kr", 128), ("gate", 128)]
    off, start = {}, 0
    for name, width in parts:
        assert start % width == 0
        off[name] = start
        start += width
    return off, start


def _src_layout(d):
    parts = (("cq", MLA_Q_RANK), ("ckv", MLA_KV_RANK), ("kr", MLA_ROPE), ("cf", 2 * CF_WIDTH),
             ("gla_q", GLA_HEADS * GLA_DK), ("gla_k", GLA_HEADS * GLA_DK), ("gla_v", GLA_HEADS * GLA_DV),
             ("gf", GLA_GATE_RANK), ("gb", GLA_GATE_RANK), ("gla_r", GLA_HEADS * GLA_DV),
             ("sc", 3 * SC_WIDTH), ("merge", N_BRANCHES * d))
    off, start = {}, 0
    for name, width in parts:
        off[name] = (start, start + width)
        start += width
    return off


def _mod_kernel(c_ref, w_ref, b_ref, o_ref):
    c = c_ref[...]
    a = (c * jax.nn.sigmoid(c)).astype(BF16)
    o_ref[...] = _dot(a, w_ref[...].astype(BF16)) + b_ref[...]


def _modulation(cc, w_mod, b_mod):
    n_l, d, n6 = w_mod.shape
    tn = _pick(n6, (1024, 512, 256, 128))
    return pl.pallas_call(
        _mod_kernel,
        out_shape=jax.ShapeDtypeStruct((n_l, 8, n6), F32),
        grid=(n_l, n6 // tn),
        in_specs=[pl.BlockSpec((8, d), lambda l, j: (0, 0)),
                  pl.BlockSpec((None, d, tn), lambda l, j: (l, 0, j)),
                  pl.BlockSpec((None, 1, tn), lambda l, j: (l, 0, j))],
        out_specs=pl.BlockSpec((None, 8, tn), lambda l, j: (l, 0, j)),
        name="adaln_mod",
        compiler_params=_cparams("parallel", "parallel"),
    )(cc, w_mod, b_mod.reshape(n_l, 1, n6))


class _Geom:
    def __init__(self, bsz, n_ctx, seq, d):
        self.b, self.c, self.s, self.d = bsz, n_ctx, seq, d
        self.n = n_ctx + seq
        self.m = bsz * self.n
        self.tr = _pick(int(np.gcd(n_ctx, seq)), (256, 128, 64))
        self.nt = self.n // self.tr
        self.nct = n_ctx // self.tr

    def mod_row(self, i):
        b, j = i // self.nt, i % self.nt
        return jnp.where(j < self.nct, self.b, b)

    def mod_spec(self, layer, part):
        return pl.BlockSpec((None, None, 1, self.d), lambda i, *_: (layer, self.mod_row(i), 0, part))


def _first_norm_kernel(x_ref, c_ref, g_ref, sh_ref, sc_ref, xa_ref, h_ref, *, geo):
    is_ctx = pl.program_id(0) % geo.nt < geo.nct
    rows = jnp.where(is_ctx, c_ref[...], x_ref[...])
    xa_ref[...] = rows
    y = rows * lax.rsqrt(jnp.mean(rows * rows, axis=-1, keepdims=True) + EPS) * g_ref[...]
    h_ref[...] = (y * (1.0 + sc_ref[...]) + sh_ref[...]).astype(h_ref.dtype)


def _first_norm(geo, x, ctx, g, mod):
    tr, d = geo.tr, geo.d
    tile = pl.BlockSpec((tr, d), lambda i: (i, 0))
    return pl.pallas_call(
        functools.partial(_first_norm_kernel, geo=geo),
        out_shape=(jax.ShapeDtypeStruct((geo.m, d), F32), jax.ShapeDtypeStruct((geo.m, d), BF16)),
        grid=(geo.m // tr,),
        in_specs=[pl.BlockSpec((None, tr, d), lambda i: (i // geo.nt, jnp.maximum(i % geo.nt - geo.nct, 0), 0)),
                  pl.BlockSpec((None, tr, d), lambda i: (i // geo.nt, jnp.minimum(i % geo.nt, geo.nct - 1), 0)),
                  pl.BlockSpec((None, 1, d), lambda i: (0, 0, 0)),
                  geo.mod_spec(0, 0), geo.mod_spec(0, 1)],
        out_specs=(tile, tile),
        name="first_norm",
        compiler_params=_cparams("parallel"),
    )(x, ctx, g, mod, mod)


def _proj_kernel(a_ref, w_hbm, o_ref, wbuf_ref, wb_ref, sem_ref, *, layer, plan, scale):
    j, i = pl.program_id(0), pl.program_id(1)
    n_tiles = pl.num_programs(0)
    tn = wb_ref.shape[0]

    def copies(t, slot):
        return [pltpu.make_async_copy(w_hbm.at[layer, pl.ds(src, rows), :], wbuf_ref.at[slot, pl.ds(dst, rows), :],
                                      sem_ref.at[slot]) for dst, src, rows in plan(t)]

    def on_tile(t, fn):
        if not plan.static:
            fn(t)
            return
        for tt in range(plan.n_tiles):
            pl.when(t == tt)(functools.partial(fn, tt))

    def start(t):
        for cp in copies(t, t % 2):
            cp.start()

    def land(t):
        slot = t % 2
        for cp in copies(t, slot):
            cp.wait()
        covered = 0
        for dst, _, rows in plan(t):
            if dst > covered:
                wb_ref[covered:dst, :] = jnp.zeros((dst - covered, wb_ref.shape[1]), BF16)
            for r0 in range(0, rows, PROJ_CAST_ROWS):
                rr = min(PROJ_CAST_ROWS, rows - r0)
                rows_f32 = wbuf_ref[slot, dst + r0:dst + r0 + rr, :]
                wb_ref[dst + r0:dst + r0 + rr, :] = (rows_f32 * scale).astype(BF16)
            covered = dst + rows
        if covered < tn:
            wb_ref[covered:tn, :] = jnp.zeros((tn - covered, wb_ref.shape[1]), BF16)

    @pl.when(i == 0)
    def _():
        @pl.when(j == 0)
        def _():
            on_tile(j, start)

        on_tile(j, land)

        @pl.when(j + 1 < n_tiles)
        def _():
            on_tile(j + 1, start)

    o_ref[...] = _dot_nt(a_ref[...], wb_ref[...]).astype(o_ref.dtype)


class _ProjPlan:
    def __init__(self, tn, n_tiles, pieces=None, first_row=None):
        self.tn, self.n_tiles, self.static = tn, n_tiles, pieces is not None
        self._pieces, self._first_row = pieces, first_row

    def __call__(self, t):
        if self.static:
            lo = t * self.tn
            return [(dst - lo, src, rows) for dst, src, rows in self._pieces if lo <= dst < lo + self.tn]
        return [(0, pl.multiple_of(self._first_row + t * self.tn, SUBLANES), self.tn)]


def _project(a, w_t, layer, plan, scale):
    m, k = a.shape
    tn = plan.tn
    tm = _pick(m, (1152, 1024, 512, 256, 128))
    return pl.pallas_call(
        functools.partial(_proj_kernel, layer=layer, plan=plan, scale=scale),
        out_shape=jax.ShapeDtypeStruct((m, tn * plan.n_tiles), BF16),
        grid=(plan.n_tiles, m // tm),
        in_specs=[pl.BlockSpec((tm, k), lambda j, i: (i, 0)), pl.BlockSpec(memory_space=pl.ANY)],
        out_specs=pl.BlockSpec((tm, tn), lambda j, i: (i, j)),
        scratch_shapes=[pltpu.VMEM((2, tn, k), F32), pltpu.VMEM((tn, k), BF16), pltpu.SemaphoreType.DMA((2,))],
        name="proj_in",
        compiler_params=_cparams("arbitrary", "arbitrary"),
    )(a, w_t)


def _rms(x, g):
    return x * lax.rsqrt(jnp.mean(x * x, axis=-1, keepdims=True) + EPS) * g


def _qkv_kernel(cq_ref, ckv_ref, kr_ref, gq_ref, gkv_ref, wqa_ref, wqb_ref, wka_ref, sela_ref, selb_ref,
                wv_ref, qca_ref, qsb_ref, kca_ref, ksb_ref, q_ref, k_ref, v_ref):
    cqn = _rms(cq_ref[...].astype(F32), gq_ref[...]).astype(BF16)
    ckvn = _rms(ckv_ref[...].astype(F32), gkv_ref[...]).astype(BF16)
    kr = kr_ref[...]
    qa = _dot(cqn, wqa_ref[...])
    qb = _dot(cqn, wqb_ref[...])
    ka = _dot(ckvn, wka_ref[...]) + _dot(kr, sela_ref[...])
    kb = _dot(kr, selb_ref[...])
    qca, qsb, kca, ksb = qca_ref[...], qsb_ref[...], kca_ref[...], ksb_ref[...]
    for h in range(MLA_HEADS):
        sl = slice(h * MLA_SLOT, (h + 1) * MLA_SLOT)
        q_ref[:, sl] = (qa[:, sl] * qca + qb[:, sl] * qsb).astype(BF16)
        k_ref[:, sl] = (ka[:, sl] * kca + kb[:, sl] * ksb).astype(BF16)
    v_ref[...] = _dot(ckvn, wv_ref[...]).astype(BF16)


def _mla_qkv(geo, u, off, layer, gq, gkv, wqa, wqb, wka, sela, selb, wv, tabs):
    tr, m = geo.tr, geo.m
    hq = MLA_HEADS * MLA_SLOT
    hv = MLA_HEADS * MLA_V
    row = lambda w, o: pl.BlockSpec((tr, w), lambda i: (i, o // w))
    lw = lambda a: pl.BlockSpec((None,) + a.shape[1:], lambda i: (layer,) + (0,) * (a.ndim - 1))
    cw = lambda a: pl.BlockSpec(a.shape, lambda i: (0,) * a.ndim)
    tab = pl.BlockSpec((tr, MLA_SLOT), lambda i: (i % geo.nt, 0))
    return pl.pallas_call(
        _qkv_kernel,
        out_shape=(jax.ShapeDtypeStruct((m, hq), BF16), jax.ShapeDtypeStruct((m, hq), BF16),
                   jax.ShapeDtypeStruct((m, hv), BF16)),
        grid=(m // tr,),
        in_specs=[row(MLA_Q_RANK, off["cq"]), row(MLA_KV_RANK, off["ckv"]), row(128, off["kr"]),
                  lw(gq), lw(gkv), lw(wqa), lw(wqb), lw(wka), cw(sela), cw(selb), lw(wv), tab, tab, tab, tab],
        out_specs=(pl.BlockSpec((tr, hq), lambda i: (i, 0)), pl.BlockSpec((tr, hq), lambda i: (i, 0)),
                   pl.BlockSpec((tr, hv), lambda i: (i, 0))),
        name="mla_qkv",
        compiler_params=_cparams("parallel"),
    )(u, u, u, gq, gkv, wqa, wqb, wka, sela, selb, wv, *tabs)


def _attn_kernel(q_ref, k_ref, v_ref, o_ref, *, n_ctx, n_all):
    tq = q_ref.shape[0]

    def run(r0, rows, nk):
        lane = lax.broadcasted_iota(jnp.int32, (rows, 2 * MLA_V), 1)
        outs = []
        for hh in range(2):
            sl = slice(hh * MLA_SLOT, (hh + 1) * MLA_SLOT)
            s = _dot_nt(q_ref[r0:r0 + rows, sl], k_ref[0:nk, sl])
            p = jnp.exp(s - jnp.max(s, axis=-1, keepdims=True))
            inv = 1.0 / jnp.sum(p, axis=-1, keepdims=True)
            outs.append(_dot(p.astype(BF16), v_ref[0:nk, :]) * inv)
        o_ref[r0:r0 + rows, :] = jnp.where(lane < MLA_V, outs[0], outs[1]).astype(o_ref.dtype)

    @pl.when(pl.program_id(2) == 0)
    def _():
        run(0, n_ctx, n_ctx)
        if tq > n_ctx:
            run(n_ctx, tq - n_ctx, n_all)

    @pl.when(pl.program_id(2) > 0)
    def _():
        run(0, tq, n_all)


def _mla_attention(geo, q, k, v):
    n, bsz = geo.n, geo.b
    groups = [g for g in range(1, geo.nt + 1) if geo.nt % g == 0 and g * geo.tr >= geo.c]
    tq = geo.tr * min(groups, key=lambda g: abs(g * geo.tr - ATTN_ROWS))
    q3, k3, v3 = (t.reshape(bsz, n, t.shape[-1]) for t in (q, k, v))
    kern = functools.partial(_attn_kernel, n_ctx=geo.c, n_all=n)
    out = pl.pallas_call(
        kern,
        out_shape=jax.ShapeDtypeStruct((bsz, n, MLA_HEADS * MLA_V), BF16),
        grid=(bsz, MLA_HEADS // 2, n // tq),
        in_specs=[pl.BlockSpec((None, tq, 2 * MLA_SLOT), lambda b, hp, i: (b, i, hp)),
                  pl.BlockSpec((None, n, 2 * MLA_SLOT), lambda b, hp, i: (b, 0, hp)),
                  pl.BlockSpec((None, n, 2 * MLA_V), lambda b, hp, i: (b, 0, hp))],
        out_specs=pl.BlockSpec((None, tq, 2 * MLA_V), lambda b, hp, i: (b, i, hp)),
        name="mla_attn",
        compiler_params=_cparams("parallel", "parallel", "parallel"),
    )(q3, k3, v3)
    return out.reshape(geo.m, MLA_HEADS * MLA_V)


def _halo_flags(geo, i):
    j = i % geo.nt
    left = jnp.logical_and(j != 0, j != geo.nct)
    right = jnp.logical_and(j != geo.nct - 1, j != geo.nt - 1)
    return left.astype(F32), right.astype(F32)


def _conv_shifts(n_taps):
    base = CONV_HALO - n_taps // 2
    return sorted({(base + kk) % SUBLANES for kk in range(n_taps)} - {0})


def _conv_taps(pad_ref, sh_ref, w_ref, n_taps, rows):
    base = CONV_HALO - n_taps // 2
    shifts = _conv_shifts(n_taps)
    span = sh_ref.shape[1]
    for s, r in enumerate(shifts):
        sh_ref[s, :, :] = pad_ref[r:r + span, :]

    def window(o):
        a, r = o - o % SUBLANES, o % SUBLANES
        return pad_ref[a:a + rows, :] if r == 0 else sh_ref[shifts.index(r), a:a + rows, :]

    acc = w_ref[0:1, :] * window(base)
    for kk in range(1, n_taps):
        acc = acc + w_ref[kk:kk + 1, :] * window(base + kk)
    return acc


def _conv_scratch(tr, width, n_taps):
    return [pltpu.VMEM((tr + 2 * CONV_HALO, width), F32),
            pltpu.VMEM((len(_conv_shifts(n_taps)), tr + 2 * CONV_HALO - SUBLANES, width), F32)]


def _conf_kernel(a_ref, b_ref, ap_ref, bp_ref, an_ref, bn_ref, w_ref, bias_ref, g_ref, beta_ref, o_ref,
                 pad_ref, sh_ref, *, geo):
    left, right = _halo_flags(geo, pl.program_id(0))
    rows = a_ref.shape[0]
    glu = lambda a, b: a[...].astype(F32) * jax.nn.sigmoid(b[...].astype(F32))
    pad_ref[0:CONV_HALO, :] = glu(ap_ref, bp_ref) * left
    pad_ref[CONV_HALO:CONV_HALO + rows, :] = glu(a_ref, b_ref)
    pad_ref[CONV_HALO + rows:2 * CONV_HALO + rows, :] = glu(an_ref, bn_ref) * right
    h = _conv_taps(pad_ref, sh_ref, w_ref, CF_KERNEL, rows) + bias_ref[...]
    hc = h - jnp.mean(h, axis=-1, keepdims=True)
    y = hc * lax.rsqrt(jnp.mean(hc * hc, axis=-1, keepdims=True) + EPS) * g_ref[...] + beta_ref[...]
    o_ref[...] = (y * jax.nn.sigmoid(y)).astype(o_ref.dtype)


def _sconv_kernel(gb_ref, gc_ref, h_ref, gcp_ref, hp_ref, gcn_ref, hn_ref, w_ref, o_ref, pad_ref, sh_ref, *, geo):
    left, right = _halo_flags(geo, pl.program_id(0))
    rows = h_ref.shape[0]
    prod = lambda a, b: a[...].astype(F32) * b[...].astype(F32)
    pad_ref[0:CONV_HALO, :] = prod(gcp_ref, hp_ref) * left
    pad_ref[CONV_HALO:CONV_HALO + rows, :] = prod(gc_ref, h_ref)
    pad_ref[CONV_HALO + rows:2 * CONV_HALO + rows, :] = prod(gcn_ref, hn_ref) * right
    y = _conv_taps(pad_ref, sh_ref, w_ref, SC_KERNEL, rows)
    o_ref[...] = (gb_ref[...].astype(F32) * y).astype(o_ref.dtype)


def _halo_specs(geo, width, offset):
    tr = geo.tr
    per = tr // CONV_HALO
    last = geo.m // CONV_HALO - 1
    cur = pl.BlockSpec((tr, width), lambda i: (i, offset // width))
    prev = pl.BlockSpec((CONV_HALO, width), lambda i: (jnp.maximum(i * per - 1, 0), offset // width))
    nxt = pl.BlockSpec((CONV_HALO, width), lambda i: (jnp.minimum((i + 1) * per, last), offset // width))
    return cur, prev, nxt


def _conformer(geo, u, off, layer, w_dw, b_dw, g_ln, b_ln):
    tr = geo.tr
    a_c, a_p, a_n = _halo_specs(geo, CF_WIDTH, off["cf_a"])
    b_c, b_p, b_n = _halo_specs(geo, CF_WIDTH, off["cf_b"])
    lw = lambda a: pl.BlockSpec((None,) + a.shape[1:], lambda i: (layer,) + (0,) * (a.ndim - 1))
    return pl.pallas_call(
        functools.partial(_conf_kernel, geo=geo),
        out_shape=jax.ShapeDtypeStruct((geo.m, CF_WIDTH), BF16),
        grid=(geo.m // tr,),
        in_specs=[a_c, b_c, a_p, b_p, a_n, b_n, lw(w_dw), lw(b_dw), lw(g_ln), lw(b_ln)],
        out_specs=pl.BlockSpec((tr, CF_WIDTH), lambda i: (i, 0)),
        scratch_shapes=_conv_scratch(tr, CF_WIDTH, CF_KERNEL),
        name="conformer",
        compiler_params=_cparams("parallel"),
    )(u, u, u, u, u, u, w_dw, b_dw, g_ln, b_ln)


def _shortconv(geo, u, off, layer, w_conv):
    tr = geo.tr
    gb_c, _, _ = _halo_specs(geo, SC_WIDTH, off["sc_b"])
    gc_c, gc_p, gc_n = _halo_specs(geo, SC_WIDTH, off["sc_c"])
    h_c, h_p, h_n = _halo_specs(geo, SC_WIDTH, off["sc_h"])
    lw = lambda a: pl.BlockSpec((None,) + a.shape[1:], lambda i: (layer,) + (0,) * (a.ndim - 1))
    return pl.pallas_call(
        functools.partial(_sconv_kernel, geo=geo),
        out_shape=jax.ShapeDtypeStruct((geo.m, SC_WIDTH), BF16),
        grid=(geo.m // tr,),
        in_specs=[gb_c, gc_c, h_c, gc_p, h_p, gc_n, h_n, lw(w_conv)],
        out_specs=pl.BlockSpec((tr, SC_WIDTH), lambda i: (i, 0)),
        scratch_shapes=_conv_scratch(tr, SC_WIDTH, SC_KERNEL),
        name="shortconv",
        compiler_params=_cparams("parallel"),
    )(u, u, u, u, u, u, u, w_conv)


def _gla_tables():
    c = GLA_CHUNK
    n_rows = (GLA_LEVELS + 2) * c + 8
    mat = np.zeros((2, n_rows, c), np.float32)
    msk = np.zeros((2, GLA_LEVELS + 1, c, c), np.float32)
    for lvl in range(GLA_LEVELS):
        hs = (c // 2) >> lvl
        for t in range(c):
            mid = (t // (2 * hs)) * 2 * hs + hs
            if t >= mid:
                mat[0, lvl * c + t, mid + 1:t + 1] = 1.0
                msk[0, lvl, t, mid - hs:mid] = 1.0
            else:
                mat[0, lvl * c + t, t + 1:mid + 1] = 1.0
    msk[0, GLA_LEVELS] = np.eye(c, dtype=np.float32)
    for t in range(c):
        mat[0, GLA_LEVELS * c + t, :t + 1] = 1.0
        mat[0, (GLA_LEVELS + 1) * c + t, t + 1:] = 1.0
    mat[0, (GLA_LEVELS + 2) * c:, :] = 1.0
    n_blk = GLA_LEVELS + 2
    mat[1, :n_blk * c] = mat[0, :n_blk * c].reshape(n_blk, c, c)[:, ::-1, ::-1].reshape(n_blk * c, c)
    mat[1, n_blk * c:] = 1.0
    msk[1] = msk[0][:, ::-1, ::-1]
    mat3 = np.concatenate([mat, mat, mat], axis=2)
    msk4 = np.tile(msk, (1, 1, GLA_HEADS, 1))
    return jnp.asarray(mat3, BF16), jnp.asarray(msk4, F32)


def _gla_kernel(qf_ref, kf_ref, vf_ref, gf_ref, qb_ref, kb_ref, vb_ref, gb_ref, w2_ref, b2_ref, mat_ref, msk_ref,
                of_ref, ob_ref, st_ref):
    @pl.when(pl.program_id(0) == 0)
    def _():
        st_ref[...] = jnp.zeros_like(st_ref)

    ins = ((qf_ref, kf_ref, vf_ref, gf_ref, of_ref), (qb_ref, kb_ref, vb_ref, gb_ref, ob_ref))
    for b in range(qf_ref.shape[0]):
        for d, (q_ref, k_ref, v_ref, g_ref, o_ref) in enumerate(ins):
            _gla_chunk(q_ref.at[b], k_ref.at[b], v_ref.at[b], g_ref.at[b], w2_ref.at[d], b2_ref.at[d],
                       mat_ref.at[d], msk_ref.at[d], o_ref.at[b], st_ref.at[d, b])


def _gla_chunk(q_ref, k_ref, v_ref, g_ref, w2_ref, b2_ref, mat_ref, msk_ref, o_ref, st_ref):
    c = GLA_CHUNK
    z = _dot(g_ref[...], w2_ref[...]) + b2_ref[...]
    la = (jnp.minimum(z, 0.0) - jnp.log1p(jnp.exp(-jnp.abs(z)))) * (1.0 / GLA_GATE_TAU)
    hi = la.astype(BF16)
    r1 = la - hi.astype(F32)
    lo = r1.astype(BF16)
    lo2 = (r1 - lo.astype(F32)).astype(BF16)
    ea = jnp.exp(_dot(mat_ref[...], jnp.concatenate([hi, lo, lo2], axis=0)))

    head = lax.broadcasted_iota(jnp.int32, (1, GLA_HEADS * GLA_DK), 1) // GLA_DK
    zero = jnp.zeros((), BF16)

    def stack(x):
        xb = x.astype(BF16)
        return jnp.concatenate([jnp.where(head == h, xb, zero) for h in range(GLA_HEADS)], axis=0)

    q = q_ref[...].astype(F32) * (GLA_DK ** -0.5)
    k = k_ref[...].astype(F32)
    vb = v_ref[...]

    attn = _dot_nt(stack(q), k.astype(BF16)) * msk_ref[GLA_LEVELS]
    for lvl in range(GLA_LEVELS):
        e = ea[lvl * c:(lvl + 1) * c]
        attn = attn + _dot_nt(stack(q * e), (k * e).astype(BF16)) * msk_ref[lvl]
    o_intra = _dot(attn.astype(BF16), vb)

    st = st_ref[...]
    o_inter = _dot_nt(stack(q * ea[GLA_LEVELS * c:(GLA_LEVELS + 1) * c]), st.astype(BF16))
    for h in range(GLA_HEADS):
        rows = slice(h * c, (h + 1) * c)
        cols = slice(h * GLA_DV, (h + 1) * GLA_DV)
        o_ref[:, cols] = o_intra[rows, cols] + o_inter[rows, :]

    k_dec = stack(k * ea[(GLA_LEVELS + 1) * c:(GLA_LEVELS + 2) * c])
    v_rows = jnp.concatenate([vb[:, h * GLA_DV:(h + 1) * GLA_DV] for h in range(GLA_HEADS)], axis=0)
    eye = (lax.broadcasted_iota(jnp.int32, (GLA_DV, GLA_DV), 0)
           == lax.broadcasted_iota(jnp.int32, (GLA_DV, GLA_DV), 1)).astype(BF16)
    v_t = _dot_nt(eye, v_rows).astype(BF16)
    tot = ea[(GLA_LEVELS + 2) * c:(GLA_LEVELS + 2) * c + 1]
    st_ref[...] = st * tot + _dot(v_t, k_dec)


def _gla_scan(geo, u, off, layer, w2, b2, mat3, msk4):
    c = GLA_CHUNK
    ncc = geo.c // c
    nc = geo.n // c
    dk, dv = GLA_HEADS * GLA_DK, GLA_HEADS * GLA_DV

    fwd = lambda i: i
    bwd = lambda i: jnp.where(i < ncc, ncc - 1 - i, nc - 1 - (i - ncc))

    u3 = u.reshape(geo.b, geo.n, u.shape[-1])
    row = lambda w, o, chunk: pl.BlockSpec((geo.b, c, w), lambda i: (0, chunk(i), o // w))
    rows = lambda chunk: [row(dk, off["gla_q"], chunk), row(dk, off["gla_k"], chunk), row(dv, off["gla_v"], chunk),
                          row(128, off["gate"], chunk)]
    out_sds = jax.ShapeDtypeStruct((geo.b, geo.n, dv), F32)
    o_f, o_b = pl.pallas_call(
        _gla_kernel,
        out_shape=(out_sds, out_sds),
        grid=(nc,),
        in_specs=rows(fwd) + rows(bwd) + [
            pl.BlockSpec((None, 2, 128, dk), lambda i: (layer, 0, 0, 0)),
            pl.BlockSpec((None, 2, 1, dk), lambda i: (layer, 0, 0, 0)),
            pl.BlockSpec(mat3.shape, lambda i: (0, 0, 0)),
            pl.BlockSpec(msk4.shape, lambda i: (0, 0, 0, 0))],
        out_specs=(pl.BlockSpec((geo.b, c, dv), lambda i: (0, fwd(i), 0)),
                   pl.BlockSpec((geo.b, c, dv), lambda i: (0, bwd(i), 0))),
        scratch_shapes=[pltpu.VMEM((2, geo.b, GLA_DV, dk), F32)],
        name="gla_scan",
        compiler_params=_cparams("arbitrary"),
    )(u3, u3, u3, u3, u3, u3, u3, u3, w2, b2, mat3, msk4)
    return o_f.reshape(geo.m, dv), o_b.reshape(geo.m, dv)


def _merge_mix_kernel(attn_ref, conf_ref, glaf_ref, glab_ref, r_ref, gn_ref, sc_ref, gate_ref, w0_ref, w1_ref,
                      w2_ref, w3_ref, wmix_ref, x_ref, gt_ref, gf_ref, shf_ref, scf_ref, wr_ref, br_ref, tri_ref,
                      o_ref, h_ref, idx_ref, wgt_ref, rank_ref, cnt_ref, y_ref, carry_ref):
    @pl.when(pl.program_id(0) == 0)
    def _():
        carry_ref[...] = jnp.zeros_like(carry_ref)

    wbr_ref = (w0_ref, w1_ref, w2_ref, w3_ref)
    o = glaf_ref[...] + glab_ref[...]
    r = r_ref[...].astype(F32)
    gn = gn_ref[...]
    normed = [_rms(o[:, h * GLA_DV:(h + 1) * GLA_DV], gn) for h in range(GLA_HEADS)]
    gla = (jnp.concatenate(normed, axis=1) * (r * jax.nn.sigmoid(r))).astype(BF16)
    acts = (attn_ref[...], conf_ref[...], gla, sc_ref[...])
    d = x_ref.shape[1]
    cw = min(MERGE_COLS, d)
    for c0 in range(0, d, cw):
        y = None
        for i in range(N_BRANCHES):
            half_b = _dot(acts[i], wbr_ref[i][:, c0:c0 + cw])
            t = jnp.tanh(gate_ref[:, i * d + c0:i * d + c0 + cw].astype(F32))
            term = t * half_b + half_b
            y = term if y is None else y + term
        y_ref[:, c0:c0 + cw] = y.astype(BF16)
    y_all = y_ref[...]
    for c0 in range(0, d, cw):
        cols = slice(c0, c0 + cw)
        o_ref[:, cols] = x_ref[:, cols] + gt_ref[:, cols] * _dot(y_all, wmix_ref[:, cols])
    _route_rows(o_ref[...], gf_ref, shf_ref, scf_ref, wr_ref, br_ref, tri_ref, h_ref, idx_ref, wgt_ref, rank_ref,
                cnt_ref, carry_ref)


def _merge_mix(geo, attn, conf, gla_f, gla_b, u, u_gates, off, g_norm, sconv, w_br_half, w_mix, x, mod, layer,
               g_ffn, wr_stack, b_router):
    tm, d, m = geo.tr, geo.d, geo.m
    w512 = GLA_HEADS * GLA_DV
    nch = d // 256
    act = pl.BlockSpec((tm, w512), lambda i: (i, 0))
    once = pl.Buffered(1)
    w_br = pl.BlockSpec((None, w512, d), lambda i: (layer, 0, 0), pipeline_mode=once)
    tri = jnp.asarray(np.triu(np.ones((tm, tm), np.float32), 1), BF16)
    pair = lambda dt: jax.ShapeDtypeStruct((2, m), dt)
    pair_spec = pl.BlockSpec((2, tm), lambda i: (0, i))
    return pl.pallas_call(
        _merge_mix_kernel,
        out_shape=(jax.ShapeDtypeStruct((m, d), F32), jax.ShapeDtypeStruct((m * nch, 128), jnp.uint32),
                   pair(jnp.int32), pair(F32), pair(jnp.int32), jax.ShapeDtypeStruct((N_EXPERTS, 1), F32)),
        grid=(m // tm,),
        in_specs=[act, act, act, act,
                  pl.BlockSpec((tm, w512), lambda i: (i, off["gla_r"] // w512)),
                  pl.BlockSpec((None, 1, GLA_DV), lambda i: (layer, 0, 0)),
                  act, pl.BlockSpec((tm, N_BRANCHES * d), lambda i: (i, 0)),
                  w_br, w_br, w_br, w_br,
                  pl.BlockSpec((None, d, d), lambda i: (layer, 0, 0), pipeline_mode=once),
                  pl.BlockSpec((tm, d), lambda i: (i, 0)), geo.mod_spec(layer, 2),
                  pl.BlockSpec((None, 1, d), lambda i: (layer, 0, 0)),
                  geo.mod_spec(layer, 3), geo.mod_spec(layer, 4),
                  pl.BlockSpec(wr_stack.shape, lambda i: (0, 0)),
                  pl.BlockSpec(b_router.shape, lambda i: (0, 0)),
                  pl.BlockSpec((tm, tm), lambda i: (0, 0))],
        out_specs=(pl.BlockSpec((tm, d), lambda i: (i, 0)), pl.BlockSpec((tm * nch, 128), lambda i: (i, 0)),
                   pair_spec, pair_spec, pair_spec, pl.BlockSpec((N_EXPERTS, 1), lambda i: (0, 0))),
        scratch_shapes=[pltpu.VMEM((tm, d), BF16), pltpu.VMEM((N_EXPERTS, 1), F32)],
        name="merge_mix",
        compiler_params=_cparams("arbitrary"),
    )(attn, conf, gla_f, gla_b, u, g_norm, sconv, u_gates, *w_br_half, w_mix, x, mod, g_ffn, mod, mod, wr_stack,
      b_router, tri)


def _cast_kernel(w_ref, o_ref, *, scale):
    o_ref[...] = (w_ref[...] * scale).astype(o_ref.dtype)


def _to_bf16(w, scale=1.0):
    n_l, rows, cols = w.shape
    tr = _pick(rows, (512, 256, 128))
    return pl.pallas_call(
        functools.partial(_cast_kernel, scale=scale),
        out_shape=jax.ShapeDtypeStruct(w.shape, BF16),
        grid=(n_l, rows // tr),
        in_specs=[pl.BlockSpec((None, tr, cols), lambda l, i: (l, i, 0))],
        out_specs=pl.BlockSpec((None, tr, cols), lambda l, i: (l, i, 0)),
        name="weight_cast",
        compiler_params=_cparams("parallel", "parallel"),
    )(w)


def _second_largest(a, b, c, d):
    return jnp.maximum(jnp.maximum(jnp.minimum(a, b), jnp.minimum(c, d)),
                       jnp.minimum(jnp.maximum(a, b), jnp.maximum(c, d)))


def _chunk_rows(rows, n, cc):
    return pl.ds(cc, rows, stride=n) if n > 1 else pl.ds(0, rows)


def _pack_rows(ref, y):
    rows = y.shape[0]
    n = ref.shape[0] // rows
    half = n * 128
    bits = lambda v: pltpu.bitcast(v.astype(BF16).astype(F32), jnp.uint32)
    packed = (bits(y[:, :half]) >> 16) | bits(y[:, half:])
    for cc in range(n):
        ref[_chunk_rows(rows, n, cc), :] = packed[:, cc * 128:(cc + 1) * 128]


def _unpack_cols(words):
    lo = pltpu.bitcast(words << 16, F32)
    hi = pltpu.bitcast(words & jnp.uint32(0xFFFF0000), F32)
    return lo, hi


def _route_rows(x, g_ref, sh_ref, sc_ref, wr_ref, br_ref, tri_ref, h_ref, idx_ref, wgt_ref, rank_ref, cnt_ref,
                carry_ref):
    y = _rms(x, g_ref[...]) * (1.0 + sc_ref[...]) + sh_ref[...]
    hi = y.astype(BF16)
    lo = (y - hi.astype(F32)).astype(BF16)
    _pack_rows(h_ref, y)
    ne = N_EXPERTS
    two = _dot_nt(wr_ref[...], hi)
    logits = two[0:ne] + two[ne:2 * ne] + _dot_nt(wr_ref[0:ne, :], lo)
    aff = 1.0 / (1.0 + jnp.exp(-logits))
    sel = aff + br_ref[...]
    per = ne // N_GROUPS
    gs = []
    for g in range(N_GROUPS):
        a, b, c, d = (sel[g * per + r:g * per + r + 1] for r in range(per))
        top1 = jnp.maximum(jnp.maximum(a, b), jnp.maximum(c, d))
        gs.append(top1 + _second_largest(a, b, c, d))
    best = gs[0]
    for g in range(1, N_GROUPS):
        best = jnp.maximum(best, gs[g])
    grp = jnp.full(best.shape, N_GROUPS, jnp.int32)
    for g in reversed(range(N_GROUPS)):
        grp = jnp.where(gs[g] == best, g, grp)
    erow = lax.broadcasted_iota(jnp.int32, sel.shape, 0)
    masked = jnp.where(erow // per == grp, sel, -jnp.inf)
    m1 = jnp.max(masked, axis=0, keepdims=True)
    i1 = jnp.min(jnp.where(masked == m1, erow, ne), axis=0, keepdims=True)
    one1 = erow == i1
    masked2 = jnp.where(one1, -jnp.inf, masked)
    m2 = jnp.max(masked2, axis=0, keepdims=True)
    i2 = jnp.min(jnp.where(masked2 == m2, erow, ne), axis=0, keepdims=True)
    one2 = erow == i2
    w1 = jnp.sum(jnp.where(one1, aff, 0.0), axis=0, keepdims=True)
    w2 = jnp.sum(jnp.where(one2, aff, 0.0), axis=0, keepdims=True)
    inv = 1.0 / (w1 + w2)
    idx_ref[0:1, :] = i1
    idx_ref[1:2, :] = i2
    wgt_ref[0:1, :] = w1 * inv
    wgt_ref[1:2, :] = w2 * inv
    o1 = jnp.where(one1, 1.0, 0.0)
    o2 = jnp.where(one2, 1.0, 0.0)
    tri = tri_ref[...]
    c0 = carry_ref[...]
    c1 = c0 + jnp.sum(o1, axis=1, keepdims=True)
    rank_ref[0:1, :] = jnp.sum(o1 * (c0 + _dot(o1.astype(BF16), tri)), axis=0, keepdims=True).astype(jnp.int32)
    rank_ref[1:2, :] = jnp.sum(o2 * (c1 + _dot(o2.astype(BF16), tri)), axis=0, keepdims=True).astype(jnp.int32)
    c2 = c1 + jnp.sum(o2, axis=1, keepdims=True)
    carry_ref[...] = c2
    cnt_ref[...] = c2


def _dispatch_kernel(dest_ref, last_ref, h_ref, xs_hbm, zero_ref, sem, zero_sem, *, m, n):
    i = pl.program_id(0)
    tr = h_ref.shape[0] // n

    @pl.when(i == 0)
    def _():
        zero_ref[...] = jnp.zeros_like(zero_ref)

        def fills(e):
            rows = zero_ref.shape[0]
            start = pl.multiple_of(last_ref[e] * n, rows)
            return pltpu.make_async_copy(zero_ref, xs_hbm.at[pl.ds(start, rows), :], zero_sem)

        for e in range(2 * N_EXPERTS):
            @pl.when(last_ref[e] >= 0)
            def _():
                fills(e).start()
        for e in range(2 * N_EXPERTS):
            @pl.when(last_ref[e] >= 0)
            def _():
                fills(e).wait()

    def issue(r, carry):
        src = h_ref.at[pl.ds(pl.multiple_of(r * n, n), n), :]
        for k in range(2):
            row = pl.multiple_of(dest_ref[k * m + i * tr + r], n)
            pltpu.make_async_copy(src, xs_hbm.at[pl.ds(row, n), :], sem).start()
        return carry

    lax.fori_loop(0, tr, issue, 0, unroll=4)
    everything = xs_hbm.at[pl.ds(0, 2 * tr * n), :]
    pltpu.make_async_copy(everything, everything, sem).wait()


def _dispatch(hfp, dest, last_tile_row, n_rows, tr, tm):
    n = hfp.shape[0] * 2 // dest.shape[0]
    m = hfp.shape[0] // n
    return pl.pallas_call(
        functools.partial(_dispatch_kernel, m=m, n=n),
        out_shape=jax.ShapeDtypeStruct((n_rows * n, 128), jnp.uint32),
        grid_spec=pltpu.PrefetchScalarGridSpec(
            num_scalar_prefetch=2, grid=(m // tr,),
            in_specs=[pl.BlockSpec((tr * n, 128), lambda i, *_: (i, 0))],
            out_specs=pl.BlockSpec(memory_space=pl.ANY),
            scratch_shapes=[pltpu.VMEM((tm * n, 128), jnp.uint32), pltpu.SemaphoreType.DMA(()),
                            pltpu.SemaphoreType.DMA(())]),
        name="moe_dispatch",
        compiler_params=_cparams("arbitrary"),
    )(dest, last_tile_row, hfp)


def _gmm_kernel(te_ref, nv_ref, first_ref, ord_ref, next_ref, xs_ref, wg_hbm, wu_hbm, wd_hbm, ys_ref, fg_ref, fu_ref,
                fd_ref, wgb_ref, wub_ref, wdb_ref, sem_ref, *, layer):
    j = pl.program_id(0)

    def fetch(e, slot):
        return [pltpu.make_async_copy(src.at[layer, e], dst.at[slot], sem_ref.at[slot])
                for src, dst in ((wg_hbm, fg_ref), (wu_hbm, fu_ref), (wd_hbm, fd_ref))]

    @pl.when(j == 0)
    def _():
        for cp in fetch(te_ref[0], 0):
            cp.start()

    @pl.when(j < nv_ref[0])
    def _():
        e = te_ref[j]

        @pl.when(j == first_ref[e])
        def _():
            slot = ord_ref[e] % 2
            for cp in fetch(e, slot):
                cp.wait()
            wgb_ref[...] = fg_ref[slot].astype(BF16)
            wub_ref[...] = fu_ref[slot].astype(BF16)
            wdb_ref[...] = fd_ref[slot].astype(BF16)

            @pl.when(next_ref[e] >= 0)
            def _():
                for cp in fetch(next_ref[e], 1 - slot):
                    cp.start()

        half = wgb_ref.shape[0] // 2
        n = half // 128
        tm = xs_ref.shape[0] // n
        parts = [_unpack_cols(xs_ref[_chunk_rows(tm, n, cc), :]) for cc in range(n)]
        xa = jnp.concatenate([p[0].astype(BF16) for p in parts], axis=1)
        xb = jnp.concatenate([p[1].astype(BF16) for p in parts], axis=1)
        g = _dot(xa, wgb_ref[0:half, :]) + _dot(xb, wgb_ref[half:2 * half, :])
        u = _dot(xa, wub_ref[0:half, :]) + _dot(xb, wub_ref[half:2 * half, :])
        act = (g * jax.nn.sigmoid(g) * u).astype(BF16)
        _pack_rows(ys_ref, _dot(act, wdb_ref[...]))

    @pl.when(j >= nv_ref[0])
    def _():
        ys_ref[...] = jnp.zeros_like(ys_ref)


def _gmm(xs, tile_expert, n_valid, first_tile, ordinal, next_expert, wg, wu, wd, layer, tm):
    d, ff = wg.shape[-2], wg.shape[-1]
    nch = d // 256
    blk = tm * nch
    any_spec = pl.BlockSpec(memory_space=pl.ANY)
    return pl.pallas_call(
        functools.partial(_gmm_kernel, layer=layer),
        out_shape=jax.ShapeDtypeStruct(xs.shape, jnp.uint32),
        grid_spec=pltpu.PrefetchScalarGridSpec(
            num_scalar_prefetch=5, grid=(xs.shape[0] // blk,),
            in_specs=[pl.BlockSpec((blk, 128), lambda j, te, nv, *_: (jnp.minimum(j, nv[0] - 1), 0)),
                      any_spec, any_spec, any_spec],
            out_specs=pl.BlockSpec((blk, 128), lambda j, *_: (j, 0)),
            scratch_shapes=[pltpu.VMEM((2, d, ff), F32), pltpu.VMEM((2, d, ff), F32), pltpu.VMEM((2, ff, d), F32),
                            pltpu.VMEM((d, ff), BF16), pltpu.VMEM((d, ff), BF16), pltpu.VMEM((ff, d), BF16),
                            pltpu.SemaphoreType.DMA((2,))]),
        name="moe_experts",
        compiler_params=_cparams("arbitrary"),
    )(tile_expert, n_valid, first_tile, ordinal, next_expert, xs, wg, wu, wd)


def _combine_kernel(dest_ref, x_ref, w_ref, gt_ref, ys_hbm, *rest, m, last):
    if last:
        g_ref, h_ref, o_ref, buf_ref, sem_ref = rest
    else:
        g_ref, sh_ref, sc_ref, o_ref, h_ref, buf_ref, sem_ref = rest
    i = pl.program_id(0)
    n_tiles = pl.num_programs(0)
    tr = x_ref.shape[0]
    n = buf_ref.shape[2] // tr

    def issue(tile, slot):
        def body(r, carry):
            for k in range(2):
                row = pl.multiple_of(dest_ref[k * m + tile * tr + r], n)
                pltpu.make_async_copy(ys_hbm.at[pl.ds(row, n), :],
                                      buf_ref.at[slot, k, pl.ds(pl.multiple_of(r * n, n), n), :],
                                      sem_ref.at[slot]).start()
            return carry
        lax.fori_loop(0, tr, body, 0, unroll=4)

    @pl.when(i == 0)
    def _():
        issue(0, 0)

    @pl.when(i + 1 < n_tiles)
    def _():
        issue(i + 1, (i + 1) % 2)

    slot = i % 2
    pltpu.make_async_copy(buf_ref.at[slot], buf_ref.at[slot], sem_ref.at[slot]).wait()
    w1, w2 = w_ref[:, 0:1], w_ref[:, 1:2]
    half = n * 128
    for cc in range(n):
        lo1, hi1 = _unpack_cols(buf_ref[slot, 0, _chunk_rows(tr, n, cc), :])
        lo2, hi2 = _unpack_cols(buf_ref[slot, 1, _chunk_rows(tr, n, cc), :])
        ca = slice(cc * 128, (cc + 1) * 128)
        cb = slice(half + cc * 128, half + (cc + 1) * 128)
        o_ref[:, ca] = x_ref[:, ca] + gt_ref[:, ca] * (w1 * lo1 + w2 * lo2)
        o_ref[:, cb] = x_ref[:, cb] + gt_ref[:, cb] * (w1 * hi1 + w2 * hi2)
    if last:
        h_ref[...] = _rms(o_ref[...], g_ref[...])
    else:
        y = _rms(o_ref[...], g_ref[...]) * (1.0 + sc_ref[...]) + sh_ref[...]
        h_ref[...] = y.astype(h_ref.dtype)


def _combine(geo, x, ys, dest, w12, mod, layer, g_next, g_final):
    tr, d = geo.tr, geo.d
    nch = d // 256
    last = g_next is None
    tile = pl.BlockSpec((tr, d), lambda i, *_: (i, 0))
    in_specs = [tile, pl.BlockSpec((tr, 2), lambda i, *_: (i, 0)), geo.mod_spec(layer, 5),
                pl.BlockSpec(memory_space=pl.ANY)]
    operands = [x, w12, mod, ys]
    scratch = [pltpu.VMEM((2, 2, tr * nch, 128), jnp.uint32), pltpu.SemaphoreType.DMA((2,))]
    if last:
        in_specs.append(pl.BlockSpec((1, d), lambda i, *_: (0, 0)))
        operands.append(g_final.reshape(1, d))
        out_shape = jax.ShapeDtypeStruct((geo.b, geo.s, d), F32)
        out_specs = pl.BlockSpec((None, tr, d), lambda i, *_: (i // geo.nt, jnp.maximum(i % geo.nt - geo.nct, 0), 0))
        scratch = [pltpu.VMEM((tr, d), F32)] + scratch
    else:
        in_specs += [pl.BlockSpec((None, 1, d), lambda i, *_: (layer + 1, 0, 0)),
                     geo.mod_spec(layer + 1, 0), geo.mod_spec(layer + 1, 1)]
        operands += [g_next, mod, mod]
        out_shape = [jax.ShapeDtypeStruct((geo.m, d), F32), jax.ShapeDtypeStruct((geo.m, d), BF16)]
        out_specs = [tile, tile]
    return pl.pallas_call(
        functools.partial(_combine_kernel, m=geo.m, last=last),
        out_shape=out_shape,
        grid_spec=pltpu.PrefetchScalarGridSpec(
            num_scalar_prefetch=1, grid=(geo.m // tr,),
            in_specs=in_specs, out_specs=out_specs, scratch_shapes=scratch),
        name="moe_combine",
        compiler_params=_cparams("arbitrary"),
    )(dest, *operands)


def _moe(geo, x, routed, mod, layer, wg, wu, wd, g_next, g_final):
    tm = 256
    hfp, ridx, rwgt, rank, cnt = routed
    n_tiles = (2 * geo.m) // tm + N_EXPERTS
    counts = cnt[:, 0].astype(jnp.int32)
    tiles_per = (counts + tm - 1) // tm
    tile_start = jnp.cumsum(tiles_per) - tiles_per
    base = tile_start * tm
    n_valid = jnp.sum(tiles_per).reshape(1)
    tile_expert = jnp.sum(jnp.arange(n_tiles)[:, None] >= tile_start[None, :], axis=1).astype(jnp.int32) - 1
    experts = jnp.arange(N_EXPERTS, dtype=jnp.int32)[:, None, None]
    dest = (jnp.sum(jnp.where(ridx[None] == experts, base[:, None, None], 0), axis=0) + rank).reshape(-1)
    dest = dest * (geo.d // 256)
    tail = n_valid + jnp.arange(N_EXPERTS)
    fill_rows = jnp.concatenate([jnp.where(tiles_per > 0, base + (tiles_per - 1) * tm, -1),
                                 jnp.where(tail < n_tiles, tail * tm, -1)])
    xs = _dispatch(hfp, dest, fill_rows, n_tiles * tm, geo.tr, tm)
    has_rows = tiles_per > 0
    ordinal = jnp.cumsum(has_rows.astype(jnp.int32)) - 1
    ar = jnp.arange(N_EXPERTS)
    later = jnp.where((ar[None, :] > ar[:, None]) & has_rows[None, :], ar[None, :], N_EXPERTS)
    next_expert = jnp.min(later, axis=1)
    next_expert = jnp.where(next_expert < N_EXPERTS, next_expert, -1).astype(jnp.int32)
    ys = _gmm(xs, tile_expert, n_valid, tile_start, ordinal, next_expert, wg, wu, wd, layer, tm)
    return _combine(geo, x, ys, dest, rwgt.T, mod, layer, g_next, g_final)


def _proj_plans(d):
    src = _src_layout(d)
    off, total = _proj_layout()
    width = lambda name: src[name][1] - src[name][0]
    cf0, sc0 = src["cf"][0], src["sc"][0]
    assert src["gb"][0] == src["gf"][1]
    pieces = [(off["cq"], src["cq"][0], width("cq")), (off["ckv"], src["ckv"][0], width("ckv")),
              (off["kr"], src["kr"][0], width("kr")),
              (off["cf_a"], cf0, CF_WIDTH), (off["cf_b"], cf0 + CF_WIDTH, CF_WIDTH),
              (off["gla_q"], src["gla_q"][0], width("gla_q")), (off["gla_k"], src["gla_k"][0], width("gla_k")),
              (off["gla_v"], src["gla_v"][0], width("gla_v")), (off["gate"], src["gf"][0], 2 * GLA_GATE_RANK),
              (off["gla_r"], src["gla_r"][0], width("gla_r")),
              (off["sc_b"], sc0, SC_WIDTH), (off["sc_c"], sc0 + SC_WIDTH, SC_WIDTH),
              (off["sc_h"], sc0 + 2 * SC_WIDTH, SC_WIDTH)]
    tn = PROJ_TILE
    assert total % tn == 0 and all(dst // tn == (dst + rows - 1) // tn for dst, _, rows in pieces)
    rest = _ProjPlan(tn, total // tn, pieces=sorted(pieces))
    gates_w = N_BRANCHES * d
    gn = min(tn, gates_w)
    gates = _ProjPlan(gn, gates_w // gn, first_row=src["merge"][0])
    return gates, rest, off


def _rope_swap_perm():
    q = MLA_ROPE // 4
    return np.concatenate([np.arange(q, 2 * q), np.arange(0, q), np.arange(3 * q, 4 * q), np.arange(2 * q, 3 * q)])


def _prep_mla(w_uq, w_ukv):
    n_l = w_uq.shape[0]
    perm = _rope_swap_perm()
    wq = w_uq.reshape(n_l, MLA_Q_RANK, MLA_HEADS, MLA_NOPE + MLA_ROPE)
    qn, qr = wq[..., :MLA_NOPE], wq[..., MLA_NOPE:]
    zeros = lambda *s: jnp.zeros((n_l,) + s, w_uq.dtype)
    wqa = jnp.concatenate([qn, qr, qr], axis=-1)
    wqb = jnp.concatenate([zeros(MLA_Q_RANK, MLA_HEADS, MLA_NOPE), qr[..., perm],
                           zeros(MLA_Q_RANK, MLA_HEADS, MLA_ROPE)], axis=-1)
    wkv = w_ukv.reshape(n_l, MLA_KV_RANK, MLA_HEADS, MLA_NOPE + MLA_V)
    wka = jnp.concatenate([wkv[..., :MLA_NOPE], zeros(MLA_KV_RANK, MLA_HEADS, MLA_SLOT - MLA_NOPE)], axis=-1)
    wv = wkv[..., MLA_NOPE:]
    flat = lambda w: w.reshape(n_l, w.shape[1], -1).astype(BF16)
    sela = np.zeros((128, MLA_HEADS, MLA_SLOT), np.float32)
    selb = np.zeros((128, MLA_HEADS, MLA_SLOT), np.float32)
    for r in range(MLA_ROPE):
        sela[r, :, MLA_NOPE + r] = 1.0
        sela[r, :, MLA_NOPE + MLA_ROPE + r] = 1.0
        selb[perm[r], :, MLA_NOPE + r] = 1.0
    sel = lambda s: jnp.asarray(s.reshape(128, -1), BF16)
    return flat(wqa), flat(wqb), flat(wka), sel(sela), sel(selb), flat(wv)


def _rope_tables(geo):
    rows = geo.s // GRID_W
    pos = jnp.arange(rows * GRID_W)
    row = (pos // GRID_W).astype(F32)
    col = (pos % GRID_W).astype(F32)
    n_freq = MLA_ROPE // 4
    freqs = ROPE_THETA ** (-jnp.arange(n_freq, dtype=F32) / n_freq)
    cr, sr = jnp.cos(row[:, None] * freqs), jnp.sin(row[:, None] * freqs)
    cc, sn = jnp.cos(col[:, None] * freqs), jnp.sin(col[:, None] * freqs)
    cos32 = jnp.concatenate([cr, cr, cc, cc], axis=-1)
    sin32 = jnp.concatenate([-sr, sr, -sn, sn], axis=-1)
    s, c = geo.s, geo.c
    one = lambda n, w: jnp.ones((n, w), F32)
    zero = lambda n, w: jnp.zeros((n, w), F32)
    scale = (MLA_NOPE + MLA_ROPE) ** -0.5
    qca = jnp.concatenate([jnp.concatenate([one(c, 64), zero(c, 32), one(c, 32)], 1),
                           jnp.concatenate([one(s, 64), cos32, one(s, 32)], 1)], 0) * scale
    qsb = jnp.concatenate([zero(c, 128), jnp.concatenate([zero(s, 64), sin32, zero(s, 32)], 1)], 0) * scale
    kca = jnp.concatenate([jnp.concatenate([one(c, 64), zero(c, 32), one(c, 32)], 1),
                           jnp.concatenate([one(s, 64), cos32, zero(s, 32)], 1)], 0)
    ksb = jnp.concatenate([zero(c, 128), jnp.concatenate([zero(s, 64), sin32, zero(s, 32)], 1)], 0)
    return qca, qsb, kca, ksb


def kernel(x, c, ctx, c_ctx, w_mod, b_mod, g_mix, g_ffn, w_in, g_q, w_uq, g_kv, w_ukv, w_mla_o, w_cf_dw, b_cf_dw, g_cf_ln, b_cf_ln, w_cf_o, w_gla_gf, b_gla_gf, w_gla_gb, b_gla_gb, g_gla_norm, w_gla_o, w_sc_conv, w_sc_o, w_mix_o, w_router, b_router, w_e_gate, w_e_up, w_e_down, g_final):
    bsz, seq, d = x.shape
    n_ctx = ctx.shape[1]
    depth = w_in.shape[0]
    geo = _Geom(bsz, n_ctx, seq, d)
    assert bsz + 1 <= 8 and seq % GRID_W == 0 and n_ctx % GLA_CHUNK == 0 and seq % GLA_CHUNK == 0

    plan_gates, plan_rest, off = _proj_plans(d)
    w_in_t = jnp.swapaxes(w_in, 1, 2)
    wqa, wqb, wka, sela, selb, wv = _prep_mla(w_uq, w_ukv)
    tabs = _rope_tables(geo)
    w_br_half = [_to_bf16(w, 0.5) for w in (w_mla_o, w_cf_o, w_gla_o, w_sc_o)]
    w_mix = _to_bf16(w_mix_o)
    gate_w = jnp.zeros((depth, 2, 128, GLA_HEADS * GLA_DK), F32)
    gate_w = gate_w.at[:, 0, :GLA_GATE_RANK].set(w_gla_gf).at[:, 1, GLA_GATE_RANK:2 * GLA_GATE_RANK].set(w_gla_gb)
    gate_w = gate_w.astype(BF16)
    gate_b = jnp.stack([b_gla_gf, b_gla_gb], axis=1)[:, :, None, :]
    mat3, msk4 = _gla_tables()
    wr_t = w_router.T
    wr_hi = wr_t.astype(BF16)
    wr_stack = jnp.concatenate([wr_hi, (wr_t - wr_hi.astype(F32)).astype(BF16)], axis=0)
    vec = lambda a: a.reshape(a.shape[0], 1, a.shape[-1])

    cc = jnp.zeros((8, d), F32).at[:bsz].set(c).at[bsz].set(c_ctx)
    mod = _modulation(cc, w_mod, b_mod).reshape(depth, 8, 1, 6 * d)

    xa, h = _first_norm(geo, x, ctx, vec(g_mix), mod)
    for l in range(depth):
        u = _project(h, w_in_t, l, plan_rest, 1.0)
        u_gates = _project(h, w_in_t, l, plan_gates, 0.5)
        q, k, v = _mla_qkv(geo, u, off, l, vec(g_q), vec(g_kv), wqa, wqb, wka, sela, selb, wv, tabs)
        attn = _mla_attention(geo, q, k, v)
        conf = _conformer(geo, u, off, l, w_cf_dw, vec(b_cf_dw), vec(g_cf_ln), vec(b_cf_ln))
        sconv = _shortconv(geo, u, off, l, w_sc_conv)
        gla_f, gla_b = _gla_scan(geo, u, off, l, gate_w, gate_b, mat3, msk4)
        xa, *routed = _merge_mix(geo, attn, conf, gla_f, gla_b, u, u_gates, off, vec(g_gla_norm), sconv, w_br_half,
                                 w_mix, xa, mod, l, vec(g_ffn), wr_stack, b_router.reshape(N_EXPERTS, 1))
        if l + 1 < depth:
            xa, h = _moe(geo, xa, routed, mod, l, w_e_gate, w_e_up, w_e_down, vec(g_mix), None)
        else:
            return _moe(geo, xa, routed, mod, l, w_e_gate, w_e_up, w_e_down, None, g_final)
```

```python
import functools

import numpy as np
import jax
import jax.numpy as jnp
from jax import lax
from jax.experimental import pallas as pl
from jax.experimental.pallas import tpu as pltpu

F32 = jnp.float32
BF16 = jnp.bfloat16
EPS = 1e-6

GRID_W = 64
ROPE_THETA = 10000.0
MLA_HEADS = 8
MLA_NOPE = 64
MLA_ROPE = 32
MLA_V = 64
MLA_Q_RANK = 512
MLA_KV_RANK = 256
MLA_SLOT = 128
CF_WIDTH = 512
CF_KERNEL = 31
GLA_HEADS = 4
GLA_DK = 64
GLA_DV = 128
GLA_GATE_RANK = 16
GLA_GATE_TAU = 16.0
GLA_CHUNK = 64
GLA_LEVELS = 6
SC_WIDTH = 512
SC_KERNEL = 3
N_BRANCHES = 4
N_EXPERTS = 16
N_GROUPS = 4
EXPERT_FF = 512
CONV_HALO = 16
SUBLANES = 8
MERGE_COLS = 512
ATTN_ROWS = 768
PROJ_TILE = 1024
PROJ_CAST_ROWS = 256
VMEM_LIMIT = 56 * 1024 * 1024


def _cparams(*sem):
    return pltpu.CompilerParams(dimension_semantics=sem, vmem_limit_bytes=VMEM_LIMIT)


def _pick(n, cands):
    for c in cands:
        if n % c == 0:
            return c
    raise ValueError(f"no tile for {n}")


def _dot(a, b):
    return jnp.dot(a, b, preferred_element_type=F32)


def _dot_nt(a, b):
    return lax.dot_general(a, b, (((1,), (1,)), ((), ())), preferred_element_type=F32)


def _proj_layout():
    parts = [("sc_b", SC_WIDTH), ("sc_c", SC_WIDTH), ("sc_h", SC_WIDTH),
             ("cf_a", CF_WIDTH), ("cf_b", CF_WIDTH), ("cq", MLA_Q_RANK), ("gla_v", GLA_HEADS * GLA_DV),
             ("gla_r", GLA_HEADS * GLA_DV), ("ckv", MLA_KV_RANK), ("gla_q", GLA_HEADS * GLA_DK),
             ("gla_k", GLA_HEADS * GLA_DK), ("kr", 128), ("gate", 128)]
    off, start = {}, 0
    for name, width in parts:
        assert start % width == 0
        off[name] = start
        start += width
    return off, start


def _src_layout(d):
    parts = (("cq", MLA_Q_RANK), ("ckv", MLA_KV_RANK), ("kr", MLA_ROPE), ("cf", 2 * CF_WIDTH),
             ("gla_q", GLA_HEADS * GLA_DK), ("gla_k", GLA_HEADS * GLA_DK), ("gla_v", GLA_HEADS * GLA_DV),
             ("gf", GLA_GATE_RANK), ("gb", GLA_GATE_RANK), ("gla_r", GLA_HEADS * GLA_DV),
             ("sc", 3 * SC_WIDTH), ("merge", N_BRANCHES * d))
    off, start = {}, 0
    for name, width in parts:
        off[name] = (start, start + width)
        start += width
    return off


def _mod_kernel(c_ref, w_ref, b_ref, o_ref):
    c = c_ref[...]
    a = (c * jax.nn.sigmoid(c)).astype(BF16)
    o_ref[...] = _dot(a, w_ref[...].astype(BF16)) + b_ref[...]


def _modulation(cc, w_mod, b_mod):
    n_l, d, n6 = w_mod.shape
    tn = _pick(n6, (1024, 512, 256, 128))
    return pl.pallas_call(
        _mod_kernel,
        out_shape=jax.ShapeDtypeStruct((n_l, 8, n6), F32),
        grid=(n_l, n6 // tn),
        in_specs=[pl.BlockSpec((8, d), lambda l, j: (0, 0)),
                  pl.BlockSpec((None, d, tn), lambda l, j: (l, 0, j)),
                  pl.BlockSpec((None, 1, tn), lambda l, j: (l, 0, j))],
        out_specs=pl.BlockSpec((None, 8, tn), lambda l, j: (l, 0, j)),
        name="adaln_mod",
        compiler_params=_cparams("parallel", "parallel"),
    )(cc, w_mod, b_mod.reshape(n_l, 1, n6))


class _Geom:
    def __init__(self, bsz, n_ctx, seq, d):
        self.b, self.c, self.s, self.d = bsz, n_ctx, seq, d
        self.n = n_ctx + seq
        self.m = bsz * self.n
        self.tr = _pick(int(np.gcd(n_ctx, seq)), (256, 128, 64))
        self.nt = self.n // self.tr
        self.nct = n_ctx // self.tr

    def mod_row(self, i):
        b, j = i // self.nt, i % self.nt
        return jnp.where(j < self.nct, self.b, b)

    def mod_spec(self, layer, part):
        return pl.BlockSpec((None, None, 1, self.d), lambda i, *_: (layer, self.mod_row(i), 0, part))


def _first_norm_kernel(x_ref, c_ref, g_ref, sh_ref, sc_ref, xa_ref, h_ref, *, geo):
    is_ctx = pl.program_id(0) % geo.nt < geo.nct
    rows = jnp.where(is_ctx, c_ref[...], x_ref[...])
    xa_ref[...] = rows
    y = rows * lax.rsqrt(jnp.mean(rows * rows, axis=-1, keepdims=True) + EPS) * g_ref[...]
    h_ref[...] = (y * (1.0 + sc_ref[...]) + sh_ref[...]).astype(h_ref.dtype)


def _first_norm(geo, x, ctx, g, mod):
    tr, d = geo.tr, geo.d
    tile = pl.BlockSpec((tr, d), lambda i: (i, 0))
    return pl.pallas_call(
        functools.partial(_first_norm_kernel, geo=geo),
        out_shape=(jax.ShapeDtypeStruct((geo.m, d), F32), jax.ShapeDtypeStruct((geo.m, d), BF16)),
        grid=(geo.m // tr,),
        in_specs=[pl.BlockSpec((None, tr, d), lambda i: (i // geo.nt, jnp.maximum(i % geo.nt - geo.nct, 0), 0)),
                  pl.BlockSpec((None, tr, d), lambda i: (i // geo.nt, jnp.minimum(i % geo.nt, geo.nct - 1), 0)),
                  pl.BlockSpec((None, 1, d), lambda i: (0, 0, 0)),
                  geo.mod_spec(0, 0), geo.mod_spec(0, 1)],
        out_specs=(tile, tile),
        name="first_norm",
        compiler_params=_cparams("parallel"),
    )(x, ctx, g, mod, mod)


def _proj_kernel(a_ref, w_hbm, o_ref, wbuf_ref, wb_ref, sem_ref, *, layer, plan, scale):
    j, i = pl.program_id(0), pl.program_id(1)
    n_tiles = pl.num_programs(0)
    tn = wb_ref.shape[0]

    def copies(t, slot):
        return [pltpu.make_async_copy(w_hbm.at[layer, pl.ds(src, rows), :], wbuf_ref.at[slot, pl.ds(dst, rows), :],
                                      sem_ref.at[slot]) for dst, src, rows in plan(t)]

    def on_tile(t, fn):
        if not plan.static:
            fn(t)
            return
        for tt in range(plan.n_tiles):
            pl.when(t == tt)(functools.partial(fn, tt))

    def start(t):
        for cp in copies(t, t % 2):
            cp.start()

    def land(t):
        slot = t % 2
        for cp in copies(t, slot):
            cp.wait()
        covered = 0
        for dst, _, rows in plan(t):
            if dst > covered:
                wb_ref[covered:dst, :] = jnp.zeros((dst - covered, wb_ref.shape[1]), BF16)
            for r0 in range(0, rows, PROJ_CAST_ROWS):
                rr = min(PROJ_CAST_ROWS, rows - r0)
                rows_f32 = wbuf_ref[slot, dst + r0:dst + r0 + rr, :]
                wb_ref[dst + r0:dst + r0 + rr, :] = (rows_f32 * scale).astype(BF16)
            covered = dst + rows
        if covered < tn:
            wb_ref[covered:tn, :] = jnp.zeros((tn - covered, wb_ref.shape[1]), BF16)

    @pl.when(i == 0)
    def _():
        @pl.when(j == 0)
        def _():
            on_tile(j, start)

        on_tile(j, land)

        @pl.when(j + 1 < n_tiles)
        def _():
            on_tile(j + 1, start)

    o_ref[...] = _dot_nt(a_ref[...], wb_ref[...]).astype(o_ref.dtype)


class _ProjPlan:
    def __init__(self, tn, n_tiles, pieces=None, first_row=None):
        self.tn, self.n_tiles, self.static = tn, n_tiles, pieces is not None
        self._pieces, self._first_row = pieces, first_row

    def __call__(self, t):
        if self.static:
            lo = t * self.tn
            return [(dst - lo, src, rows) for dst, src, rows in self._pieces if lo <= dst < lo + self.tn]
        return [(0, pl.multiple_of(self._first_row + t * self.tn, SUBLANES), self.tn)]


def _project(a, w_t, layer, plan, scale):
    m, k = a.shape
    tn = plan.tn
    tm = _pick(m, (1152, 1024, 512, 256, 128))
    return pl.pallas_call(
        functools.partial(_proj_kernel, layer=layer, plan=plan, scale=scale),
        out_shape=jax.ShapeDtypeStruct((m, tn * plan.n_tiles), BF16),
        grid=(plan.n_tiles, m // tm),
        in_specs=[pl.BlockSpec((tm, k), lambda j, i: (i, 0)), pl.BlockSpec(memory_space=pl.ANY)],
        out_specs=pl.BlockSpec((tm, tn), lambda j, i: (i, j)),
        scratch_shapes=[pltpu.VMEM((2, tn, k), F32), pltpu.VMEM((tn, k), BF16), pltpu.SemaphoreType.DMA((2,))],
        name="proj_in",
        compiler_params=_cparams("arbitrary", "arbitrary"),
    )(a, w_t)


def _rms(x, g):
    return x * lax.rsqrt(jnp.mean(x * x, axis=-1, keepdims=True) + EPS) * g


def _qkv_kernel(cq_ref, ckv_ref, kr_ref, gq_ref, gkv_ref, wqa_ref, wqb_ref, wka_ref, sela_ref, selb_ref,
                wv_ref, qca_ref, qsb_ref, kca_ref, ksb_ref, q_ref, k_ref, v_ref):
    cqn = _rms(cq_ref[...].astype(F32), gq_ref[...]).astype(BF16)
    ckvn = _rms(ckv_ref[...].astype(F32), gkv_ref[...]).astype(BF16)
    kr = kr_ref[...]
    qa = _dot(cqn, wqa_ref[...])
    qb = _dot(cqn, wqb_ref[...])
    ka = _dot(ckvn, wka_ref[...]) + _dot(kr, sela_ref[...])
    kb = _dot(kr, selb_ref[...])
    qca, qsb, kca, ksb = qca_ref[...], qsb_ref[...], kca_ref[...], ksb_ref[...]
    for h in range(MLA_HEADS):
        sl = slice(h * MLA_SLOT, (h + 1) * MLA_SLOT)
        q_ref[:, sl] = (qa[:, sl] * qca + qb[:, sl] * qsb).astype(BF16)
        k_ref[:, sl] = (ka[:, sl] * kca + kb[:, sl] * ksb).astype(BF16)
    v_ref[...] = _dot(ckvn, wv_ref[...]).astype(BF16)


def _mla_qkv(geo, u, off, layer, gq, gkv, wqa, wqb, wka, sela, selb, wv, tabs):
    tr, m = geo.tr, geo.m
    hq = MLA_HEADS * MLA_SLOT
    hv = MLA_HEADS * MLA_V
    row = lambda w, o: pl.BlockSpec((tr, w), lambda i: (i, o // w))
    lw = lambda a: pl.BlockSpec((None,) + a.shape[1:], lambda i: (layer,) + (0,) * (a.ndim - 1))
    cw = lambda a: pl.BlockSpec(a.shape, lambda i: (0,) * a.ndim)
    tab = pl.BlockSpec((tr, MLA_SLOT), lambda i: (i % geo.nt, 0))
    return pl.pallas_call(
        _qkv_kernel,
        out_shape=(jax.ShapeDtypeStruct((m, hq), BF16), jax.ShapeDtypeStruct((m, hq), BF16),
                   jax.ShapeDtypeStruct((m, hv), BF16)),
        grid=(m // tr,),
        in_specs=[row(MLA_Q_RANK, off["cq"]), row(MLA_KV_RANK, off["ckv"]), row(128, off["kr"]),
                  lw(gq), lw(gkv), lw(wqa), lw(wqb), lw(wka), cw(sela), cw(selb), lw(wv), tab, tab, tab, tab],
        out_specs=(pl.BlockSpec((tr, hq), lambda i: (i, 0)), pl.BlockSpec((tr, hq), lambda i: (i, 0)),
                   pl.BlockSpec((tr, hv), lambda i: (i, 0))),
        name="mla_qkv",
        compiler_params=_cparams("parallel"),
    )(u, u, u, gq, gkv, wqa, wqb, wka, sela, selb, wv, *tabs)


def _attn_kernel(q_ref, k_ref, v_ref, o_ref, *, n_ctx, n_all):
    tq = q_ref.shape[0]

    def run(r0, rows, nk):
        lane = lax.broadcasted_iota(jnp.int32, (rows, 2 * MLA_V), 1)
        outs = []
        for hh in range(2):
            sl = slice(hh * MLA_SLOT, (hh + 1) * MLA_SLOT)
            s = _dot_nt(q_ref[r0:r0 + rows, sl], k_ref[0:nk, sl])
            p = jnp.exp(s - jnp.max(s, axis=-1, keepdims=True))
            inv = 1.0 / jnp.sum(p, axis=-1, keepdims=True)
            outs.append(_dot(p.astype(BF16), v_ref[0:nk, :]) * inv)
        o_ref[r0:r0 + rows, :] = jnp.where(lane < MLA_V, outs[0], outs[1]).astype(o_ref.dtype)

    @pl.when(pl.program_id(2) == 0)
    def _():
        run(0, n_ctx, n_ctx)
        if tq > n_ctx:
            run(n_ctx, tq - n_ctx, n_all)

    @pl.when(pl.program_id(2) > 0)
    def _():
        run(0, tq, n_all)


def _mla_attention(geo, q, k, v):
    n, bsz = geo.n, geo.b
    groups = [g for g in range(1, geo.nt + 1) if geo.nt % g == 0 and g * geo.tr >= geo.c]
    tq = geo.tr * min(groups, key=lambda g: abs(g * geo.tr - ATTN_ROWS))
    q3, k3, v3 = (t.reshape(bsz, n, t.shape[-1]) for t in (q, k, v))
    kern = functools.partial(_attn_kernel, n_ctx=geo.c, n_all=n)
    out = pl.pallas_call(
        kern,
        out_shape=jax.ShapeDtypeStruct((bsz, n, MLA_HEADS * MLA_V), BF16),
        grid=(bsz, MLA_HEADS // 2, n // tq),
        in_specs=[pl.BlockSpec((None, tq, 2 * MLA_SLOT), lambda b, hp, i: (b, i, hp)),
                  pl.BlockSpec((None, n, 2 * MLA_SLOT), lambda b, hp, i: (b, 0, hp)),
                  pl.BlockSpec((None, n, 2 * MLA_V), lambda b, hp, i: (b, 0, hp))],
        out_specs=pl.BlockSpec((None, tq, 2 * MLA_V), lambda b, hp, i: (b, i, hp)),
        name="mla_attn",
        compiler_params=_cparams("parallel", "parallel", "parallel"),
    )(q3, k3, v3)
    return out.reshape(geo.m, MLA_HEADS * MLA_V)


def _halo_flags(geo, i):
    j = i % geo.nt
    left = jnp.logical_and(j != 0, j != geo.nct)
    right = jnp.logical_and(j != geo.nct - 1, j != geo.nt - 1)
    return left.astype(F32), right.astype(F32)


def _conv_shifts(n_taps):
    base = CONV_HALO - n_taps // 2
    return sorted({(base + kk) % SUBLANES for kk in range(n_taps)} - {0})


def _conv_taps(pad_ref, sh_ref, w_ref, n_taps, rows):
    base = CONV_HALO - n_taps // 2
    shifts = _conv_shifts(n_taps)
    span = sh_ref.shape[1]
    for s, r in enumerate(shifts):
        sh_ref[s, :, :] = pad_ref[r:r + span, :]

    def window(o):
        a, r = o - o % SUBLANES, o % SUBLANES
        return pad_ref[a:a + rows, :] if r == 0 else sh_ref[shifts.index(r), a:a + rows, :]

    acc = w_ref[0:1, :] * window(base)
    for kk in range(1, n_taps):
        acc = acc + w_ref[kk:kk + 1, :] * window(base + kk)
    return acc


def _conv_scratch(tr, width, n_taps):
    return [pltpu.VMEM((tr + 2 * CONV_HALO, width), F32),
            pltpu.VMEM((len(_conv_shifts(n_taps)), tr + 2 * CONV_HALO - SUBLANES, width), F32)]


def _conf_kernel(a_ref, b_ref, ap_ref, bp_ref, an_ref, bn_ref, w_ref, bias_ref, g_ref, beta_ref, o_ref,
                 pad_ref, sh_ref, *, geo):
    left, right = _halo_flags(geo, pl.program_id(0))
    rows = a_ref.shape[0]
    glu = lambda a, b: a[...].astype(F32) * jax.nn.sigmoid(b[...].astype(F32))
    pad_ref[0:CONV_HALO, :] = glu(ap_ref, bp_ref) * left
    pad_ref[CONV_HALO:CONV_HALO + rows, :] = glu(a_ref, b_ref)
    pad_ref[CONV_HALO + rows:2 * CONV_HALO + rows, :] = glu(an_ref, bn_ref) * right
    h = _conv_taps(pad_ref, sh_ref, w_ref, CF_KERNEL, rows) + bias_ref[...]
    hc = h - jnp.mean(h, axis=-1, keepdims=True)
    y = hc * lax.rsqrt(jnp.mean(hc * hc, axis=-1, keepdims=True) + EPS) * g_ref[...] + beta_ref[...]
    o_ref[...] = (y * jax.nn.sigmoid(y)).astype(o_ref.dtype)


def _sconv_kernel(gb_ref, gc_ref, h_ref, gcp_ref, hp_ref, gcn_ref, hn_ref, w_ref, o_ref, pad_ref, sh_ref, *, geo):
    left, right = _halo_flags(geo, pl.program_id(0))
    rows = h_ref.shape[0]
    prod = lambda a, b: a[...].astype(F32) * b[...].astype(F32)
    pad_ref[0:CONV_HALO, :] = prod(gcp_ref, hp_ref) * left
    pad_ref[CONV_HALO:CONV_HALO + rows, :] = prod(gc_ref, h_ref)
    pad_ref[CONV_HALO + rows:2 * CONV_HALO + rows, :] = prod(gcn_ref, hn_ref) * right
    y = _conv_taps(pad_ref, sh_ref, w_ref, SC_KERNEL, rows)
    o_ref[...] = (gb_ref[...].astype(F32) * y).astype(o_ref.dtype)


def _halo_specs(geo, width, offset):
    tr = geo.tr
    per = tr // CONV_HALO
    last = geo.m // CONV_HALO - 1
    cur = pl.BlockSpec((tr, width), lambda i: (i, offset // width))
    prev = pl.BlockSpec((CONV_HALO, width), lambda i: (jnp.maximum(i * per - 1, 0), offset // width))
    nxt = pl.BlockSpec((CONV_HALO, width), lambda i: (jnp.minimum((i + 1) * per, last), offset // width))
    return cur, prev, nxt


N_CONF_IN, N_SCONV_IN = 10, 8


def _convs_kernel(*refs, geo):
    n_in = N_CONF_IN + N_SCONV_IN
    conf_o, sconv_o = refs[n_in], refs[n_in + 1]
    conf_scratch, sconv_scratch = refs[n_in + 2:n_in + 4], refs[n_in + 4:n_in + 6]
    _conf_kernel(*refs[:N_CONF_IN], conf_o, *conf_scratch, geo=geo)
    _sconv_kernel(*refs[N_CONF_IN:n_in], sconv_o, *sconv_scratch, geo=geo)


def _conv_branches(geo, u, off, layer, w_dw, b_dw, g_ln, b_ln, w_conv):
    tr = geo.tr
    a_c, a_p, a_n = _halo_specs(geo, CF_WIDTH, off["cf_a"])
    b_c, b_p, b_n = _halo_specs(geo, CF_WIDTH, off["cf_b"])
    gb_c, _, _ = _halo_specs(geo, SC_WIDTH, off["sc_b"])
    gc_c, gc_p, gc_n = _halo_specs(geo, SC_WIDTH, off["sc_c"])
    h_c, h_p, h_n = _halo_specs(geo, SC_WIDTH, off["sc_h"])
    lw = lambda a: pl.BlockSpec((None,) + a.shape[1:], lambda i: (layer,) + (0,) * (a.ndim - 1))
    in_specs = [a_c, b_c, a_p, b_p, a_n, b_n, lw(w_dw), lw(b_dw), lw(g_ln), lw(b_ln),
                gb_c, gc_c, h_c, gc_p, h_p, gc_n, h_n, lw(w_conv)]
    assert len(in_specs) == N_CONF_IN + N_SCONV_IN
    return pl.pallas_call(
        functools.partial(_convs_kernel, geo=geo),
        out_shape=(jax.ShapeDtypeStruct((geo.m, CF_WIDTH), BF16), jax.ShapeDtypeStruct((geo.m, SC_WIDTH), BF16)),
        grid=(geo.m // tr,),
        in_specs=in_specs,
        out_specs=(pl.BlockSpec((tr, CF_WIDTH), lambda i: (i, 0)), pl.BlockSpec((tr, SC_WIDTH), lambda i: (i, 0))),
        scratch_shapes=_conv_scratch(tr, CF_WIDTH, CF_KERNEL) + _conv_scratch(tr, SC_WIDTH, SC_KERNEL),
        name="conv_branches",
        compiler_params=_cparams("parallel"),
    )(*([u] * 6), w_dw, b_dw, g_ln, b_ln, *([u] * 7), w_conv)


def _gla_tables():
    c = GLA_CHUNK
    n_rows = (GLA_LEVELS + 2) * c + 8
    mat = np.zeros((2, n_rows, c), np.float32)
    msk = np.zeros((2, GLA_LEVELS + 1, c, c), np.float32)
    for lvl in range(GLA_LEVELS):
        hs = (c // 2) >> lvl
        for t in range(c):
            mid = (t // (2 * hs)) * 2 * hs + hs
            if t >= mid:
                mat[0, lvl * c + t, mid + 1:t + 1] = 1.0
                msk[0, lvl, t, mid - hs:mid] = 1.0
            else:
                mat[0, lvl * c + t, t + 1:mid + 1] = 1.0
    msk[0, GLA_LEVELS] = np.eye(c, dtype=np.float32)
    for t in range(c):
        mat[0, GLA_LEVELS * c + t, :t + 1] = 1.0
        mat[0, (GLA_LEVELS + 1) * c + t, t + 1:] = 1.0
    mat[0, (GLA_LEVELS + 2) * c:, :] = 1.0
    n_blk = GLA_LEVELS + 2
    mat[1, :n_blk * c] = mat[0, :n_blk * c].reshape(n_blk, c, c)[:, ::-1, ::-1].reshape(n_blk * c, c)
    mat[1, n_blk * c:] = 1.0
    msk[1] = msk[0][:, ::-1, ::-1]
    mat3 = np.concatenate([mat, mat, mat], axis=2)
    msk4 = np.tile(msk, (1, 1, GLA_HEADS, 1))
    return jnp.asarray(mat3, BF16), jnp.asarray(msk4, F32)


def _gla_kernel(qf_ref, kf_ref, vf_ref, gf_ref, qb_ref, kb_ref, vb_ref, gb_ref, w2_ref, b2_ref, mat_ref, msk_ref,
                of_ref, ob_ref, st_ref):
    @pl.when(pl.program_id(0) == 0)
    def _():
        st_ref[...] = jnp.zeros_like(st_ref)

    ins = ((qf_ref, kf_ref, vf_ref, gf_ref, of_ref), (qb_ref, kb_ref, vb_ref, gb_ref, ob_ref))
    for b in range(qf_ref.shape[0]):
        for d, (q_ref, k_ref, v_ref, g_ref, o_ref) in enumerate(ins):
            _gla_chunk(q_ref.at[b], k_ref.at[b], v_ref.at[b], g_ref.at[b], w2_ref.at[d], b2_ref.at[d],
                       mat_ref.at[d], msk_ref.at[d], o_ref.at[b], st_ref.at[d, b])


def _gla_chunk(q_ref, k_ref, v_ref, g_ref, w2_ref, b2_ref, mat_ref, msk_ref, o_ref, st_ref):
    c = GLA_CHUNK
    z = _dot(g_ref[...], w2_ref[...]) + b2_ref[...]
    la = (jnp.minimum(z, 0.0) - jnp.log1p(jnp.exp(-jnp.abs(z)))) * (1.0 / GLA_GATE_TAU)
    hi = la.astype(BF16)
    r1 = la - hi.astype(F32)
    lo = r1.astype(BF16)
    lo2 = (r1 - lo.astype(F32)).astype(BF16)
    ea = jnp.exp(_dot(mat_ref[...], jnp.concatenate([hi, lo, lo2], axis=0)))

    head = lax.broadcasted_iota(jnp.int32, (1, GLA_HEADS * GLA_DK), 1) // GLA_DK
    zero = jnp.zeros((), BF16)

    def stack(x):
        xb = x.astype(BF16)
        return jnp.concatenate([jnp.where(head == h, xb, zero) for h in range(GLA_HEADS)], axis=0)

    q = q_ref[...].astype(F32) * (GLA_DK ** -0.5)
    k = k_ref[...].astype(F32)
    vb = v_ref[...]

    attn = _dot_nt(stack(q), k.astype(BF16)) * msk_ref[GLA_LEVELS]
    for lvl in range(GLA_LEVELS):
        e = ea[lvl * c:(lvl + 1) * c]
        attn = attn + _dot_nt(stack(q * e), (k * e).astype(BF16)) * msk_ref[lvl]
    o_intra = _dot(attn.astype(BF16), vb)

    st = st_ref[...]
    o_inter = _dot_nt(stack(q * ea[GLA_LEVELS * c:(GLA_LEVELS + 1) * c]), st.astype(BF16))
    for h in range(GLA_HEADS):
        rows = slice(h * c, (h + 1) * c)
        cols = slice(h * GLA_DV, (h + 1) * GLA_DV)
        o_ref[:, cols] = o_intra[rows, cols] + o_inter[rows, :]

    k_dec = stack(k * ea[(GLA_LEVELS + 1) * c:(GLA_LEVELS + 2) * c])
    v_rows = jnp.concatenate([vb[:, h * GLA_DV:(h + 1) * GLA_DV] for h in range(GLA_HEADS)], axis=0)
    eye = (lax.broadcasted_iota(jnp.int32, (GLA_DV, GLA_DV), 0)
           == lax.broadcasted_iota(jnp.int32, (GLA_DV, GLA_DV), 1)).astype(BF16)
    v_t = _dot_nt(eye, v_rows).astype(BF16)
    tot = ea[(GLA_LEVELS + 2) * c:(GLA_LEVELS + 2) * c + 1]
    st_ref[...] = st * tot + _dot(v_t, k_dec)


def _gla_scan(geo, u, off, layer, w2, b2, mat3, msk4):
    c = GLA_CHUNK
    ncc = geo.c // c
    nc = geo.n // c
    dk, dv = GLA_HEADS * GLA_DK, GLA_HEADS * GLA_DV

    fwd = lambda i: i
    bwd = lambda i: jnp.where(i < ncc, ncc - 1 - i, nc - 1 - (i - ncc))

    u3 = u.reshape(geo.b, geo.n, u.shape[-1])
    row = lambda w, o, chunk: pl.BlockSpec((geo.b, c, w), lambda i: (0, chunk(i), o // w))
    rows = lambda chunk: [row(dk, off["gla_q"], chunk), row(dk, off["gla_k"], chunk), row(dv, off["gla_v"], chunk),
                          row(128, off["gate"], chunk)]
    out_sds = jax.ShapeDtypeStruct((geo.b, geo.n, dv), F32)
    o_f, o_b = pl.pallas_call(
        _gla_kernel,
        out_shape=(out_sds, out_sds),
        grid=(nc,),
        in_specs=rows(fwd) + rows(bwd) + [
            pl.BlockSpec((None, 2, 128, dk), lambda i: (layer, 0, 0, 0)),
            pl.BlockSpec((None, 2, 1, dk), lambda i: (layer, 0, 0, 0)),
            pl.BlockSpec(mat3.shape, lambda i: (0, 0, 0)),
            pl.BlockSpec(msk4.shape, lambda i: (0, 0, 0, 0))],
        out_specs=(pl.BlockSpec((geo.b, c, dv), lambda i: (0, fwd(i), 0)),
                   pl.BlockSpec((geo.b, c, dv), lambda i: (0, bwd(i), 0))),
        scratch_shapes=[pltpu.VMEM((2, geo.b, GLA_DV, dk), F32)],
        name="gla_scan",
        compiler_params=_cparams("arbitrary"),
    )(u3, u3, u3, u3, u3, u3, u3, u3, w2, b2, mat3, msk4)
    return o_f.reshape(geo.m, dv), o_b.reshape(geo.m, dv)


def _merge_mix_kernel(attn_ref, conf_ref, glaf_ref, glab_ref, r_ref, gn_ref, sc_ref, gate_ref, w0_ref, w1_ref,
                      w2_ref, w3_ref, wmix_ref, x_ref, gt_ref, gf_ref, shf_ref, scf_ref, wr_ref, br_ref, tri_ref,
                      o_ref, h_ref, idx_ref, wgt_ref, rank_ref, cnt_ref, y_ref, carry_ref):
    @pl.when(pl.program_id(0) == 0)
    def _():
        carry_ref[...] = jnp.zeros_like(carry_ref)

    wbr_ref = (w0_ref, w1_ref, w2_ref, w3_ref)
    o = glaf_ref[...] + glab_ref[...]
    r = r_ref[...].astype(F32)
    gn = gn_ref[...]
    normed = [_rms(o[:, h * GLA_DV:(h + 1) * GLA_DV], gn) for h in range(GLA_HEADS)]
    gla = (jnp.concatenate(normed, axis=1) * (r * jax.nn.sigmoid(r))).astype(BF16)
    acts = (attn_ref[...], conf_ref[...], gla, sc_ref[...])
    d = x_ref.shape[1]
    cw = min(MERGE_COLS, d)
    for c0 in range(0, d, cw):
        y = None
        for i in range(N_BRANCHES):
            half_b = _dot(acts[i], wbr_ref[i][:, c0:c0 + cw])
            t = jnp.tanh(gate_ref[:, i * d + c0:i * d + c0 + cw].astype(F32))
            term = t * half_b + half_b
            y = term if y is None else y + term
        y_ref[:, c0:c0 + cw] = y.astype(BF16)
    y_all = y_ref[...]
    for c0 in range(0, d, cw):
        cols = slice(c0, c0 + cw)
        o_ref[:, cols] = x_ref[:, cols] + gt_ref[:, cols] * _dot(y_all, wmix_ref[:, cols])
    _route_rows(o_ref[...], gf_ref, shf_ref, scf_ref, wr_ref, br_ref, tri_ref, h_ref, idx_ref, wgt_ref, rank_ref,
                cnt_ref, carry_ref)


def _merge_mix(geo, attn, conf, gla_f, gla_b, u, u_gates, off, g_norm, sconv, w_br_half, w_mix, x, mod, layer,
               g_ffn, wr_stack, b_router):
    tm, d, m = geo.tr, geo.d, geo.m
    w512 = GLA_HEADS * GLA_DV
    nch = d // 256
    act = pl.BlockSpec((tm, w512), lambda i: (i, 0))
    once = pl.Buffered(1)
    w_br = pl.BlockSpec((None, w512, d), lambda i: (layer, 0, 0), pipeline_mode=once)
    tri = jnp.asarray(np.triu(np.ones((tm, tm), np.float32), 1), BF16)
    pair = lambda dt: jax.ShapeDtypeStruct((2, m), dt)
    pair_spec = pl.BlockSpec((2, tm), lambda i: (0, i))
    return pl.pallas_call(
        _merge_mix_kernel,
        out_shape=(jax.ShapeDtypeStruct((m, d), F32), jax.ShapeDtypeStruct((m * nch, 128), jnp.uint32),
                   pair(jnp.int32), pair(F32), pair(jnp.int32), jax.ShapeDtypeStruct((N_EXPERTS, 1), F32)),
        grid=(m // tm,),
        in_specs=[act, act, act, act,
                  pl.BlockSpec((tm, w512), lambda i: (i, off["gla_r"] // w512)),
                  pl.BlockSpec((None, 1, GLA_DV), lambda i: (layer, 0, 0)),
                  act, pl.BlockSpec((tm, N_BRANCHES * d), lambda i: (i, 0)),
                  w_br, w_br, w_br, w_br,
                  pl.BlockSpec((None, d, d), lambda i: (layer, 0, 0), pipeline_mode=once),
                  pl.BlockSpec((tm, d), lambda i: (i, 0)), geo.mod_spec(layer, 2),
                  pl.BlockSpec((None, 1, d), lambda i: (layer, 0, 0)),
                  geo.mod_spec(layer, 3), geo.mod_spec(layer, 4),
                  pl.BlockSpec(wr_stack.shape, lambda i: (0, 0)),
                  pl.BlockSpec(b_router.shape, lambda i: (0, 0)),
                  pl.BlockSpec((tm, tm), lambda i: (0, 0))],
        out_specs=(pl.BlockSpec((tm, d), lambda i: (i, 0)), pl.BlockSpec((tm * nch, 128), lambda i: (i, 0)),
                   pair_spec, pair_spec, pair_spec, pl.BlockSpec((N_EXPERTS, 1), lambda i: (0, 0))),
        scratch_shapes=[pltpu.VMEM((tm, d), BF16), pltpu.VMEM((N_EXPERTS, 1), F32)],
        name="merge_mix",
        compiler_params=_cparams("arbitrary"),
    )(attn, conf, gla_f, gla_b, u, g_norm, sconv, u_gates, *w_br_half, w_mix, x, mod, g_ffn, mod, mod, wr_stack,
      b_router, tri)


def _cast_kernel(w_ref, o_ref, *, scale):
    o_ref[...] = (w_ref[...] * scale).astype(o_ref.dtype)


def _to_bf16(w, scale=1.0):
    n_l, rows, cols = w.shape
    tr = _pick(rows, (512, 256, 128))
    return pl.pallas_call(
        functools.partial(_cast_kernel, scale=scale),
        out_shape=jax.ShapeDtypeStruct(w.shape, BF16),
        grid=(n_l, rows // tr),
        in_specs=[pl.BlockSpec((None, tr, cols), lambda l, i: (l, i, 0))],
        out_specs=pl.BlockSpec((None, tr, cols), lambda l, i: (l, i, 0)),
        name="weight_cast",
        compiler_params=_cparams("parallel", "parallel"),
    )(w)


def _second_largest(a, b, c, d):
    return jnp.maximum(jnp.maximum(jnp.minimum(a, b), jnp.minimum(c, d)),
                       jnp.minimum(jnp.maximum(a, b), jnp.maximum(c, d)))


def _chunk_rows(rows, n, cc):
    return pl.ds(cc, rows, stride=n) if n > 1 else pl.ds(0, rows)


def _pack_rows(ref, y):
    rows = y.shape[0]
    n = ref.shape[0] // rows
    half = n * 128
    bits = lambda v: pltpu.bitcast(v.astype(BF16).astype(F32), jnp.uint32)
    packed = (bits(y[:, :half]) >> 16) | bits(y[:, half:])
    for cc in range(n):
        ref[_chunk_rows(rows, n, cc), :] = packed[:, cc * 128:(cc + 1) * 128]


def _unpack_cols(words):
    lo = pltpu.bitcast(words << 16, F32)
    hi = pltpu.bitcast(words & jnp.uint32(0xFFFF0000), F32)
    return lo, hi


def _route_rows(x, g_ref, sh_ref, sc_ref, wr_ref, br_ref, tri_ref, h_ref, idx_ref, wgt_ref, rank_ref, cnt_ref,
                carry_ref):
    y = _rms(x, g_ref[...]) * (1.0 + sc_ref[...]) + sh_ref[...]
    hi = y.astype(BF16)
    lo = (y - hi.astype(F32)).astype(BF16)
    _pack_rows(h_ref, y)
    ne = N_EXPERTS
    two = _dot_nt(wr_ref[...], hi)
    logits = two[0:ne] + two[ne:2 * ne] + _dot_nt(wr_ref[0:ne, :], lo)
    aff = 1.0 / (1.0 + jnp.exp(-logits))
    sel = aff + br_ref[...]
    per = ne // N_GROUPS
    gs = []
    for g in range(N_GROUPS):
        a, b, c, d = (sel[g * per + r:g * per + r + 1] for r in range(per))
        top1 = jnp.maximum(jnp.maximum(a, b), jnp.maximum(c, d))
        gs.append(top1 + _second_largest(a, b, c, d))
    best = gs[0]
    for g in range(1, N_GROUPS):
        best = jnp.maximum(best, gs[g])
    grp = jnp.full(best.shape, N_GROUPS, jnp.int32)
    for g in reversed(range(N_GROUPS)):
        grp = jnp.where(gs[g] == best, g, grp)
    erow = lax.broadcasted_iota(jnp.int32, sel.shape, 0)
    masked = jnp.where(erow // per == grp, sel, -jnp.inf)
    m1 = jnp.max(masked, axis=0, keepdims=True)
    i1 = jnp.min(jnp.where(masked == m1, erow, ne), axis=0, keepdims=True)
    one1 = erow == i1
    masked2 = jnp.where(one1, -jnp.inf, masked)
    m2 = jnp.max(masked2, axis=0, keepdims=True)
    i2 = jnp.min(jnp.where(masked2 == m2, erow, ne), axis=0, keepdims=True)
    one2 = erow == i2
    w1 = jnp.sum(jnp.where(one1, aff, 0.0), axis=0, keepdims=True)
    w2 = jnp.sum(jnp.where(one2, aff, 0.0), axis=0, keepdims=True)
    inv = 1.0 / (w1 + w2)
    idx_ref[0:1, :] = i1
    idx_ref[1:2, :] = i2
    wgt_ref[0:1, :] = w1 * inv
    wgt_ref[1:2, :] = w2 * inv
    o1 = jnp.where(one1, 1.0, 0.0)
    o2 = jnp.where(one2, 1.0, 0.0)
    tri = tri_ref[...]
    c0 = carry_ref[...]
    c1 = c0 + jnp.sum(o1, axis=1, keepdims=True)
    rank_ref[0:1, :] = jnp.sum(o1 * (c0 + _dot(o1.astype(BF16), tri)), axis=0, keepdims=True).astype(jnp.int32)
    rank_ref[1:2, :] = jnp.sum(o2 * (c1 + _dot(o2.astype(BF16), tri)), axis=0, keepdims=True).astype(jnp.int32)
    c2 = c1 + jnp.sum(o2, axis=1, keepdims=True)
    carry_ref[...] = c2
    cnt_ref[...] = c2


def _dispatch_kernel(dest_ref, last_ref, h_ref, xs_hbm, zero_ref, sem, zero_sem, *, m, n):
    i = pl.program_id(0)
    tr = h_ref.shape[0] // n

    @pl.when(i == 0)
    def _():
        zero_ref[...] = jnp.zeros_like(zero_ref)

        def fills(e):
            rows = zero_ref.shape[0]
            start = pl.multiple_of(last_ref[e] * n, rows)
            return pltpu.make_async_copy(zero_ref, xs_hbm.at[pl.ds(start, rows), :], zero_sem)

        for e in range(2 * N_EXPERTS):
            @pl.when(last_ref[e] >= 0)
            def _():
                fills(e).start()
        for e in range(2 * N_EXPERTS):
            @pl.when(last_ref[e] >= 0)
            def _():
                fills(e).wait()

    def issue(r, carry):
        src = h_ref.at[pl.ds(pl.multiple_of(r * n, n), n), :]
        for k in range(2):
            row = pl.multiple_of(dest_ref[k * m + i * tr + r], n)
            pltpu.make_async_copy(src, xs_hbm.at[pl.ds(row, n), :], sem).start()
        return carry

    lax.fori_loop(0, tr, issue, 0, unroll=8)
    everything = xs_hbm.at[pl.ds(0, 2 * tr * n), :]
    pltpu.make_async_copy(everything, everything, sem).wait()


def _dispatch(hfp, dest, last_tile_row, n_rows, tr, tm):
    n = hfp.shape[0] * 2 // dest.shape[0]
    m = hfp.shape[0] // n
    return pl.pallas_call(
        functools.partial(_dispatch_kernel, m=m, n=n),
        out_shape=jax.ShapeDtypeStruct((n_rows * n, 128), jnp.uint32),
        grid_spec=pltpu.PrefetchScalarGridSpec(
            num_scalar_prefetch=2, grid=(m // tr,),
            in_specs=[pl.BlockSpec((tr * n, 128), lambda i, *_: (i, 0))],
            out_specs=pl.BlockSpec(memory_space=pl.ANY),
            scratch_shapes=[pltpu.VMEM((tm * n, 128), jnp.uint32), pltpu.SemaphoreType.DMA(()),
                            pltpu.SemaphoreType.DMA(())]),
        name="moe_dispatch",
        compiler_params=_cparams("arbitrary"),
    )(dest, last_tile_row, hfp)


def _gmm_kernel(te_ref, nv_ref, first_ref, ord_ref, next_ref, xs_ref, wg_hbm, wu_hbm, wd_hbm, ys_ref, fg_ref, fu_ref,
                fd_ref, wgb_ref, wub_ref, wdb_ref, sem_ref, *, layer):
    j = pl.program_id(0)

    def fetch(e, slot):
        return [pltpu.make_async_copy(src.at[layer, e], dst.at[slot], sem_ref.at[slot])
                for src, dst in ((wg_hbm, fg_ref), (wu_hbm, fu_ref), (wd_hbm, fd_ref))]

    @pl.when(j == 0)
    def _():
        for cp in fetch(te_ref[0], 0):
            cp.start()

    @pl.when(j < nv_ref[0])
    def _():
        e = te_ref[j]

        @pl.when(j == first_ref[e])
        def _():
            slot = ord_ref[e] % 2
            for cp in fetch(e, slot):
                cp.wait()
            wgb_ref[...] = fg_ref[slot].astype(BF16)
            wub_ref[...] = fu_ref[slot].astype(BF16)
            wdb_ref[...] = fd_ref[slot].astype(BF16)

            @pl.when(next_ref[e] >= 0)
            def _():
                for cp in fetch(next_ref[e], 1 - slot):
                    cp.start()

        half = wgb_ref.shape[0] // 2
        n = half // 128
        tm = xs_ref.shape[0] // n
        parts = [_unpack_cols(xs_ref[_chunk_rows(tm, n, cc), :]) for cc in range(n)]
        xa = jnp.concatenate([p[0].astype(BF16) for p in parts], axis=1)
        xb = jnp.concatenate([p[1].astype(BF16) for p in parts], axis=1)
        g = _dot(xa, wgb_ref[0:half, :]) + _dot(xb, wgb_ref[half:2 * half, :])
        u = _dot(xa, wub_ref[0:half, :]) + _dot(xb, wub_ref[half:2 * half, :])
        act = (g * jax.nn.sigmoid(g) * u).astype(BF16)
        _pack_rows(ys_ref, _dot(act, wdb_ref[...]))

    @pl.when(j >= nv_ref[0])
    def _():
        ys_ref[...] = jnp.zeros_like(ys_ref)


def _gmm(xs, tile_expert, n_valid, first_tile, ordinal, next_expert, wg, wu, wd, layer, tm):
    d, ff = wg.shape[-2], wg.shape[-1]
    nch = d // 256
    blk = tm * nch
    any_spec = pl.BlockSpec(memory_space=pl.ANY)
    return pl.pallas_call(
        functools.partial(_gmm_kernel, layer=layer),
        out_shape=jax.ShapeDtypeStruct(xs.shape, jnp.uint32),
        grid_spec=pltpu.PrefetchScalarGridSpec(
            num_scalar_prefetch=5, grid=(xs.shape[0] // blk,),
            in_specs=[pl.BlockSpec((blk, 128), lambda j, te, nv, *_: (jnp.minimum(j, nv[0] - 1), 0)),
                      any_spec, any_spec, any_spec],
            out_specs=pl.BlockSpec((blk, 128), lambda j, *_: (j, 0)),
            scratch_shapes=[pltpu.VMEM((2, d, ff), F32), pltpu.VMEM((2, d, ff), F32), pltpu.VMEM((2, ff, d), F32),
                            pltpu.VMEM((d, ff), BF16), pltpu.VMEM((d, ff), BF16), pltpu.VMEM((ff, d), BF16),
                            pltpu.SemaphoreType.DMA((2,))]),
        name="moe_experts",
        compiler_params=_cparams("arbitrary"),
    )(tile_expert, n_valid, first_tile, ordinal, next_expert, xs, wg, wu, wd)


def _combine_kernel(dest_ref, x_ref, w_ref, gt_ref, ys_hbm, *rest, m, last):
    if last:
        g_ref, h_ref, o_ref, buf_ref, sem_ref = rest
    else:
        g_ref, sh_ref, sc_ref, o_ref, h_ref, buf_ref, sem_ref = rest
    i = pl.program_id(0)
    n_tiles = pl.num_programs(0)
    tr = x_ref.shape[0]
    n = buf_ref.shape[2] // tr

    def issue(tile, slot):
        def body(r, carry):
            for k in range(2):
                row = pl.multiple_of(dest_ref[k * m + tile * tr + r], n)
                pltpu.make_async_copy(ys_hbm.at[pl.ds(row, n), :],
                                      buf_ref.at[slot, k, pl.ds(pl.multiple_of(r * n, n), n), :],
                                      sem_ref.at[slot]).start()
            return carry
        lax.fori_loop(0, tr, body, 0, unroll=8)

    @pl.when(i == 0)
    def _():
        issue(0, 0)

    @pl.when(i + 1 < n_tiles)
    def _():
        issue(i + 1, (i + 1) % 2)

    slot = i % 2
    pltpu.make_async_copy(buf_ref.at[slot], buf_ref.at[slot], sem_ref.at[slot]).wait()
    w1, w2 = w_ref[:, 0:1], w_ref[:, 1:2]
    half = n * 128
    for cc in range(n):
        lo1, hi1 = _unpack_cols(buf_ref[slot, 0, _chunk_rows(tr, n, cc), :])
        lo2, hi2 = _unpack_cols(buf_ref[slot, 1, _chunk_rows(tr, n, cc), :])
        ca = slice(cc * 128, (cc + 1) * 128)
        cb = slice(half + cc * 128, half + (cc + 1) * 128)
        o_ref[:, ca] = x_ref[:, ca] + gt_ref[:, ca] * (w1 * lo1 + w2 * lo2)
        o_ref[:, cb] = x_ref[:, cb] + gt_ref[:, cb] * (w1 * hi1 + w2 * hi2)
    if last:
        h_ref[...] = _rms(o_ref[...], g_ref[...])
    else:
        y = _rms(o_ref[...], g_ref[...]) * (1.0 + sc_ref[...]) + sh_ref[...]
        h_ref[...] = y.astype(h_ref.dtype)


def _combine(geo, x, ys, dest, w12, mod, layer, g_next, g_final):
    tr, d = geo.tr, geo.d
    nch = d // 256
    last = g_next is None
    tile = pl.BlockSpec((tr, d), lambda i, *_: (i, 0))
    in_specs = [tile, pl.BlockSpec((tr, 2), lambda i, *_: (i, 0)), geo.mod_spec(layer, 5),
                pl.BlockSpec(memory_space=pl.ANY)]
    operands = [x, w12, mod, ys]
    scratch = [pltpu.VMEM((2, 2, tr * nch, 128), jnp.uint32), pltpu.SemaphoreType.DMA((2,))]
    if last:
        in_specs.append(pl.BlockSpec((1, d), lambda i, *_: (0, 0)))
        operands.append(g_final.reshape(1, d))
        out_shape = jax.ShapeDtypeStruct((geo.b, geo.s, d), F32)
        out_specs = pl.BlockSpec((None, tr, d), lambda i, *_: (i // geo.nt, jnp.maximum(i % geo.nt - geo.nct, 0), 0))
        scratch = [pltpu.VMEM((tr, d), F32)] + scratch
    else:
        in_specs += [pl.BlockSpec((None, 1, d), lambda i, *_: (layer + 1, 0, 0)),
                     geo.mod_spec(layer + 1, 0), geo.mod_spec(layer + 1, 1)]
        operands += [g_next, mod, mod]
        out_shape = [jax.ShapeDtypeStruct((geo.m, d), F32), jax.ShapeDtypeStruct((geo.m, d), BF16)]
        out_specs = [tile, tile]
    return pl.pallas_call(
        functools.partial(_combine_kernel, m=geo.m, last=last),
        out_shape=out_shape,
        grid_spec=pltpu.PrefetchScalarGridSpec(
            num_scalar_prefetch=1, grid=(geo.m // tr,),
            in_specs=in_specs, out_specs=out_specs, scratch_shapes=scratch),
        name="moe_combine",
        compiler_params=_cparams("arbitrary"),
    )(dest, *operands)


def _moe(geo, x, routed, mod, layer, wg, wu, wd, g_next, g_final):
    tm = 256
    hfp, ridx, rwgt, rank, cnt = routed
    n_tiles = (2 * geo.m) // tm + N_EXPERTS
    counts = cnt[:, 0].astype(jnp.int32)
    tiles_per = (counts + tm - 1) // tm
    tile_start = jnp.cumsum(tiles_per) - tiles_per
    base = tile_start * tm
    n_valid = jnp.sum(tiles_per).reshape(1)
    tile_expert = jnp.sum(jnp.arange(n_tiles)[:, None] >= tile_start[None, :], axis=1).astype(jnp.int32) - 1
    experts = jnp.arange(N_EXPERTS, dtype=jnp.int32)[:, None, None]
    dest = (jnp.sum(jnp.where(ridx[None] == experts, base[:, None, None], 0), axis=0) + rank).reshape(-1)
    dest = dest * (geo.d // 256)
    tail = n_valid + jnp.arange(N_EXPERTS)
    fill_rows = jnp.concatenate([jnp.where(tiles_per > 0, base + (tiles_per - 1) * tm, -1),
                                 jnp.where(tail < n_tiles, tail * tm, -1)])
    xs = _dispatch(hfp, dest, fill_rows, n_tiles * tm, geo.tr, tm)
    has_rows = tiles_per > 0
    ordinal = jnp.cumsum(has_rows.astype(jnp.int32)) - 1
    ar = jnp.arange(N_EXPERTS)
    later = jnp.where((ar[None, :] > ar[:, None]) & has_rows[None, :], ar[None, :], N_EXPERTS)
    next_expert = jnp.min(later, axis=1)
    next_expert = jnp.where(next_expert < N_EXPERTS, next_expert, -1).astype(jnp.int32)
    ys = _gmm(xs, tile_expert, n_valid, tile_start, ordinal, next_expert, wg, wu, wd, layer, tm)
    return _combine(geo, x, ys, dest, rwgt.T, mod, layer, g_next, g_final)


def _proj_plans(d):
    src = _src_layout(d)
    off, total = _proj_layout()
    width = lambda name: src[name][1] - src[name][0]
    cf0, sc0 = src["cf"][0], src["sc"][0]
    assert src["gb"][0] == src["gf"][1]
    pieces = [(off["cq"], src["cq"][0], width("cq")), (off["ckv"], src["ckv"][0], width("ckv")),
              (off["kr"], src["kr"][0], width("kr")),
              (off["cf_a"], cf0, CF_WIDTH), (off["cf_b"], cf0 + CF_WIDTH, CF_WIDTH),
              (off["gla_q"], src["gla_q"][0], width("gla_q")), (off["gla_k"], src["gla_k"][0], width("gla_k")),
              (off["gla_v"], src["gla_v"][0], width("gla_v")), (off["gate"], src["gf"][0], 2 * GLA_GATE_RANK),
              (off["gla_r"], src["gla_r"][0], width("gla_r")),
              (off["sc_b"], sc0, SC_WIDTH), (off["sc_c"], sc0 + SC_WIDTH, SC_WIDTH),
              (off["sc_h"], sc0 + 2 * SC_WIDTH, SC_WIDTH)]
    tn = PROJ_TILE
    assert total % tn == 0 and all(dst // tn == (dst + rows - 1) // tn for dst, _, rows in pieces)
    rest = _ProjPlan(tn, total // tn, pieces=sorted(pieces))
    gates_w = N_BRANCHES * d
    gn = min(tn, gates_w)
    gates = _ProjPlan(gn, gates_w // gn, first_row=src["merge"][0])
    return gates, rest, off


def _rope_swap_perm():
    q = MLA_ROPE // 4
    return np.concatenate([np.arange(q, 2 * q), np.arange(0, q), np.arange(3 * q, 4 * q), np.arange(2 * q, 3 * q)])


def _prep_mla(w_uq, w_ukv):
    n_l = w_uq.shape[0]
    perm = _rope_swap_perm()
    wq = w_uq.reshape(n_l, MLA_Q_RANK, MLA_HEADS, MLA_NOPE + MLA_ROPE)
    qn, qr = wq[..., :MLA_NOPE], wq[..., MLA_NOPE:]
    zeros = lambda *s: jnp.zeros((n_l,) + s, w_uq.dtype)
    wqa = jnp.concatenate([qn, qr, qr], axis=-1)
    wqb = jnp.concatenate([zeros(MLA_Q_RANK, MLA_HEADS, MLA_NOPE), qr[..., perm],
                           zeros(MLA_Q_RANK, MLA_HEADS, MLA_ROPE)], axis=-1)
    wkv = w_ukv.reshape(n_l, MLA_KV_RANK, MLA_HEADS, MLA_NOPE + MLA_V)
    wka = jnp.concatenate([wkv[..., :MLA_NOPE], zeros(MLA_KV_RANK, MLA_HEADS, MLA_SLOT - MLA_NOPE)], axis=-1)
    wv = wkv[..., MLA_NOPE:]
    flat = lambda w: w.reshape(n_l, w.shape[1], -1).astype(BF16)
    sela = np.zeros((128, MLA_HEADS, MLA_SLOT), np.float32)
    selb = np.zeros((128, MLA_HEADS, MLA_SLOT), np.float32)
    for r in range(MLA_ROPE):
        sela[r, :, MLA_NOPE + r] = 1.0
        sela[r, :, MLA_NOPE + MLA_ROPE + r] = 1.0
        selb[perm[r], :, MLA_NOPE + r] = 1.0
    sel = lambda s: jnp.asarray(s.reshape(128, -1), BF16)
    return flat(wqa), flat(wqb), flat(wka), sel(sela), sel(selb), flat(wv)


def _rope_tables(geo):
    rows = geo.s // GRID_W
    pos = jnp.arange(rows * GRID_W)
    row = (pos // GRID_W).astype(F32)
    col = (pos % GRID_W).astype(F32)
    n_freq = MLA_ROPE // 4
    freqs = ROPE_THETA ** (-jnp.arange(n_freq, dtype=F32) / n_freq)
    cr, sr = jnp.cos(row[:, None] * freqs), jnp.sin(row[:, None] * freqs)
    cc, sn = jnp.cos(col[:, None] * freqs), jnp.sin(col[:, None] * freqs)
    cos32 = jnp.concatenate([cr, cr, cc, cc], axis=-1)
    sin32 = jnp.concatenate([-sr, sr, -sn, sn], axis=-1)
    s, c = geo.s, geo.c
    one = lambda n, w: jnp.ones((n, w), F32)
    zero = lambda n, w: jnp.zeros((n, w), F32)
    scale = (MLA_NOPE + MLA_ROPE) ** -0.5
    qca = jnp.concatenate([jnp.concatenate([one(c, 64), zero(c, 32), one(c, 32)], 1),
                           jnp.concatenate([one(s, 64), cos32, one(s, 32)], 1)], 0) * scale
    qsb = jnp.concatenate([zero(c, 128), jnp.concatenate([zero(s, 64), sin32, zero(s, 32)], 1)], 0) * scale
    kca = jnp.concatenate([jnp.concatenate([one(c, 64), zero(c, 32), one(c, 32)], 1),
                           jnp.concatenate([one(s, 64), cos32, zero(s, 32)], 1)], 0)
    ksb = jnp.concatenate([zero(c, 128), jnp.concatenate([zero(s, 64), sin32, zero(s, 32)], 1)], 0)
    return qca, qsb, kca, ksb


def kernel(x, c, ctx, c_ctx, w_mod, b_mod, g_mix, g_ffn, w_in, g_q, w_uq, g_kv, w_ukv, w_mla_o, w_cf_dw, b_cf_dw, g_cf_ln, b_cf_ln, w_cf_o, w_gla_gf, b_gla_gf, w_gla_gb, b_gla_gb, g_gla_norm, w_gla_o, w_sc_conv, w_sc_o, w_mix_o, w_router, b_router, w_e_gate, w_e_up, w_e_down, g_final):
    bsz, seq, d = x.shape
    n_ctx = ctx.shape[1]
    depth = w_in.shape[0]
    geo = _Geom(bsz, n_ctx, seq, d)
    assert bsz + 1 <= 8 and seq % GRID_W == 0 and n_ctx % GLA_CHUNK == 0 and seq % GLA_CHUNK == 0

    plan_gates, plan_rest, off = _proj_plans(d)
    w_in_t = jnp.swapaxes(w_in, 1, 2)
    wqa, wqb, wka, sela, selb, wv = _prep_mla(w_uq, w_ukv)
    tabs = _rope_tables(geo)
    w_br_half = [_to_bf16(w, 0.5) for w in (w_mla_o, w_cf_o, w_gla_o, w_sc_o)]
    w_mix = _to_bf16(w_mix_o)
    gate_w = jnp.zeros((depth, 2, 128, GLA_HEADS * GLA_DK), F32)
    gate_w = gate_w.at[:, 0, :GLA_GATE_RANK].set(w_gla_gf).at[:, 1, GLA_GATE_RANK:2 * GLA_GATE_RANK].set(w_gla_gb)
    gate_w = gate_w.astype(BF16)
    gate_b = jnp.stack([b_gla_gf, b_gla_gb], axis=1)[:, :, None, :]
    mat3, msk4 = _gla_tables()
    wr_t = w_router.T
    wr_hi = wr_t.astype(BF16)
    wr_stack = jnp.concatenate([wr_hi, (wr_t - wr_hi.astype(F32)).astype(BF16)], axis=0)
    vec = lambda a: a.reshape(a.shape[0], 1, a.shape[-1])

    cc = jnp.zeros((8, d), F32).at[:bsz].set(c).at[bsz].set(c_ctx)
    mod = _modulation(cc, w_mod, b_mod).reshape(depth, 8, 1, 6 * d)

    xa, h = _first_norm(geo, x, ctx, vec(g_mix), mod)
    for l in range(depth):
        u = _project(h, w_in_t, l, plan_rest, 1.0)
        u_gates = _project(h, w_in_t, l, plan_gates, 0.5)
        q, k, v = _mla_qkv(geo, u, off, l, vec(g_q), vec(g_kv), wqa, wqb, wka, sela, selb, wv, tabs)
        attn = _mla_attention(geo, q, k, v)
        conf, sconv = _conv_branches(geo, u, off, l, w_cf_dw, vec(b_cf_dw), vec(g_cf_ln), vec(b_cf_ln), w_sc_conv)
        gla_f, gla_b = _gla_scan(geo, u, off, l, gate_w, gate_b, mat3, msk4)
        xa, *routed = _merge_mix(geo, attn, conf, gla_f, gla_b, u, u_gates, off, vec(g_gla_norm), sconv, w_br_half,
                                 w_mix, xa, mod, l, vec(g_ffn), wr_stack, b_router.reshape(N_EXPERTS, 1))
        if l + 1 < depth:
            xa, h = _moe(geo, xa, routed, mod, l, w_e_gate, w_e_up, w_e_down, vec(g_mix), None)
        else:
            return _moe(geo, xa, routed, mod, l, w_e_gate, w_e_up, w_e_down, None, g_final)
```

```python
import functools

import numpy as np
import jax
import jax.numpy as jnp
from jax import lax
from jax.experimental import pallas as pl
from jax.experimental.pallas import tpu as pltpu

F32 = jnp.float32
BF16 = jnp.bfloat16
EPS = 1e-6

GRID_W = 64
ROPE_THETA = 10000.0
MLA_HEADS = 8
MLA_NOPE = 64
MLA_ROPE = 32
MLA_V = 64
MLA_Q_RANK = 512
MLA_KV_RANK = 256
MLA_SLOT = 128
CF_WIDTH = 512
CF_KERNEL = 31
GLA_HEADS = 4
GLA_DK = 64
GLA_DV = 128
GLA_GATE_RANK = 16
GLA_GATE_TAU = 16.0
GLA_CHUNK = 64
GLA_LEVELS = 6
SC_WIDTH = 512
SC_KERNEL = 3
N_BRANCHES = 4
N_EXPERTS = 16
N_GROUPS = 4
EXPERT_FF = 512
CONV_HALO = 16
SUBLANES = 8
MERGE_COLS = 512
ATTN_ROWS = 768
PROJ_TILE = 1024
PROJ_CAST_ROWS = 256
VMEM_LIMIT = 56 * 1024 * 1024


def _cparams(*sem):
    return pltpu.CompilerParams(dimension_semantics=sem, vmem_limit_bytes=VMEM_LIMIT)


def _pick(n, cands):
    for c in cands:
        if n % c == 0:
            return c
    raise ValueError(f"no tile for {n}")


def _dot(a, b):
    return jnp.dot(a, b, preferred_element_type=F32)


def _dot_nt(a, b):
    return lax.dot_general(a, b, (((1,), (1,)), ((), ())), preferred_element_type=F32)


def _proj_layout():
    parts = [("sc_b", SC_WIDTH), ("sc_c", SC_WIDTH), ("sc_h", SC_WIDTH),
             ("cf_a", CF_WIDTH), ("cf_b", CF_WIDTH), ("cq", MLA_Q_RANK), ("gla_v", GLA_HEADS * GLA_DV),
             ("gla_r", GLA_HEADS * GLA_DV), ("ckv", MLA_KV_RANK), ("gla_q", GLA_HEADS * GLA_DK),
             ("gla_k", GLA_HEADS * GLA_DK), ("kr", 128), ("gate", 128)]
    off, start = {}, 0
    for name, width in parts:
        assert start % width == 0
        off[name] = start
        start += width
    return off, start


def _src_layout(d):
    parts = (("cq", MLA_Q_RANK), ("ckv", MLA_KV_RANK), ("kr", MLA_ROPE), ("cf", 2 * CF_WIDTH),
             ("gla_q", GLA_HEADS * GLA_DK), ("gla_k", GLA_HEADS * GLA_DK), ("gla_v", GLA_HEADS * GLA_DV),
             ("gf", GLA_GATE_RANK), ("gb", GLA_GATE_RANK), ("gla_r", GLA_HEADS * GLA_DV),
             ("sc", 3 * SC_WIDTH), ("merge", N_BRANCHES * d))
    off, start = {}, 0
    for name, width in parts:
        off[name] = (start, start + width)
        start += width
    return off


def _mod_kernel(c_ref, w_ref, b_ref, o_ref):
    c = c_ref[...]
    a = (c * jax.nn.sigmoid(c)).astype(BF16)
    o_ref[...] = _dot(a, w_ref[...].astype(BF16)) + b_ref[...]


def _modulation(cc, w_mod, b_mod):
    n_l, d, n6 = w_mod.shape
    tn = _pick(n6, (1024, 512, 256, 128))
    return pl.pallas_call(
        _mod_kernel,
        out_shape=jax.ShapeDtypeStruct((n_l, 8, n6), F32),
        grid=(n_l, n6 // tn),
        in_specs=[pl.BlockSpec((8, d), lambda l, j: (0, 0)),
                  pl.BlockSpec((None, d, tn), lambda l, j: (l, 0, j)),
                  pl.BlockSpec((None, 1, tn), lambda l, j: (l, 0, j))],
        out_specs=pl.BlockSpec((None, 8, tn), lambda l, j: (l, 0, j)),
        name="adaln_mod",
        compiler_params=_cparams("parallel", "parallel"),
    )(cc, w_mod, b_mod.reshape(n_l, 1, n6))


class _Geom:
    def __init__(self, bsz, n_ctx, seq, d):
        self.b, self.c, self.s, self.d = bsz, n_ctx, seq, d
        self.n = n_ctx + seq
        self.m = bsz * self.n
        self.tr = _pick(int(np.gcd(n_ctx, seq)), (256, 128, 64))
        self.nt = self.n // self.tr
        self.nct = n_ctx // self.tr

    def mod_row(self, i):
        b, j = i // self.nt, i % self.nt
        return jnp.where(j < self.nct, self.b, b)

    def mod_spec(self, layer, part):
        return pl.BlockSpec((None, None, 1, self.d), lambda i, *_: (layer, self.mod_row(i), 0, part))


def _first_norm_kernel(x_ref, c_ref, g_ref, sh_ref, sc_ref, xa_ref, h_ref, *, geo):
    is_ctx = pl.program_id(0) % geo.nt < geo.nct
    rows = jnp.where(is_ctx, c_ref[...], x_ref[...])
    xa_ref[...] = rows
    y = rows * lax.rsqrt(jnp.mean(rows * rows, axis=-1, keepdims=True) + EPS) * g_ref[...]
    h_ref[...] = (y * (1.0 + sc_ref[...]) + sh_ref[...]).astype(h_ref.dtype)


def _first_norm(geo, x, ctx, g, mod):
    tr, d = geo.tr, geo.d
    tile = pl.BlockSpec((tr, d), lambda i: (i, 0))
    return pl.pallas_call(
        functools.partial(_first_norm_kernel, geo=geo),
        out_shape=(jax.ShapeDtypeStruct((geo.m, d), F32), jax.ShapeDtypeStruct((geo.m, d), BF16)),
        grid=(geo.m // tr,),
        in_specs=[pl.BlockSpec((None, tr, d), lambda i: (i // geo.nt, jnp.maximum(i % geo.nt - geo.nct, 0), 0)),
                  pl.BlockSpec((None, tr, d), lambda i: (i // geo.nt, jnp.minimum(i % geo.nt, geo.nct - 1), 0)),
                  pl.BlockSpec((None, 1, d), lambda i: (0, 0, 0)),
                  geo.mod_spec(0, 0), geo.mod_spec(0, 1)],
        out_specs=(tile, tile),
        name="first_norm",
        compiler_params=_cparams("parallel"),
    )(x, ctx, g, mod, mod)


def _proj_kernel(a_ref, w_hbm, o_ref, wbuf_ref, wb_ref, sem_ref, *, layer, plan, scale):
    j, i = pl.program_id(0), pl.program_id(1)
    n_tiles = pl.num_programs(0)
    tn = wb_ref.shape[0]

    def copies(t, slot):
        return [pltpu.make_async_copy(w_hbm.at[layer, pl.ds(src, rows), :], wbuf_ref.at[slot, pl.ds(dst, rows), :],
                                      sem_ref.at[slot]) for dst, src, rows in plan(t)]

    def on_tile(t, fn):
        if not plan.static:
            fn(t)
            return
        for tt in range(plan.n_tiles):
            pl.when(t == tt)(functools.partial(fn, tt))

    def start(t):
        for cp in copies(t, t % 2):
            cp.start()

    def land(t):
        slot = t % 2
        for cp in copies(t, slot):
            cp.wait()
        covered = 0
        for dst, _, rows in plan(t):
            if dst > covered:
                wb_ref[covered:dst, :] = jnp.zeros((dst - covered, wb_ref.shape[1]), BF16)
            for r0 in range(0, rows, PROJ_CAST_ROWS):
                rr = min(PROJ_CAST_ROWS, rows - r0)
                rows_f32 = wbuf_ref[slot, dst + r0:dst + r0 + rr, :]
                wb_ref[dst + r0:dst + r0 + rr, :] = (rows_f32 * scale).astype(BF16)
            covered = dst + rows
        if covered < tn:
            wb_ref[covered:tn, :] = jnp.zeros((tn - covered, wb_ref.shape[1]), BF16)

    @pl.when(i == 0)
    def _():
        @pl.when(j == 0)
        def _():
            on_tile(j, start)

        on_tile(j, land)

        @pl.when(j + 1 < n_tiles)
        def _():
            on_tile(j + 1, start)

    o_ref[...] = _dot_nt(a_ref[...], wb_ref[...]).astype(o_ref.dtype)


class _ProjPlan:
    def __init__(self, tn, n_tiles, pieces=None, first_row=None):
        self.tn, self.n_tiles, self.static = tn, n_tiles, pieces is not None
        self._pieces, self._first_row = pieces, first_row

    def __call__(self, t):
        if self.static:
            lo = t * self.tn
            return [(dst - lo, src, rows) for dst, src, rows in self._pieces if lo <= dst < lo + self.tn]
        return [(0, pl.multiple_of(self._first_row + t * self.tn, SUBLANES), self.tn)]


def _project(a, w_t, layer, plan, scale):
    m, k = a.shape
    tn = plan.tn
    tm = _pick(m, (1536, 1152, 1024, 512, 256, 128))
    return pl.pallas_call(
        functools.partial(_proj_kernel, layer=layer, plan=plan, scale=scale),
        out_shape=jax.ShapeDtypeStruct((m, tn * plan.n_tiles), BF16),
        grid=(plan.n_tiles, m // tm),
        in_specs=[pl.BlockSpec((tm, k), lambda j, i: (i, 0)), pl.BlockSpec(memory_space=pl.ANY)],
        out_specs=pl.BlockSpec((tm, tn), lambda j, i: (i, j)),
        scratch_shapes=[pltpu.VMEM((2, tn, k), F32), pltpu.VMEM((tn, k), BF16), pltpu.SemaphoreType.DMA((2,))],
        name="proj_in",
        compiler_params=_cparams("arbitrary", "arbitrary"),
    )(a, w_t)


def _rms(x, g):
    return x * lax.rsqrt(jnp.mean(x * x, axis=-1, keepdims=True) + EPS) * g


def _qkv_kernel(cq_ref, ckv_ref, kr_ref, gq_ref, gkv_ref, wqa_ref, wqb_ref, wka_ref, sela_ref, selb_ref,
                wv_ref, qca_ref, qsb_ref, kca_ref, ksb_ref, q_ref, k_ref, v_ref):
    cqn = _rms(cq_ref[...].astype(F32), gq_ref[...]).astype(BF16)
    ckvn = _rms(ckv_ref[...].astype(F32), gkv_ref[...]).astype(BF16)
    kr = kr_ref[...]
    qa = _dot(cqn, wqa_ref[...])
    qb = _dot(cqn, wqb_ref[...])
    ka = _dot(ckvn, wka_ref[...]) + _dot(kr, sela_ref[...])
    kb = _dot(kr, selb_ref[...])
    qca, qsb, kca, ksb = qca_ref[...], qsb_ref[...], kca_ref[...], ksb_ref[...]
    for h in range(MLA_HEADS):
        sl = slice(h * MLA_SLOT, (h + 1) * MLA_SLOT)
        q_ref[:, sl] = (qa[:, sl] * qca + qb[:, sl] * qsb).astype(BF16)
        k_ref[:, sl] = (ka[:, sl] * kca + kb[:, sl] * ksb).astype(BF16)
    v_ref[...] = _dot(ckvn, wv_ref[...]).astype(BF16)


def _mla_qkv(geo, u, off, layer, gq, gkv, wqa, wqb, wka, sela, selb, wv, tabs):
    tr, m = geo.tr, geo.m
    hq = MLA_HEADS * MLA_SLOT
    hv = MLA_HEADS * MLA_V
    row = lambda w, o: pl.BlockSpec((tr, w), lambda i: (i, o // w))
    lw = lambda a: pl.BlockSpec((None,) + a.shape[1:], lambda i: (layer,) + (0,) * (a.ndim - 1))
    cw = lambda a: pl.BlockSpec(a.shape, lambda i: (0,) * a.ndim)
    tab = pl.BlockSpec((tr, MLA_SLOT), lambda i: (i % geo.nt, 0))
    return pl.pallas_call(
        _qkv_kernel,
        out_shape=(jax.ShapeDtypeStruct((m, hq), BF16), jax.ShapeDtypeStruct((m, hq), BF16),
                   jax.ShapeDtypeStruct((m, hv), BF16)),
        grid=(m // tr,),
        in_specs=[row(MLA_Q_RANK, off["cq"]), row(MLA_KV_RANK, off["ckv"]), row(128, off["kr"]),
                  lw(gq), lw(gkv), lw(wqa), lw(wqb), lw(wka), cw(sela), cw(selb), lw(wv), tab, tab, tab, tab],
        out_specs=(pl.BlockSpec((tr, hq), lambda i: (i, 0)), pl.BlockSpec((tr, hq), lambda i: (i, 0)),
                   pl.BlockSpec((tr, hv), lambda i: (i, 0))),
        name="mla_qkv",
        compiler_params=_cparams("parallel"),
    )(u, u, u, gq, gkv, wqa, wqb, wka, sela, selb, wv, *tabs)


def _attn_kernel(q_ref, k_ref, v_ref, o_ref, *, n_ctx, n_all):
    tq = q_ref.shape[0]

    def run(r0, rows, nk):
        lane = lax.broadcasted_iota(jnp.int32, (rows, 2 * MLA_V), 1)
        outs = []
        for hh in range(2):
            sl = slice(hh * MLA_SLOT, (hh + 1) * MLA_SLOT)
            s = _dot_nt(q_ref[r0:r0 + rows, sl], k_ref[0:nk, sl])
            p = jnp.exp(s - jnp.max(s, axis=-1, keepdims=True))
            inv = 1.0 / jnp.sum(p, axis=-1, keepdims=True)
            outs.append(_dot(p.astype(BF16), v_ref[0:nk, :]) * inv)
        o_ref[r0:r0 + rows, :] = jnp.where(lane < MLA_V, outs[0], outs[1]).astype(o_ref.dtype)

    @pl.when(pl.program_id(2) == 0)
    def _():
        run(0, n_ctx, n_ctx)
        if tq > n_ctx:
            run(n_ctx, tq - n_ctx, n_all)

    @pl.when(pl.program_id(2) > 0)
    def _():
        run(0, tq, n_all)


def _mla_attention(geo, q, k, v):
    n, bsz = geo.n, geo.b
    groups = [g for g in range(1, geo.nt + 1) if geo.nt % g == 0 and g * geo.tr >= geo.c]
    tq = geo.tr * min(groups, key=lambda g: abs(g * geo.tr - ATTN_ROWS))
    q3, k3, v3 = (t.reshape(bsz, n, t.shape[-1]) for t in (q, k, v))
    kern = functools.partial(_attn_kernel, n_ctx=geo.c, n_all=n)
    out = pl.pallas_call(
        kern,
        out_shape=jax.ShapeDtypeStruct((bsz, n, MLA_HEADS * MLA_V), BF16),
        grid=(bsz, MLA_HEADS // 2, n // tq),
        in_specs=[pl.BlockSpec((None, tq, 2 * MLA_SLOT), lambda b, hp, i: (b, i, hp)),
                  pl.BlockSpec((None, n, 2 * MLA_SLOT), lambda b, hp, i: (b, 0, hp)),
                  pl.BlockSpec((None, n, 2 * MLA_V), lambda b, hp, i: (b, 0, hp))],
        out_specs=pl.BlockSpec((None, tq, 2 * MLA_V), lambda b, hp, i: (b, i, hp)),
        name="mla_attn",
        compiler_params=_cparams("parallel", "parallel", "parallel"),
    )(q3, k3, v3)
    return out.reshape(geo.m, MLA_HEADS * MLA_V)


def _halo_flags(geo, i):
    j = i % geo.nt
    left = jnp.logical_and(j != 0, j != geo.nct)
    right = jnp.logical_and(j != geo.nct - 1, j != geo.nt - 1)
    return left.astype(F32), right.astype(F32)


def _conv_shifts(n_taps):
    base = CONV_HALO - n_taps // 2
    return sorted({(base + kk) % SUBLANES for kk in range(n_taps)} - {0})


def _conv_taps(pad_ref, sh_ref, w_ref, n_taps, rows):
    base = CONV_HALO - n_taps // 2
    shifts = _conv_shifts(n_taps)
    span = sh_ref.shape[1]
    for s, r in enumerate(shifts):
        sh_ref[s, :, :] = pad_ref[r:r + span, :]

    def window(o):
        a, r = o - o % SUBLANES, o % SUBLANES
        return pad_ref[a:a + rows, :] if r == 0 else sh_ref[shifts.index(r), a:a + rows, :]

    acc = w_ref[0:1, :] * window(base)
    for kk in range(1, n_taps):
        acc = acc + w_ref[kk:kk + 1, :] * window(base + kk)
    return acc


def _conv_scratch(tr, width, n_taps):
    return [pltpu.VMEM((tr + 2 * CONV_HALO, width), F32),
            pltpu.VMEM((len(_conv_shifts(n_taps)), tr + 2 * CONV_HALO - SUBLANES, width), F32)]


def _conf_kernel(a_ref, b_ref, ap_ref, bp_ref, an_ref, bn_ref, w_ref, bias_ref, g_ref, beta_ref, o_ref,
                 pad_ref, sh_ref, *, geo):
    left, right = _halo_flags(geo, pl.program_id(0))
    rows = a_ref.shape[0]
    glu = lambda a, b: a[...].astype(F32) * jax.nn.sigmoid(b[...].astype(F32))
    pad_ref[0:CONV_HALO, :] = glu(ap_ref, bp_ref) * left
    pad_ref[CONV_HALO:CONV_HALO + rows, :] = glu(a_ref, b_ref)
    pad_ref[CONV_HALO + rows:2 * CONV_HALO + rows, :] = glu(an_ref, bn_ref) * right
    h = _conv_taps(pad_ref, sh_ref, w_ref, CF_KERNEL, rows) + bias_ref[...]
    hc = h - jnp.mean(h, axis=-1, keepdims=True)
    y = hc * lax.rsqrt(jnp.mean(hc * hc, axis=-1, keepdims=True) + EPS) * g_ref[...] + beta_ref[...]
    o_ref[...] = (y * jax.nn.sigmoid(y)).astype(o_ref.dtype)


def _sconv_kernel(gb_ref, gc_ref, h_ref, gcp_ref, hp_ref, gcn_ref, hn_ref, w_ref, o_ref, pad_ref, sh_ref, *, geo):
    left, right = _halo_flags(geo, pl.program_id(0))
    rows = h_ref.shape[0]
    prod = lambda a, b: a[...].astype(F32) * b[...].astype(F32)
    pad_ref[0:CONV_HALO, :] = prod(gcp_ref, hp_ref) * left
    pad_ref[CONV_HALO:CONV_HALO + rows, :] = prod(gc_ref, h_ref)
    pad_ref[CONV_HALO + rows:2 * CONV_HALO + rows, :] = prod(gcn_ref, hn_ref) * right
    y = _conv_taps(pad_ref, sh_ref, w_ref, SC_KERNEL, rows)
    o_ref[...] = (gb_ref[...].astype(F32) * y).astype(o_ref.dtype)


def _halo_specs(geo, width, offset):
    tr = geo.tr
    per = tr // CONV_HALO
    last = geo.m // CONV_HALO - 1
    cur = pl.BlockSpec((tr, width), lambda i: (i, offset // width))
    prev = pl.BlockSpec((CONV_HALO, width), lambda i: (jnp.maximum(i * per - 1, 0), offset // width))
    nxt = pl.BlockSpec((CONV_HALO, width), lambda i: (jnp.minimum((i + 1) * per, last), offset // width))
    return cur, prev, nxt


N_CONF_IN, N_SCONV_IN = 10, 8


def _convs_kernel(*refs, geo):
    n_in = N_CONF_IN + N_SCONV_IN
    conf_o, sconv_o = refs[n_in], refs[n_in + 1]
    conf_scratch, sconv_scratch = refs[n_in + 2:n_in + 4], refs[n_in + 4:n_in + 6]
    _conf_kernel(*refs[:N_CONF_IN], conf_o, *conf_scratch, geo=geo)
    _sconv_kernel(*refs[N_CONF_IN:n_in], sconv_o, *sconv_scratch, geo=geo)


def _conv_branches(geo, u, off, layer, w_dw, b_dw, g_ln, b_ln, w_conv):
    tr = geo.tr
    a_c, a_p, a_n = _halo_specs(geo, CF_WIDTH, off["cf_a"])
    b_c, b_p, b_n = _halo_specs(geo, CF_WIDTH, off["cf_b"])
    gb_c, _, _ = _halo_specs(geo, SC_WIDTH, off["sc_b"])
    gc_c, gc_p, gc_n = _halo_specs(geo, SC_WIDTH, off["sc_c"])
    h_c, h_p, h_n = _halo_specs(geo, SC_WIDTH, off["sc_h"])
    lw = lambda a: pl.BlockSpec((None,) + a.shape[1:], lambda i: (layer,) + (0,) * (a.ndim - 1))
    in_specs = [a_c, b_c, a_p, b_p, a_n, b_n, lw(w_dw), lw(b_dw), lw(g_ln), lw(b_ln),
                gb_c, gc_c, h_c, gc_p, h_p, gc_n, h_n, lw(w_conv)]
    assert len(in_specs) == N_CONF_IN + N_SCONV_IN
    return pl.pallas_call(
        functools.partial(_convs_kernel, geo=geo),
        out_shape=(jax.ShapeDtypeStruct((geo.m, CF_WIDTH), BF16), jax.ShapeDtypeStruct((geo.m, SC_WIDTH), BF16)),
        grid=(geo.m // tr,),
        in_specs=in_specs,
        out_specs=(pl.BlockSpec((tr, CF_WIDTH), lambda i: (i, 0)), pl.BlockSpec((tr, SC_WIDTH), lambda i: (i, 0))),
        scratch_shapes=_conv_scratch(tr, CF_WIDTH, CF_KERNEL) + _conv_scratch(tr, SC_WIDTH, SC_KERNEL),
        name="conv_branches",
        compiler_params=_cparams("parallel"),
    )(*([u] * 6), w_dw, b_dw, g_ln, b_ln, *([u] * 7), w_conv)


def _gla_tables():
    c = GLA_CHUNK
    n_rows = (GLA_LEVELS + 2) * c + 8
    mat = np.zeros((2, n_rows, c), np.float32)
    msk = np.zeros((2, GLA_LEVELS + 1, c, c), np.float32)
    for lvl in range(GLA_LEVELS):
        hs = (c // 2) >> lvl
        for t in range(c):
            mid = (t // (2 * hs)) * 2 * hs + hs
            if t >= mid:
                mat[0, lvl * c + t, mid + 1:t + 1] = 1.0
                msk[0, lvl, t, mid - hs:mid] = 1.0
            else:
                mat[0, lvl * c + t, t + 1:mid + 1] = 1.0
    msk[0, GLA_LEVELS] = np.eye(c, dtype=np.float32)
    for t in range(c):
        mat[0, GLA_LEVELS * c + t, :t + 1] = 1.0
        mat[0, (GLA_LEVELS + 1) * c + t, t + 1:] = 1.0
    mat[0, (GLA_LEVELS + 2) * c:, :] = 1.0
    n_blk = GLA_LEVELS + 2
    mat[1, :n_blk * c] = mat[0, :n_blk * c].reshape(n_blk, c, c)[:, ::-1, ::-1].reshape(n_blk * c, c)
    mat[1, n_blk * c:] = 1.0
    msk[1] = msk[0][:, ::-1, ::-1]
    mat3 = np.concatenate([mat, mat, mat], axis=2)
    msk4 = np.tile(msk, (1, 1, GLA_HEADS, 1))
    return jnp.asarray(mat3, BF16), jnp.asarray(msk4, F32)


def _gla_kernel(qf_ref, kf_ref, vf_ref, gf_ref, qb_ref, kb_ref, vb_ref, gb_ref, w2_ref, b2_ref, mat_ref, msk_ref,
                of_ref, ob_ref, st_ref):
    @pl.when(pl.program_id(0) == 0)
    def _():
        st_ref[...] = jnp.zeros_like(st_ref)

    ins = ((qf_ref, kf_ref, vf_ref, gf_ref, of_ref), (qb_ref, kb_ref, vb_ref, gb_ref, ob_ref))
    for b in range(qf_ref.shape[0]):
        for d, (q_ref, k_ref, v_ref, g_ref, o_ref) in enumerate(ins):
            _gla_chunk(q_ref.at[b], k_ref.at[b], v_ref.at[b], g_ref.at[b], w2_ref.at[d], b2_ref.at[d],
                       mat_ref.at[d], msk_ref.at[d], o_ref.at[b], st_ref.at[d, b])


def _gla_chunk(q_ref, k_ref, v_ref, g_ref, w2_ref, b2_ref, mat_ref, msk_ref, o_ref, st_ref):
    c = GLA_CHUNK
    z = _dot(g_ref[...], w2_ref[...]) + b2_ref[...]
    la = (jnp.minimum(z, 0.0) - jnp.log1p(jnp.exp(-jnp.abs(z)))) * (1.0 / GLA_GATE_TAU)
    hi = la.astype(BF16)
    r1 = la - hi.astype(F32)
    lo = r1.astype(BF16)
    lo2 = (r1 - lo.astype(F32)).astype(BF16)
    ea = jnp.exp(_dot(mat_ref[...], jnp.concatenate([hi, lo, lo2], axis=0)))

    head = lax.broadcasted_iota(jnp.int32, (1, GLA_HEADS * GLA_DK), 1) // GLA_DK
    zero = jnp.zeros((), BF16)

    def stack(x):
        xb = x.astype(BF16)
        return jnp.concatenate([jnp.where(head == h, xb, zero) for h in range(GLA_HEADS)], axis=0)

    q = q_ref[...].astype(F32) * (GLA_DK ** -0.5)
    k = k_ref[...].astype(F32)
    vb = v_ref[...]

    attn = _dot_nt(stack(q), k.astype(BF16)) * msk_ref[GLA_LEVELS]
    for lvl in range(GLA_LEVELS):
        e = ea[lvl * c:(lvl + 1) * c]
        attn = attn + _dot_nt(stack(q * e), (k * e).astype(BF16)) * msk_ref[lvl]
    o_intra = _dot(attn.astype(BF16), vb)

    st = st_ref[...]
    o_inter = _dot_nt(stack(q * ea[GLA_LEVELS * c:(GLA_LEVELS + 1) * c]), st.astype(BF16))
    for h in range(GLA_HEADS):
        rows = slice(h * c, (h + 1) * c)
        cols = slice(h * GLA_DV, (h + 1) * GLA_DV)
        o_ref[:, cols] = o_intra[rows, cols] + o_inter[rows, :]

    k_dec = stack(k * ea[(GLA_LEVELS + 1) * c:(GLA_LEVELS + 2) * c])
    v_rows = jnp.concatenate([vb[:, h * GLA_DV:(h + 1) * GLA_DV] for h in range(GLA_HEADS)], axis=0)
    eye = (lax.broadcasted_iota(jnp.int32, (GLA_DV, GLA_DV), 0)
           == lax.broadcasted_iota(jnp.int32, (GLA_DV, GLA_DV), 1)).astype(BF16)
    v_t = _dot_nt(eye, v_rows).astype(BF16)
    tot = ea[(GLA_LEVELS + 2) * c:(GLA_LEVELS + 2) * c + 1]
    st_ref[...] = st * tot + _dot(v_t, k_dec)


def _gla_scan(geo, u, off, layer, w2, b2, mat3, msk4):
    c = GLA_CHUNK
    ncc = geo.c // c
    nc = geo.n // c
    dk, dv = GLA_HEADS * GLA_DK, GLA_HEADS * GLA_DV

    fwd = lambda i: i
    bwd = lambda i: jnp.where(i < ncc, ncc - 1 - i, nc - 1 - (i - ncc))

    u3 = u.reshape(geo.b, geo.n, u.shape[-1])
    row = lambda w, o, chunk: pl.BlockSpec((geo.b, c, w), lambda i: (0, chunk(i), o // w))
    rows = lambda chunk: [row(dk, off["gla_q"], chunk), row(dk, off["gla_k"], chunk), row(dv, off["gla_v"], chunk),
                          row(128, off["gate"], chunk)]
    out_sds = jax.ShapeDtypeStruct((geo.b, geo.n, dv), F32)
    o_f, o_b = pl.pallas_call(
        _gla_kernel,
        out_shape=(out_sds, out_sds),
        grid=(nc,),
        in_specs=rows(fwd) + rows(bwd) + [
            pl.BlockSpec((None, 2, 128, dk), lambda i: (layer, 0, 0, 0)),
            pl.BlockSpec((None, 2, 1, dk), lambda i: (layer, 0, 0, 0)),
            pl.BlockSpec(mat3.shape, lambda i: (0, 0, 0)),
            pl.BlockSpec(msk4.shape, lambda i: (0, 0, 0, 0))],
        out_specs=(pl.BlockSpec((geo.b, c, dv), lambda i: (0, fwd(i), 0)),
                   pl.BlockSpec((geo.b, c, dv), lambda i: (0, bwd(i), 0))),
        scratch_shapes=[pltpu.VMEM((2, geo.b, GLA_DV, dk), F32)],
        name="gla_scan",
        compiler_params=_cparams("arbitrary"),
    )(u3, u3, u3, u3, u3, u3, u3, u3, w2, b2, mat3, msk4)
    return o_f.reshape(geo.m, dv), o_b.reshape(geo.m, dv)


def _merge_mix_kernel(attn_ref, conf_ref, glaf_ref, glab_ref, r_ref, gn_ref, sc_ref, gate_ref, w0_ref, w1_ref,
                      w2_ref, w3_ref, wmix_ref, x_ref, gt_ref, gf_ref, shf_ref, scf_ref, wr_ref, br_ref, tri_ref,
                      o_ref, h_ref, idx_ref, wgt_ref, rank_ref, cnt_ref, y_ref, carry_ref):
    @pl.when(pl.program_id(0) == 0)
    def _():
        carry_ref[...] = jnp.zeros_like(carry_ref)

    wbr_ref = (w0_ref, w1_ref, w2_ref, w3_ref)
    o = glaf_ref[...] + glab_ref[...]
    r = r_ref[...].astype(F32)
    gn = gn_ref[...]
    normed = [_rms(o[:, h * GLA_DV:(h + 1) * GLA_DV], gn) for h in range(GLA_HEADS)]
    gla = (jnp.concatenate(normed, axis=1) * (r * jax.nn.sigmoid(r))).astype(BF16)
    acts = (attn_ref[...], conf_ref[...], gla, sc_ref[...])
    d = x_ref.shape[1]
    cw = min(MERGE_COLS, d)
    for c0 in range(0, d, cw):
        y = None
        for i in range(N_BRANCHES):
            half_b = _dot(acts[i], wbr_ref[i][:, c0:c0 + cw])
            t = jnp.tanh(gate_ref[:, i * d + c0:i * d + c0 + cw].astype(F32))
            term = t * half_b + half_b
            y = term if y is None else y + term
        y_ref[:, c0:c0 + cw] = y.astype(BF16)
    y_all = y_ref[...]
    for c0 in range(0, d, cw):
        cols = slice(c0, c0 + cw)
        o_ref[:, cols] = x_ref[:, cols] + gt_ref[:, cols] * _dot(y_all, wmix_ref[:, cols])
    _route_rows(o_ref[...], gf_ref, shf_ref, scf_ref, wr_ref, br_ref, tri_ref, h_ref, idx_ref, wgt_ref, rank_ref,
                cnt_ref, carry_ref)


def _merge_mix(geo, attn, conf, gla_f, gla_b, u, u_gates, off, g_norm, sconv, w_br_half, w_mix, x, mod, layer,
               g_ffn, wr_stack, b_router):
    tm, d, m = geo.tr, geo.d, geo.m
    w512 = GLA_HEADS * GLA_DV
    nch = d // 256
    act = pl.BlockSpec((tm, w512), lambda i: (i, 0))
    once = pl.Buffered(1)
    w_br = pl.BlockSpec((None, w512, d), lambda i: (layer, 0, 0), pipeline_mode=once)
    tri = jnp.asarray(np.triu(np.ones((tm, tm), np.float32), 1), BF16)
    pair = lambda dt: jax.ShapeDtypeStruct((2, m), dt)
    pair_spec = pl.BlockSpec((2, tm), lambda i: (0, i))
    return pl.pallas_call(
        _merge_mix_kernel,
        out_shape=(jax.ShapeDtypeStruct((m, d), F32), jax.ShapeDtypeStruct((m * nch, 128), jnp.uint32),
                   pair(jnp.int32), pair(F32), pair(jnp.int32), jax.ShapeDtypeStruct((N_EXPERTS, 1), F32)),
        grid=(m // tm,),
        in_specs=[act, act, act, act,
                  pl.BlockSpec((tm, w512), lambda i: (i, off["gla_r"] // w512)),
                  pl.BlockSpec((None, 1, GLA_DV), lambda i: (layer, 0, 0)),
                  act, pl.BlockSpec((tm, N_BRANCHES * d), lambda i: (i, 0)),
                  w_br, w_br, w_br, w_br,
                  pl.BlockSpec((None, d, d), lambda i: (layer, 0, 0), pipeline_mode=once),
                  pl.BlockSpec((tm, d), lambda i: (i, 0)), geo.mod_spec(layer, 2),
                  pl.BlockSpec((None, 1, d), lambda i: (layer, 0, 0)),
                  geo.mod_spec(layer, 3), geo.mod_spec(layer, 4),
                  pl.BlockSpec(wr_stack.shape, lambda i: (0, 0)),
                  pl.BlockSpec(b_router.shape, lambda i: (0, 0)),
                  pl.BlockSpec((tm, tm), lambda i: (0, 0))],
        out_specs=(pl.BlockSpec((tm, d), lambda i: (i, 0)), pl.BlockSpec((tm * nch, 128), lambda i: (i, 0)),
                   pair_spec, pair_spec, pair_spec, pl.BlockSpec((N_EXPERTS, 1), lambda i: (0, 0))),
        scratch_shapes=[pltpu.VMEM((tm, d), BF16), pltpu.VMEM((N_EXPERTS, 1), F32)],
        name="merge_mix",
        compiler_params=_cparams("arbitrary"),
    )(attn, conf, gla_f, gla_b, u, g_norm, sconv, u_gates, *w_br_half, w_mix, x, mod, g_ffn, mod, mod, wr_stack,
      b_router, tri)


def _cast_kernel(w_ref, o_ref, *, scale):
    o_ref[...] = (w_ref[...] * scale).astype(o_ref.dtype)


def _to_bf16(w, scale=1.0):
    n_l, rows, cols = w.shape
    tr = _pick(rows, (512, 256, 128))
    return pl.pallas_call(
        functools.partial(_cast_kernel, scale=scale),
        out_shape=jax.ShapeDtypeStruct(w.shape, BF16),
        grid=(n_l, rows // tr),
        in_specs=[pl.BlockSpec((None, tr, cols), lambda l, i: (l, i, 0))],
        out_specs=pl.BlockSpec((None, tr, cols), lambda l, i: (l, i, 0)),
        name="weight_cast",
        compiler_params=_cparams("parallel", "parallel"),
    )(w)


def _second_largest(a, b, c, d):
    return jnp.maximum(jnp.maximum(jnp.minimum(a, b), jnp.minimum(c, d)),
                       jnp.minimum(jnp.maximum(a, b), jnp.maximum(c, d)))


def _chunk_rows(rows, n, cc):
    return pl.ds(cc, rows, stride=n) if n > 1 else pl.ds(0, rows)


def _pack_rows(ref, y):
    rows = y.shape[0]
    n = ref.shape[0] // rows
    half = n * 128
    bits = lambda v: pltpu.bitcast(v.astype(BF16).astype(F32), jnp.uint32)
    packed = (bits(y[:, :half]) >> 16) | bits(y[:, half:])
    for cc in range(n):
        ref[_chunk_rows(rows, n, cc), :] = packed[:, cc * 128:(cc + 1) * 128]


def _unpack_cols(words):
    lo = pltpu.bitcast(words << 16, F32)
    hi = pltpu.bitcast(words & jnp.uint32(0xFFFF0000), F32)
    return lo, hi


def _route_rows(x, g_ref, sh_ref, sc_ref, wr_ref, br_ref, tri_ref, h_ref, idx_ref, wgt_ref, rank_ref, cnt_ref,
                carry_ref):
    y = _rms(x, g_ref[...]) * (1.0 + sc_ref[...]) + sh_ref[...]
    hi = y.astype(BF16)
    lo = (y - hi.astype(F32)).astype(BF16)
    _pack_rows(h_ref, y)
    ne = N_EXPERTS
    two = _dot_nt(wr_ref[...], hi)
    logits = two[0:ne] + two[ne:2 * ne] + _dot_nt(wr_ref[0:ne, :], lo)
    aff = 1.0 / (1.0 + jnp.exp(-logits))
    sel = aff + br_ref[...]
    per = ne // N_GROUPS
    gs = []
    for g in range(N_GROUPS):
        a, b, c, d = (sel[g * per + r:g * per + r + 1] for r in range(per))
        top1 = jnp.maximum(jnp.maximum(a, b), jnp.maximum(c, d))
        gs.append(top1 + _second_largest(a, b, c, d))
    best = gs[0]
    for g in range(1, N_GROUPS):
        best = jnp.maximum(best, gs[g])
    grp = jnp.full(best.shape, N_GROUPS, jnp.int32)
    for g in reversed(range(N_GROUPS)):
        grp = jnp.where(gs[g] == best, g, grp)
    erow = lax.broadcasted_iota(jnp.int32, sel.shape, 0)
    masked = jnp.where(erow // per == grp, sel, -jnp.inf)
    m1 = jnp.max(masked, axis=0, keepdims=True)
    i1 = jnp.min(jnp.where(masked == m1, erow, ne), axis=0, keepdims=True)
    one1 = erow == i1
    masked2 = jnp.where(one1, -jnp.inf, masked)
    m2 = jnp.max(masked2, axis=0, keepdims=True)
    i2 = jnp.min(jnp.where(masked2 == m2, erow, ne), axis=0, keepdims=True)
    one2 = erow == i2
    w1 = jnp.sum(jnp.where(one1, aff, 0.0), axis=0, keepdims=True)
    w2 = jnp.sum(jnp.where(one2, aff, 0.0), axis=0, keepdims=True)
    inv = 1.0 / (w1 + w2)
    idx_ref[0:1, :] = i1
    idx_ref[1:2, :] = i2
    wgt_ref[0:1, :] = w1 * inv
    wgt_ref[1:2, :] = w2 * inv
    o1 = jnp.where(one1, 1.0, 0.0)
    o2 = jnp.where(one2, 1.0, 0.0)
    tri = tri_ref[...]
    c0 = carry_ref[...]
    c1 = c0 + jnp.sum(o1, axis=1, keepdims=True)
    rank_ref[0:1, :] = jnp.sum(o1 * (c0 + _dot(o1.astype(BF16), tri)), axis=0, keepdims=True).astype(jnp.int32)
    rank_ref[1:2, :] = jnp.sum(o2 * (c1 + _dot(o2.astype(BF16), tri)), axis=0, keepdims=True).astype(jnp.int32)
    c2 = c1 + jnp.sum(o2, axis=1, keepdims=True)
    carry_ref[...] = c2
    cnt_ref[...] = c2


def _dispatch_kernel(dest_ref, last_ref, h_ref, xs_hbm, zero_ref, sem, zero_sem, *, m, n):
    i = pl.program_id(0)
    tr = h_ref.shape[0] // n

    @pl.when(i == 0)
    def _():
        zero_ref[...] = jnp.zeros_like(zero_ref)

        def fills(e):
            rows = zero_ref.shape[0]
            start = pl.multiple_of(last_ref[e] * n, rows)
            return pltpu.make_async_copy(zero_ref, xs_hbm.at[pl.ds(start, rows), :], zero_sem)

        for e in range(2 * N_EXPERTS):
            @pl.when(last_ref[e] >= 0)
            def _():
                fills(e).start()
        for e in range(2 * N_EXPERTS):
            @pl.when(last_ref[e] >= 0)
            def _():
                fills(e).wait()

    def issue(r, carry):
        src = h_ref.at[pl.ds(pl.multiple_of(r * n, n), n), :]
        for k in range(2):
            row = pl.multiple_of(dest_ref[k * m + i * tr + r], n)
            pltpu.make_async_copy(src, xs_hbm.at[pl.ds(row, n), :], sem).start()
        return carry

    lax.fori_loop(0, tr, issue, 0, unroll=8)
    everything = xs_hbm.at[pl.ds(0, 2 * tr * n), :]
    pltpu.make_async_copy(everything, everything, sem).wait()


def _dispatch(hfp, dest, last_tile_row, n_rows, tr, tm):
    n = hfp.shape[0] * 2 // dest.shape[0]
    m = hfp.shape[0] // n
    return pl.pallas_call(
        functools.partial(_dispatch_kernel, m=m, n=n),
        out_shape=jax.ShapeDtypeStruct((n_rows * n, 128), jnp.uint32),
        grid_spec=pltpu.PrefetchScalarGridSpec(
            num_scalar_prefetch=2, grid=(m // tr,),
            in_specs=[pl.BlockSpec((tr * n, 128), lambda i, *_: (i, 0))],
            out_specs=pl.BlockSpec(memory_space=pl.ANY),
            scratch_shapes=[pltpu.VMEM((tm * n, 128), jnp.uint32), pltpu.SemaphoreType.DMA(()),
                            pltpu.SemaphoreType.DMA(())]),
        name="moe_dispatch",
        compiler_params=_cparams("arbitrary"),
    )(dest, last_tile_row, hfp)


def _gmm_kernel(te_ref, nv_ref, first_ref, ord_ref, next_ref, xs_ref, wg_hbm, wu_hbm, wd_hbm, ys_ref, fg_ref, fu_ref,
                fd_ref, wgb_ref, wub_ref, wdb_ref, sem_ref, *, layer):
    j = pl.program_id(0)

    def fetch(e, slot):
        return [pltpu.make_async_copy(src.at[layer, e], dst.at[slot], sem_ref.at[slot])
                for src, dst in ((wg_hbm, fg_ref), (wu_hbm, fu_ref), (wd_hbm, fd_ref))]

    @pl.when(j == 0)
    def _():
        for cp in fetch(te_ref[0], 0):
            cp.start()

    @pl.when(j < nv_ref[0])
    def _():
        e = te_ref[j]

        @pl.when(j == first_ref[e])
        def _():
            slot = ord_ref[e] % 2
            for cp in fetch(e, slot):
                cp.wait()
            wgb_ref[...] = fg_ref[slot].astype(BF16)
            wub_ref[...] = fu_ref[slot].astype(BF16)
            wdb_ref[...] = fd_ref[slot].astype(BF16)

            @pl.when(next_ref[e] >= 0)
            def _():
                for cp in fetch(next_ref[e], 1 - slot):
                    cp.start()

        half = wgb_ref.shape[0] // 2
        n = half // 128
        tm = xs_ref.shape[0] // n
        parts = [_unpack_cols(xs_ref[_chunk_rows(tm, n, cc), :]) for cc in range(n)]
        xa = jnp.concatenate([p[0].astype(BF16) for p in parts], axis=1)
        xb = jnp.concatenate([p[1].astype(BF16) for p in parts], axis=1)
        g = _dot(xa, wgb_ref[0:half, :]) + _dot(xb, wgb_ref[half:2 * half, :])
        u = _dot(xa, wub_ref[0:half, :]) + _dot(xb, wub_ref[half:2 * half, :])
        act = (g * jax.nn.sigmoid(g) * u).astype(BF16)
        _pack_rows(ys_ref, _dot(act, wdb_ref[...]))

    @pl.when(j >= nv_ref[0])
    def _():
        ys_ref[...] = jnp.zeros_like(ys_ref)


def _gmm(xs, tile_expert, n_valid, first_tile, ordinal, next_expert, wg, wu, wd, layer, tm):
    d, ff = wg.shape[-2], wg.shape[-1]
    nch = d // 256
    blk = tm * nch
    any_spec = pl.BlockSpec(memory_space=pl.ANY)
    return pl.pallas_call(
        functools.partial(_gmm_kernel, layer=layer),
        out_shape=jax.ShapeDtypeStruct(xs.shape, jnp.uint32),
        grid_spec=pltpu.PrefetchScalarGridSpec(
            num_scalar_prefetch=5, grid=(xs.shape[0] // blk,),
            in_specs=[pl.BlockSpec((blk, 128), lambda j, te, nv, *_: (jnp.minimum(j, nv[0] - 1), 0)),
                      any_spec, any_spec, any_spec],
            out_specs=pl.BlockSpec((blk, 128), lambda j, *_: (j, 0)),
            scratch_shapes=[pltpu.VMEM((2, d, ff), F32), pltpu.VMEM((2, d, ff), F32), pltpu.VMEM((2, ff, d), F32),
                            pltpu.VMEM((d, ff), BF16), pltpu.VMEM((d, ff), BF16), pltpu.VMEM((ff, d), BF16),
                            pltpu.SemaphoreType.DMA((2,))]),
        name="moe_experts",
        compiler_params=_cparams("arbitrary"),
    )(tile_expert, n_valid, first_tile, ordinal, next_expert, xs, wg, wu, wd)


def _combine_kernel(dest_ref, x_ref, w_ref, gt_ref, ys_hbm, *rest, m, last):
    if last:
        g_ref, h_ref, o_ref, buf_ref, sem_ref = rest
    else:
        g_ref, sh_ref, sc_ref, o_ref, h_ref, buf_ref, sem_ref = rest
    i = pl.program_id(0)
    n_tiles = pl.num_programs(0)
    tr = x_ref.shape[0]
    n = buf_ref.shape[2] // tr

    def issue(tile, slot):
        def body(r, carry):
            for k in range(2):
                row = pl.multiple_of(dest_ref[k * m + tile * tr + r], n)
                pltpu.make_async_copy(ys_hbm.at[pl.ds(row, n), :],
                                      buf_ref.at[slot, k, pl.ds(pl.multiple_of(r * n, n), n), :],
                                      sem_ref.at[slot]).start()
            return carry
        lax.fori_loop(0, tr, body, 0, unroll=8)

    @pl.when(i == 0)
    def _():
        issue(0, 0)

    @pl.when(i + 1 < n_tiles)
    def _():
        issue(i + 1, (i + 1) % 2)

    slot = i % 2
    pltpu.make_async_copy(buf_ref.at[slot], buf_ref.at[slot], sem_ref.at[slot]).wait()
    w1, w2 = w_ref[:, 0:1], w_ref[:, 1:2]
    half = n * 128
    for cc in range(n):
        lo1, hi1 = _unpack_cols(buf_ref[slot, 0, _chunk_rows(tr, n, cc), :])
        lo2, hi2 = _unpack_cols(buf_ref[slot, 1, _chunk_rows(tr, n, cc), :])
        ca = slice(cc * 128, (cc + 1) * 128)
        cb = slice(half + cc * 128, half + (cc + 1) * 128)
        o_ref[:, ca] = x_ref[:, ca] + gt_ref[:, ca] * (w1 * lo1 + w2 * lo2)
        o_ref[:, cb] = x_ref[:, cb] + gt_ref[:, cb] * (w1 * hi1 + w2 * hi2)
    if last:
        h_ref[...] = _rms(o_ref[...], g_ref[...])
    else:
        y = _rms(o_ref[...], g_ref[...]) * (1.0 + sc_ref[...]) + sh_ref[...]
        h_ref[...] = y.astype(h_ref.dtype)


def _combine(geo, x, ys, dest, w12, mod, layer, g_next, g_final):
    tr, d = geo.tr, geo.d
    nch = d // 256
    last = g_next is None
    tile = pl.BlockSpec((tr, d), lambda i, *_: (i, 0))
    in_specs = [tile, pl.BlockSpec((tr, 2), lambda i, *_: (i, 0)), geo.mod_spec(layer, 5),
                pl.BlockSpec(memory_space=pl.ANY)]
    operands = [x, w12, mod, ys]
    scratch = [pltpu.VMEM((2, 2, tr * nch, 128), jnp.uint32), pltpu.SemaphoreType.DMA((2,))]
    if last:
        in_specs.append(pl.BlockSpec((1, d), lambda i, *_: (0, 0)))
        operands.append(g_final.reshape(1, d))
        out_shape = jax.ShapeDtypeStruct((geo.b, geo.s, d), F32)
        out_specs = pl.BlockSpec((None, tr, d), lambda i, *_: (i // geo.nt, jnp.maximum(i % geo.nt - geo.nct, 0), 0))
        scratch = [pltpu.VMEM((tr, d), F32)] + scratch
    else:
        in_specs += [pl.BlockSpec((None, 1, d), lambda i, *_: (layer + 1, 0, 0)),
                     geo.mod_spec(layer + 1, 0), geo.mod_spec(layer + 1, 1)]
        operands += [g_next, mod, mod]
        out_shape = [jax.ShapeDtypeStruct((geo.m, d), F32), jax.ShapeDtypeStruct((geo.m, d), BF16)]
        out_specs = [tile, tile]
    return pl.pallas_call(
        functools.partial(_combine_kernel, m=geo.m, last=last),
        out_shape=out_shape,
        grid_spec=pltpu.PrefetchScalarGridSpec(
            num_scalar_prefetch=1, grid=(geo.m // tr,),
            in_specs=in_specs, out_specs=out_specs, scratch_shapes=scratch),
        name="moe_combine",
        compiler_params=_cparams("arbitrary"),
    )(dest, *operands)


def _moe(geo, x, routed, mod, layer, wg, wu, wd, g_next, g_final):
    tm = 256
    hfp, ridx, rwgt, rank, cnt = routed
    n_tiles = (2 * geo.m) // tm + N_EXPERTS
    counts = cnt[:, 0].astype(jnp.int32)
    tiles_per = (counts + tm - 1) // tm
    tile_start = jnp.cumsum(tiles_per) - tiles_per
    base = tile_start * tm
    n_valid = jnp.sum(tiles_per).reshape(1)
    tile_expert = jnp.sum(jnp.arange(n_tiles)[:, None] >= tile_start[None, :], axis=1).astype(jnp.int32) - 1
    experts = jnp.arange(N_EXPERTS, dtype=jnp.int32)[:, None, None]
    dest = (jnp.sum(jnp.where(ridx[None] == experts, base[:, None, None], 0), axis=0) + rank).reshape(-1)
    dest = dest * (geo.d // 256)
    tail = n_valid + jnp.arange(N_EXPERTS)
    fill_rows = jnp.concatenate([jnp.where(tiles_per > 0, base + (tiles_per - 1) * tm, -1),
                                 jnp.where(tail < n_tiles, tail * tm, -1)])
    xs = _dispatch(hfp, dest, fill_rows, n_tiles * tm, geo.tr, tm)
    has_rows = tiles_per > 0
    ordinal = jnp.cumsum(has_rows.astype(jnp.int32)) - 1
    ar = jnp.arange(N_EXPERTS)
    later = jnp.where((ar[None, :] > ar[:, None]) & has_rows[None, :], ar[None, :], N_EXPERTS)
    next_expert = jnp.min(later, axis=1)
    next_expert = jnp.where(next_expert < N_EXPERTS, next_expert, -1).astype(jnp.int32)
    ys = _gmm(xs, tile_expert, n_valid, tile_start, ordinal, next_expert, wg, wu, wd, layer, tm)
    return _combine(geo, x, ys, dest, rwgt.T, mod, layer, g_next, g_final)


def _proj_plans(d):
    src = _src_layout(d)
    off, total = _proj_layout()
    width = lambda name: src[name][1] - src[name][0]
    cf0, sc0 = src["cf"][0], src["sc"][0]
    assert src["gb"][0] == src["gf"][1]
    pieces = [(off["cq"], src["cq"][0], width("cq")), (off["ckv"], src["ckv"][0], width("ckv")),
              (off["kr"], src["kr"][0], width("kr")),
              (off["cf_a"], cf0, CF_WIDTH), (off["cf_b"], cf0 + CF_WIDTH, CF_WIDTH),
              (off["gla_q"], src["gla_q"][0], width("gla_q")), (off["gla_k"], src["gla_k"][0], width("gla_k")),
              (off["gla_v"], src["gla_v"][0], width("gla_v")), (off["gate"], src["gf"][0], 2 * GLA_GATE_RANK),
              (off["gla_r"], src["gla_r"][0], width("gla_r")),
              (off["sc_b"], sc0, SC_WIDTH), (off["sc_c"], sc0 + SC_WIDTH, SC_WIDTH),
              (off["sc_h"], sc0 + 2 * SC_WIDTH, SC_WIDTH)]
    tn = PROJ_TILE
    assert total % tn == 0 and all(dst // tn == (dst + rows - 1) // tn for dst, _, rows in pieces)
    rest = _ProjPlan(tn, total // tn, pieces=sorted(pieces))
    gates_w = N_BRANCHES * d
    gn = min(tn, gates_w)
    gates = _ProjPlan(gn, gates_w // gn, first_row=src["merge"][0])
    return gates, rest, off


def _rope_swap_perm():
    q = MLA_ROPE // 4
    return np.concatenate([np.arange(q, 2 * q), np.arange(0, q), np.arange(3 * q, 4 * q), np.arange(2 * q, 3 * q)])


def _prep_mla(w_uq, w_ukv):
    n_l = w_uq.shape[0]
    perm = _rope_swap_perm()
    wq = w_uq.reshape(n_l, MLA_Q_RANK, MLA_HEADS, MLA_NOPE + MLA_ROPE)
    qn, qr = wq[..., :MLA_NOPE], wq[..., MLA_NOPE:]
    zeros = lambda *s: jnp.zeros((n_l,) + s, w_uq.dtype)
    wqa = jnp.concatenate([qn, qr, qr], axis=-1)
    wqb = jnp.concatenate([zeros(MLA_Q_RANK, MLA_HEADS, MLA_NOPE), qr[..., perm],
                           zeros(MLA_Q_RANK, MLA_HEADS, MLA_ROPE)], axis=-1)
    wkv = w_ukv.reshape(n_l, MLA_KV_RANK, MLA_HEADS, MLA_NOPE + MLA_V)
    wka = jnp.concatenate([wkv[..., :MLA_NOPE], zeros(MLA_KV_RANK, MLA_HEADS, MLA_SLOT - MLA_NOPE)], axis=-1)
    wv = wkv[..., MLA_NOPE:]
    flat = lambda w: w.reshape(n_l, w.shape[1], -1).astype(BF16)
    sela = np.zeros((128, MLA_HEADS, MLA_SLOT), np.float32)
    selb = np.zeros((128, MLA_HEADS, MLA_SLOT), np.float32)
    for r in range(MLA_ROPE):
        sela[r, :, MLA_NOPE + r] = 1.0
        sela[r, :, MLA_NOPE + MLA_ROPE + r] = 1.0
        selb[perm[r], :, MLA_NOPE + r] = 1.0
    sel = lambda s: jnp.asarray(s.reshape(128, -1), BF16)
    return flat(wqa), flat(wqb), flat(wka), sel(sela), sel(selb), flat(wv)


def _rope_tables(geo):
    rows = geo.s // GRID_W
    pos = jnp.arange(rows * GRID_W)
    row = (pos // GRID_W).astype(F32)
    col = (pos % GRID_W).astype(F32)
    n_freq = MLA_ROPE // 4
    freqs = ROPE_THETA ** (-jnp.arange(n_freq, dtype=F32) / n_freq)
    cr, sr = jnp.cos(row[:, None] * freqs), jnp.sin(row[:, None] * freqs)
    cc, sn = jnp.cos(col[:, None] * freqs), jnp.sin(col[:, None] * freqs)
    cos32 = jnp.concatenate([cr, cr, cc, cc], axis=-1)
    sin32 = jnp.concatenate([-sr, sr, -sn, sn], axis=-1)
    s, c = geo.s, geo.c
    one = lambda n, w: jnp.ones((n, w), F32)
    zero = lambda n, w: jnp.zeros((n, w), F32)
    scale = (MLA_NOPE + MLA_ROPE) ** -0.5
    qca = jnp.concatenate([jnp.concatenate([one(c, 64), zero(c, 32), one(c, 32)], 1),
                           jnp.concatenate([one(s, 64), cos32, one(s, 32)], 1)], 0) * scale
    qsb = jnp.concatenate([zero(c, 128), jnp.concatenate([zero(s, 64), sin32, zero(s, 32)], 1)], 0) * scale
    kca = jnp.concatenate([jnp.concatenate([one(c, 64), zero(c, 32), one(c, 32)], 1),
                           jnp.concatenate([one(s, 64), cos32, zero(s, 32)], 1)], 0)
    ksb = jnp.concatenate([zero(c, 128), jnp.concatenate([zero(s, 64), sin32, zero(s, 32)], 1)], 0)
    return qca, qsb, kca, ksb


def kernel(x, c, ctx, c_ctx, w_mod, b_mod, g_mix, g_ffn, w_in, g_q, w_uq, g_kv, w_ukv, w_mla_o, w_cf_dw, b_cf_dw, g_cf_ln, b_cf_ln, w_cf_o, w_gla_gf, b_gla_gf, w_gla_gb, b_gla_gb, g_gla_norm, w_gla_o, w_sc_conv, w_sc_o, w_mix_o, w_router, b_router, w_e_gate, w_e_up, w_e_down, g_final):
    bsz, seq, d = x.shape
    n_ctx = ctx.shape[1]
    depth = w_in.shape[0]
    geo = _Geom(bsz, n_ctx, seq, d)
    assert bsz + 1 <= 8 and seq % GRID_W == 0 and n_ctx % GLA_CHUNK == 0 and seq % GLA_CHUNK == 0

    plan_gates, plan_rest, off = _proj_plans(d)
    w_in_t = jnp.swapaxes(w_in, 1, 2)
    wqa, wqb, wka, sela, selb, wv = _prep_mla(w_uq, w_ukv)
    tabs = _rope_tables(geo)
    w_br_half = [_to_bf16(w, 0.5) for w in (w_mla_o, w_cf_o, w_gla_o, w_sc_o)]
    w_mix = _to_bf16(w_mix_o)
    gate_w = jnp.zeros((depth, 2, 128, GLA_HEADS * GLA_DK), F32)
    gate_w = gate_w.at[:, 0, :GLA_GATE_RANK].set(w_gla_gf).at[:, 1, GLA_GATE_RANK:2 * GLA_GATE_RANK].set(w_gla_gb)
    gate_w = gate_w.astype(BF16)
    gate_b = jnp.stack([b_gla_gf, b_gla_gb], axis=1)[:, :, None, :]
    mat3, msk4 = _gla_tables()
    wr_t = w_router.T
    wr_hi = wr_t.astype(BF16)
    wr_stack = jnp.concatenate([wr_hi, (wr_t - wr_hi.astype(F32)).astype(BF16)], axis=0)
    vec = lambda a: a.reshape(a.shape[0], 1, a.shape[-1])

    cc = jnp.zeros((8, d), F32).at[:bsz].set(c).at[bsz].set(c_ctx)
    mod = _modulation(cc, w_mod, b_mod).reshape(depth, 8, 1, 6 * d)

    xa, h = _first_norm(geo, x, ctx, vec(g_mix), mod)
    for l in range(depth):
        u = _project(h, w_in_t, l, plan_rest, 1.0)
        u_gates = _project(h, w_in_t, l, plan_gates, 0.5)
        q, k, v = _mla_qkv(geo, u, off, l, vec(g_q), vec(g_kv), wqa, wqb, wka, sela, selb, wv, tabs)
        attn = _mla_attention(geo, q, k, v)
        conf, sconv = _conv_branches(geo, u, off, l, w_cf_dw, vec(b_cf_dw), vec(g_cf_ln), vec(b_cf_ln), w_sc_conv)
        gla_f, gla_b = _gla_scan(geo, u, off, l, gate_w, gate_b, mat3, msk4)
        xa, *routed = _merge_mix(geo, attn, conf, gla_f, gla_b, u, u_gates, off, vec(g_gla_norm), sconv, w_br_half,
                                 w_mix, xa, mod, l, vec(g_ffn), wr_stack, b_router.reshape(N_EXPERTS, 1))
        if l + 1 < depth:
            xa, h = _moe(geo, xa, routed, mod, l, w_e_gate, w_e_up, w_e_down, vec(g_mix), None)
        else:
            return _moe(geo, xa, routed, mod, l, w_e_gate, w_e_up, w_e_down, None, g_final)
```

```python
import functools

import numpy as np
import jax
import jax.numpy as jnp
from jax import lax
from jax.experimental import pallas as pl
from jax.experimental.pallas import tpu as pltpu

F32 = jnp.float32
BF16 = jnp.bfloat16
EPS = 1e-6

GRID_W = 64
ROPE_THETA = 10000.0
MLA_HEADS = 8
MLA_NOPE = 64
MLA_ROPE = 32
MLA_V = 64
MLA_Q_RANK = 512
MLA_KV_RANK = 256
MLA_SLOT = 128
CF_WIDTH = 512
CF_KERNEL = 31
GLA_HEADS = 4
GLA_DK = 64
GLA_DV = 128
GLA_GATE_RANK = 16
GLA_GATE_TAU = 16.0
GLA_CHUNK = 64
GLA_LEVELS = 6
SC_WIDTH = 512
SC_KERNEL = 3
N_BRANCHES = 4
N_EXPERTS = 16
N_GROUPS = 4
EXPERT_FF = 512
CONV_HALO = 16
SUBLANES = 8
MERGE_COLS = 512
ATTN_ROWS = 2304
PROJ_TILE = 1024
PROJ_CAST_ROWS = 256
VMEM_LIMIT = 56 * 1024 * 1024


def _cparams(*sem):
    return pltpu.CompilerParams(dimension_semantics=sem, vmem_limit_bytes=VMEM_LIMIT)


def _pick(n, cands):
    for c in cands:
        if n % c == 0:
            return c
    raise ValueError(f"no tile for {n}")


def _dot(a, b):
    return jnp.dot(a, b, preferred_element_type=F32)


def _dot_nt(a, b):
    return lax.dot_general(a, b, (((1,), (1,)), ((), ())), preferred_element_type=F32)


def _proj_layout():
    parts = [("sc_b", SC_WIDTH), ("sc_c", SC_WIDTH), ("sc_h", SC_WIDTH),
             ("cf_a", CF_WIDTH), ("cf_b", CF_WIDTH), ("cq", MLA_Q_RANK), ("gla_v", GLA_HEADS * GLA_DV),
             ("gla_r", GLA_HEADS * GLA_DV), ("ckv", MLA_KV_RANK), ("gla_q", GLA_HEADS * GLA_DK),
             ("gla_k", GLA_HEADS * GLA_DK), ("kr", 128), ("gate", 128)]
    off, start = {}, 0
    for name, width in parts:
        assert start % width == 0
        off[name] = start
        start += width
    return off, start


def _src_layout(d):
    parts = (("cq", MLA_Q_RANK), ("ckv", MLA_KV_RANK), ("kr", MLA_ROPE), ("cf", 2 * CF_WIDTH),
             ("gla_q", GLA_HEADS * GLA_DK), ("gla_k", GLA_HEADS * GLA_DK), ("gla_v", GLA_HEADS * GLA_DV),
             ("gf", GLA_GATE_RANK), ("gb", GLA_GATE_RANK), ("gla_r", GLA_HEADS * GLA_DV),
             ("sc", 3 * SC_WIDTH), ("merge", N_BRANCHES * d))
    off, start = {}, 0
    for name, width in parts:
        off[name] = (start, start + width)
        start += width
    return off


def _mod_kernel(c_ref, w_ref, b_ref, o_ref):
    c = c_ref[...]
    a = (c * jax.nn.sigmoid(c)).astype(BF16)
    o_ref[...] = _dot(a, w_ref[...].astype(BF16)) + b_ref[...]


def _modulation(cc, w_mod, b_mod):
    n_l, d, n6 = w_mod.shape
    tn = _pick(n6, (1024, 512, 256, 128))
    return pl.pallas_call(
        _mod_kernel,
        out_shape=jax.ShapeDtypeStruct((n_l, 8, n6), F32),
        grid=(n_l, n6 // tn),
        in_specs=[pl.BlockSpec((8, d), lambda l, j: (0, 0)),
                  pl.BlockSpec((None, d, tn), lambda l, j: (l, 0, j)),
                  pl.BlockSpec((None, 1, tn), lambda l, j: (l, 0, j))],
        out_specs=pl.BlockSpec((None, 8, tn), lambda l, j: (l, 0, j)),
        name="adaln_mod",
        compiler_params=_cparams("parallel", "parallel"),
    )(cc, w_mod, b_mod.reshape(n_l, 1, n6))


class _Geom:
    def __init__(self, bsz, n_ctx, seq, d):
        self.b, self.c, self.s, self.d = bsz, n_ctx, seq, d
        self.n = n_ctx + seq
        self.m = bsz * self.n
        self.tr = _pick(int(np.gcd(n_ctx, seq)), (256, 128, 64))
        self.nt = self.n // self.tr
        self.nct = n_ctx // self.tr

    def mod_row(self, i):
        b, j = i // self.nt, i % self.nt
        return jnp.where(j < self.nct, self.b, b)

    def mod_spec(self, layer, part):
        return pl.BlockSpec((None, None, 1, self.d), lambda i, *_: (layer, self.mod_row(i), 0, part))


def _first_norm_kernel(x_ref, c_ref, g_ref, sh_ref, sc_ref, xa_ref, h_ref, *, geo):
    is_ctx = pl.program_id(0) % geo.nt < geo.nct
    rows = jnp.where(is_ctx, c_ref[...], x_ref[...])
    xa_ref[...] = rows
    y = rows * lax.rsqrt(jnp.mean(rows * rows, axis=-1, keepdims=True) + EPS) * g_ref[...]
    h_ref[...] = (y * (1.0 + sc_ref[...]) + sh_ref[...]).astype(h_ref.dtype)


def _first_norm(geo, x, ctx, g, mod):
    tr, d = geo.tr, geo.d
    tile = pl.BlockSpec((tr, d), lambda i: (i, 0))
    return pl.pallas_call(
        functools.partial(_first_norm_kernel, geo=geo),
        out_shape=(jax.ShapeDtypeStruct((geo.m, d), F32), jax.ShapeDtypeStruct((geo.m, d), BF16)),
        grid=(geo.m // tr,),
        in_specs=[pl.BlockSpec((None, tr, d), lambda i: (i // geo.nt, jnp.maximum(i % geo.nt - geo.nct, 0), 0)),
                  pl.BlockSpec((None, tr, d), lambda i: (i // geo.nt, jnp.minimum(i % geo.nt, geo.nct - 1), 0)),
                  pl.BlockSpec((None, 1, d), lambda i: (0, 0, 0)),
                  geo.mod_spec(0, 0), geo.mod_spec(0, 1)],
        out_specs=(tile, tile),
        name="first_norm",
        compiler_params=_cparams("parallel"),
    )(x, ctx, g, mod, mod)


def _proj_kernel(a_ref, w_hbm, o_ref, wbuf_ref, wb_ref, sem_ref, *, layer, plan, scale):
    j, i = pl.program_id(0), pl.program_id(1)
    n_tiles = pl.num_programs(0)
    tn = wb_ref.shape[0]

    def copies(t, slot):
        return [pltpu.make_async_copy(w_hbm.at[layer, pl.ds(src, rows), :], wbuf_ref.at[slot, pl.ds(dst, rows), :],
                                      sem_ref.at[slot]) for dst, src, rows in plan(t)]

    def on_tile(t, fn):
        if not plan.static:
            fn(t)
            return
        for tt in range(plan.n_tiles):
            pl.when(t == tt)(functools.partial(fn, tt))

    def start(t):
        for cp in copies(t, t % 2):
            cp.start()

    def land(t):
        slot = t % 2
        for cp in copies(t, slot):
            cp.wait()
        covered = 0
        for dst, _, rows in plan(t):
            if dst > covered:
                wb_ref[covered:dst, :] = jnp.zeros((dst - covered, wb_ref.shape[1]), BF16)
            for r0 in range(0, rows, PROJ_CAST_ROWS):
                rr = min(PROJ_CAST_ROWS, rows - r0)
                rows_f32 = wbuf_ref[slot, dst + r0:dst + r0 + rr, :]
                wb_ref[dst + r0:dst + r0 + rr, :] = (rows_f32 * scale).astype(BF16)
            covered = dst + rows
        if covered < tn:
            wb_ref[covered:tn, :] = jnp.zeros((tn - covered, wb_ref.shape[1]), BF16)

    @pl.when(i == 0)
    def _():
        @pl.when(j == 0)
        def _():
            on_tile(j, start)

        on_tile(j, land)

        @pl.when(j + 1 < n_tiles)
        def _():
            on_tile(j + 1, start)

    o_ref[...] = _dot_nt(a_ref[...], wb_ref[...]).astype(o_ref.dtype)


class _ProjPlan:
    def __init__(self, tn, n_tiles, pieces=None, first_row=None):
        self.tn, self.n_tiles, self.static = tn, n_tiles, pieces is not None
        self._pieces, self._first_row = pieces, first_row

    def __call__(self, t):
        if self.static:
            lo = t * self.tn
            return [(dst - lo, src, rows) for dst, src, rows in self._pieces if lo <= dst < lo + self.tn]
        return [(0, pl.multiple_of(self._first_row + t * self.tn, SUBLANES), self.tn)]


def _project(a, w_t, layer, plan, scale):
    m, k = a.shape
    tn = plan.tn
    tm = _pick(m, (1536, 1152, 1024, 512, 256, 128))
    return pl.pallas_call(
        functools.partial(_proj_kernel, layer=layer, plan=plan, scale=scale),
        out_shape=jax.ShapeDtypeStruct((m, tn * plan.n_tiles), BF16),
        grid=(plan.n_tiles, m // tm),
        in_specs=[pl.BlockSpec((tm, k), lambda j, i: (i, 0)), pl.BlockSpec(memory_space=pl.ANY)],
        out_specs=pl.BlockSpec((tm, tn), lambda j, i: (i, j)),
        scratch_shapes=[pltpu.VMEM((2, tn, k), F32), pltpu.VMEM((tn, k), BF16), pltpu.SemaphoreType.DMA((2,))],
        name="proj_in",
        compiler_params=_cparams("arbitrary", "arbitrary"),
    )(a, w_t)


def _rms(x, g):
    return x * lax.rsqrt(jnp.mean(x * x, axis=-1, keepdims=True) + EPS) * g


def _qkv_kernel(cq_ref, ckv_ref, kr_ref, gq_ref, gkv_ref, wqa_ref, wqb_ref, wka_ref, sela_ref, selb_ref,
                wv_ref, qca_ref, qsb_ref, kca_ref, ksb_ref, q_ref, k_ref, v_ref):
    cqn = _rms(cq_ref[...].astype(F32), gq_ref[...]).astype(BF16)
    ckvn = _rms(ckv_ref[...].astype(F32), gkv_ref[...]).astype(BF16)
    kr = kr_ref[...]
    qa = _dot(cqn, wqa_ref[...])
    qb = _dot(cqn, wqb_ref[...])
    ka = _dot(ckvn, wka_ref[...]) + _dot(kr, sela_ref[...])
    kb = _dot(kr, selb_ref[...])
    qca, qsb, kca, ksb = qca_ref[...], qsb_ref[...], kca_ref[...], ksb_ref[...]
    for h in range(MLA_HEADS):
        sl = slice(h * MLA_SLOT, (h + 1) * MLA_SLOT)
        q_ref[:, sl] = (qa[:, sl] * qca + qb[:, sl] * qsb).astype(BF16)
        k_ref[:, sl] = (ka[:, sl] * kca + kb[:, sl] * ksb).astype(BF16)
    v_ref[...] = _dot(ckvn, wv_ref[...]).astype(BF16)


def _mla_qkv(geo, u, off, layer, gq, gkv, wqa, wqb, wka, sela, selb, wv, tabs):
    tr, m = geo.tr, geo.m
    hq = MLA_HEADS * MLA_SLOT
    hv = MLA_HEADS * MLA_V
    row = lambda w, o: pl.BlockSpec((tr, w), lambda i: (i, o // w))
    lw = lambda a: pl.BlockSpec((None,) + a.shape[1:], lambda i: (layer,) + (0,) * (a.ndim - 1))
    cw = lambda a: pl.BlockSpec(a.shape, lambda i: (0,) * a.ndim)
    tab = pl.BlockSpec((tr, MLA_SLOT), lambda i: (i % geo.nt, 0))
    return pl.pallas_call(
        _qkv_kernel,
        out_shape=(jax.ShapeDtypeStruct((m, hq), BF16), jax.ShapeDtypeStruct((m, hq), BF16),
                   jax.ShapeDtypeStruct((m, hv), BF16)),
        grid=(m // tr,),
        in_specs=[row(MLA_Q_RANK, off["cq"]), row(MLA_KV_RANK, off["ckv"]), row(128, off["kr"]),
                  lw(gq), lw(gkv), lw(wqa), lw(wqb), lw(wka), cw(sela), cw(selb), lw(wv), tab, tab, tab, tab],
        out_specs=(pl.BlockSpec((tr, hq), lambda i: (i, 0)), pl.BlockSpec((tr, hq), lambda i: (i, 0)),
                   pl.BlockSpec((tr, hv), lambda i: (i, 0))),
        name="mla_qkv",
        compiler_params=_cparams("parallel"),
    )(u, u, u, gq, gkv, wqa, wqb, wka, sela, selb, wv, *tabs)


def _attn_kernel(q_ref, k_ref, v_ref, o_ref, *, n_ctx, n_all):
    tq = q_ref.shape[0]

    def run(r0, rows, nk):
        lane = lax.broadcasted_iota(jnp.int32, (rows, 2 * MLA_V), 1)
        outs = []
        for hh in range(2):
            sl = slice(hh * MLA_SLOT, (hh + 1) * MLA_SLOT)
            s = _dot_nt(q_ref[r0:r0 + rows, sl], k_ref[0:nk, sl])
            p = jnp.exp(s - jnp.max(s, axis=-1, keepdims=True))
            inv = 1.0 / jnp.sum(p, axis=-1, keepdims=True)
            outs.append(_dot(p.astype(BF16), v_ref[0:nk, :]) * inv)
        o_ref[r0:r0 + rows, :] = jnp.where(lane < MLA_V, outs[0], outs[1]).astype(o_ref.dtype)

    @pl.when(pl.program_id(2) == 0)
    def _():
        run(0, n_ctx, n_ctx)
        if tq > n_ctx:
            run(n_ctx, tq - n_ctx, n_all)

    @pl.when(pl.program_id(2) > 0)
    def _():
        run(0, tq, n_all)


def _mla_attention(geo, q, k, v):
    n, bsz = geo.n, geo.b
    groups = [g for g in range(1, geo.nt + 1) if geo.nt % g == 0 and g * geo.tr >= geo.c]
    tq = geo.tr * min(groups, key=lambda g: abs(g * geo.tr - ATTN_ROWS))
    q3, k3, v3 = (t.reshape(bsz, n, t.shape[-1]) for t in (q, k, v))
    kern = functools.partial(_attn_kernel, n_ctx=geo.c, n_all=n)
    out = pl.pallas_call(
        kern,
        out_shape=jax.ShapeDtypeStruct((bsz, n, MLA_HEADS * MLA_V), BF16),
        grid=(bsz, MLA_HEADS // 2, n // tq),
        in_specs=[pl.BlockSpec((None, tq, 2 * MLA_SLOT), lambda b, hp, i: (b, i, hp)),
                  pl.BlockSpec((None, n, 2 * MLA_SLOT), lambda b, hp, i: (b, 0, hp)),
                  pl.BlockSpec((None, n, 2 * MLA_V), lambda b, hp, i: (b, 0, hp))],
        out_specs=pl.BlockSpec((None, tq, 2 * MLA_V), lambda b, hp, i: (b, i, hp)),
        name="mla_attn",
        compiler_params=_cparams("parallel", "parallel", "parallel"),
    )(q3, k3, v3)
    return out.reshape(geo.m, MLA_HEADS * MLA_V)


def _halo_flags(geo, i):
    j = i % geo.nt
    left = jnp.logical_and(j != 0, j != geo.nct)
    right = jnp.logical_and(j != geo.nct - 1, j != geo.nt - 1)
    return left.astype(F32), right.astype(F32)


def _conv_shifts(n_taps):
    base = CONV_HALO - n_taps // 2
    return sorted({(base + kk) % SUBLANES for kk in range(n_taps)} - {0})


def _conv_taps(pad_ref, sh_ref, w_ref, n_taps, rows):
    base = CONV_HALO - n_taps // 2
    shifts = _conv_shifts(n_taps)
    span = sh_ref.shape[1]
    for s, r in enumerate(shifts):
        sh_ref[s, :, :] = pad_ref[r:r + span, :]

    def window(o):
        a, r = o - o % SUBLANES, o % SUBLANES
        return pad_ref[a:a + rows, :] if r == 0 else sh_ref[shifts.index(r), a:a + rows, :]

    acc = w_ref[0:1, :] * window(base)
    for kk in range(1, n_taps):
        acc = acc + w_ref[kk:kk + 1, :] * window(base + kk)
    return acc


def _conv_scratch(tr, width, n_taps):
    return [pltpu.VMEM((tr + 2 * CONV_HALO, width), F32),
            pltpu.VMEM((len(_conv_shifts(n_taps)), tr + 2 * CONV_HALO - SUBLANES, width), F32)]


def _conf_kernel(a_ref, b_ref, ap_ref, bp_ref, an_ref, bn_ref, w_ref, bias_ref, g_ref, beta_ref, o_ref,
                 pad_ref, sh_ref, *, geo):
    left, right = _halo_flags(geo, pl.program_id(0))
    rows = a_ref.shape[0]
    glu = lambda a, b: a[...].astype(F32) * jax.nn.sigmoid(b[...].astype(F32))
    pad_ref[0:CONV_HALO, :] = glu(ap_ref, bp_ref) * left
    pad_ref[CONV_HALO:CONV_HALO + rows, :] = glu(a_ref, b_ref)
    pad_ref[CONV_HALO + rows:2 * CONV_HALO + rows, :] = glu(an_ref, bn_ref) * right
    h = _conv_taps(pad_ref, sh_ref, w_ref, CF_KERNEL, rows) + bias_ref[...]
    hc = h - jnp.mean(h, axis=-1, keepdims=True)
    y = hc * lax.rsqrt(jnp.mean(hc * hc, axis=-1, keepdims=True) + EPS) * g_ref[...] + beta_ref[...]
    o_ref[...] = (y * jax.nn.sigmoid(y)).astype(o_ref.dtype)


def _sconv_kernel(gb_ref, gc_ref, h_ref, gcp_ref, hp_ref, gcn_ref, hn_ref, w_ref, o_ref, pad_ref, sh_ref, *, geo):
    left, right = _halo_flags(geo, pl.program_id(0))
    rows = h_ref.shape[0]
    prod = lambda a, b: a[...].astype(F32) * b[...].astype(F32)
    pad_ref[0:CONV_HALO, :] = prod(gcp_ref, hp_ref) * left
    pad_ref[CONV_HALO:CONV_HALO + rows, :] = prod(gc_ref, h_ref)
    pad_ref[CONV_HALO + rows:2 * CONV_HALO + rows, :] = prod(gcn_ref, hn_ref) * right
    y = _conv_taps(pad_ref, sh_ref, w_ref, SC_KERNEL, rows)
    o_ref[...] = (gb_ref[...].astype(F32) * y).astype(o_ref.dtype)


def _halo_specs(geo, width, offset):
    tr = geo.tr
    per = tr // CONV_HALO
    last = geo.m // CONV_HALO - 1
    cur = pl.BlockSpec((tr, width), lambda i: (i, offset // width))
    prev = pl.BlockSpec((CONV_HALO, width), lambda i: (jnp.maximum(i * per - 1, 0), offset // width))
    nxt = pl.BlockSpec((CONV_HALO, width), lambda i: (jnp.minimum((i + 1) * per, last), offset // width))
    return cur, prev, nxt


N_CONF_IN, N_SCONV_IN = 10, 8


def _convs_kernel(*refs, geo):
    n_in = N_CONF_IN + N_SCONV_IN
    conf_o, sconv_o = refs[n_in], refs[n_in + 1]
    conf_scratch, sconv_scratch = refs[n_in + 2:n_in + 4], refs[n_in + 4:n_in + 6]
    _conf_kernel(*refs[:N_CONF_IN], conf_o, *conf_scratch, geo=geo)
    _sconv_kernel(*refs[N_CONF_IN:n_in], sconv_o, *sconv_scratch, geo=geo)


def _conv_branches(geo, u, off, layer, w_dw, b_dw, g_ln, b_ln, w_conv):
    tr = geo.tr
    a_c, a_p, a_n = _halo_specs(geo, CF_WIDTH, off["cf_a"])
    b_c, b_p, b_n = _halo_specs(geo, CF_WIDTH, off["cf_b"])
    gb_c, _, _ = _halo_specs(geo, SC_WIDTH, off["sc_b"])
    gc_c, gc_p, gc_n = _halo_specs(geo, SC_WIDTH, off["sc_c"])
    h_c, h_p, h_n = _halo_specs(geo, SC_WIDTH, off["sc_h"])
    lw = lambda a: pl.BlockSpec((None,) + a.shape[1:], lambda i: (layer,) + (0,) * (a.ndim - 1))
    in_specs = [a_c, b_c, a_p, b_p, a_n, b_n, lw(w_dw), lw(b_dw), lw(g_ln), lw(b_ln),
                gb_c, gc_c, h_c, gc_p, h_p, gc_n, h_n, lw(w_conv)]
    assert len(in_specs) == N_CONF_IN + N_SCONV_IN
    return pl.pallas_call(
        functools.partial(_convs_kernel, geo=geo),
        out_shape=(jax.ShapeDtypeStruct((geo.m, CF_WIDTH), BF16), jax.ShapeDtypeStruct((geo.m, SC_WIDTH), BF16)),
        grid=(geo.m // tr,),
        in_specs=in_specs,
        out_specs=(pl.BlockSpec((tr, CF_WIDTH), lambda i: (i, 0)), pl.BlockSpec((tr, SC_WIDTH), lambda i: (i, 0))),
        scratch_shapes=_conv_scratch(tr, CF_WIDTH, CF_KERNEL) + _conv_scratch(tr, SC_WIDTH, SC_KERNEL),
        name="conv_branches",
        compiler_params=_cparams("parallel"),
    )(*([u] * 6), w_dw, b_dw, g_ln, b_ln, *([u] * 7), w_conv)


def _gla_tables():
    c = GLA_CHUNK
    n_rows = (GLA_LEVELS + 2) * c + 8
    mat = np.zeros((2, n_rows, c), np.float32)
    msk = np.zeros((2, GLA_LEVELS + 1, c, c), np.float32)
    for lvl in range(GLA_LEVELS):
        hs = (c // 2) >> lvl
        for t in range(c):
            mid = (t // (2 * hs)) * 2 * hs + hs
            if t >= mid:
                mat[0, lvl * c + t, mid + 1:t + 1] = 1.0
                msk[0, lvl, t, mid - hs:mid] = 1.0
            else:
                mat[0, lvl * c + t, t + 1:mid + 1] = 1.0
    msk[0, GLA_LEVELS] = np.eye(c, dtype=np.float32)
    for t in range(c):
        mat[0, GLA_LEVELS * c + t, :t + 1] = 1.0
        mat[0, (GLA_LEVELS + 1) * c + t, t + 1:] = 1.0
    mat[0, (GLA_LEVELS + 2) * c:, :] = 1.0
    n_blk = GLA_LEVELS + 2
    mat[1, :n_blk * c] = mat[0, :n_blk * c].reshape(n_blk, c, c)[:, ::-1, ::-1].reshape(n_blk * c, c)
    mat[1, n_blk * c:] = 1.0
    msk[1] = msk[0][:, ::-1, ::-1]
    mat3 = np.concatenate([mat, mat, mat], axis=2)
    msk4 = np.tile(msk, (1, 1, GLA_HEADS, 1))
    return jnp.asarray(mat3, BF16), jnp.asarray(msk4, F32)


def _gla_kernel(qf_ref, kf_ref, vf_ref, gf_ref, qb_ref, kb_ref, vb_ref, gb_ref, w2_ref, b2_ref, mat_ref, msk_ref,
                of_ref, ob_ref, st_ref):
    @pl.when(pl.program_id(0) == 0)
    def _():
        st_ref[...] = jnp.zeros_like(st_ref)

    ins = ((qf_ref, kf_ref, vf_ref, gf_ref, of_ref), (qb_ref, kb_ref, vb_ref, gb_ref, ob_ref))
    for b in range(qf_ref.shape[0]):
        for d, (q_ref, k_ref, v_ref, g_ref, o_ref) in enumerate(ins):
            _gla_chunk(q_ref.at[b], k_ref.at[b], v_ref.at[b], g_ref.at[b], w2_ref.at[d], b2_ref.at[d],
                       mat_ref.at[d], msk_ref.at[d], o_ref.at[b], st_ref.at[d, b])


def _gla_chunk(q_ref, k_ref, v_ref, g_ref, w2_ref, b2_ref, mat_ref, msk_ref, o_ref, st_ref):
    c = GLA_CHUNK
    z = _dot(g_ref[...], w2_ref[...]) + b2_ref[...]
    la = (jnp.minimum(z, 0.0) - jnp.log1p(jnp.exp(-jnp.abs(z)))) * (1.0 / GLA_GATE_TAU)
    hi = la.astype(BF16)
    r1 = la - hi.astype(F32)
    lo = r1.astype(BF16)
    lo2 = (r1 - lo.astype(F32)).astype(BF16)
    ea = jnp.exp(_dot(mat_ref[...], jnp.concatenate([hi, lo, lo2], axis=0)))

    head = lax.broadcasted_iota(jnp.int32, (1, GLA_HEADS * GLA_DK), 1) // GLA_DK
    zero = jnp.zeros((), BF16)

    def stack(x):
        xb = x.astype(BF16)
        return jnp.concatenate([jnp.where(head == h, xb, zero) for h in range(GLA_HEADS)], axis=0)

    q = q_ref[...].astype(F32) * (GLA_DK ** -0.5)
    k = k_ref[...].astype(F32)
    vb = v_ref[...]

    attn = _dot_nt(stack(q), k.astype(BF16)) * msk_ref[GLA_LEVELS]
    for lvl in range(GLA_LEVELS):
        e = ea[lvl * c:(lvl + 1) * c]
        attn = attn + _dot_nt(stack(q * e), (k * e).astype(BF16)) * msk_ref[lvl]
    o_intra = _dot(attn.astype(BF16), vb)

    st = st_ref[...]
    o_inter = _dot_nt(stack(q * ea[GLA_LEVELS * c:(GLA_LEVELS + 1) * c]), st.astype(BF16))
    for h in range(GLA_HEADS):
        rows = slice(h * c, (h + 1) * c)
        cols = slice(h * GLA_DV, (h + 1) * GLA_DV)
        o_ref[:, cols] = o_intra[rows, cols] + o_inter[rows, :]

    k_dec = stack(k * ea[(GLA_LEVELS + 1) * c:(GLA_LEVELS + 2) * c])
    v_rows = jnp.concatenate([vb[:, h * GLA_DV:(h + 1) * GLA_DV] for h in range(GLA_HEADS)], axis=0)
    eye = (lax.broadcasted_iota(jnp.int32, (GLA_DV, GLA_DV), 0)
           == lax.broadcasted_iota(jnp.int32, (GLA_DV, GLA_DV), 1)).astype(BF16)
    v_t = _dot_nt(eye, v_rows).astype(BF16)
    tot = ea[(GLA_LEVELS + 2) * c:(GLA_LEVELS + 2) * c + 1]
    st_ref[...] = st * tot + _dot(v_t, k_dec)


def _gla_scan(geo, u, off, layer, w2, b2, mat3, msk4):
    c = GLA_CHUNK
    ncc = geo.c // c
    nc = geo.n // c
    dk, dv = GLA_HEADS * GLA_DK, GLA_HEADS * GLA_DV

    fwd = lambda i: i
    bwd = lambda i: jnp.where(i < ncc, ncc - 1 - i, nc - 1 - (i - ncc))

    u3 = u.reshape(geo.b, geo.n, u.shape[-1])
    row = lambda w, o, chunk: pl.BlockSpec((geo.b, c, w), lambda i: (0, chunk(i), o // w))
    rows = lambda chunk: [row(dk, off["gla_q"], chunk), row(dk, off["gla_k"], chunk), row(dv, off["gla_v"], chunk),
                          row(128, off["gate"], chunk)]
    out_sds = jax.ShapeDtypeStruct((geo.b, geo.n, dv), F32)
    o_f, o_b = pl.pallas_call(
        _gla_kernel,
        out_shape=(out_sds, out_sds),
        grid=(nc,),
        in_specs=rows(fwd) + rows(bwd) + [
            pl.BlockSpec((None, 2, 128, dk), lambda i: (layer, 0, 0, 0)),
            pl.BlockSpec((None, 2, 1, dk), lambda i: (layer, 0, 0, 0)),
            pl.BlockSpec(mat3.shape, lambda i: (0, 0, 0)),
            pl.BlockSpec(msk4.shape, lambda i: (0, 0, 0, 0))],
        out_specs=(pl.BlockSpec((geo.b, c, dv), lambda i: (0, fwd(i), 0)),
                   pl.BlockSpec((geo.b, c, dv), lambda i: (0, bwd(i), 0))),
        scratch_shapes=[pltpu.VMEM((2, geo.b, GLA_DV, dk), F32)],
        name="gla_scan",
        compiler_params=_cparams("arbitrary"),
    )(u3, u3, u3, u3, u3, u3, u3, u3, w2, b2, mat3, msk4)
    return o_f.reshape(geo.m, dv), o_b.reshape(geo.m, dv)


def _merge_mix_kernel(attn_ref, conf_ref, glaf_ref, glab_ref, r_ref, gn_ref, sc_ref, gate_ref, w0_ref, w1_ref,
                      w2_ref, w3_ref, wmix_ref, x_ref, gt_ref, gf_ref, shf_ref, scf_ref, wr_ref, br_ref, tri_ref,
                      o_ref, h_ref, idx_ref, wgt_ref, rank_ref, cnt_ref, y_ref, carry_ref):
    @pl.when(pl.program_id(0) == 0)
    def _():
        carry_ref[...] = jnp.zeros_like(carry_ref)

    wbr_ref = (w0_ref, w1_ref, w2_ref, w3_ref)
    o = glaf_ref[...] + glab_ref[...]
    r = r_ref[...].astype(F32)
    gn = gn_ref[...]
    normed = [_rms(o[:, h * GLA_DV:(h + 1) * GLA_DV], gn) for h in range(GLA_HEADS)]
    gla = (jnp.concatenate(normed, axis=1) * (r * jax.nn.sigmoid(r))).astype(BF16)
    acts = (attn_ref[...], conf_ref[...], gla, sc_ref[...])
    d = x_ref.shape[1]
    cw = min(MERGE_COLS, d)
    for c0 in range(0, d, cw):
        y = None
        for i in range(N_BRANCHES):
            half_b = _dot(acts[i], wbr_ref[i][:, c0:c0 + cw])
            t = jnp.tanh(gate_ref[:, i * d + c0:i * d + c0 + cw].astype(F32))
            term = t * half_b + half_b
            y = term if y is None else y + term
        y_ref[:, c0:c0 + cw] = y.astype(BF16)
    y_all = y_ref[...]
    for c0 in range(0, d, cw):
        cols = slice(c0, c0 + cw)
        o_ref[:, cols] = x_ref[:, cols] + gt_ref[:, cols] * _dot(y_all, wmix_ref[:, cols])
    _route_rows(o_ref[...], gf_ref, shf_ref, scf_ref, wr_ref, br_ref, tri_ref, h_ref, idx_ref, wgt_ref, rank_ref,
                cnt_ref, carry_ref)


def _merge_mix(geo, attn, conf, gla_f, gla_b, u, u_gates, off, g_norm, sconv, w_br_half, w_mix, x, mod, layer,
               g_ffn, wr_stack, b_router):
    tm, d, m = geo.tr, geo.d, geo.m
    w512 = GLA_HEADS * GLA_DV
    nch = d // 256
    act = pl.BlockSpec((tm, w512), lambda i: (i, 0))
    once = pl.Buffered(1)
    w_br = pl.BlockSpec((None, w512, d), lambda i: (layer, 0, 0), pipeline_mode=once)
    tri = jnp.asarray(np.triu(np.ones((tm, tm), np.float32), 1), BF16)
    pair = lambda dt: jax.ShapeDtypeStruct((2, m), dt)
    pair_spec = pl.BlockSpec((2, tm), lambda i: (0, i))
    return pl.pallas_call(
        _merge_mix_kernel,
        out_shape=(jax.ShapeDtypeStruct((m, d), F32), jax.ShapeDtypeStruct((m * nch, 128), jnp.uint32),
                   pair(jnp.int32), pair(F32), pair(jnp.int32), jax.ShapeDtypeStruct((N_EXPERTS, 1), F32)),
        grid=(m // tm,),
        in_specs=[act, act, act, act,
                  pl.BlockSpec((tm, w512), lambda i: (i, off["gla_r"] // w512)),
                  pl.BlockSpec((None, 1, GLA_DV), lambda i: (layer, 0, 0)),
                  act, pl.BlockSpec((tm, N_BRANCHES * d), lambda i: (i, 0)),
                  w_br, w_br, w_br, w_br,
                  pl.BlockSpec((None, d, d), lambda i: (layer, 0, 0), pipeline_mode=once),
                  pl.BlockSpec((tm, d), lambda i: (i, 0)), geo.mod_spec(layer, 2),
                  pl.BlockSpec((None, 1, d), lambda i: (layer, 0, 0)),
                  geo.mod_spec(layer, 3), geo.mod_spec(layer, 4),
                  pl.BlockSpec(wr_stack.shape, lambda i: (0, 0)),
                  pl.BlockSpec(b_router.shape, lambda i: (0, 0)),
                  pl.BlockSpec((tm, tm), lambda i: (0, 0))],
        out_specs=(pl.BlockSpec((tm, d), lambda i: (i, 0)), pl.BlockSpec((tm * nch, 128), lambda i: (i, 0)),
                   pair_spec, pair_spec, pair_spec, pl.BlockSpec((N_EXPERTS, 1), lambda i: (0, 0))),
        scratch_shapes=[pltpu.VMEM((tm, d), BF16), pltpu.VMEM((N_EXPERTS, 1), F32)],
        name="merge_mix",
        compiler_params=_cparams("arbitrary"),
    )(attn, conf, gla_f, gla_b, u, g_norm, sconv, u_gates, *w_br_half, w_mix, x, mod, g_ffn, mod, mod, wr_stack,
      b_router, tri)


def _cast_kernel(w_ref, o_ref, *, scale):
    o_ref[...] = (w_ref[...] * scale).astype(o_ref.dtype)


def _to_bf16(w, scale=1.0):
    n_l, rows, cols = w.shape
    tr = _pick(rows, (512, 256, 128))
    return pl.pallas_call(
        functools.partial(_cast_kernel, scale=scale),
        out_shape=jax.ShapeDtypeStruct(w.shape, BF16),
        grid=(n_l, rows // tr),
        in_specs=[pl.BlockSpec((None, tr, cols), lambda l, i: (l, i, 0))],
        out_specs=pl.BlockSpec((None, tr, cols), lambda l, i: (l, i, 0)),
        name="weight_cast",
        compiler_params=_cparams("parallel", "parallel"),
    )(w)


def _second_largest(a, b, c, d):
    return jnp.maximum(jnp.maximum(jnp.minimum(a, b), jnp.minimum(c, d)),
                       jnp.minimum(jnp.maximum(a, b), jnp.maximum(c, d)))


def _chunk_rows(rows, n, cc):
    return pl.ds(cc, rows, stride=n) if n > 1 else pl.ds(0, rows)


def _pack_rows(ref, y):
    rows = y.shape[0]
    n = ref.shape[0] // rows
    half = n * 128
    bits = lambda v: pltpu.bitcast(v.astype(BF16).astype(F32), jnp.uint32)
    packed = (bits(y[:, :half]) >> 16) | bits(y[:, half:])
    for cc in range(n):
        ref[_chunk_rows(rows, n, cc), :] = packed[:, cc * 128:(cc + 1) * 128]


def _unpack_cols(words):
    lo = pltpu.bitcast(words << 16, F32)
    hi = pltpu.bitcast(words & jnp.uint32(0xFFFF0000), F32)
    return lo, hi


def _route_rows(x, g_ref, sh_ref, sc_ref, wr_ref, br_ref, tri_ref, h_ref, idx_ref, wgt_ref, rank_ref, cnt_ref,
                carry_ref):
    y = _rms(x, g_ref[...]) * (1.0 + sc_ref[...]) + sh_ref[...]
    hi = y.astype(BF16)
    lo = (y - hi.astype(F32)).astype(BF16)
    _pack_rows(h_ref, y)
    ne = N_EXPERTS
    two = _dot_nt(wr_ref[...], hi)
    logits = two[0:ne] + two[ne:2 * ne] + _dot_nt(wr_ref[0:ne, :], lo)
    aff = 1.0 / (1.0 + jnp.exp(-logits))
    sel = aff + br_ref[...]
    per = ne // N_GROUPS
    gs = []
    for g in range(N_GROUPS):
        a, b, c, d = (sel[g * per + r:g * per + r + 1] for r in range(per))
        top1 = jnp.maximum(jnp.maximum(a, b), jnp.maximum(c, d))
        gs.append(top1 + _second_largest(a, b, c, d))
    best = gs[0]
    for g in range(1, N_GROUPS):
        best = jnp.maximum(best, gs[g])
    grp = jnp.full(best.shape, N_GROUPS, jnp.int32)
    for g in reversed(range(N_GROUPS)):
        grp = jnp.where(gs[g] == best, g, grp)
    erow = lax.broadcasted_iota(jnp.int32, sel.shape, 0)
    masked = jnp.where(erow // per == grp, sel, -jnp.inf)
    m1 = jnp.max(masked, axis=0, keepdims=True)
    i1 = jnp.min(jnp.where(masked == m1, erow, ne), axis=0, keepdims=True)
    one1 = erow == i1
    masked2 = jnp.where(one1, -jnp.inf, masked)
    m2 = jnp.max(masked2, axis=0, keepdims=True)
    i2 = jnp.min(jnp.where(masked2 == m2, erow, ne), axis=0, keepdims=True)
    one2 = erow == i2
    w1 = jnp.sum(jnp.where(one1, aff, 0.0), axis=0, keepdims=True)
    w2 = jnp.sum(jnp.where(one2, aff, 0.0), axis=0, keepdims=True)
    inv = 1.0 / (w1 + w2)
    idx_ref[0:1, :] = i1
    idx_ref[1:2, :] = i2
    wgt_ref[0:1, :] = w1 * inv
    wgt_ref[1:2, :] = w2 * inv
    o1 = jnp.where(one1, 1.0, 0.0)
    o2 = jnp.where(one2, 1.0, 0.0)
    tri = tri_ref[...]
    c0 = carry_ref[...]
    c1 = c0 + jnp.sum(o1, axis=1, keepdims=True)
    rank_ref[0:1, :] = jnp.sum(o1 * (c0 + _dot(o1.astype(BF16), tri)), axis=0, keepdims=True).astype(jnp.int32)
    rank_ref[1:2, :] = jnp.sum(o2 * (c1 + _dot(o2.astype(BF16), tri)), axis=0, keepdims=True).astype(jnp.int32)
    c2 = c1 + jnp.sum(o2, axis=1, keepdims=True)
    carry_ref[...] = c2
    cnt_ref[...] = c2


def _dispatch_kernel(dest_ref, last_ref, h_ref, xs_hbm, zero_ref, sem, zero_sem, *, m, n):
    i = pl.program_id(0)
    tr = h_ref.shape[0] // n

    @pl.when(i == 0)
    def _():
        zero_ref[...] = jnp.zeros_like(zero_ref)

        def fills(e):
            rows = zero_ref.shape[0]
            start = pl.multiple_of(last_ref[e] * n, rows)
            return pltpu.make_async_copy(zero_ref, xs_hbm.at[pl.ds(start, rows), :], zero_sem)

        for e in range(2 * N_EXPERTS):
            @pl.when(last_ref[e] >= 0)
            def _():
                fills(e).start()
        for e in range(2 * N_EXPERTS):
            @pl.when(last_ref[e] >= 0)
            def _():
                fills(e).wait()

    def issue(r, carry):
        src = h_ref.at[pl.ds(pl.multiple_of(r * n, n), n), :]
        for k in range(2):
            row = pl.multiple_of(dest_ref[k * m + i * tr + r], n)
            pltpu.make_async_copy(src, xs_hbm.at[pl.ds(row, n), :], sem).start()
        return carry

    lax.fori_loop(0, tr, issue, 0, unroll=8)
    everything = xs_hbm.at[pl.ds(0, 2 * tr * n), :]
    pltpu.make_async_copy(everything, everything, sem).wait()


def _dispatch(hfp, dest, last_tile_row, n_rows, tr, tm):
    n = hfp.shape[0] * 2 // dest.shape[0]
    m = hfp.shape[0] // n
    return pl.pallas_call(
        functools.partial(_dispatch_kernel, m=m, n=n),
        out_shape=jax.ShapeDtypeStruct((n_rows * n, 128), jnp.uint32),
        grid_spec=pltpu.PrefetchScalarGridSpec(
            num_scalar_prefetch=2, grid=(m // tr,),
            in_specs=[pl.BlockSpec((tr * n, 128), lambda i, *_: (i, 0))],
            out_specs=pl.BlockSpec(memory_space=pl.ANY),
            scratch_shapes=[pltpu.VMEM((tm * n, 128), jnp.uint32), pltpu.SemaphoreType.DMA(()),
                            pltpu.SemaphoreType.DMA(())]),
        name="moe_dispatch",
        compiler_params=_cparams("arbitrary"),
    )(dest, last_tile_row, hfp)


def _gmm_kernel(te_ref, nv_ref, first_ref, ord_ref, next_ref, xs_ref, wg_hbm, wu_hbm, wd_hbm, ys_ref, fg_ref, fu_ref,
                fd_ref, wgb_ref, wub_ref, wdb_ref, sem_ref, *, layer):
    j = pl.program_id(0)

    def fetch(e, slot):
        return [pltpu.make_async_copy(src.at[layer, e], dst.at[slot], sem_ref.at[slot])
                for src, dst in ((wg_hbm, fg_ref), (wu_hbm, fu_ref), (wd_hbm, fd_ref))]

    @pl.when(j == 0)
    def _():
        for cp in fetch(te_ref[0], 0):
            cp.start()

    @pl.when(j < nv_ref[0])
    def _():
        e = te_ref[j]

        @pl.when(j == first_ref[e])
        def _():
            slot = ord_ref[e] % 2
            for cp in fetch(e, slot):
                cp.wait()
            wgb_ref[...] = fg_ref[slot].astype(BF16)
            wub_ref[...] = fu_ref[slot].astype(BF16)
            wdb_ref[...] = fd_ref[slot].astype(BF16)

            @pl.when(next_ref[e] >= 0)
            def _():
                for cp in fetch(next_ref[e], 1 - slot):
                    cp.start()

        half = wgb_ref.shape[0] // 2
        n = half // 128
        tm = xs_ref.shape[0] // n
        parts = [_unpack_cols(xs_ref[_chunk_rows(tm, n, cc), :]) for cc in range(n)]
        xa = jnp.concatenate([p[0].astype(BF16) for p in parts], axis=1)
        xb = jnp.concatenate([p[1].astype(BF16) for p in parts], axis=1)
        g = _dot(xa, wgb_ref[0:half, :]) + _dot(xb, wgb_ref[half:2 * half, :])
        u = _dot(xa, wub_ref[0:half, :]) + _dot(xb, wub_ref[half:2 * half, :])
        act = (g * jax.nn.sigmoid(g) * u).astype(BF16)
        _pack_rows(ys_ref, _dot(act, wdb_ref[...]))

    @pl.when(j >= nv_ref[0])
    def _():
        ys_ref[...] = jnp.zeros_like(ys_ref)


def _gmm(xs, tile_expert, n_valid, first_tile, ordinal, next_expert, wg, wu, wd, layer, tm):
    d, ff = wg.shape[-2], wg.shape[-1]
    nch = d // 256
    blk = tm * nch
    any_spec = pl.BlockSpec(memory_space=pl.ANY)
    return pl.pallas_call(
        functools.partial(_gmm_kernel, layer=layer),
        out_shape=jax.ShapeDtypeStruct(xs.shape, jnp.uint32),
        grid_spec=pltpu.PrefetchScalarGridSpec(
            num_scalar_prefetch=5, grid=(xs.shape[0] // blk,),
            in_specs=[pl.BlockSpec((blk, 128), lambda j, te, nv, *_: (jnp.minimum(j, nv[0] - 1), 0)),
                      any_spec, any_spec, any_spec],
            out_specs=pl.BlockSpec((blk, 128), lambda j, *_: (j, 0)),
            scratch_shapes=[pltpu.VMEM((2, d, ff), F32), pltpu.VMEM((2, d, ff), F32), pltpu.VMEM((2, ff, d), F32),
                            pltpu.VMEM((d, ff), BF16), pltpu.VMEM((d, ff), BF16), pltpu.VMEM((ff, d), BF16),
                            pltpu.SemaphoreType.DMA((2,))]),
        name="moe_experts",
        compiler_params=_cparams("arbitrary"),
    )(tile_expert, n_valid, first_tile, ordinal, next_expert, xs, wg, wu, wd)


def _combine_kernel(dest_ref, x_ref, w_ref, gt_ref, ys_hbm, *rest, m, last):
    if last:
        g_ref, h_ref, o_ref, buf_ref, sem_ref = rest
    else:
        g_ref, sh_ref, sc_ref, o_ref, h_ref, buf_ref, sem_ref = rest
    i = pl.program_id(0)
    n_tiles = pl.num_programs(0)
    tr = x_ref.shape[0]
    n = buf_ref.shape[2] // tr

    def issue(tile, slot):
        def body(r, carry):
            for k in range(2):
                row = pl.multiple_of(dest_ref[k * m + tile * tr + r], n)
                pltpu.make_async_copy(ys_hbm.at[pl.ds(row, n), :],
                                      buf_ref.at[slot, k, pl.ds(pl.multiple_of(r * n, n), n), :],
                                      sem_ref.at[slot]).start()
            return carry
        lax.fori_loop(0, tr, body, 0, unroll=8)

    @pl.when(i == 0)
    def _():
        issue(0, 0)

    @pl.when(i + 1 < n_tiles)
    def _():
        issue(i + 1, (i + 1) % 2)

    slot = i % 2
    pltpu.make_async_copy(buf_ref.at[slot], buf_ref.at[slot], sem_ref.at[slot]).wait()
    w1, w2 = w_ref[:, 0:1], w_ref[:, 1:2]
    half = n * 128
    for cc in range(n):
        lo1, hi1 = _unpack_cols(buf_ref[slot, 0, _chunk_rows(tr, n, cc), :])
        lo2, hi2 = _unpack_cols(buf_ref[slot, 1, _chunk_rows(tr, n, cc), :])
        ca = slice(cc * 128, (cc + 1) * 128)
        cb = slice(half + cc * 128, half + (cc + 1) * 128)
        o_ref[:, ca] = x_ref[:, ca] + gt_ref[:, ca] * (w1 * lo1 + w2 * lo2)
        o_ref[:, cb] = x_ref[:, cb] + gt_ref[:, cb] * (w1 * hi1 + w2 * hi2)
    if last:
        h_ref[...] = _rms(o_ref[...], g_ref[...])
    else:
        y = _rms(o_ref[...], g_ref[...]) * (1.0 + sc_ref[...]) + sh_ref[...]
        h_ref[...] = y.astype(h_ref.dtype)


def _combine(geo, x, ys, dest, w12, mod, layer, g_next, g_final):
    tr, d = geo.tr, geo.d
    nch = d // 256
    last = g_next is None
    tile = pl.BlockSpec((tr, d), lambda i, *_: (i, 0))
    in_specs = [tile, pl.BlockSpec((tr, 2), lambda i, *_: (i, 0)), geo.mod_spec(layer, 5),
                pl.BlockSpec(memory_space=pl.ANY)]
    operands = [x, w12, mod, ys]
    scratch = [pltpu.VMEM((2, 2, tr * nch, 128), jnp.uint32), pltpu.SemaphoreType.DMA((2,))]
    if last:
        in_specs.append(pl.BlockSpec((1, d), lambda i, *_: (0, 0)))
        operands.append(g_final.reshape(1, d))
        out_shape = jax.ShapeDtypeStruct((geo.b, geo.s, d), F32)
        out_specs = pl.BlockSpec((None, tr, d), lambda i, *_: (i // geo.nt, jnp.maximum(i % geo.nt - geo.nct, 0), 0))
        scratch = [pltpu.VMEM((tr, d), F32)] + scratch
    else:
        in_specs += [pl.BlockSpec((None, 1, d), lambda i, *_: (layer + 1, 0, 0)),
                     geo.mod_spec(layer + 1, 0), geo.mod_spec(layer + 1, 1)]
        operands += [g_next, mod, mod]
        out_shape = [jax.ShapeDtypeStruct((geo.m, d), F32), jax.ShapeDtypeStruct((geo.m, d), BF16)]
        out_specs = [tile, tile]
    return pl.pallas_call(
        functools.partial(_combine_kernel, m=geo.m, last=last),
        out_shape=out_shape,
        grid_spec=pltpu.PrefetchScalarGridSpec(
            num_scalar_prefetch=1, grid=(geo.m // tr,),
            in_specs=in_specs, out_specs=out_specs, scratch_shapes=scratch),
        name="moe_combine",
        compiler_params=_cparams("arbitrary"),
    )(dest, *operands)


def _moe(geo, x, routed, mod, layer, wg, wu, wd, g_next, g_final):
    tm = 256
    hfp, ridx, rwgt, rank, cnt = routed
    n_tiles = (2 * geo.m) // tm + N_EXPERTS
    counts = cnt[:, 0].astype(jnp.int32)
    tiles_per = (counts + tm - 1) // tm
    tile_start = jnp.cumsum(tiles_per) - tiles_per
    base = tile_start * tm
    n_valid = jnp.sum(tiles_per).reshape(1)
    tile_expert = jnp.sum(jnp.arange(n_tiles)[:, None] >= tile_start[None, :], axis=1).astype(jnp.int32) - 1
    experts = jnp.arange(N_EXPERTS, dtype=jnp.int32)[:, None, None]
    dest = (jnp.sum(jnp.where(ridx[None] == experts, base[:, None, None], 0), axis=0) + rank).reshape(-1)
    dest = dest * (geo.d // 256)
    tail = n_valid + jnp.arange(N_EXPERTS)
    fill_rows = jnp.concatenate([jnp.where(tiles_per > 0, base + (tiles_per - 1) * tm, -1),
                                 jnp.where(tail < n_tiles, tail * tm, -1)])
    xs = _dispatch(hfp, dest, fill_rows, n_tiles * tm, geo.tr, tm)
    has_rows = tiles_per > 0
    ordinal = jnp.cumsum(has_rows.astype(jnp.int32)) - 1
    ar = jnp.arange(N_EXPERTS)
    later = jnp.where((ar[None, :] > ar[:, None]) & has_rows[None, :], ar[None, :], N_EXPERTS)
    next_expert = jnp.min(later, axis=1)
    next_expert = jnp.where(next_expert < N_EXPERTS, next_expert, -1).astype(jnp.int32)
    ys = _gmm(xs, tile_expert, n_valid, tile_start, ordinal, next_expert, wg, wu, wd, layer, tm)
    return _combine(geo, x, ys, dest, rwgt.T, mod, layer, g_next, g_final)


def _proj_plans(d):
    src = _src_layout(d)
    off, total = _proj_layout()
    width = lambda name: src[name][1] - src[name][0]
    cf0, sc0 = src["cf"][0], src["sc"][0]
    assert src["gb"][0] == src["gf"][1]
    pieces = [(off["cq"], src["cq"][0], width("cq")), (off["ckv"], src["ckv"][0], width("ckv")),
              (off["kr"], src["kr"][0], width("kr")),
              (off["cf_a"], cf0, CF_WIDTH), (off["cf_b"], cf0 + CF_WIDTH, CF_WIDTH),
              (off["gla_q"], src["gla_q"][0], width("gla_q")), (off["gla_k"], src["gla_k"][0], width("gla_k")),
              (off["gla_v"], src["gla_v"][0], width("gla_v")), (off["gate"], src["gf"][0], 2 * GLA_GATE_RANK),
              (off["gla_r"], src["gla_r"][0], width("gla_r")),
              (off["sc_b"], sc0, SC_WIDTH), (off["sc_c"], sc0 + SC_WIDTH, SC_WIDTH),
              (off["sc_h"], sc0 + 2 * SC_WIDTH, SC_WIDTH)]
    tn = PROJ_TILE
    assert total % tn == 0 and all(dst // tn == (dst + rows - 1) // tn for dst, _, rows in pieces)
    rest = _ProjPlan(tn, total // tn, pieces=sorted(pieces))
    gates_w = N_BRANCHES * d
    gn = min(tn, gates_w)
    gates = _ProjPlan(gn, gates_w // gn, first_row=src["merge"][0])
    return gates, rest, off


def _rope_swap_perm():
    q = MLA_ROPE // 4
    return np.concatenate([np.arange(q, 2 * q), np.arange(0, q), np.arange(3 * q, 4 * q), np.arange(2 * q, 3 * q)])


def _prep_mla(w_uq, w_ukv):
    n_l = w_uq.shape[0]
    perm = _rope_swap_perm()
    wq = w_uq.reshape(n_l, MLA_Q_RANK, MLA_HEADS, MLA_NOPE + MLA_ROPE)
    qn, qr = wq[..., :MLA_NOPE], wq[..., MLA_NOPE:]
    zeros = lambda *s: jnp.zeros((n_l,) + s, w_uq.dtype)
    wqa = jnp.concatenate([qn, qr, qr], axis=-1)
    wqb = jnp.concatenate([zeros(MLA_Q_RANK, MLA_HEADS, MLA_NOPE), qr[..., perm],
                           zeros(MLA_Q_RANK, MLA_HEADS, MLA_ROPE)], axis=-1)
    wkv = w_ukv.reshape(n_l, MLA_KV_RANK, MLA_HEADS, MLA_NOPE + MLA_V)
    wka = jnp.concatenate([wkv[..., :MLA_NOPE], zeros(MLA_KV_RANK, MLA_HEADS, MLA_SLOT - MLA_NOPE)], axis=-1)
    wv = wkv[..., MLA_NOPE:]
    flat = lambda w: w.reshape(n_l, w.shape[1], -1).astype(BF16)
    sela = np.zeros((128, MLA_HEADS, MLA_SLOT), np.float32)
    selb = np.zeros((128, MLA_HEADS, MLA_SLOT), np.float32)
    for r in range(MLA_ROPE):
        sela[r, :, MLA_NOPE + r] = 1.0
        sela[r, :, MLA_NOPE + MLA_ROPE + r] = 1.0
        selb[perm[r], :, MLA_NOPE + r] = 1.0
    sel = lambda s: jnp.asarray(s.reshape(128, -1), BF16)
    return flat(wqa), flat(wqb), flat(wka), sel(sela), sel(selb), flat(wv)


def _rope_tables(geo):
    rows = geo.s // GRID_W
    pos = jnp.arange(rows * GRID_W)
    row = (pos // GRID_W).astype(F32)
    col = (pos % GRID_W).astype(F32)
    n_freq = MLA_ROPE // 4
    freqs = ROPE_THETA ** (-jnp.arange(n_freq, dtype=F32) / n_freq)
    cr, sr = jnp.cos(row[:, None] * freqs), jnp.sin(row[:, None] * freqs)
    cc, sn = jnp.cos(col[:, None] * freqs), jnp.sin(col[:, None] * freqs)
    cos32 = jnp.concatenate([cr, cr, cc, cc], axis=-1)
    sin32 = jnp.concatenate([-sr, sr, -sn, sn], axis=-1)
    s, c = geo.s, geo.c
    one = lambda n, w: jnp.ones((n, w), F32)
    zero = lambda n, w: jnp.zeros((n, w), F32)
    scale = (MLA_NOPE + MLA_ROPE) ** -0.5
    qca = jnp.concatenate([jnp.concatenate([one(c, 64), zero(c, 32), one(c, 32)], 1),
                           jnp.concatenate([one(s, 64), cos32, one(s, 32)], 1)], 0) * scale
    qsb = jnp.concatenate([zero(c, 128), jnp.concatenate([zero(s, 64), sin32, zero(s, 32)], 1)], 0) * scale
    kca = jnp.concatenate([jnp.concatenate([one(c, 64), zero(c, 32), one(c, 32)], 1),
                           jnp.concatenate([one(s, 64), cos32, zero(s, 32)], 1)], 0)
    ksb = jnp.concatenate([zero(c, 128), jnp.concatenate([zero(s, 64), sin32, zero(s, 32)], 1)], 0)
    return qca, qsb, kca, ksb


def kernel(x, c, ctx, c_ctx, w_mod, b_mod, g_mix, g_ffn, w_in, g_q, w_uq, g_kv, w_ukv, w_mla_o, w_cf_dw, b_cf_dw, g_cf_ln, b_cf_ln, w_cf_o, w_gla_gf, b_gla_gf, w_gla_gb, b_gla_gb, g_gla_norm, w_gla_o, w_sc_conv, w_sc_o, w_mix_o, w_router, b_router, w_e_gate, w_e_up, w_e_down, g_final):
    bsz, seq, d = x.shape
    n_ctx = ctx.shape[1]
    depth = w_in.shape[0]
    geo = _Geom(bsz, n_ctx, seq, d)
    assert bsz + 1 <= 8 and seq % GRID_W == 0 and n_ctx % GLA_CHUNK == 0 and seq % GLA_CHUNK == 0

    plan_gates, plan_rest, off = _proj_plans(d)
    w_in_t = jnp.swapaxes(w_in, 1, 2)
    wqa, wqb, wka, sela, selb, wv = _prep_mla(w_uq, w_ukv)
    tabs = _rope_tables(geo)
    w_br_half = [_to_bf16(w, 0.5) for w in (w_mla_o, w_cf_o, w_gla_o, w_sc_o)]
    w_mix = _to_bf16(w_mix_o)
    gate_w = jnp.zeros((depth, 2, 128, GLA_HEADS * GLA_DK), F32)
    gate_w = gate_w.at[:, 0, :GLA_GATE_RANK].set(w_gla_gf).at[:, 1, GLA_GATE_RANK:2 * GLA_GATE_RANK].set(w_gla_gb)
    gate_w = gate_w.astype(BF16)
    gate_b = jnp.stack([b_gla_gf, b_gla_gb], axis=1)[:, :, None, :]
    mat3, msk4 = _gla_tables()
    wr_t = w_router.T
    wr_hi = wr_t.astype(BF16)
    wr_stack = jnp.concatenate([wr_hi, (wr_t - wr_hi.astype(F32)).astype(BF16)], axis=0)
    vec = lambda a: a.reshape(a.shape[0], 1, a.shape[-1])

    cc = jnp.zeros((8, d), F32).at[:bsz].set(c).at[bsz].set(c_ctx)
    mod = _modulation(cc, w_mod, b_mod).reshape(depth, 8, 1, 6 * d)

    xa, h = _first_norm(geo, x, ctx, vec(g_mix), mod)
    for l in range(depth):
        u = _project(h, w_in_t, l, plan_rest, 1.0)
        u_gates = _project(h, w_in_t, l, plan_gates, 0.5)
        q, k, v = _mla_qkv(geo, u, off, l, vec(g_q), vec(g_kv), wqa, wqb, wka, sela, selb, wv, tabs)
        attn = _mla_attention(geo, q, k, v)
        conf, sconv = _conv_branches(geo, u, off, l, w_cf_dw, vec(b_cf_dw), vec(g_cf_ln), vec(b_cf_ln), w_sc_conv)
        gla_f, gla_b = _gla_scan(geo, u, off, l, gate_w, gate_b, mat3, msk4)
        xa, *routed = _merge_mix(geo, attn, conf, gla_f, gla_b, u, u_gates, off, vec(g_gla_norm), sconv, w_br_half,
                                 w_mix, xa, mod, l, vec(g_ffn), wr_stack, b_router.reshape(N_EXPERTS, 1))
        if l + 1 < depth:
            xa, h = _moe(geo, xa, routed, mod, l, w_e_gate, w_e_up, w_e_down, vec(g_mix), None)
        else:
            return _moe(geo, xa, routed, mod, l, w_e_gate, w_e_up, w_e_down, None, g_final)
```

```python
import functools

import numpy as np
import jax
import jax.numpy as jnp
from jax import lax
from jax.experimental import pallas as pl
from jax.experimental.pallas import tpu as pltpu

F32 = jnp.float32
BF16 = jnp.bfloat16
EPS = 1e-6

GRID_W = 64
ROPE_THETA = 10000.0
MLA_HEADS = 8
MLA_NOPE = 64
MLA_ROPE = 32
MLA_V = 64
MLA_Q_RANK = 512
MLA_KV_RANK = 256
MLA_SLOT = 128
CF_WIDTH = 512
CF_KERNEL = 31
GLA_HEADS = 4
GLA_DK = 64
GLA_DV = 128
GLA_GATE_RANK = 16
GLA_GATE_TAU = 16.0
GLA_CHUNK = 64
GLA_LEVELS = 6
SC_WIDTH = 512
SC_KERNEL = 3
N_BRANCHES = 4
N_EXPERTS = 16
N_GROUPS = 4
EXPERT_FF = 512
CONV_HALO = 16
SUBLANES = 8
MERGE_COLS = 512
ATTN_ROWS = 2304
PROJ_TILE = 1024
PROJ_CAST_ROWS = 256
VMEM_LIMIT = 56 * 1024 * 1024


def _cparams(*sem):
    return pltpu.CompilerParams(dimension_semantics=sem, vmem_limit_bytes=VMEM_LIMIT)


def _pick(n, cands):
    for c in cands:
        if n % c == 0:
            return c
    raise ValueError(f"no tile for {n}")


def _dot(a, b):
    return jnp.dot(a, b, preferred_element_type=F32)


def _dot_nt(a, b):
    return lax.dot_general(a, b, (((1,), (1,)), ((), ())), preferred_element_type=F32)


def _proj_layout():
    parts = [("sc_b", SC_WIDTH), ("sc_c", SC_WIDTH), ("sc_h", SC_WIDTH),
             ("cf_a", CF_WIDTH), ("cf_b", CF_WIDTH), ("cq", MLA_Q_RANK), ("gla_v", GLA_HEADS * GLA_DV),
             ("gla_r", GLA_HEADS * GLA_DV), ("ckv", MLA_KV_RANK), ("gla_q", GLA_HEADS * GLA_DK),
             ("gla_k", GLA_HEADS * GLA_DK), ("kr", 128), ("gate", 128)]
    off, start = {}, 0
    for name, width in parts:
        assert start % width == 0
        off[name] = start
        start += width
    return off, start


def _src_layout(d):
    parts = (("cq", MLA_Q_RANK), ("ckv", MLA_KV_RANK), ("kr", MLA_ROPE), ("cf", 2 * CF_WIDTH),
             ("gla_q", GLA_HEADS * GLA_DK), ("gla_k", GLA_HEADS * GLA_DK), ("gla_v", GLA_HEADS * GLA_DV),
             ("gf", GLA_GATE_RANK), ("gb", GLA_GATE_RANK), ("gla_r", GLA_HEADS * GLA_DV),
             ("sc", 3 * SC_WIDTH), ("merge", N_BRANCHES * d))
    off, start = {}, 0
    for name, width in parts:
        off[name] = (start, start + width)
        start += width
    return off


def _mod_kernel(c_ref, w_ref, b_ref, o_ref):
    c = c_ref[...]
    a = (c * jax.nn.sigmoid(c)).astype(BF16)
    o_ref[...] = _dot(a, w_ref[...].astype(BF16)) + b_ref[...]


def _modulation(cc, w_mod, b_mod):
    n_l, d, n6 = w_mod.shape
    tn = _pick(n6, (1024, 512, 256, 128))
    return pl.pallas_call(
        _mod_kernel,
        out_shape=jax.ShapeDtypeStruct((n_l, 8, n6), F32),
        grid=(n_l, n6 // tn),
        in_specs=[pl.BlockSpec((8, d), lambda l, j: (0, 0)),
                  pl.BlockSpec((None, d, tn), lambda l, j: (l, 0, j)),
                  pl.BlockSpec((None, 1, tn), lambda l, j: (l, 0, j))],
        out_specs=pl.BlockSpec((None, 8, tn), lambda l, j: (l, 0, j)),
        name="adaln_mod",
        compiler_params=_cparams("parallel", "parallel"),
    )(cc, w_mod, b_mod.reshape(n_l, 1, n6))


class _Geom:
    def __init__(self, bsz, n_ctx, seq, d):
        self.b, self.c, self.s, self.d = bsz, n_ctx, seq, d
        self.n = n_ctx + seq
        self.m = bsz * self.n
        self.tr = _pick(int(np.gcd(n_ctx, seq)), (256, 128, 64))
        self.nt = self.n // self.tr
        self.nct = n_ctx // self.tr

    def mod_row(self, i):
        b, j = i // self.nt, i % self.nt
        return jnp.where(j < self.nct, self.b, b)

    def mod_spec(self, layer, part):
        return pl.BlockSpec((None, None, 1, self.d), lambda i, *_: (layer, self.mod_row(i), 0, part))


def _first_norm_kernel(x_ref, c_ref, g_ref, sh_ref, sc_ref, xa_ref, h_ref, *, geo):
    is_ctx = pl.program_id(0) % geo.nt < geo.nct
    rows = jnp.where(is_ctx, c_ref[...], x_ref[...])
    xa_ref[...] = rows
    y = rows * lax.rsqrt(jnp.mean(rows * rows, axis=-1, keepdims=True) + EPS) * g_ref[...]
    h_ref[...] = (y * (1.0 + sc_ref[...]) + sh_ref[...]).astype(h_ref.dtype)


def _first_norm(geo, x, ctx, g, mod):
    tr, d = geo.tr, geo.d
    tile = pl.BlockSpec((tr, d), lambda i: (i, 0))
    return pl.pallas_call(
        functools.partial(_first_norm_kernel, geo=geo),
        out_shape=(jax.ShapeDtypeStruct((geo.m, d), F32), jax.ShapeDtypeStruct((geo.m, d), BF16)),
        grid=(geo.m // tr,),
        in_specs=[pl.BlockSpec((None, tr, d), lambda i: (i // geo.nt, jnp.maximum(i % geo.nt - geo.nct, 0), 0)),
                  pl.BlockSpec((None, tr, d), lambda i: (i // geo.nt, jnp.minimum(i % geo.nt, geo.nct - 1), 0)),
                  pl.BlockSpec((None, 1, d), lambda i: (0, 0, 0)),
                  geo.mod_spec(0, 0), geo.mod_spec(0, 1)],
        out_specs=(tile, tile),
        name="first_norm",
        compiler_params=_cparams("parallel"),
    )(x, ctx, g, mod, mod)


def _proj_kernel(a_ref, w_hbm, o_ref, wbuf_ref, wb_ref, sem_ref, *, layer, plan, scale):
    j, i = pl.program_id(0), pl.program_id(1)
    n_tiles = pl.num_programs(0)
    tn = wb_ref.shape[0]

    def copies(t, slot):
        return [pltpu.make_async_copy(w_hbm.at[layer, pl.ds(src, rows), :], wbuf_ref.at[slot, pl.ds(dst, rows), :],
                                      sem_ref.at[slot]) for dst, src, rows in plan(t)]

    def on_tile(t, fn):
        if not plan.static:
            fn(t)
            return
        for tt in range(plan.n_tiles):
            pl.when(t == tt)(functools.partial(fn, tt))

    def start(t):
        for cp in copies(t, t % 2):
            cp.start()

    def land(t):
        slot = t % 2
        for cp in copies(t, slot):
            cp.wait()
        covered = 0
        for dst, _, rows in plan(t):
            if dst > covered:
                wb_ref[covered:dst, :] = jnp.zeros((dst - covered, wb_ref.shape[1]), BF16)
            for r0 in range(0, rows, PROJ_CAST_ROWS):
                rr = min(PROJ_CAST_ROWS, rows - r0)
                rows_f32 = wbuf_ref[slot, dst + r0:dst + r0 + rr, :]
                wb_ref[dst + r0:dst + r0 + rr, :] = (rows_f32 * scale).astype(BF16)
            covered = dst + rows
        if covered < tn:
            wb_ref[covered:tn, :] = jnp.zeros((tn - covered, wb_ref.shape[1]), BF16)

    @pl.when(i == 0)
    def _():
        @pl.when(j == 0)
        def _():
            on_tile(j, start)

        on_tile(j, land)

        @pl.when(j + 1 < n_tiles)
        def _():
            on_tile(j + 1, start)

    o_ref[...] = _dot_nt(a_ref[...], wb_ref[...]).astype(o_ref.dtype)


class _ProjPlan:
    def __init__(self, tn, n_tiles, pieces=None, first_row=None):
        self.tn, self.n_tiles, self.static = tn, n_tiles, pieces is not None
        self._pieces, self._first_row = pieces, first_row

    def __call__(self, t):
        if self.static:
            lo = t * self.tn
            return [(dst - lo, src, rows) for dst, src, rows in self._pieces if lo <= dst < lo + self.tn]
        return [(0, pl.multiple_of(self._first_row + t * self.tn, SUBLANES), self.tn)]


def _project(a, w_t, layer, plan, scale):
    m, k = a.shape
    tn = plan.tn
    tm = _pick(m, (1536, 1152, 1024, 512, 256, 128))
    return pl.pallas_call(
        functools.partial(_proj_kernel, layer=layer, plan=plan, scale=scale),
        out_shape=jax.ShapeDtypeStruct((m, tn * plan.n_tiles), BF16),
        grid=(plan.n_tiles, m // tm),
        in_specs=[pl.BlockSpec((tm, k), lambda j, i: (i, 0)), pl.BlockSpec(memory_space=pl.ANY)],
        out_specs=pl.BlockSpec((tm, tn), lambda j, i: (i, j)),
        scratch_shapes=[pltpu.VMEM((2, tn, k), F32), pltpu.VMEM((tn, k), BF16), pltpu.SemaphoreType.DMA((2,))],
        name="proj_in",
        compiler_params=_cparams("arbitrary", "arbitrary"),
    )(a, w_t)


def _rms(x, g):
    return x * lax.rsqrt(jnp.mean(x * x, axis=-1, keepdims=True) + EPS) * g


def _qkv_kernel(cq_ref, ckv_ref, kr_ref, gq_ref, gkv_ref, wqa_ref, wqb_ref, wka_ref, sela_ref, selb_ref,
                wv_ref, qca_ref, qsb_ref, kca_ref, ksb_ref, q_ref, k_ref, v_ref):
    cqn = _rms(cq_ref[...].astype(F32), gq_ref[...]).astype(BF16)
    ckvn = _rms(ckv_ref[...].astype(F32), gkv_ref[...]).astype(BF16)
    kr = kr_ref[...]
    qa = _dot(cqn, wqa_ref[...])
    qb = _dot(cqn, wqb_ref[...])
    ka = _dot(ckvn, wka_ref[...]) + _dot(kr, sela_ref[...])
    kb = _dot(kr, selb_ref[...])
    qca, qsb, kca, ksb = qca_ref[...], qsb_ref[...], kca_ref[...], ksb_ref[...]
    for h in range(MLA_HEADS):
        sl = slice(h * MLA_SLOT, (h + 1) * MLA_SLOT)
        q_ref[:, sl] = (qa[:, sl] * qca + qb[:, sl] * qsb).astype(BF16)
        k_ref[:, sl] = (ka[:, sl] * kca + kb[:, sl] * ksb).astype(BF16)
    v_ref[...] = _dot(ckvn, wv_ref[...]).astype(BF16)


def _mla_qkv(geo, u, off, layer, gq, gkv, wqa, wqb, wka, sela, selb, wv, tabs):
    tr, m = geo.tr, geo.m
    hq = MLA_HEADS * MLA_SLOT
    hv = MLA_HEADS * MLA_V
    row = lambda w, o: pl.BlockSpec((tr, w), lambda i: (i, o // w))
    lw = lambda a: pl.BlockSpec((None,) + a.shape[1:], lambda i: (layer,) + (0,) * (a.ndim - 1))
    cw = lambda a: pl.BlockSpec(a.shape, lambda i: (0,) * a.ndim)
    tab = pl.BlockSpec((tr, MLA_SLOT), lambda i: (i % geo.nt, 0))
    return pl.pallas_call(
        _qkv_kernel,
        out_shape=(jax.ShapeDtypeStruct((m, hq), BF16), jax.ShapeDtypeStruct((m, hq), BF16),
                   jax.ShapeDtypeStruct((m, hv), BF16)),
        grid=(m // tr,),
        in_specs=[row(MLA_Q_RANK, off["cq"]), row(MLA_KV_RANK, off["ckv"]), row(128, off["kr"]),
                  lw(gq), lw(gkv), lw(wqa), lw(wqb), lw(wka), cw(sela), cw(selb), lw(wv), tab, tab, tab, tab],
        out_specs=(pl.BlockSpec((tr, hq), lambda i: (i, 0)), pl.BlockSpec((tr, hq), lambda i: (i, 0)),
                   pl.BlockSpec((tr, hv), lambda i: (i, 0))),
        name="mla_qkv",
        compiler_params=_cparams("parallel"),
    )(u, u, u, gq, gkv, wqa, wqb, wka, sela, selb, wv, *tabs)


def _attn_kernel(q_ref, k_ref, v_ref, o_ref, *, n_ctx, n_all):
    tq = q_ref.shape[0]

    def run(r0, rows, nk):
        lane = lax.broadcasted_iota(jnp.int32, (rows, 2 * MLA_V), 1)
        outs = []
        for hh in range(2):
            sl = slice(hh * MLA_SLOT, (hh + 1) * MLA_SLOT)
            s = _dot_nt(q_ref[r0:r0 + rows, sl], k_ref[0:nk, sl])
            p = jnp.exp(s - jnp.max(s, axis=-1, keepdims=True))
            inv = 1.0 / jnp.sum(p, axis=-1, keepdims=True)
            outs.append(_dot(p.astype(BF16), v_ref[0:nk, :]) * inv)
        o_ref[r0:r0 + rows, :] = jnp.where(lane < MLA_V, outs[0], outs[1]).astype(o_ref.dtype)

    @pl.when(pl.program_id(2) == 0)
    def _():
        run(0, n_ctx, n_ctx)
        if tq > n_ctx:
            run(n_ctx, tq - n_ctx, n_all)

    @pl.when(pl.program_id(2) > 0)
    def _():
        run(0, tq, n_all)


def _mla_attention(geo, q, k, v):
    n, bsz = geo.n, geo.b
    groups = [g for g in range(1, geo.nt + 1) if geo.nt % g == 0 and g * geo.tr >= geo.c]
    tq = geo.tr * min(groups, key=lambda g: abs(g * geo.tr - ATTN_ROWS))
    q3, k3, v3 = (t.reshape(bsz, n, t.shape[-1]) for t in (q, k, v))
    kern = functools.partial(_attn_kernel, n_ctx=geo.c, n_all=n)
    out = pl.pallas_call(
        kern,
        out_shape=jax.ShapeDtypeStruct((bsz, n, MLA_HEADS * MLA_V), BF16),
        grid=(bsz, MLA_HEADS // 2, n // tq),
        in_specs=[pl.BlockSpec((None, tq, 2 * MLA_SLOT), lambda b, hp, i: (b, i, hp)),
                  pl.BlockSpec((None, n, 2 * MLA_SLOT), lambda b, hp, i: (b, 0, hp)),
                  pl.BlockSpec((None, n, 2 * MLA_V), lambda b, hp, i: (b, 0, hp))],
        out_specs=pl.BlockSpec((None, tq, 2 * MLA_V), lambda b, hp, i: (b, i, hp)),
        name="mla_attn",
        compiler_params=_cparams("parallel", "parallel", "parallel"),
    )(q3, k3, v3)
    return out.reshape(geo.m, MLA_HEADS * MLA_V)


def _halo_flags(geo, i):
    j = i % geo.nt
    left = jnp.logical_and(j != 0, j != geo.nct)
    right = jnp.logical_and(j != geo.nct - 1, j != geo.nt - 1)
    return left.astype(F32), right.astype(F32)


def _conv_shifts(n_taps):
    base = CONV_HALO - n_taps // 2
    return sorted({(base + kk) % SUBLANES for kk in range(n_taps)} - {0})


def _conv_taps(pad_ref, sh_ref, w_ref, n_taps, rows):
    base = CONV_HALO - n_taps // 2
    shifts = _conv_shifts(n_taps)
    span = sh_ref.shape[1]
    for s, r in enumerate(shifts):
        sh_ref[s, :, :] = pad_ref[r:r + span, :]

    def window(o):
        a, r = o - o % SUBLANES, o % SUBLANES
        return pad_ref[a:a + rows, :] if r == 0 else sh_ref[shifts.index(r), a:a + rows, :]

    acc = w_ref[0:1, :] * window(base)
    for kk in range(1, n_taps):
        acc = acc + w_ref[kk:kk + 1, :] * window(base + kk)
    return acc


def _conv_scratch(tr, width, n_taps):
    return [pltpu.VMEM((tr + 2 * CONV_HALO, width), F32),
            pltpu.VMEM((len(_conv_shifts(n_taps)), tr + 2 * CONV_HALO - SUBLANES, width), F32)]


def _conf_kernel(a_ref, b_ref, ap_ref, bp_ref, an_ref, bn_ref, w_ref, bias_ref, g_ref, beta_ref, o_ref,
                 pad_ref, sh_ref, *, geo):
    left, right = _halo_flags(geo, pl.program_id(0))
    rows = a_ref.shape[0]
    glu = lambda a, b: a[...].astype(F32) * jax.nn.sigmoid(b[...].astype(F32))
    pad_ref[0:CONV_HALO, :] = glu(ap_ref, bp_ref) * left
    pad_ref[CONV_HALO:CONV_HALO + rows, :] = glu(a_ref, b_ref)
    pad_ref[CONV_HALO + rows:2 * CONV_HALO + rows, :] = glu(an_ref, bn_ref) * right
    h = _conv_taps(pad_ref, sh_ref, w_ref, CF_KERNEL, rows) + bias_ref[...]
    hc = h - jnp.mean(h, axis=-1, keepdims=True)
    y = hc * lax.rsqrt(jnp.mean(hc * hc, axis=-1, keepdims=True) + EPS) * g_ref[...] + beta_ref[...]
    o_ref[...] = (y * jax.nn.sigmoid(y)).astype(o_ref.dtype)


def _sconv_kernel(gb_ref, gc_ref, h_ref, gcp_ref, hp_ref, gcn_ref, hn_ref, w_ref, o_ref, pad_ref, sh_ref, *, geo):
    left, right = _halo_flags(geo, pl.program_id(0))
    rows = h_ref.shape[0]
    prod = lambda a, b: a[...].astype(F32) * b[...].astype(F32)
    pad_ref[0:CONV_HALO, :] = prod(gcp_ref, hp_ref) * left
    pad_ref[CONV_HALO:CONV_HALO + rows, :] = prod(gc_ref, h_ref)
    pad_ref[CONV_HALO + rows:2 * CONV_HALO + rows, :] = prod(gcn_ref, hn_ref) * right
    y = _conv_taps(pad_ref, sh_ref, w_ref, SC_KERNEL, rows)
    o_ref[...] = (gb_ref[...].astype(F32) * y).astype(o_ref.dtype)


def _halo_specs(geo, width, offset):
    tr = geo.tr
    per = tr // CONV_HALO
    last = geo.m // CONV_HALO - 1
    cur = pl.BlockSpec((tr, width), lambda i: (i, offset // width))
    prev = pl.BlockSpec((CONV_HALO, width), lambda i: (jnp.maximum(i * per - 1, 0), offset // width))
    nxt = pl.BlockSpec((CONV_HALO, width), lambda i: (jnp.minimum((i + 1) * per, last), offset // width))
    return cur, prev, nxt


N_CONF_IN, N_SCONV_IN = 10, 8


def _convs_kernel(*refs, geo):
    n_in = N_CONF_IN + N_SCONV_IN
    conf_o, sconv_o = refs[n_in], refs[n_in + 1]
    conf_scratch, sconv_scratch = refs[n_in + 2:n_in + 4], refs[n_in + 4:n_in + 6]
    _conf_kernel(*refs[:N_CONF_IN], conf_o, *conf_scratch, geo=geo)
    _sconv_kernel(*refs[N_CONF_IN:n_in], sconv_o, *sconv_scratch, geo=geo)


def _conv_branches(geo, u, off, layer, w_dw, b_dw, g_ln, b_ln, w_conv):
    tr = geo.tr
    a_c, a_p, a_n = _halo_specs(geo, CF_WIDTH, off["cf_a"])
    b_c, b_p, b_n = _halo_specs(geo, CF_WIDTH, off["cf_b"])
    gb_c, _, _ = _halo_specs(geo, SC_WIDTH, off["sc_b"])
    gc_c, gc_p, gc_n = _halo_specs(geo, SC_WIDTH, off["sc_c"])
    h_c, h_p, h_n = _halo_specs(geo, SC_WIDTH, off["sc_h"])
    lw = lambda a: pl.BlockSpec((None,) + a.shape[1:], lambda i: (layer,) + (0,) * (a.ndim - 1))
    in_specs = [a_c, b_c, a_p, b_p, a_n, b_n, lw(w_dw), lw(b_dw), lw(g_ln), lw(b_ln),
                gb_c, gc_c, h_c, gc_p, h_p, gc_n, h_n, lw(w_conv)]
    assert len(in_specs) == N_CONF_IN + N_SCONV_IN
    return pl.pallas_call(
        functools.partial(_convs_kernel, geo=geo),
        out_shape=(jax.ShapeDtypeStruct((geo.m, CF_WIDTH), BF16), jax.ShapeDtypeStruct((geo.m, SC_WIDTH), BF16)),
        grid=(geo.m // tr,),
        in_specs=in_specs,
        out_specs=(pl.BlockSpec((tr, CF_WIDTH), lambda i: (i, 0)), pl.BlockSpec((tr, SC_WIDTH), lambda i: (i, 0))),
        scratch_shapes=_conv_scratch(tr, CF_WIDTH, CF_KERNEL) + _conv_scratch(tr, SC_WIDTH, SC_KERNEL),
        name="conv_branches",
        compiler_params=_cparams("parallel"),
    )(*([u] * 6), w_dw, b_dw, g_ln, b_ln, *([u] * 7), w_conv)


def _gla_tables():
    c = GLA_CHUNK
    n_rows = (GLA_LEVELS + 2) * c + 8
    mat = np.zeros((2, n_rows, c), np.float32)
    msk = np.zeros((2, GLA_LEVELS + 1, c, c), np.float32)
    for lvl in range(GLA_LEVELS):
        hs = (c // 2) >> lvl
        for t in range(c):
            mid = (t // (2 * hs)) * 2 * hs + hs
            if t >= mid:
                mat[0, lvl * c + t, mid + 1:t + 1] = 1.0
                msk[0, lvl, t, mid - hs:mid] = 1.0
            else:
                mat[0, lvl * c + t, t + 1:mid + 1] = 1.0
    msk[0, GLA_LEVELS] = np.eye(c, dtype=np.float32)
    for t in range(c):
        mat[0, GLA_LEVELS * c + t, :t + 1] = 1.0
        mat[0, (GLA_LEVELS + 1) * c + t, t + 1:] = 1.0
    mat[0, (GLA_LEVELS + 2) * c:, :] = 1.0
    n_blk = GLA_LEVELS + 2
    mat[1, :n_blk * c] = mat[0, :n_blk * c].reshape(n_blk, c, c)[:, ::-1, ::-1].reshape(n_blk * c, c)
    mat[1, n_blk * c:] = 1.0
    msk[1] = msk[0][:, ::-1, ::-1]
    mat3 = np.concatenate([mat, mat, mat], axis=2)
    msk4 = np.tile(msk, (1, 1, GLA_HEADS, 1))
    return jnp.asarray(mat3, BF16), jnp.asarray(msk4, F32)


def _gla_kernel(qf_ref, kf_ref, vf_ref, gf_ref, qb_ref, kb_ref, vb_ref, gb_ref, w2_ref, b2_ref, mat_ref, msk_ref,
                of_ref, ob_ref, st_ref):
    @pl.when(pl.program_id(0) == 0)
    def _():
        st_ref[...] = jnp.zeros_like(st_ref)

    ins = ((qf_ref, kf_ref, vf_ref, gf_ref, of_ref), (qb_ref, kb_ref, vb_ref, gb_ref, ob_ref))
    for b in range(qf_ref.shape[0]):
        for d, (q_ref, k_ref, v_ref, g_ref, o_ref) in enumerate(ins):
            _gla_chunk(q_ref.at[b], k_ref.at[b], v_ref.at[b], g_ref.at[b], w2_ref.at[d], b2_ref.at[d],
                       mat_ref.at[d], msk_ref.at[d], o_ref.at[b], st_ref.at[d, b])


def _gla_chunk(q_ref, k_ref, v_ref, g_ref, w2_ref, b2_ref, mat_ref, msk_ref, o_ref, st_ref):
    c = GLA_CHUNK
    z = _dot(g_ref[...], w2_ref[...]) + b2_ref[...]
    la = (jnp.minimum(z, 0.0) - jnp.log1p(jnp.exp(-jnp.abs(z)))) * (1.0 / GLA_GATE_TAU)
    hi = la.astype(BF16)
    r1 = la - hi.astype(F32)
    lo = r1.astype(BF16)
    lo2 = (r1 - lo.astype(F32)).astype(BF16)
    ea = jnp.exp(_dot(mat_ref[...], jnp.concatenate([hi, lo, lo2], axis=0)))

    head = lax.broadcasted_iota(jnp.int32, (1, GLA_HEADS * GLA_DK), 1) // GLA_DK
    zero = jnp.zeros((), BF16)

    def stack(x):
        xb = x.astype(BF16)
        return jnp.concatenate([jnp.where(head == h, xb, zero) for h in range(GLA_HEADS)], axis=0)

    q = q_ref[...].astype(F32) * (GLA_DK ** -0.5)
    k = k_ref[...].astype(F32)
    vb = v_ref[...]

    attn = _dot_nt(stack(q), k.astype(BF16)) * msk_ref[GLA_LEVELS]
    for lvl in range(GLA_LEVELS):
        e = ea[lvl * c:(lvl + 1) * c]
        attn = attn + _dot_nt(stack(q * e), (k * e).astype(BF16)) * msk_ref[lvl]
    o_intra = _dot(attn.astype(BF16), vb)

    st = st_ref[...]
    o_inter = _dot_nt(stack(q * ea[GLA_LEVELS * c:(GLA_LEVELS + 1) * c]), st.astype(BF16))
    for h in range(GLA_HEADS):
        rows = slice(h * c, (h + 1) * c)
        cols = slice(h * GLA_DV, (h + 1) * GLA_DV)
        o_ref[:, cols] = o_intra[rows, cols] + o_inter[rows, :]

    k_dec = stack(k * ea[(GLA_LEVELS + 1) * c:(GLA_LEVELS + 2) * c])
    v_rows = jnp.concatenate([vb[:, h * GLA_DV:(h + 1) * GLA_DV] for h in range(GLA_HEADS)], axis=0)
    eye = (lax.broadcasted_iota(jnp.int32, (GLA_DV, GLA_DV), 0)
           == lax.broadcasted_iota(jnp.int32, (GLA_DV, GLA_DV), 1)).astype(BF16)
    v_t = _dot_nt(eye, v_rows).astype(BF16)
    tot = ea[(GLA_LEVELS + 2) * c:(GLA_LEVELS + 2) * c + 1]
    st_ref[...] = st * tot + _dot(v_t, k_dec)


def _gla_scan(geo, u, off, layer, w2, b2, mat3, msk4):
    c = GLA_CHUNK
    ncc = geo.c // c
    nc = geo.n // c
    dk, dv = GLA_HEADS * GLA_DK, GLA_HEADS * GLA_DV

    fwd = lambda i: i
    bwd = lambda i: jnp.where(i < ncc, ncc - 1 - i, nc - 1 - (i - ncc))

    u3 = u.reshape(geo.b, geo.n, u.shape[-1])
    row = lambda w, o, chunk: pl.BlockSpec((geo.b, c, w), lambda i: (0, chunk(i), o // w))
    rows = lambda chunk: [row(dk, off["gla_q"], chunk), row(dk, off["gla_k"], chunk), row(dv, off["gla_v"], chunk),
                          row(128, off["gate"], chunk)]
    out_sds = jax.ShapeDtypeStruct((geo.b, geo.n, dv), F32)
    o_f, o_b = pl.pallas_call(
        _gla_kernel,
        out_shape=(out_sds, out_sds),
        grid=(nc,),
        in_specs=rows(fwd) + rows(bwd) + [
            pl.BlockSpec((None, 2, 128, dk), lambda i: (layer, 0, 0, 0)),
            pl.BlockSpec((None, 2, 1, dk), lambda i: (layer, 0, 0, 0)),
            pl.BlockSpec(mat3.shape, lambda i: (0, 0, 0)),
            pl.BlockSpec(msk4.shape, lambda i: (0, 0, 0, 0))],
        out_specs=(pl.BlockSpec((geo.b, c, dv), lambda i: (0, fwd(i), 0)),
                   pl.BlockSpec((geo.b, c, dv), lambda i: (0, bwd(i), 0))),
        scratch_shapes=[pltpu.VMEM((2, geo.b, GLA_DV, dk), F32)],
        name="gla_scan",
        compiler_params=_cparams("arbitrary"),
    )(u3, u3, u3, u3, u3, u3, u3, u3, w2, b2, mat3, msk4)
    return o_f.reshape(geo.m, dv), o_b.reshape(geo.m, dv)


def _merge_mix_kernel(attn_ref, conf_ref, glaf_ref, glab_ref, r_ref, gn_ref, sc_ref, gate_ref, w0_ref, w1_ref,
                      w2_ref, w3_ref, wmix_ref, x_ref, gt_ref, gf_ref, shf_ref, scf_ref, wr_ref, br_ref, tri_ref,
                      o_ref, h_ref, idx_ref, wgt_ref, rank_ref, cnt_ref, y_ref, carry_ref):
    @pl.when(pl.program_id(0) == 0)
    def _():
        carry_ref[...] = jnp.zeros_like(carry_ref)

    wbr_ref = (w0_ref, w1_ref, w2_ref, w3_ref)
    o = glaf_ref[...] + glab_ref[...]
    r = r_ref[...].astype(F32)
    gn = gn_ref[...]
    normed = [_rms(o[:, h * GLA_DV:(h + 1) * GLA_DV], gn) for h in range(GLA_HEADS)]
    gla = (jnp.concatenate(normed, axis=1) * (r * jax.nn.sigmoid(r))).astype(BF16)
    acts = (attn_ref[...], conf_ref[...], gla, sc_ref[...])
    d = x_ref.shape[1]
    cw = min(MERGE_COLS, d)
    for c0 in range(0, d, cw):
        y = None
        for i in range(N_BRANCHES):
            half_b = _dot(acts[i], wbr_ref[i][:, c0:c0 + cw])
            t = jnp.tanh(gate_ref[:, i * d + c0:i * d + c0 + cw].astype(F32))
            term = t * half_b + half_b
            y = term if y is None else y + term
        y_ref[:, c0:c0 + cw] = y.astype(BF16)
    y_all = y_ref[...]
    for c0 in range(0, d, cw):
        cols = slice(c0, c0 + cw)
        o_ref[:, cols] = x_ref[:, cols] + gt_ref[:, cols] * _dot(y_all, wmix_ref[:, cols])
    _route_rows(o_ref[...], gf_ref, shf_ref, scf_ref, wr_ref, br_ref, tri_ref, h_ref, idx_ref, wgt_ref, rank_ref,
                cnt_ref, carry_ref)


def _merge_mix(geo, attn, conf, gla_f, gla_b, u, u_gates, off, g_norm, sconv, w_br_half, w_mix, x, mod, layer,
               g_ffn, wr_stack, b_router):
    tm, d, m = geo.tr, geo.d, geo.m
    w512 = GLA_HEADS * GLA_DV
    nch = d // 256
    act = pl.BlockSpec((tm, w512), lambda i: (i, 0))
    once = pl.Buffered(1)
    w_br = pl.BlockSpec((None, w512, d), lambda i: (layer, 0, 0), pipeline_mode=once)
    tri = jnp.asarray(np.triu(np.ones((tm, tm), np.float32), 1), BF16)
    pair = lambda dt: jax.ShapeDtypeStruct((2, m), dt)
    pair_spec = pl.BlockSpec((2, tm), lambda i: (0, i))
    return pl.pallas_call(
        _merge_mix_kernel,
        out_shape=(jax.ShapeDtypeStruct((m, d), F32), jax.ShapeDtypeStruct((m * nch, 128), jnp.uint32),
                   pair(jnp.int32), pair(F32), pair(jnp.int32), jax.ShapeDtypeStruct((N_EXPERTS, 1), F32)),
        grid=(m // tm,),
        in_specs=[act, act, act, act,
                  pl.BlockSpec((tm, w512), lambda i: (i, off["gla_r"] // w512)),
                  pl.BlockSpec((None, 1, GLA_DV), lambda i: (layer, 0, 0)),
                  act, pl.BlockSpec((tm, N_BRANCHES * d), lambda i: (i, 0)),
                  w_br, w_br, w_br, w_br,
                  pl.BlockSpec((None, d, d), lambda i: (layer, 0, 0), pipeline_mode=once),
                  pl.BlockSpec((tm, d), lambda i: (i, 0)), geo.mod_spec(layer, 2),
                  pl.BlockSpec((None, 1, d), lambda i: (layer, 0, 0)),
                  geo.mod_spec(layer, 3), geo.mod_spec(layer, 4),
                  pl.BlockSpec(wr_stack.shape, lambda i: (0, 0)),
                  pl.BlockSpec(b_router.shape, lambda i: (0, 0)),
                  pl.BlockSpec((tm, tm), lambda i: (0, 0))],
        out_specs=(pl.BlockSpec((tm, d), lambda i: (i, 0)), pl.BlockSpec((tm * nch, 128), lambda i: (i, 0)),
                   pair_spec, pair_spec, pair_spec, pl.BlockSpec((N_EXPERTS, 1), lambda i: (0, 0))),
        scratch_shapes=[pltpu.VMEM((tm, d), BF16), pltpu.VMEM((N_EXPERTS, 1), F32)],
        name="merge_mix",
        compiler_params=_cparams("arbitrary"),
    )(attn, conf, gla_f, gla_b, u, g_norm, sconv, u_gates, *w_br_half, w_mix, x, mod, g_ffn, mod, mod, wr_stack,
      b_router, tri)


def _cast_kernel(w_ref, o_ref, *, scale):
    o_ref[...] = (w_ref[...] * scale).astype(o_ref.dtype)


def _to_bf16(w, scale=1.0):
    n_l, rows, cols = w.shape
    tr = _pick(rows, (512, 256, 128))
    return pl.pallas_call(
        functools.partial(_cast_kernel, scale=scale),
        out_shape=jax.ShapeDtypeStruct(w.shape, BF16),
        grid=(n_l, rows // tr),
        in_specs=[pl.BlockSpec((None, tr, cols), lambda l, i: (l, i, 0))],
        out_specs=pl.BlockSpec((None, tr, cols), lambda l, i: (l, i, 0)),
        name="weight_cast",
        compiler_params=_cparams("parallel", "parallel"),
    )(w)


def _second_largest(a, b, c, d):
    return jnp.maximum(jnp.maximum(jnp.minimum(a, b), jnp.minimum(c, d)),
                       jnp.minimum(jnp.maximum(a, b), jnp.maximum(c, d)))


def _chunk_rows(rows, n, cc):
    return pl.ds(cc, rows, stride=n) if n > 1 else pl.ds(0, rows)


def _pack_rows(ref, y):
    rows = y.shape[0]
    n = ref.shape[0] // rows
    half = n * 128
    bits = lambda v: pltpu.bitcast(v.astype(BF16).astype(F32), jnp.uint32)
    packed = (bits(y[:, :half]) >> 16) | bits(y[:, half:])
    for cc in range(n):
        ref[_chunk_rows(rows, n, cc), :] = packed[:, cc * 128:(cc + 1) * 128]


def _unpack_cols(words):
    lo = pltpu.bitcast(words << 16, F32)
    hi = pltpu.bitcast(words & jnp.uint32(0xFFFF0000), F32)
    return lo, hi


def _route_rows(x, g_ref, sh_ref, sc_ref, wr_ref, br_ref, tri_ref, h_ref, idx_ref, wgt_ref, rank_ref, cnt_ref,
                carry_ref):
    y = _rms(x, g_ref[...]) * (1.0 + sc_ref[...]) + sh_ref[...]
    hi = y.astype(BF16)
    lo = (y - hi.astype(F32)).astype(BF16)
    _pack_rows(h_ref, y)
    ne = N_EXPERTS
    two = _dot_nt(wr_ref[...], hi)
    logits = two[0:ne] + two[ne:2 * ne] + _dot_nt(wr_ref[0:ne, :], lo)
    aff = 1.0 / (1.0 + jnp.exp(-logits))
    sel = aff + br_ref[...]
    per = ne // N_GROUPS
    gs = []
    for g in range(N_GROUPS):
        a, b, c, d = (sel[g * per + r:g * per + r + 1] for r in range(per))
        top1 = jnp.maximum(jnp.maximum(a, b), jnp.maximum(c, d))
        gs.append(top1 + _second_largest(a, b, c, d))
    best = gs[0]
    for g in range(1, N_GROUPS):
        best = jnp.maximum(best, gs[g])
    grp = jnp.full(best.shape, N_GROUPS, jnp.int32)
    for g in reversed(range(N_GROUPS)):
        grp = jnp.where(gs[g] == best, g, grp)
    erow = lax.broadcasted_iota(jnp.int32, sel.shape, 0)
    masked = jnp.where(erow // per == grp, sel, -jnp.inf)
    m1 = jnp.max(masked, axis=0, keepdims=True)
    i1 = jnp.min(jnp.where(masked == m1, erow, ne), axis=0, keepdims=True)
    one1 = erow == i1
    masked2 = jnp.where(one1, -jnp.inf, masked)
    m2 = jnp.max(masked2, axis=0, keepdims=True)
    i2 = jnp.min(jnp.where(masked2 == m2, erow, ne), axis=0, keepdims=True)
    one2 = erow == i2
    w1 = jnp.sum(jnp.where(one1, aff, 0.0), axis=0, keepdims=True)
    w2 = jnp.sum(jnp.where(one2, aff, 0.0), axis=0, keepdims=True)
    inv = 1.0 / (w1 + w2)
    idx_ref[0:1, :] = i1
    idx_ref[1:2, :] = i2
    wgt_ref[0:1, :] = w1 * inv
    wgt_ref[1:2, :] = w2 * inv
    o1 = jnp.where(one1, 1.0, 0.0)
    o2 = jnp.where(one2, 1.0, 0.0)
    tri = tri_ref[...]
    c0 = carry_ref[...]
    c1 = c0 + jnp.sum(o1, axis=1, keepdims=True)
    rank_ref[0:1, :] = jnp.sum(o1 * (c0 + _dot(o1.astype(BF16), tri)), axis=0, keepdims=True).astype(jnp.int32)
    rank_ref[1:2, :] = jnp.sum(o2 * (c1 + _dot(o2.astype(BF16), tri)), axis=0, keepdims=True).astype(jnp.int32)
    c2 = c1 + jnp.sum(o2, axis=1, keepdims=True)
    carry_ref[...] = c2
    cnt_ref[...] = c2


def _dispatch_kernel(dest_ref, last_ref, h_ref, xs_hbm, zero_ref, sem, zero_sem, *, m, n):
    i = pl.program_id(0)
    tr = h_ref.shape[0] // n

    @pl.when(i == 0)
    def _():
        zero_ref[...] = jnp.zeros_like(zero_ref)

        def fills(e):
            rows = zero_ref.shape[0]
            start = pl.multiple_of(last_ref[e] * n, rows)
            return pltpu.make_async_copy(zero_ref, xs_hbm.at[pl.ds(start, rows), :], zero_sem)

        for e in range(2 * N_EXPERTS):
            @pl.when(last_ref[e] >= 0)
            def _():
                fills(e).start()
        for e in range(2 * N_EXPERTS):
            @pl.when(last_ref[e] >= 0)
            def _():
                fills(e).wait()

    def issue(r, carry):
        src = h_ref.at[pl.ds(pl.multiple_of(r * n, n), n), :]
        for k in range(2):
            row = pl.multiple_of(dest_ref[k * m + i * tr + r], n)
            pltpu.make_async_copy(src, xs_hbm.at[pl.ds(row, n), :], sem).start(priority=k)
        return carry

    lax.fori_loop(0, tr, issue, 0, unroll=8)
    everything = xs_hbm.at[pl.ds(0, 2 * tr * n), :]
    pltpu.make_async_copy(everything, everything, sem).wait()


def _dispatch(hfp, dest, last_tile_row, n_rows, tr, tm):
    n = hfp.shape[0] * 2 // dest.shape[0]
    m = hfp.shape[0] // n
    return pl.pallas_call(
        functools.partial(_dispatch_kernel, m=m, n=n),
        out_shape=jax.ShapeDtypeStruct((n_rows * n, 128), jnp.uint32),
        grid_spec=pltpu.PrefetchScalarGridSpec(
            num_scalar_prefetch=2, grid=(m // tr,),
            in_specs=[pl.BlockSpec((tr * n, 128), lambda i, *_: (i, 0))],
            out_specs=pl.BlockSpec(memory_space=pl.ANY),
            scratch_shapes=[pltpu.VMEM((tm * n, 128), jnp.uint32), pltpu.SemaphoreType.DMA(()),
                            pltpu.SemaphoreType.DMA(())]),
        name="moe_dispatch",
        compiler_params=_cparams("arbitrary"),
    )(dest, last_tile_row, hfp)


def _gmm_kernel(te_ref, nv_ref, first_ref, ord_ref, next_ref, xs_ref, wg_hbm, wu_hbm, wd_hbm, ys_ref, fg_ref, fu_ref,
                fd_ref, wgb_ref, wub_ref, wdb_ref, sem_ref, *, layer):
    j = pl.program_id(0)

    def fetch(e, slot):
        return [pltpu.make_async_copy(src.at[layer, e], dst.at[slot], sem_ref.at[slot])
                for src, dst in ((wg_hbm, fg_ref), (wu_hbm, fu_ref), (wd_hbm, fd_ref))]

    @pl.when(j == 0)
    def _():
        for cp in fetch(te_ref[0], 0):
            cp.start()

    @pl.when(j < nv_ref[0])
    def _():
        e = te_ref[j]

        @pl.when(j == first_ref[e])
        def _():
            slot = ord_ref[e] % 2
            for cp in fetch(e, slot):
                cp.wait()
            wgb_ref[...] = fg_ref[slot].astype(BF16)
            wub_ref[...] = fu_ref[slot].astype(BF16)
            wdb_ref[...] = fd_ref[slot].astype(BF16)

            @pl.when(next_ref[e] >= 0)
            def _():
                for cp in fetch(next_ref[e], 1 - slot):
                    cp.start()

        half = wgb_ref.shape[0] // 2
        n = half // 128
        tm = xs_ref.shape[0] // n
        parts = [_unpack_cols(xs_ref[_chunk_rows(tm, n, cc), :]) for cc in range(n)]
        xa = jnp.concatenate([p[0].astype(BF16) for p in parts], axis=1)
        xb = jnp.concatenate([p[1].astype(BF16) for p in parts], axis=1)
        g = _dot(xa, wgb_ref[0:half, :]) + _dot(xb, wgb_ref[half:2 * half, :])
        u = _dot(xa, wub_ref[0:half, :]) + _dot(xb, wub_ref[half:2 * half, :])
        act = (g * jax.nn.sigmoid(g) * u).astype(BF16)
        _pack_rows(ys_ref, _dot(act, wdb_ref[...]))

    @pl.when(j >= nv_ref[0])
    def _():
        ys_ref[...] = jnp.zeros_like(ys_ref)


def _gmm(xs, tile_expert, n_valid, first_tile, ordinal, next_expert, wg, wu, wd, layer, tm):
    d, ff = wg.shape[-2], wg.shape[-1]
    nch = d // 256
    blk = tm * nch
    any_spec = pl.BlockSpec(memory_space=pl.ANY)
    return pl.pallas_call(
        functools.partial(_gmm_kernel, layer=layer),
        out_shape=jax.ShapeDtypeStruct(xs.shape, jnp.uint32),
        grid_spec=pltpu.PrefetchScalarGridSpec(
            num_scalar_prefetch=5, grid=(xs.shape[0] // blk,),
            in_specs=[pl.BlockSpec((blk, 128), lambda j, te, nv, *_: (jnp.minimum(j, nv[0] - 1), 0)),
                      any_spec, any_spec, any_spec],
            out_specs=pl.BlockSpec((blk, 128), lambda j, *_: (j, 0)),
            scratch_shapes=[pltpu.VMEM((2, d, ff), F32), pltpu.VMEM((2, d, ff), F32), pltpu.VMEM((2, ff, d), F32),
                            pltpu.VMEM((d, ff), BF16), pltpu.VMEM((d, ff), BF16), pltpu.VMEM((ff, d), BF16),
                            pltpu.SemaphoreType.DMA((2,))]),
        name="moe_experts",
        compiler_params=_cparams("arbitrary"),
    )(tile_expert, n_valid, first_tile, ordinal, next_expert, xs, wg, wu, wd)


def _combine_kernel(dest_ref, x_ref, w_ref, gt_ref, ys_hbm, *rest, m, last):
    if last:
        g_ref, h_ref, o_ref, buf_ref, sem_ref = rest
    else:
        g_ref, sh_ref, sc_ref, o_ref, h_ref, buf_ref, sem_ref = rest
    i = pl.program_id(0)
    n_tiles = pl.num_programs(0)
    tr = x_ref.shape[0]
    n = buf_ref.shape[2] // tr

    def issue(tile, slot):
        def body(r, carry):
            for k in range(2):
                row = pl.multiple_of(dest_ref[k * m + tile * tr + r], n)
                pltpu.make_async_copy(ys_hbm.at[pl.ds(row, n), :],
                                      buf_ref.at[slot, k, pl.ds(pl.multiple_of(r * n, n), n), :],
                                      sem_ref.at[slot]).start(priority=k)
            return carry
        lax.fori_loop(0, tr, body, 0, unroll=8)

    @pl.when(i == 0)
    def _():
        issue(0, 0)

    @pl.when(i + 1 < n_tiles)
    def _():
        issue(i + 1, (i + 1) % 2)

    slot = i % 2
    pltpu.make_async_copy(buf_ref.at[slot], buf_ref.at[slot], sem_ref.at[slot]).wait()
    w1, w2 = w_ref[:, 0:1], w_ref[:, 1:2]
    half = n * 128
    for cc in range(n):
        lo1, hi1 = _unpack_cols(buf_ref[slot, 0, _chunk_rows(tr, n, cc), :])
        lo2, hi2 = _unpack_cols(buf_ref[slot, 1, _chunk_rows(tr, n, cc), :])
        ca = slice(cc * 128, (cc + 1) * 128)
        cb = slice(half + cc * 128, half + (cc + 1) * 128)
        o_ref[:, ca] = x_ref[:, ca] + gt_ref[:, ca] * (w1 * lo1 + w2 * lo2)
        o_ref[:, cb] = x_ref[:, cb] + gt_ref[:, cb] * (w1 * hi1 + w2 * hi2)
    if last:
        h_ref[...] = _rms(o_ref[...], g_ref[...])
    else:
        y = _rms(o_ref[...], g_ref[...]) * (1.0 + sc_ref[...]) + sh_ref[...]
        h_ref[...] = y.astype(h_ref.dtype)


def _combine(geo, x, ys, dest, w12, mod, layer, g_next, g_final):
    tr, d = geo.tr, geo.d
    nch = d // 256
    last = g_next is None
    tile = pl.BlockSpec((tr, d), lambda i, *_: (i, 0))
    in_specs = [tile, pl.BlockSpec((tr, 2), lambda i, *_: (i, 0)), geo.mod_spec(layer, 5),
                pl.BlockSpec(memory_space=pl.ANY)]
    operands = [x, w12, mod, ys]
    scratch = [pltpu.VMEM((2, 2, tr * nch, 128), jnp.uint32), pltpu.SemaphoreType.DMA((2,))]
    if last:
        in_specs.append(pl.BlockSpec((1, d), lambda i, *_: (0, 0)))
        operands.append(g_final.reshape(1, d))
        out_shape = jax.ShapeDtypeStruct((geo.b, geo.s, d), F32)
        out_specs = pl.BlockSpec((None, tr, d), lambda i, *_: (i // geo.nt, jnp.maximum(i % geo.nt - geo.nct, 0), 0))
        scratch = [pltpu.VMEM((tr, d), F32)] + scratch
    else:
        in_specs += [pl.BlockSpec((None, 1, d), lambda i, *_: (layer + 1, 0, 0)),
                     geo.mod_spec(layer + 1, 0), geo.mod_spec(layer + 1, 1)]
        operands += [g_next, mod, mod]
        out_shape = [jax.ShapeDtypeStruct((geo.m, d), F32), jax.ShapeDtypeStruct((geo.m, d), BF16)]
        out_specs = [tile, tile]
    return pl.pallas_call(
        functools.partial(_combine_kernel, m=geo.m, last=last),
        out_shape=out_shape,
        grid_spec=pltpu.PrefetchScalarGridSpec(
            num_scalar_prefetch=1, grid=(geo.m // tr,),
            in_specs=in_specs, out_specs=out_specs, scratch_shapes=scratch),
        name="moe_combine",
        compiler_params=_cparams("arbitrary"),
    )(dest, *operands)


def _moe(geo, x, routed, mod, layer, wg, wu, wd, g_next, g_final):
    tm = 256
    hfp, ridx, rwgt, rank, cnt = routed
    n_tiles = (2 * geo.m) // tm + N_EXPERTS
    counts = cnt[:, 0].astype(jnp.int32)
    tiles_per = (counts + tm - 1) // tm
    tile_start = jnp.cumsum(tiles_per) - tiles_per
    base = tile_start * tm
    n_valid = jnp.sum(tiles_per).reshape(1)
    tile_expert = jnp.sum(jnp.arange(n_tiles)[:, None] >= tile_start[None, :], axis=1).astype(jnp.int32) - 1
    experts = jnp.arange(N_EXPERTS, dtype=jnp.int32)[:, None, None]
    dest = (jnp.sum(jnp.where(ridx[None] == experts, base[:, None, None], 0), axis=0) + rank).reshape(-1)
    dest = dest * (geo.d // 256)
    tail = n_valid + jnp.arange(N_EXPERTS)
    fill_rows = jnp.concatenate([jnp.where(tiles_per > 0, base + (tiles_per - 1) * tm, -1),
                                 jnp.where(tail < n_tiles, tail * tm, -1)])
    xs = _dispatch(hfp, dest, fill_rows, n_tiles * tm, geo.tr, tm)
    has_rows = tiles_per > 0
    ordinal = jnp.cumsum(has_rows.astype(jnp.int32)) - 1
    ar = jnp.arange(N_EXPERTS)
    later = jnp.where((ar[None, :] > ar[:, None]) & has_rows[None, :], ar[None, :], N_EXPERTS)
    next_expert = jnp.min(later, axis=1)
    next_expert = jnp.where(next_expert < N_EXPERTS, next_expert, -1).astype(jnp.int32)
    ys = _gmm(xs, tile_expert, n_valid, tile_start, ordinal, next_expert, wg, wu, wd, layer, tm)
    return _combine(geo, x, ys, dest, rwgt.T, mod, layer, g_next, g_final)


def _proj_plans(d):
    src = _src_layout(d)
    off, total = _proj_layout()
    width = lambda name: src[name][1] - src[name][0]
    cf0, sc0 = src["cf"][0], src["sc"][0]
    assert src["gb"][0] == src["gf"][1]
    pieces = [(off["cq"], src["cq"][0], width("cq")), (off["ckv"], src["ckv"][0], width("ckv")),
              (off["kr"], src["kr"][0], width("kr")),
              (off["cf_a"], cf0, CF_WIDTH), (off["cf_b"], cf0 + CF_WIDTH, CF_WIDTH),
              (off["gla_q"], src["gla_q"][0], width("gla_q")), (off["gla_k"], src["gla_k"][0], width("gla_k")),
              (off["gla_v"], src["gla_v"][0], width("gla_v")), (off["gate"], src["gf"][0], 2 * GLA_GATE_RANK),
              (off["gla_r"], src["gla_r"][0], width("gla_r")),
              (off["sc_b"], sc0, SC_WIDTH), (off["sc_c"], sc0 + SC_WIDTH, SC_WIDTH),
              (off["sc_h"], sc0 + 2 * SC_WIDTH, SC_WIDTH)]
    tn = PROJ_TILE
    assert total % tn == 0 and all(dst // tn == (dst + rows - 1) // tn for dst, _, rows in pieces)
    rest = _ProjPlan(tn, total // tn, pieces=sorted(pieces))
    gates_w = N_BRANCHES * d
    gn = min(tn, gates_w)
    gates = _ProjPlan(gn, gates_w // gn, first_row=src["merge"][0])
    return gates, rest, off


def _rope_swap_perm():
    q = MLA_ROPE // 4
    return np.concatenate([np.arange(q, 2 * q), np.arange(0, q), np.arange(3 * q, 4 * q), np.arange(2 * q, 3 * q)])


def _prep_mla(w_uq, w_ukv):
    n_l = w_uq.shape[0]
    perm = _rope_swap_perm()
    wq = w_uq.reshape(n_l, MLA_Q_RANK, MLA_HEADS, MLA_NOPE + MLA_ROPE)
    qn, qr = wq[..., :MLA_NOPE], wq[..., MLA_NOPE:]
    zeros = lambda *s: jnp.zeros((n_l,) + s, w_uq.dtype)
    wqa = jnp.concatenate([qn, qr, qr], axis=-1)
    wqb = jnp.concatenate([zeros(MLA_Q_RANK, MLA_HEADS, MLA_NOPE), qr[..., perm],
                           zeros(MLA_Q_RANK, MLA_HEADS, MLA_ROPE)], axis=-1)
    wkv = w_ukv.reshape(n_l, MLA_KV_RANK, MLA_HEADS, MLA_NOPE + MLA_V)
    wka = jnp.concatenate([wkv[..., :MLA_NOPE], zeros(MLA_KV_RANK, MLA_HEADS, MLA_SLOT - MLA_NOPE)], axis=-1)
    wv = wkv[..., MLA_NOPE:]
    flat = lambda w: w.reshape(n_l, w.shape[1], -1).astype(BF16)
    sela = np.zeros((128, MLA_HEADS, MLA_SLOT), np.float32)
    selb = np.zeros((128, MLA_HEADS, MLA_SLOT), np.float32)
    for r in range(MLA_ROPE):
        sela[r, :, MLA_NOPE + r] = 1.0
        sela[r, :, MLA_NOPE + MLA_ROPE + r] = 1.0
        selb[perm[r], :, MLA_NOPE + r] = 1.0
    sel = lambda s: jnp.asarray(s.reshape(128, -1), BF16)
    return flat(wqa), flat(wqb), flat(wka), sel(sela), sel(selb), flat(wv)


def _rope_tables(geo):
    rows = geo.s // GRID_W
    pos = jnp.arange(rows * GRID_W)
    row = (pos // GRID_W).astype(F32)
    col = (pos % GRID_W).astype(F32)
    n_freq = MLA_ROPE // 4
    freqs = ROPE_THETA ** (-jnp.arange(n_freq, dtype=F32) / n_freq)
    cr, sr = jnp.cos(row[:, None] * freqs), jnp.sin(row[:, None] * freqs)
    cc, sn = jnp.cos(col[:, None] * freqs), jnp.sin(col[:, None] * freqs)
    cos32 = jnp.concatenate([cr, cr, cc, cc], axis=-1)
    sin32 = jnp.concatenate([-sr, sr, -sn, sn], axis=-1)
    s, c = geo.s, geo.c
    one = lambda n, w: jnp.ones((n, w), F32)
    zero = lambda n, w: jnp.zeros((n, w), F32)
    scale = (MLA_NOPE + MLA_ROPE) ** -0.5
    qca = jnp.concatenate([jnp.concatenate([one(c, 64), zero(c, 32), one(c, 32)], 1),
                           jnp.concatenate([one(s, 64), cos32, one(s, 32)], 1)], 0) * scale
    qsb = jnp.concatenate([zero(c, 128), jnp.concatenate([zero(s, 64), sin32, zero(s, 32)], 1)], 0) * scale
    kca = jnp.concatenate([jnp.concatenate([one(c, 64), zero(c, 32), one(c, 32)], 1),
                           jnp.concatenate([one(s, 64), cos32, zero(s, 32)], 1)], 0)
    ksb = jnp.concatenate([zero(c, 128), jnp.concatenate([zero(s, 64), sin32, zero(s, 32)], 1)], 0)
    return qca, qsb, kca, ksb


def kernel(x, c, ctx, c_ctx, w_mod, b_mod, g_mix, g_ffn, w_in, g_q, w_uq, g_kv, w_ukv, w_mla_o, w_cf_dw, b_cf_dw, g_cf_ln, b_cf_ln, w_cf_o, w_gla_gf, b_gla_gf, w_gla_gb, b_gla_gb, g_gla_norm, w_gla_o, w_sc_conv, w_sc_o, w_mix_o, w_router, b_router, w_e_gate, w_e_up, w_e_down, g_final):
    bsz, seq, d = x.shape
    n_ctx = ctx.shape[1]
    depth = w_in.shape[0]
    geo = _Geom(bsz, n_ctx, seq, d)
    assert bsz + 1 <= 8 and seq % GRID_W == 0 and n_ctx % GLA_CHUNK == 0 and seq % GLA_CHUNK == 0

    plan_gates, plan_rest, off = _proj_plans(d)
    w_in_t = jnp.swapaxes(w_in, 1, 2)
    wqa, wqb, wka, sela, selb, wv = _prep_mla(w_uq, w_ukv)
    tabs = _rope_tables(geo)
    w_br_half = [_to_bf16(w, 0.5) for w in (w_mla_o, w_cf_o, w_gla_o, w_sc_o)]
    w_mix = _to_bf16(w_mix_o)
    gate_w = jnp.zeros((depth, 2, 128, GLA_HEADS * GLA_DK), F32)
    gate_w = gate_w.at[:, 0, :GLA_GATE_RANK].set(w_gla_gf).at[:, 1, GLA_GATE_RANK:2 * GLA_GATE_RANK].set(w_gla_gb)
    gate_w = gate_w.astype(BF16)
    gate_b = jnp.stack([b_gla_gf, b_gla_gb], axis=1)[:, :, None, :]
    mat3, msk4 = _gla_tables()
    wr_t = w_router.T
    wr_hi = wr_t.astype(BF16)
    wr_stack = jnp.concatenate([wr_hi, (wr_t - wr_hi.astype(F32)).astype(BF16)], axis=0)
    vec = lambda a: a.reshape(a.shape[0], 1, a.shape[-1])

    cc = jnp.zeros((8, d), F32).at[:bsz].set(c).at[bsz].set(c_ctx)
    mod = _modulation(cc, w_mod, b_mod).reshape(depth, 8, 1, 6 * d)

    xa, h = _first_norm(geo, x, ctx, vec(g_mix), mod)
    for l in range(depth):
        u = _project(h, w_in_t, l, plan_rest, 1.0)
        u_gates = _project(h, w_in_t, l, plan_gates, 0.5)
        q, k, v = _mla_qkv(geo, u, off, l, vec(g_q), vec(g_kv), wqa, wqb, wka, sela, selb, wv, tabs)
        attn = _mla_attention(geo, q, k, v)
        conf, sconv = _conv_branches(geo, u, off, l, w_cf_dw, vec(b_cf_dw), vec(g_cf_ln), vec(b_cf_ln), w_sc_conv)
        gla_f, gla_b = _gla_scan(geo, u, off, l, gate_w, gate_b, mat3, msk4)
        xa, *routed = _merge_mix(geo, attn, conf, gla_f, gla_b, u, u_gates, off, vec(g_gla_norm), sconv, w_br_half,
                                 w_mix, xa, mod, l, vec(g_ffn), wr_stack, b_router.reshape(N_EXPERTS, 1))
        if l + 1 < depth:
            xa, h = _moe(geo, xa, routed, mod, l, w_e_gate, w_e_up, w_e_down, vec(g_mix), None)
        else:
            return _moe(geo, xa, routed, mod, l, w_e_gate, w_e_up, w_e_down, None, g_final)
```
